```python
import jax, jax.numpy as jnp
from jax import lax
import numpy as np

D_MODEL = 1024
BATCH = 32
SEQ = 2048
DEPTH = 1
DEC_BATCH = 16
DEC_SEQ = 2048
PAST_LEN = 128

HEAD_DIM = 64
ATTN_GROUPS = ((128, 1), (512, 4), (2048, 16))
N_GROUPS = len(ATTN_GROUPS)
HEADS_PER_GROUP = 8
ATTN_WIDTH = HEADS_PER_GROUP * HEAD_DIM
ROPE_DIM = HEAD_DIM // 4
ROPE_THETA = 500000.0
FOURIER_GROUPS = 8
FOURIER_GROUP_DIM = 64
FOURIER_WIDTH = FOURIER_GROUPS * FOURIER_GROUP_DIM
QKV_WIDTH = N_GROUPS * 3 * ATTN_WIDTH
N_BRANCHES = 2
IN_WIDTH = QKV_WIDTH + FOURIER_WIDTH + N_BRANCHES * D_MODEL
N_EXPERTS = 32
TOP_K = 4
D_FF = D_MODEL
SWIGLU_ALPHA = 1.702
SWIGLU_LIMIT = 7.0
EXPERT_BLOCK = 256
NORM_EPS = 1e-6
NEG_INF = -1e30

kernel_name = "hybrid_dilated_fourier_moe_encoder"


def rms_norm(x, g):
    xf = x.astype(jnp.float32)
    return xf * lax.rsqrt(jnp.mean(xf * xf, axis=-1, keepdims=True) + NORM_EPS) * g.astype(jnp.float32)


def partial_rope(x, pos):
    half = ROPE_DIM // 2
    inv_freq = jnp.power(jnp.float32(ROPE_THETA), -jnp.arange(half, dtype=jnp.float32) * 2.0 / ROPE_DIM)
    ang = pos[:, None] * inv_freq[None, :]
    cos = jnp.cos(ang)[None, :, None, :]
    sin = jnp.sin(ang)[None, :, None, :]
    x1, x2, rest = x[..., :half], x[..., half:ROPE_DIM], x[..., ROPE_DIM:]
    return jnp.concatenate([x1 * cos - x2 * sin, x2 * cos + x1 * sin, rest], axis=-1)


def dilated_band_attention(q, k, v, window, dilation):
    b, s, h, hd = q.shape
    half = window // (2 * dilation)
    L = s // dilation
    nb = -(-L // half)
    Lp = nb * half

    def to_strided(t):
        return t.reshape(b, L, dilation, h, hd).transpose(0, 2, 1, 3, 4)

    qs, ks, vs = to_strided(q), to_strided(k), to_strided(v)
    qb = jnp.pad(qs, ((0, 0), (0, 0), (0, Lp - L), (0, 0), (0, 0))).reshape(b, dilation, nb, half, h, hd)

    def slabs(t):
        tp = jnp.pad(t, ((0, 0), (0, 0), (half, half + Lp - L), (0, 0), (0, 0)))
        tp = tp.reshape(b, dilation, nb + 2, half, h, hd)
        return jnp.concatenate([tp[:, :, :-2], tp[:, :, 1:-1], tp[:, :, 2:]], axis=3)

    kb, vb = slabs(ks), slabs(vs)
    scores = jnp.einsum('brnqhc,brnkhc->brnhqk', qb, kb) * (hd ** -0.5)
    mq = jnp.arange(nb)[:, None] * half + jnp.arange(half)[None, :]
    mk = jnp.arange(nb)[:, None] * half - half + jnp.arange(3 * half)[None, :]
    valid = ((jnp.abs(mq[:, :, None] - mk[:, None, :]) <= half)
             & (mk[:, None, :] >= 0) & (mk[:, None, :] < L))
    scores = jnp.where(valid[None, None, :, None], scores, NEG_INF)
    mx = jnp.max(scores, axis=-1, keepdims=True)
    p = jnp.exp(scores - mx)
    den = jnp.sum(p, axis=-1, keepdims=True)
    lse = (mx + jnp.log(den))[..., 0].transpose(0, 1, 2, 4, 3)
    o = jnp.einsum('brnhqk,brnkhc->brnqhc', p, vb) / den[..., 0].transpose(0, 1, 2, 4, 3)[..., None]
    o = o.reshape(b, dilation, Lp, h, hd)[:, :, :L].transpose(0, 2, 1, 3, 4).reshape(b, s, h, hd)
    lse = lse.reshape(b, dilation, Lp, h)[:, :, :L].transpose(0, 2, 1, 3).reshape(b, s, h)
    return o, lse


def token_mixer(u, pos, w_in, q_gain, k_gain, w_attn_proj, w_fourier_proj, w_out):
    b, s, _ = u.shape
    z = jnp.einsum('bsd,de->bse', u, w_in).astype(jnp.float32)
    qkv = z[..., :QKV_WIDTH].reshape(b, s, N_GROUPS, 3, HEADS_PER_GROUP, HEAD_DIM)
    fz = z[..., QKV_WIDTH:QKV_WIDTH + FOURIER_WIDTH]
    gates = z[..., QKV_WIDTH + FOURIER_WIDTH:].reshape(b, s, N_BRANCHES, D_MODEL)

    outs, lses = [], []
    for g, (window, dilation) in enumerate(ATTN_GROUPS):
        q = partial_rope(rms_norm(qkv[:, :, g, 0], q_gain[g]), pos)
        k = partial_rope(rms_norm(qkv[:, :, g, 1], k_gain[g]), pos)
        v = qkv[:, :, g, 2]
        o, lse = dilated_band_attention(q, k, v, window, dilation)
        outs.append(o)
        lses.append(lse)
    wts = jax.nn.softmax(jnp.stack(lses, axis=0), axis=0)
    attn = jnp.sum(wts[..., None] * jnp.stack(outs, axis=0), axis=0)
    a = jnp.einsum('bse,ed->bsd', attn.reshape(b, s, ATTN_WIDTH), w_attn_proj)

    fg = fz.reshape(b, s, FOURIER_GROUPS, FOURIER_GROUP_DIM).astype(jnp.float32)
    fm = jnp.fft.fft2(fg, axes=(1, 3), norm='ortho').real.astype(jnp.float32)
    f = jnp.einsum('bse,ed->bsd', fm.reshape(b, s, FOURIER_WIDTH), w_fourier_proj)

    merged = jax.nn.sigmoid(gates[:, :, 0]) * a + jax.nn.sigmoid(gates[:, :, 1]) * f
    return jnp.einsum('bsd,de->bse', merged, w_out).astype(jnp.float32)


def moe_ffn(u, w_router, b_router, w1, b1, w2, b2):
    b, s, d = u.shape
    t = b * s
    xt = u.reshape(t, d)
    logits = (xt @ w_router + b_router).astype(jnp.float32)
    top_val, top_idx = lax.top_k(logits, TOP_K)
    top_w = jax.nn.softmax(top_val, axis=-1)

    n_assign = t * TOP_K
    flat_e = top_idx.reshape(-1)
    flat_tok = jnp.arange(n_assign) // TOP_K
    order = jnp.argsort(flat_e)
    sorted_e = flat_e[order]
    sorted_tok = flat_tok[order]
    counts = jnp.bincount(flat_e, length=N_EXPERTS)
    padded = (counts + EXPERT_BLOCK - 1) // EXPERT_BLOCK * EXPERT_BLOCK
    start = jnp.cumsum(counts) - counts
    pend = jnp.cumsum(padded)
    pstart = pend - padded
    dest = pstart[sorted_e] + jnp.arange(n_assign) - start[sorted_e]
    n_blocks = -(-(n_assign + N_EXPERTS * (EXPERT_BLOCK - 1)) // EXPERT_BLOCK)
    xs = jnp.zeros((n_blocks * EXPERT_BLOCK, d), xt.dtype).at[dest].set(xt[sorted_tok])
    block_e = jnp.minimum(jnp.searchsorted(pend, jnp.arange(n_blocks) * EXPERT_BLOCK, side='right'),
                          N_EXPERTS - 1)

    def expert_block(args):
        xb, e = args
        hdn = (xb @ w1[e] + b1[e]).astype(jnp.float32)
        glu = jnp.minimum(hdn[:, :D_FF], SWIGLU_LIMIT)
        lin = jnp.clip(hdn[:, D_FF:], -SWIGLU_LIMIT, SWIGLU_LIMIT)
        act = (lin + 1.0) * (glu * jax.nn.sigmoid(SWIGLU_ALPHA * glu))
        return (act @ w2[e] + b2[e]).astype(jnp.float32)

    ys = lax.map(expert_block, (xs.reshape(n_blocks, EXPERT_BLOCK, d), block_e)).reshape(-1, d)
    y_assign = ys[dest] * top_w.reshape(-1)[order][:, None]
    out = jax.ops.segment_sum(y_assign, sorted_tok, num_segments=t)
    return out.reshape(b, s, d)


def encoder_layer(x, c, pos, w_ada, b_ada, norm1_g, norm2_g, w_in, q_gain, k_gain,
                  w_attn_proj, w_fourier_proj, w_out, w_router, b_router, w1, b1, w2, b2):
    mod = (jax.nn.silu(c.astype(jnp.float32)) @ w_ada + b_ada).astype(jnp.float32)
    shift1, scale1, gate1, shift2, scale2, gate2 = jnp.split(mod[:, None, :], 6, axis=-1)
    u = rms_norm(x, norm1_g) * (1.0 + scale1) + shift1
    h = x.astype(jnp.float32) + gate1 * token_mixer(u, pos, w_in, q_gain, k_gain,
                                                     w_attn_proj, w_fourier_proj, w_out)
    u2 = rms_norm(h, norm2_g) * (1.0 + scale2) + shift2
    out = h + gate2 * moe_ffn(u2, w_router, b_router, w1, b1, w2, b2)
    return out.astype(x.dtype)


def trunk(x, c, w_ada, b_ada, norm1_g, norm2_g, w_in, q_gain, k_gain,
          w_attn_proj, w_fourier_proj, w_out, w_router, b_router, w1, b1, w2, b2):
    pos = jnp.arange(x.shape[1], dtype=jnp.float32)
    for layer in range(DEPTH):
        x = encoder_layer(x, c, pos, w_ada[layer], b_ada[layer], norm1_g[layer], norm2_g[layer],
                          w_in[layer], q_gain[layer], k_gain[layer], w_attn_proj[layer],
                          w_fourier_proj[layer], w_out[layer], w_router[layer], b_router[layer],
                          w1[layer], b1[layer], w2[layer], b2[layer])
    return x


def setup_inputs(seed: int = 0) -> dict:
    key = jax.random.key(seed)
    ks = jax.random.split(key, 20)
    f32 = jnp.float32
    nrm = lambda k, shape, scale: jax.random.normal(k, shape, f32) * scale
    return {
        "x_prompt": nrm(ks[0], (BATCH, SEQ, D_MODEL), 1.0),
        "x_sample": nrm(ks[1], (DEC_BATCH, DEC_SEQ, D_MODEL), 1.0),
        "c_prompt": nrm(ks[2], (BATCH, D_MODEL), 1.0),
        "c_sample": nrm(ks[3], (DEC_BATCH, D_MODEL), 1.0),
        "w_ada": nrm(ks[4], (DEPTH, D_MODEL, 6 * D_MODEL), 0.5 * D_MODEL ** -0.5),
        "b_ada": nrm(ks[5], (DEPTH, 6 * D_MODEL), 0.02),
        "norm1_g": 1.0 + nrm(ks[6], (DEPTH, D_MODEL), 0.02),
        "norm2_g": 1.0 + nrm(ks[7], (DEPTH, D_MODEL), 0.02),
        "w_in": nrm(ks[8], (DEPTH, D_MODEL, IN_WIDTH), D_MODEL ** -0.5),
        "q_gain": 1.0 + nrm(ks[9], (DEPTH, N_GROUPS, HEAD_DIM), 0.02),
        "k_gain": 1.0 + nrm(ks[10], (DEPTH, N_GROUPS, HEAD_DIM), 0.02),
        "w_attn_proj": nrm(ks[11], (DEPTH, ATTN_WIDTH, D_MODEL), ATTN_WIDTH ** -0.5),
        "w_fourier_proj": nrm(ks[12], (DEPTH, FOURIER_WIDTH, D_MODEL), FOURIER_WIDTH ** -0.5),
        "w_out": nrm(ks[13], (DEPTH, D_MODEL, D_MODEL), D_MODEL ** -0.5),
        "w_router": nrm(ks[14], (DEPTH, D_MODEL, N_EXPERTS), D_MODEL ** -0.5),
        "b_router": nrm(ks[15], (DEPTH, N_EXPERTS), 0.01),
        "w1": nrm(ks[16], (DEPTH, N_EXPERTS, D_MODEL, 2 * D_FF), D_MODEL ** -0.5),
        "b1": nrm(ks[17], (DEPTH, N_EXPERTS, 2 * D_FF), 0.02),
        "w2": nrm(ks[18], (DEPTH, N_EXPERTS, D_FF, D_MODEL), D_FF ** -0.5),
        "b2": nrm(ks[19], (DEPTH, N_EXPERTS, D_MODEL), 0.02),
    }


def reference(x_prompt, x_sample, c_prompt, c_sample, w_ada, b_ada, norm1_g, norm2_g, w_in,
              q_gain, k_gain, w_attn_proj, w_fourier_proj, w_out, w_router, b_router,
              w1, b1, w2, b2):
    y_prompt = trunk(x_prompt, c_prompt, w_ada, b_ada, norm1_g, norm2_g, w_in, q_gain, k_gain,
                     w_attn_proj, w_fourier_proj, w_out, w_router, b_router, w1, b1, w2, b2)
    y_sample = trunk(x_sample, c_sample, w_ada, b_ada, norm1_g, norm2_g, w_in, q_gain, k_gain,
                     w_attn_proj, w_fourier_proj, w_out, w_router, b_router, w1, b1, w2, b2)
    return (y_prompt, y_sample)
```

```python
import functools

import jax
import jax.numpy as jnp
from jax import lax
from jax.experimental import pallas as pl
from jax.experimental.pallas import tpu as pltpu

F32 = jnp.float32
BF16 = jnp.bfloat16

D_MODEL = 1024
HEAD_DIM = 64
HEADS = 8
ATTN_WIDTH = HEADS * HEAD_DIM
DILATIONS = (1, 4, 16)
BAND = 64
N_GROUPS = 3
QKVF_WIDTH = 5120
FOURIER_GROUP_DIM = 64
ROPE_DIM = 16
ROPE_THETA = 500000.0
N_EXPERTS = 32
TOP_K = 4
D_FF = 1024
SWIGLU_ALPHA = 1.702
SWIGLU_LIMIT = 7.0
NORM_EPS = 1e-6
NEG_INF = -1e30

LANES = 128
EXPERT_ROWS = 256
VMEM_LIMIT = 56 * 1024 * 1024


def _dot(a, b):
    return jnp.dot(a, b, preferred_element_type=F32)


def _split_bf16(v):
    hi = v.astype(BF16)
    lo = (v - hi.astype(F32)).astype(BF16)
    return hi, lo


def _rms_mod(x, gain, scale, shift):
    ms = jnp.mean(x * x, axis=-1, keepdims=True)
    return x * lax.rsqrt(ms + NORM_EPS) * gain * (1.0 + scale) + shift


def _const_spec(shape):
    return pl.BlockSpec(shape, lambda *_: (0,) * len(shape), pipeline_mode=pl.Buffered(1))


def _ada_kernel(c_ref, w_ref, b_ref, o_ref):
    c = c_ref[...]
    a = c * jax.nn.sigmoid(c)
    a_hi, a_lo = _split_bf16(a)
    w_hi, w_lo = _split_bf16(w_ref[...])
    o_ref[...] = _dot(a_hi, w_hi) + _dot(a_lo, w_hi) + _dot(a_hi, w_lo) + b_ref[...]


def _ada(c, w_ada, b_ada):
    b = c.shape[0]
    n = w_ada.shape[1]
    tn = D_MODEL
    return pl.pallas_call(
        _ada_kernel,
        grid=(n // tn,),
        in_specs=[pl.BlockSpec((b, D_MODEL), lambda j: (0, 0)),
                  pl.BlockSpec((D_MODEL, tn), lambda j: (0, j)),
                  pl.BlockSpec((1, tn), lambda j: (0, j))],
        out_specs=pl.BlockSpec((b, tn), lambda j: (0, j)),
        out_shape=jax.ShapeDtypeStruct((b, n), F32),
        compiler_params=pltpu.CompilerParams(vmem_limit_bytes=VMEM_LIMIT),
        name="ada",
    )(c, w_ada, b_ada.reshape(1, n))


def _inproj_kernel(x_ref, mod_ref, g1_ref, w_ref, gain_ref, rope_ref, gmat_ref, o_ref):
    x = x_ref[0]
    mod = mod_ref[0]
    u = _rms_mod(x, g1_ref[...], mod[1:2], mod[0:1]).astype(BF16)
    cos_t, sin_prev, sin_next = rope_ref[0], rope_ref[1], rope_ref[2]
    for g in range(N_GROUPS):
        for j in range(3):
            c0 = (3 * g + j) * ATTN_WIDTH
            z = _dot(u, w_ref[:, c0:c0 + ATTN_WIDTH])
            if j < 2:
                ms = _dot((z * z).astype(BF16), gmat_ref[...])
                gain = gain_ref[2 * g + j:2 * g + j + 1, :]
                if j == 0:
                    gain = gain * (HEAD_DIM ** -0.5)
                z = z * lax.rsqrt(ms + NORM_EPS) * gain
                blocks = []
                for blk in range(ATTN_WIDTH // LANES):
                    zb = z[:, blk * LANES:(blk + 1) * LANES]
                    blocks.append(zb * cos_t
                                  + pltpu.roll(zb, ROPE_DIM // 2, 1) * sin_prev
                                  + pltpu.roll(zb, LANES - ROPE_DIM // 2, 1) * sin_next)
                z = jnp.concatenate(blocks, axis=1)
            o_ref[0, :, c0:c0 + ATTN_WIDTH] = z.astype(BF16)
    c0 = 9 * ATTN_WIDTH
    o_ref[0, :, c0:] = _dot(u, w_ref[:, c0:]).astype(BF16)


def _inproj(x, mod, g1, w_qkvf, gains, rope, gmat, ts=512):
    b, s, _ = x.shape
    return pl.pallas_call(
        _inproj_kernel,
        grid=(b, s // ts),
        in_specs=[pl.BlockSpec((1, ts, D_MODEL), lambda i, j: (i, j, 0)),
                  pl.BlockSpec((1, 6, D_MODEL), lambda i, j: (i, 0, 0)),
                  _const_spec((1, D_MODEL)),
                  _const_spec((D_MODEL, QKVF_WIDTH)),
                  _const_spec((8, ATTN_WIDTH)),
                  pl.BlockSpec((3, ts, LANES), lambda i, j: (0, j, 0)),
                  _const_spec((ATTN_WIDTH, ATTN_WIDTH))],
        out_specs=pl.BlockSpec((1, ts, QKVF_WIDTH), lambda i, j: (i, j, 0)),
        out_shape=jax.ShapeDtypeStruct((b, s, QKVF_WIDTH), BF16),
        compiler_params=pltpu.CompilerParams(
            dimension_semantics=("arbitrary", "arbitrary"), vmem_limit_bytes=VMEM_LIMIT),
        name="inproj",
    )(x, mod, g1, w_qkvf, gains, rope, gmat)


def _attn_kernel(q_ref, k_ref, v_ref, o_ref, lse_ref, *, length):
    tq = LANES
    tk = min(2 * LANES, length)
    lane = lax.broadcasted_iota(jnp.int32, (tq, LANES), 1)
    first_head = lane < HEAD_DIM
    rel = (lax.broadcasted_iota(jnp.int32, (tq, tk), 1)
           - lax.broadcasted_iota(jnp.int32, (tq, tk), 0))

    def q_block(i, carry):
        q0 = pl.multiple_of(i * tq, tq)
        ws = pl.multiple_of(jnp.clip(i * tq - BAND, 0, length - tk), BAND)
        valid = jnp.abs(rel + (ws - i * tq)) <= BAND
        lse_tile = jnp.zeros((tq, LANES), F32)
        for p in range(ATTN_WIDTH // LANES):
            cs = slice(p * LANES, (p + 1) * LANES)
            q2 = q_ref[0, pl.ds(q0, tq), cs]
            k2 = k_ref[0, pl.ds(ws, tk), cs]
            v2 = v_ref[0, pl.ds(ws, tk), cs]
            o_pair = None
            for hh in range(2):
                sel = first_head if hh == 0 else jnp.logical_not(first_head)
                qa = jnp.where(sel, q2, jnp.zeros_like(q2))
                s = lax.dot_general(qa, k2, (((1,), (1,)), ((), ())), preferred_element_type=F32)
                s = jnp.where(valid, s, NEG_INF)
                m = jnp.max(s, axis=-1, keepdims=True)
                pr = jnp.exp(s - m)
                den = jnp.sum(pr, axis=-1, keepdims=True)
                o = _dot(pr.astype(BF16), v2) / den
                lse_tile = jnp.where(lane == 2 * p + hh, m + jnp.log(den), lse_tile)
                o_pair = o if hh == 0 else jnp.where(first_head, o_pair, o)
            o_ref[0, pl.ds(q0, tq), cs] = o_pair.astype(BF16)
        lse_ref[0, pl.ds(q0, tq), :] = lse_tile
        return carry

    lax.fori_loop(0, length // tq, q_block, 0)


def _attn(qkvf, g):
    b, s, _ = qkvf.shape
    d = DILATIONS[g]
    length = s // d
    nblk = QKVF_WIDTH // ATTN_WIDTH
    view = qkvf.reshape(b, length, d * QKVF_WIDTH)

    def in_spec(j):
        return pl.BlockSpec((1, length, ATTN_WIDTH), lambda i, r: (i, 0, r * nblk + 3 * g + j))

    o, lse = pl.pallas_call(
        functools.partial(_attn_kernel, length=length),
        grid=(b, d),
        in_specs=[in_spec(0), in_spec(1), in_spec(2)],
        out_specs=[pl.BlockSpec((1, length, ATTN_WIDTH), lambda i, r: (i, 0, r)),
                   pl.BlockSpec((1, length, LANES), lambda i, r: (i, 0, r))],
        out_shape=[jax.ShapeDtypeStruct((b, length, d * ATTN_WIDTH), BF16),
                   jax.ShapeDtypeStruct((b, length, d * LANES), F32)],
        compiler_params=pltpu.CompilerParams(
            dimension_semantics=("arbitrary", "arbitrary"), vmem_limit_bytes=VMEM_LIMIT),
        name=f"attn{g}",
    )(view, view, view)
    return o.reshape(b, s, ATTN_WIDTH), lse.reshape(b, s, LANES)


def _fourier_kernel(x_ref, bd_ref, cs_ref, ns_ref, o_ref, y_ref, *, seq, rows):
    for r in range(seq // rows):
        rs = slice(r * rows, (r + 1) * rows)
        y_ref[rs, :] = _dot(x_ref[0, rs, :], bd_ref[...]).astype(BF16)
    for r in range(seq // rows):
        rs = slice(r * rows, (r + 1) * rows)
        o = _dot(cs_ref[rs, :], y_ref[:, :ATTN_WIDTH]) + _dot(ns_ref[rs, :], y_ref[:, ATTN_WIDTH:])
        o_ref[0, rs, :] = o.astype(BF16)


def _fourier(qkvf, bd, cs, ns):
    b, s, _ = qkvf.shape
    return pl.pallas_call(
        functools.partial(_fourier_kernel, seq=s, rows=256),
        grid=(b,),
        in_specs=[pl.BlockSpec((1, s, ATTN_WIDTH), lambda i: (i, 0, 9)),
                  _const_spec((ATTN_WIDTH, 2 * ATTN_WIDTH)),
                  _const_spec((s, s)),
                  _const_spec((s, s))],
        out_specs=pl.BlockSpec((1, s, ATTN_WIDTH), lambda i: (i, 0, 0)),
        out_shape=jax.ShapeDtypeStruct((b, s, ATTN_WIDTH), BF16),
        scratch_shapes=[pltpu.VMEM((s, 2 * ATTN_WIDTH), BF16)],
        compiler_params=pltpu.CompilerParams(
            dimension_semantics=("arbitrary",), vmem_limit_bytes=VMEM_LIMIT),
        name="fourier",
    )(qkvf, bd, cs, ns)


def _merge_kernel(x_ref, mod_ref, o0_ref, o1_ref, o2_ref, l0_ref, l1_ref, l2_ref, fm_ref,
                  g1_ref, g2_ref, wg_ref, wap_ref, wfp_ref, wout_ref, wrh_ref, wrl_ref, br_ref,
                  expand_ref, ltri_ref,
                  h_ref, u2_ref, ridx_ref, rw_ref, cnt_ref, cnt_scr):
    @pl.when((pl.program_id(0) == 0) & (pl.program_id(1) == 0))
    def _():
        cnt_scr[...] = jnp.zeros_like(cnt_scr)

    x = x_ref[0]
    mod = mod_ref[0]
    shift1, scale1, gate1 = mod[0:1], mod[1:2], mod[2:3]
    shift2, scale2 = mod[3:4], mod[4:5]
    u = _rms_mod(x, g1_ref[...], scale1, shift1).astype(BF16)
    gates = _dot(u, wg_ref[...])

    lses = (l0_ref[0], l1_ref[0], l2_ref[0])
    mx = jnp.maximum(jnp.maximum(lses[0], lses[1]), lses[2])
    es = [jnp.exp(l - mx) for l in lses]
    den = es[0] + es[1] + es[2]
    attn = None
    for e, o_ref in zip(es, (o0_ref, o1_ref, o2_ref)):
        w_hi, w_lo = _split_bf16(e / den)
        wf = _dot(w_hi, expand_ref[...]) + _dot(w_lo, expand_ref[...])
        term = wf * o_ref[0].astype(F32)
        attn = term if attn is None else attn + term

    a = _dot(attn.astype(BF16), wap_ref[...])
    f = _dot(fm_ref[0], wfp_ref[...])
    merged = jax.nn.sigmoid(gates[:, :D_MODEL]) * a + jax.nn.sigmoid(gates[:, D_MODEL:]) * f
    h = x + gate1 * _dot(merged.astype(BF16), wout_ref[...])
    h_ref[0] = h

    u2 = _rms_mod(h, g2_ref[...], scale2, shift2)
    u2_ref[0] = u2
    u_hi, u_lo = _split_bf16(u2)
    logits = (_dot(u_hi, wrh_ref[...]) + _dot(u_lo, wrh_ref[...]) + _dot(u_hi, wrl_ref[...])
              + br_ref[...])

    ts = logits.shape[0]
    lane = lax.broadcasted_iota(jnp.int32, (ts, LANES), 1)
    lg = jnp.where(lane < N_EXPERTS, logits, -jnp.inf)
    vals, ids = [], []
    onehot = jnp.zeros((ts, LANES), F32)
    for _ in range(TOP_K):
        m = jnp.max(lg, axis=-1, keepdims=True)
        idx = jnp.min(jnp.where(lg == m, lane, LANES), axis=-1, keepdims=True)
        hit = lane == idx
        vals.append(m)
        ids.append(idx)
        onehot = jnp.where(hit, 1.0, onehot)
        lg = jnp.where(hit, -jnp.inf, lg)
    exps = [jnp.exp(v - vals[0]) for v in vals]
    esum = exps[0] + exps[1] + exps[2] + exps[3]

    cum = _dot(ltri_ref[...], onehot.astype(BF16)) + cnt_scr[0:1, :]
    ridx = jnp.zeros((ts, LANES), F32)
    rw = jnp.zeros((ts, LANES), F32)
    for k in range(TOP_K):
        rank = jnp.sum(jnp.where(lane == ids[k], cum, 0.0), axis=-1, keepdims=True)
        ridx = jnp.where(lane == k, ids[k].astype(F32), ridx)
        ridx = jnp.where(lane == TOP_K + k, rank, ridx)
        rw = jnp.where(lane == k, exps[k] / esum, rw)
    ridx_ref[0] = ridx.astype(jnp.int32)
    rw_ref[0] = rw
    cnt_scr[...] = cnt_scr[...] + jnp.sum(onehot, axis=0, keepdims=True)
    cnt_ref[...] = cnt_scr[...]


def _merge(x, mod, os_, lses, fm, g1, g2, wg, wap, wfp, wout, wrh, wrl, br, expand, ltri, ts=256):
    b, s, _ = x.shape
    tile = lambda w: pl.BlockSpec((1, ts, w), lambda i, j: (i, j, 0))
    in_specs = ([tile(D_MODEL), pl.BlockSpec((1, 6, D_MODEL), lambda i, j: (i, 0, 0))]
                + [tile(ATTN_WIDTH)] * 3 + [tile(LANES)] * 3 + [tile(ATTN_WIDTH)]
                + [_const_spec(a.shape) for a in (g1, g2, wg, wap, wfp, wout, wrh, wrl, br, expand, ltri)])
    return pl.pallas_call(
        _merge_kernel,
        grid=(b, s // ts),
        in_specs=in_specs,
        out_specs=[tile(D_MODEL), tile(D_MODEL), tile(LANES), tile(LANES),
                   pl.BlockSpec((8, LANES), lambda i, j: (0, 0))],
        out_shape=[jax.ShapeDtypeStruct((b, s, D_MODEL), F32),
                   jax.ShapeDtypeStruct((b, s, D_MODEL), F32),
                   jax.ShapeDtypeStruct((b, s, LANES), jnp.int32),
                   jax.ShapeDtypeStruct((b, s, LANES), F32),
                   jax.ShapeDtypeStruct((8, LANES), F32)],
        scratch_shapes=[pltpu.VMEM((8, LANES), F32)],
        compiler_params=pltpu.CompilerParams(
            dimension_semantics=("arbitrary", "arbitrary"), vmem_limit_bytes=VMEM_LIMIT),
        name="merge",
    )(x, mod, *os_, *lses, fm, g1, g2, wg, wap, wfp, wout, wrh, wrl, br, expand, ltri)


def _row_copy(src_ref, src_row, dst_ref, dst_row, sem):
    return pltpu.make_async_copy(src_ref.at[pl.ds(src_row, 1)], dst_ref.at[pl.ds(dst_row, 1)], sem)


def _scatter_kernel(dest_ref, u2_ref, xs_in_ref, xs_ref, sem, *, tt):
    del xs_in_ref
    base = pl.program_id(0) * tt

    def copies(t):
        return [_row_copy(u2_ref, base + t, xs_ref, dest_ref[t * TOP_K + k], sem) for k in range(TOP_K)]

    def start(t, c):
        for cp in copies(t):
            cp.start()
        return c

    def wait(t, c):
        for cp in copies(t):
            cp.wait()
        return c

    lax.fori_loop(0, tt, start, 0)
    lax.fori_loop(0, tt, wait, 0)


def _scatter(dest, u2, xs_zero, tt=256):
    t = u2.shape[0]
    return pl.pallas_call(
        functools.partial(_scatter_kernel, tt=tt),
        grid=(t // tt,),
        in_specs=[pl.BlockSpec((tt * TOP_K,), lambda i: (i,), memory_space=pltpu.SMEM),
                  pl.BlockSpec(memory_space=pl.ANY),
                  pl.BlockSpec(memory_space=pl.ANY)],
        out_specs=pl.BlockSpec(memory_space=pl.ANY),
        out_shape=jax.ShapeDtypeStruct(xs_zero.shape, xs_zero.dtype),
        scratch_shapes=[pltpu.SemaphoreType.DMA],
        input_output_aliases={2: 0},
        compiler_params=pltpu.CompilerParams(
            dimension_semantics=("arbitrary",), has_side_effects=True, vmem_limit_bytes=VMEM_LIMIT),
        name="scatter",
    )(dest, u2, xs_zero)


def _expert_kernel(be_ref, nb_ref, xs_ref, w1_ref, b1_ref, w2_ref, b2_ref, ys_ref):
    del be_ref
    i = pl.program_id(0)

    @pl.when(i < nb_ref[0])
    def _():
        hdn = _dot(xs_ref[...].astype(BF16), w1_ref[0]) + b1_ref[0]
        glu = jnp.minimum(hdn[:, :D_FF], SWIGLU_LIMIT)
        lin = jnp.clip(hdn[:, D_FF:], -SWIGLU_LIMIT, SWIGLU_LIMIT)
        act = (lin + 1.0) * (glu * jax.nn.sigmoid(SWIGLU_ALPHA * glu))
        ys_ref[...] = _dot(act.astype(BF16), w2_ref[0]) + b2_ref[0]

    @pl.when(i >= nb_ref[0])
    def _():
        ys_ref[...] = jnp.zeros_like(ys_ref)


def _expert(block_e, nb_used, xs, w1, b1, w2, b2):
    n_blocks = xs.shape[0] // EXPERT_ROWS
    grid_spec = pltpu.PrefetchScalarGridSpec(
        num_scalar_prefetch=2,
        grid=(n_blocks,),
        in_specs=[pl.BlockSpec((EXPERT_ROWS, D_MODEL), lambda i, be, nb: (i, 0)),
                  pl.BlockSpec((1, D_MODEL, 2 * D_FF), lambda i, be, nb: (be[i], 0, 0)),
                  pl.BlockSpec((1, 1, 2 * D_FF), lambda i, be, nb: (be[i], 0, 0)),
                  pl.BlockSpec((1, D_FF, D_MODEL), lambda i, be, nb: (be[i], 0, 0)),
                  pl.BlockSpec((1, 1, D_MODEL), lambda i, be, nb: (be[i], 0, 0))],
        out_specs=pl.BlockSpec((EXPERT_ROWS, D_MODEL), lambda i, be, nb: (i, 0)),
    )
    return pl.pallas_call(
        _expert_kernel,
        grid_spec=grid_spec,
        out_shape=jax.ShapeDtypeStruct(xs.shape, F32),
        compiler_params=pltpu.CompilerParams(
            dimension_semantics=("arbitrary",), vmem_limit_bytes=VMEM_LIMIT),
        name="expert",
    )(block_e, nb_used, xs, w1, b1, w2, b2)


def _combine_kernel(dest_ref, h_ref, rw_ref, mod_ref, ys_ref, o_ref, ybuf, sem, *, tt):
    def copies(t):
        return [pltpu.make_async_copy(ys_ref.at[pl.ds(dest_ref[t * TOP_K + k], 1)],
                                      ybuf.at[k, pl.ds(t, 1)], sem) for k in range(TOP_K)]

    def start(t, c):
        for cp in copies(t):
            cp.start()
        return c

    def wait(t, c):
        for cp in copies(t):
            cp.wait()
        return c

    lax.fori_loop(0, tt, start, 0)
    lax.fori_loop(0, tt, wait, 0)
    rw = rw_ref[...]
    acc = ybuf[0] * rw[:, 0:1]
    for k in range(1, TOP_K):
        acc = acc + ybuf[k] * rw[:, k:k + 1]
    o_ref[...] = h_ref[...] + mod_ref[0][5:6] * acc


def _combine(dest, h, rw, mod, ys, seq, tt=256):
    t = h.shape[0]
    per_seq = seq // tt
    return pl.pallas_call(
        functools.partial(_combine_kernel, tt=tt),
        grid=(t // tt,),
        in_specs=[pl.BlockSpec((tt * TOP_K,), lambda i: (i,), memory_space=pltpu.SMEM),
                  pl.BlockSpec((tt, D_MODEL), lambda i: (i, 0)),
                  pl.BlockSpec((tt, LANES), lambda i: (i, 0)),
                  pl.BlockSpec((1, 6, D_MODEL), lambda i: (i // per_seq, 0, 0)),
                  pl.BlockSpec(memory_space=pl.ANY)],
        out_specs=pl.BlockSpec((tt, D_MODEL), lambda i: (i, 0)),
        out_shape=jax.ShapeDtypeStruct((t, D_MODEL), F32),
        scratch_shapes=[pltpu.VMEM((TOP_K, tt, D_MODEL), F32), pltpu.SemaphoreType.DMA],
        compiler_params=pltpu.CompilerParams(
            dimension_semantics=("arbitrary",), vmem_limit_bytes=VMEM_LIMIT),
        name="combine",
    )(dest, h, rw, mod, ys)


def _rope_tables(seq):
    half = ROPE_DIM // 2
    inv_freq = jnp.power(jnp.float32(ROPE_THETA), -jnp.arange(half, dtype=F32) * 2.0 / ROPE_DIM)
    ang = jnp.arange(seq, dtype=F32)[:, None] * inv_freq[None, :]
    cos, sin = jnp.cos(ang), jnp.sin(ang)
    ones = jnp.ones((seq, HEAD_DIM - ROPE_DIM), F32)
    zeros = jnp.zeros_like(ones)
    zh = jnp.zeros((seq, half), F32)
    cos_t = jnp.concatenate([cos, cos, ones], axis=1)
    sin_prev = jnp.concatenate([zh, sin, zeros], axis=1)
    sin_next = jnp.concatenate([-sin, zh, zeros], axis=1)
    return jnp.stack([jnp.tile(t, (1, LANES // HEAD_DIM)) for t in (cos_t, sin_prev, sin_next)])


def _dft_tables(seq):
    k = jnp.arange(seq, dtype=jnp.int32)
    ang = ((k[:, None] * k[None, :]) % seq).astype(F32) * (2.0 * jnp.pi / seq)
    scale = seq ** -0.5
    cs = jnp.cos(ang) * scale
    ns = -jnp.sin(ang) * scale
    c = jnp.arange(FOURIER_GROUP_DIM, dtype=jnp.int32)
    angc = ((c[:, None] * c[None, :]) % FOURIER_GROUP_DIM).astype(F32) * (2.0 * jnp.pi / FOURIER_GROUP_DIM)
    eye = jnp.eye(ATTN_WIDTH // FOURIER_GROUP_DIM, dtype=F32)
    cscale = FOURIER_GROUP_DIM ** -0.5
    bd = jnp.concatenate([jnp.kron(eye, jnp.cos(angc) * cscale),
                          jnp.kron(eye, jnp.sin(angc) * cscale)], axis=1)
    return bd.astype(BF16), cs.astype(BF16), ns.astype(BF16)


def _trunk(x, c, p):
    b, s, _ = x.shape
    t = b * s
    mod = _ada(c, p["w_ada"], p["b_ada"]).reshape(b, 6, D_MODEL)
    qkvf = _inproj(x, mod, p["g1"], p["w_qkvf"], p["gains"], p["rope"], p["gmat"])
    os_, lses = zip(*[_attn(qkvf, g) for g in range(N_GROUPS)])
    fm = _fourier(qkvf, p["bd"], p["cs"], p["ns"])
    h, u2, ridx, rw, cnt = _merge(x, mod, os_, lses, fm, p["g1"], p["g2"], p["wg"], p["wap"], p["wfp"],
                                  p["wout"], p["wrh"], p["wrl"], p["br"], p["expand"], p["ltri"])

    rows = EXPERT_ROWS
    counts = cnt[0, :N_EXPERTS].astype(jnp.int32)
    padded = (counts + rows - 1) // rows * rows
    pend = jnp.cumsum(padded)
    pstart = pend - padded
    idx = ridx[..., :TOP_K].reshape(t, TOP_K)
    rank = ridx[..., TOP_K:2 * TOP_K].reshape(t, TOP_K)
    start_of = jnp.sum(jnp.where(idx[..., None] == jnp.arange(N_EXPERTS), pstart, 0), axis=-1)
    dest = (start_of + rank).reshape(-1)
    n_blocks = -(-(t * TOP_K + N_EXPERTS * (rows - 1)) // rows)
    block_e = jnp.minimum(jnp.searchsorted(pend, jnp.arange(n_blocks) * rows, side="right"),
                          N_EXPERTS - 1).astype(jnp.int32)
    nb_used = (pend[-1:] // rows).astype(jnp.int32)

    xs = _scatter(dest, u2.reshape(t, D_MODEL), jnp.zeros((n_blocks * rows, D_MODEL), F32))
    ys = _expert(block_e, nb_used, xs, p["w1"], p["b1"], p["w2"], p["b2"])
    out = _combine(dest, h.reshape(t, D_MODEL), rw.reshape(t, LANES), mod, ys, s)
    return out.reshape(b, s, D_MODEL)


def kernel(x_prompt, x_sample, c_prompt, c_sample, w_ada, b_ada, norm1_g, norm2_g, w_in, q_gain, k_gain,
           w_attn_proj, w_fourier_proj, w_out, w_router, b_router, w1, b1, w2, b2):
    seq = x_prompt.shape[1]
    assert w_ada.shape[0] == 1, "single-layer trunk"
    w_in0 = w_in[0]
    gains = jnp.zeros((8, ATTN_WIDTH), F32)
    for g in range(N_GROUPS):
        gains = gains.at[2 * g].set(jnp.tile(q_gain[0, g], HEADS))
        gains = gains.at[2 * g + 1].set(jnp.tile(k_gain[0, g], HEADS))
    heads = jnp.arange(ATTN_WIDTH) // HEAD_DIM
    gmat = ((heads[:, None] == heads[None, :]).astype(F32) / HEAD_DIM).astype(BF16)
    expand = (jnp.arange(LANES)[:, None] == heads[None, :]).astype(BF16)
    ts_merge = 256
    ltri = (jnp.arange(ts_merge)[:, None] > jnp.arange(ts_merge)[None, :]).astype(BF16)
    wr = jnp.zeros((D_MODEL, LANES), F32).at[:, :N_EXPERTS].set(w_router[0])
    wrh = wr.astype(BF16)
    wrl = (wr - wrh.astype(F32)).astype(BF16)
    br = jnp.zeros((1, LANES), F32).at[0, :N_EXPERTS].set(b_router[0])
    bd, cs, ns = _dft_tables(seq)
    p = dict(
        w_ada=w_ada[0], b_ada=b_ada[0], g1=norm1_g, g2=norm2_g,
        w_qkvf=w_in0[:, :QKVF_WIDTH].astype(BF16), wg=w_in0[:, QKVF_WIDTH:].astype(BF16),
        gains=gains, rope=_rope_tables(seq), gmat=gmat, bd=bd, cs=cs, ns=ns,
        wap=w_attn_proj[0].astype(BF16), wfp=w_fourier_proj[0].astype(BF16), wout=w_out[0].astype(BF16),
        wrh=wrh, wrl=wrl, br=br, expand=expand, ltri=ltri,
        w1=w1[0].astype(BF16), b1=b1[0].reshape(N_EXPERTS, 1, 2 * D_FF),
        w2=w2[0].astype(BF16), b2=b2[0].reshape(N_EXPERTS, 1, D_MODEL),
    )
    return _trunk(x_prompt, c_prompt, p), _trunk(x_sample, c_sample, p)
```

```python
import functools

import jax
import jax.numpy as jnp
from jax import lax
from jax.experimental import pallas as pl
from jax.experimental.pallas import tpu as pltpu

F32 = jnp.float32
BF16 = jnp.bfloat16

D_MODEL = 1024
HEAD_DIM = 64
HEADS = 8
ATTN_WIDTH = HEADS * HEAD_DIM
DILATIONS = (1, 4, 16)
BAND = 64
N_GROUPS = 3
QKVF_WIDTH = 5120
FOURIER_GROUP_DIM = 64
ROPE_DIM = 16
ROPE_THETA = 500000.0
N_EXPERTS = 32
TOP_K = 4
D_FF = 1024
SWIGLU_ALPHA = 1.702
SWIGLU_LIMIT = 7.0
NORM_EPS = 1e-6
NEG_INF = -1e30

LANES = 128
EXPERT_ROWS = 256
INPROJ_ROWS = 512
MERGE_ROWS = 256
VMEM_LIMIT = 56 * 1024 * 1024


def _dot(a, b):
    return jnp.dot(a, b, preferred_element_type=F32)


def _split_bf16(v):
    hi = v.astype(BF16)
    lo = (v - hi.astype(F32)).astype(BF16)
    return hi, lo


def _rms_mod(x, gain, scale, shift):
    ms = jnp.mean(x * x, axis=-1, keepdims=True)
    return x * lax.rsqrt(ms + NORM_EPS) * gain * (1.0 + scale) + shift


def _const_spec(shape):
    return pl.BlockSpec(shape, lambda *_: (0,) * len(shape), pipeline_mode=pl.Buffered(1))


def _ada_kernel(c_ref, w_ref, b_ref, o_ref):
    c = c_ref[...]
    a = c * jax.nn.sigmoid(c)
    a_hi, a_lo = _split_bf16(a)
    w_hi, w_lo = _split_bf16(w_ref[...])
    o_ref[...] = _dot(a_hi, w_hi) + _dot(a_lo, w_hi) + _dot(a_hi, w_lo) + b_ref[...]


def _ada(c, w_ada, b_ada):
    b = c.shape[0]
    n = w_ada.shape[1]
    tn = D_MODEL
    return pl.pallas_call(
        _ada_kernel,
        grid=(n // tn,),
        in_specs=[pl.BlockSpec((b, D_MODEL), lambda j: (0, 0)),
                  pl.BlockSpec((D_MODEL, tn), lambda j: (0, j)),
                  pl.BlockSpec((1, tn), lambda j: (0, j))],
        out_specs=pl.BlockSpec((b, tn), lambda j: (0, j)),
        out_shape=jax.ShapeDtypeStruct((b, n), F32),
        compiler_params=pltpu.CompilerParams(vmem_limit_bytes=VMEM_LIMIT),
        name="ada",
    )(c, w_ada, b_ada.reshape(1, n))


def _inproj_kernel(x_ref, mod_ref, g1_ref, w_ref, gain_ref, rope_ref, gmat_ref,
                   o0_ref, o1_ref, o2_ref, of_ref, u_scr):
    x = x_ref[0]
    ts = x.shape[0]
    mod = mod_ref[0]
    u = _rms_mod(x, g1_ref[...], mod[1:2], mod[0:1])
    ub = u.astype(BF16)
    for c in range(D_MODEL // LANES):
        u_scr[c] = u[:, c * LANES:(c + 1) * LANES]
    for g, o_ref in enumerate((o0_ref, o1_ref, o2_ref)):
        d = DILATIONS[g]
        n = ts // d
        if d == 1:
            ug = ub
        else:
            ug = jnp.concatenate(
                [jnp.concatenate([u_scr[c, pl.ds(r, n, stride=d), :] for r in range(d)], axis=0)
                 for c in range(D_MODEL // LANES)], axis=1).astype(BF16)
        cos_t, sin_prev, sin_next = rope_ref[g, 0], rope_ref[g, 1], rope_ref[g, 2]
        for j in range(3):
            c0 = (3 * g + j) * ATTN_WIDTH
            z = _dot(ug, w_ref[:, c0:c0 + ATTN_WIDTH])
            if j < 2:
                ms = _dot((z * z).astype(BF16), gmat_ref[...])
                gain = gain_ref[2 * g + j:2 * g + j + 1, :]
                if j == 0:
                    gain = gain * (HEAD_DIM ** -0.5)
                z = z * lax.rsqrt(ms + NORM_EPS) * gain
                blocks = []
                for blk in range(ATTN_WIDTH // LANES):
                    zb = z[:, blk * LANES:(blk + 1) * LANES]
                    blocks.append(zb * cos_t
                                  + pltpu.roll(zb, ROPE_DIM // 2, 1) * sin_prev
                                  + pltpu.roll(zb, LANES - ROPE_DIM // 2, 1) * sin_next)
                z = jnp.concatenate(blocks, axis=1)
            zb16 = z.astype(BF16)
            for r in range(d):
                o_ref[0, r, :, j * ATTN_WIDTH:(j + 1) * ATTN_WIDTH] = zb16[r * n:(r + 1) * n, :]
    of_ref[0] = _dot(ub, w_ref[:, 9 * ATTN_WIDTH:]).astype(BF16)


def _inproj(x, mod, g1, w_qkvf, gains, rope, gmat, ts=INPROJ_ROWS):
    b, s, _ = x.shape
    qkv_w = 3 * ATTN_WIDTH
    out_specs = [pl.BlockSpec((1, d, ts // d, qkv_w), lambda i, j: (i, 0, j, 0)) for d in DILATIONS]
    out_shape = [jax.ShapeDtypeStruct((b, d, s // d, qkv_w), BF16) for d in DILATIONS]
    return pl.pallas_call(
        _inproj_kernel,
        grid=(b, s // ts),
        in_specs=[pl.BlockSpec((1, ts, D_MODEL), lambda i, j: (i, j, 0)),
                  pl.BlockSpec((1, 6, D_MODEL), lambda i, j: (i, 0, 0)),
                  _const_spec((1, D_MODEL)),
                  _const_spec((D_MODEL, QKVF_WIDTH)),
                  _const_spec((8, ATTN_WIDTH)),
                  pl.BlockSpec((N_GROUPS, 3, ts, LANES), lambda i, j: (0, 0, j, 0)),
                  _const_spec((ATTN_WIDTH, ATTN_WIDTH))],
        out_specs=out_specs + [pl.BlockSpec((1, ts, ATTN_WIDTH), lambda i, j: (i, j, 0))],
        out_shape=out_shape + [jax.ShapeDtypeStruct((b, s, ATTN_WIDTH), BF16)],
        scratch_shapes=[pltpu.VMEM((D_MODEL // LANES, ts, LANES), F32)],
        compiler_params=pltpu.CompilerParams(
            dimension_semantics=("arbitrary", "arbitrary"), vmem_limit_bytes=VMEM_LIMIT),
        name="inproj",
    )(x, mod, g1, w_qkvf, gains, rope, gmat)


def _attn_kernel(qkv_ref, o_ref, lse_ref, o_scr, *, dilation, length):
    tq = LANES
    tk = min(2 * LANES, length)
    nq = length // tq
    lane = lax.broadcasted_iota(jnp.int32, (tq, LANES), 1)
    first_head = lane < HEAD_DIM
    rel = (lax.broadcasted_iota(jnp.int32, (tq, tk), 1)
           - lax.broadcasted_iota(jnp.int32, (tq, tk), 0))

    def q_block(it, carry):
        r = it // nq
        i = it % nq
        q0 = pl.multiple_of(i * tq, tq)
        ws = pl.multiple_of(jnp.clip(i * tq - BAND, 0, length - tk), BAND)
        valid = jnp.abs(rel + (ws - i * tq)) <= BAND
        if dilation == 1:
            out_rows = pl.ds(q0, tq)
        else:
            out_rows = pl.ds(q0 * dilation + r, tq, stride=dilation)
        lse_tile = jnp.zeros((tq, LANES), F32)
        for p in range(ATTN_WIDTH // LANES):
            q2 = qkv_ref[0, r, pl.ds(q0, tq), p * LANES:(p + 1) * LANES]
            k2 = qkv_ref[0, r, pl.ds(ws, tk), ATTN_WIDTH + p * LANES:ATTN_WIDTH + (p + 1) * LANES]
            v2 = qkv_ref[0, r, pl.ds(ws, tk), 2 * ATTN_WIDTH + p * LANES:2 * ATTN_WIDTH + (p + 1) * LANES]
            o_pair = None
            for hh in range(2):
                sel = first_head if hh == 0 else jnp.logical_not(first_head)
                qa = jnp.where(sel, q2, jnp.zeros_like(q2))
                s = lax.dot_general(qa, k2, (((1,), (1,)), ((), ())), preferred_element_type=F32)
                s = jnp.where(valid, s, NEG_INF)
                m = jnp.max(s, axis=-1, keepdims=True)
                pr = jnp.exp(s - m)
                den = jnp.sum(pr, axis=-1, keepdims=True)
                o = _dot(pr.astype(BF16), v2) / den
                lse_tile = jnp.where(lane == 2 * p + hh, m + jnp.log(den), lse_tile)
                o_pair = o if hh == 0 else jnp.where(first_head, o_pair, o)
            o_scr[p, out_rows, :] = o_pair
        lse_ref[0, out_rows, :] = lse_tile
        return carry

    lax.fori_loop(0, dilation * nq, q_block, 0)
    for p in range(ATTN_WIDTH // LANES):
        o_ref[0, :, p * LANES:(p + 1) * LANES] = o_scr[p].astype(BF16)


def _attn(qkv, g):
    b, d, length, width = qkv.shape
    s = d * length
    return pl.pallas_call(
        functools.partial(_attn_kernel, dilation=d, length=length),
        grid=(b,),
        in_specs=[pl.BlockSpec((1, d, length, width), lambda i: (i, 0, 0, 0))],
        out_specs=[pl.BlockSpec((1, s, ATTN_WIDTH), lambda i: (i, 0, 0)),
                   pl.BlockSpec((1, s, LANES), lambda i: (i, 0, 0))],
        out_shape=[jax.ShapeDtypeStruct((b, s, ATTN_WIDTH), BF16),
                   jax.ShapeDtypeStruct((b, s, LANES), F32)],
        scratch_shapes=[pltpu.VMEM((ATTN_WIDTH // LANES, s, LANES), F32)],
        compiler_params=pltpu.CompilerParams(
            dimension_semantics=("arbitrary",), vmem_limit_bytes=VMEM_LIMIT),
        name=f"attn{g}",
    )(qkv)


def _fourier_kernel(x_ref, bd_ref, cs_ref, ns_ref, o_ref, y_ref, *, seq, rows):
    for r in range(seq // rows):
        rs = slice(r * rows, (r + 1) * rows)
        y_ref[rs, :] = _dot(x_ref[0, rs, :], bd_ref[...]).astype(BF16)
    for r in range(seq // rows):
        rs = slice(r * rows, (r + 1) * rows)
        o = _dot(cs_ref[rs, :], y_ref[:, :ATTN_WIDTH]) + _dot(ns_ref[rs, :], y_ref[:, ATTN_WIDTH:])
        o_ref[0, rs, :] = o.astype(BF16)


def _fourier(qkvf, bd, cs, ns):
    b, s, _ = qkvf.shape
    return pl.pallas_call(
        functools.partial(_fourier_kernel, seq=s, rows=256),
        grid=(b,),
        in_specs=[pl.BlockSpec((1, s, ATTN_WIDTH), lambda i: (i, 0, 0)),
                  _const_spec((ATTN_WIDTH, 2 * ATTN_WIDTH)),
                  _const_spec((s, s)),
                  _const_spec((s, s))],
        out_specs=pl.BlockSpec((1, s, ATTN_WIDTH), lambda i: (i, 0, 0)),
        out_shape=jax.ShapeDtypeStruct((b, s, ATTN_WIDTH), BF16),
        scratch_shapes=[pltpu.VMEM((s, 2 * ATTN_WIDTH), BF16)],
        compiler_params=pltpu.CompilerParams(
            dimension_semantics=("arbitrary",), vmem_limit_bytes=VMEM_LIMIT),
        name="fourier",
    )(qkvf, bd, cs, ns)


def _merge_kernel(x_ref, mod_ref, o0_ref, o1_ref, o2_ref, l0_ref, l1_ref, l2_ref, fm_ref,
                  g1_ref, g2_ref, wg_ref, wap_ref, wfp_ref, wout_ref, wrh_ref, wrl_ref, br_ref,
                  expand_ref, ltri_ref,
                  h_ref, u2_ref, ridx_ref, rw_ref, cnt_ref, cnt_scr):
    @pl.when((pl.program_id(0) == 0) & (pl.program_id(1) == 0))
    def _():
        cnt_scr[...] = jnp.zeros_like(cnt_scr)

    x = x_ref[0]
    mod = mod_ref[0]
    shift1, scale1, gate1 = mod[0:1], mod[1:2], mod[2:3]
    shift2, scale2 = mod[3:4], mod[4:5]
    u = _rms_mod(x, g1_ref[...], scale1, shift1).astype(BF16)
    gates = _dot(u, wg_ref[...])

    lses = (l0_ref[0], l1_ref[0], l2_ref[0])
    mx = jnp.maximum(jnp.maximum(lses[0], lses[1]), lses[2])
    es = [jnp.exp(l - mx) for l in lses]
    den = es[0] + es[1] + es[2]
    attn = None
    for e, o_ref in zip(es, (o0_ref, o1_ref, o2_ref)):
        w_hi, w_lo = _split_bf16(e / den)
        wf = _dot(w_hi, expand_ref[...]) + _dot(w_lo, expand_ref[...])
        term = wf * o_ref[0].astype(F32)
        attn = term if attn is None else attn + term

    a = _dot(attn.astype(BF16), wap_ref[...])
    f = _dot(fm_ref[0], wfp_ref[...])
    merged = jax.nn.sigmoid(gates[:, :D_MODEL]) * a + jax.nn.sigmoid(gates[:, D_MODEL:]) * f
    h = x + gate1 * _dot(merged.astype(BF16), wout_ref[...])
    h_ref[0] = h

    u2 = _rms_mod(h, g2_ref[...], scale2, shift2)
    u2_ref[0] = u2
    u_hi, u_lo = _split_bf16(u2)
    logits = (_dot(u_hi, wrh_ref[...]) + _dot(u_lo, wrh_ref[...]) + _dot(u_hi, wrl_ref[...])
              + br_ref[...])

    ts = logits.shape[0]
    lane = lax.broadcasted_iota(jnp.int32, (ts, LANES), 1)
    lg = jnp.where(lane < N_EXPERTS, logits, -jnp.inf)
    vals, ids = [], []
    onehot = jnp.zeros((ts, LANES), F32)
    for _ in range(TOP_K):
        m = jnp.max(lg, axis=-1, keepdims=True)
        idx = jnp.min(jnp.where(lg == m, lane, LANES), axis=-1, keepdims=True)
        hit = lane == idx
        vals.append(m)
        ids.append(idx)
        onehot = jnp.where(hit, 1.0, onehot)
        lg = jnp.where(hit, -jnp.inf, lg)
    exps = [jnp.exp(v - vals[0]) for v in vals]
    esum = exps[0] + exps[1] + exps[2] + exps[3]

    cum = _dot(ltri_ref[...], onehot.astype(BF16)) + cnt_scr[0:1, :]
    ridx = jnp.zeros((ts, LANES), F32)
    rw = jnp.zeros((ts, LANES), F32)
    for k in range(TOP_K):
        rank = jnp.sum(jnp.where(lane == ids[k], cum, 0.0), axis=-1, keepdims=True)
        ridx = jnp.where(lane == k, ids[k].astype(F32), ridx)
        ridx = jnp.where(lane == TOP_K + k, rank, ridx)
        rw = jnp.where(lane == k, exps[k] / esum, rw)
    ridx_ref[0] = ridx.astype(jnp.int32)
    rw_ref[0] = rw
    cnt_scr[...] = cnt_scr[...] + jnp.sum(onehot, axis=0, keepdims=True)
    cnt_ref[...] = cnt_scr[...]


def _merge(x, mod, os_, lses, fm, g1, g2, wg, wap, wfp, wout, wrh, wrl, br, expand, ltri, ts=MERGE_ROWS):
    b, s, _ = x.shape
    tile = lambda w: pl.BlockSpec((1, ts, w), lambda i, j: (i, j, 0))
    in_specs = ([tile(D_MODEL), pl.BlockSpec((1, 6, D_MODEL), lambda i, j: (i, 0, 0))]
                + [tile(ATTN_WIDTH)] * 3 + [tile(LANES)] * 3 + [tile(ATTN_WIDTH)]
                + [_const_spec(a.shape) for a in (g1, g2, wg, wap, wfp, wout, wrh, wrl, br, expand, ltri)])
    return pl.pallas_call(
        _merge_kernel,
        grid=(b, s // ts),
        in_specs=in_specs,
        out_specs=[tile(D_MODEL), tile(D_MODEL), tile(LANES), tile(LANES),
                   pl.BlockSpec((8, LANES), lambda i, j: (0, 0))],
        out_shape=[jax.ShapeDtypeStruct((b, s, D_MODEL), F32),
                   jax.ShapeDtypeStruct((b, s, D_MODEL), F32),
                   jax.ShapeDtypeStruct((b, s, LANES), jnp.int32),
                   jax.ShapeDtypeStruct((b, s, LANES), F32),
                   jax.ShapeDtypeStruct((8, LANES), F32)],
        scratch_shapes=[pltpu.VMEM((8, LANES), F32)],
        compiler_params=pltpu.CompilerParams(
            dimension_semantics=("arbitrary", "arbitrary"), vmem_limit_bytes=VMEM_LIMIT),
        name="merge",
    )(x, mod, *os_, *lses, fm, g1, g2, wg, wap, wfp, wout, wrh, wrl, br, expand, ltri)


def _row_copy(src_ref, src_row, dst_ref, dst_row, sem):
    return pltpu.make_async_copy(src_ref.at[pl.ds(src_row, 1)], dst_ref.at[pl.ds(dst_row, 1)], sem)


def _scatter_kernel(dest_ref, u2_ref, xs_in_ref, xs_ref, sem, *, tt):
    del xs_in_ref

    def copies(t):
        return [_row_copy(u2_ref, t, xs_ref, dest_ref[t * TOP_K + k], sem) for k in range(TOP_K)]

    def start(t, c):
        for cp in copies(t):
            cp.start()
        return c

    def wait(t, c):
        for cp in copies(t):
            cp.wait()
        return c

    lax.fori_loop(0, tt, start, 0)
    lax.fori_loop(0, tt, wait, 0)


def _scatter(dest, u2, xs_zero, tt=256):
    t = u2.shape[0]
    return pl.pallas_call(
        functools.partial(_scatter_kernel, tt=tt),
        grid=(t // tt,),
        in_specs=[pl.BlockSpec((tt * TOP_K,), lambda i: (i,), memory_space=pltpu.SMEM),
                  pl.BlockSpec((tt, D_MODEL), lambda i: (i, 0)),
                  pl.BlockSpec(memory_space=pl.ANY)],
        out_specs=pl.BlockSpec(memory_space=pl.ANY),
        out_shape=jax.ShapeDtypeStruct(xs_zero.shape, xs_zero.dtype),
        scratch_shapes=[pltpu.SemaphoreType.DMA],
        input_output_aliases={2: 0},
        compiler_params=pltpu.CompilerParams(
            dimension_semantics=("arbitrary",), has_side_effects=True, vmem_limit_bytes=VMEM_LIMIT),
        name="scatter",
    )(dest, u2, xs_zero)


def _expert_kernel(be_ref, nb_ref, xs_ref, w1_ref, b1_ref, w2_ref, b2_ref, ys_ref):
    del be_ref
    i = pl.program_id(0)

    @pl.when(i < nb_ref[0])
    def _():
        hdn = _dot(xs_ref[...].astype(BF16), w1_ref[0]) + b1_ref[0]
        glu = jnp.minimum(hdn[:, :D_FF], SWIGLU_LIMIT)
        lin = jnp.clip(hdn[:, D_FF:], -SWIGLU_LIMIT, SWIGLU_LIMIT)
        act = (lin + 1.0) * (glu * jax.nn.sigmoid(SWIGLU_ALPHA * glu))
        ys_ref[...] = _dot(act.astype(BF16), w2_ref[0]) + b2_ref[0]

    @pl.when(i >= nb_ref[0])
    def _():
        ys_ref[...] = jnp.zeros_like(ys_ref)


def _expert(block_e, nb_used, xs, w1, b1, w2, b2):
    n_blocks = xs.shape[0] // EXPERT_ROWS
    grid_spec = pltpu.PrefetchScalarGridSpec(
        num_scalar_prefetch=2,
        grid=(n_blocks,),
        in_specs=[pl.BlockSpec((EXPERT_ROWS, D_MODEL), lambda i, be, nb: (i, 0)),
                  pl.BlockSpec((1, D_MODEL, 2 * D_FF), lambda i, be, nb: (be[i], 0, 0)),
                  pl.BlockSpec((1, 1, 2 * D_FF), lambda i, be, nb: (be[i], 0, 0)),
                  pl.BlockSpec((1, D_FF, D_MODEL), lambda i, be, nb: (be[i], 0, 0)),
                  pl.BlockSpec((1, 1, D_MODEL), lambda i, be, nb: (be[i], 0, 0))],
        out_specs=pl.BlockSpec((EXPERT_ROWS, D_MODEL), lambda i, be, nb: (i, 0)),
    )
    return pl.pallas_call(
        _expert_kernel,
        grid_spec=grid_spec,
        out_shape=jax.ShapeDtypeStruct(xs.shape, F32),
        compiler_params=pltpu.CompilerParams(
            dimension_semantics=("arbitrary",), vmem_limit_bytes=VMEM_LIMIT),
        name="expert",
    )(block_e, nb_used, xs, w1, b1, w2, b2)


def _combine_kernel(dest_ref, h_ref, rw_ref, mod_ref, ys_ref, o_ref, ybuf, sem, *, tt):
    def copies(t):
        return [pltpu.make_async_copy(ys_ref.at[pl.ds(dest_ref[t * TOP_K + k], 1)],
                                      ybuf.at[k, pl.ds(t, 1)], sem) for k in range(TOP_K)]

    def start(t, c):
        for cp in copies(t):
            cp.start()
        return c

    def wait(t, c):
        for cp in copies(t):
            cp.wait()
        return c

    lax.fori_loop(0, tt, start, 0)
    lax.fori_loop(0, tt, wait, 0)
    rw = rw_ref[...]
    acc = ybuf[0] * rw[:, 0:1]
    for k in range(1, TOP_K):
        acc = acc + ybuf[k] * rw[:, k:k + 1]
    o_ref[...] = h_ref[...] + mod_ref[0][5:6] * acc


def _combine(dest, h, rw, mod, ys, seq, tt=256):
    t = h.shape[0]
    per_seq = seq // tt
    return pl.pallas_call(
        functools.partial(_combine_kernel, tt=tt),
        grid=(t // tt,),
        in_specs=[pl.BlockSpec((tt * TOP_K,), lambda i: (i,), memory_space=pltpu.SMEM),
                  pl.BlockSpec((tt, D_MODEL), lambda i: (i, 0)),
                  pl.BlockSpec((tt, LANES), lambda i: (i, 0)),
                  pl.BlockSpec((1, 6, D_MODEL), lambda i: (i // per_seq, 0, 0)),
                  pl.BlockSpec(memory_space=pl.ANY)],
        out_specs=pl.BlockSpec((tt, D_MODEL), lambda i: (i, 0)),
        out_shape=jax.ShapeDtypeStruct((t, D_MODEL), F32),
        scratch_shapes=[pltpu.VMEM((TOP_K, tt, D_MODEL), F32), pltpu.SemaphoreType.DMA],
        compiler_params=pltpu.CompilerParams(
            dimension_semantics=("arbitrary",), vmem_limit_bytes=VMEM_LIMIT),
        name="combine",
    )(dest, h, rw, mod, ys)


def _rope_tables(seq, ts):
    half = ROPE_DIM // 2
    inv_freq = jnp.power(jnp.float32(ROPE_THETA), -jnp.arange(half, dtype=F32) * 2.0 / ROPE_DIM)
    ang = jnp.arange(seq, dtype=F32)[:, None] * inv_freq[None, :]
    cos, sin = jnp.cos(ang), jnp.sin(ang)
    ones = jnp.ones((seq, HEAD_DIM - ROPE_DIM), F32)
    zeros = jnp.zeros_like(ones)
    zh = jnp.zeros((seq, half), F32)
    cos_t = jnp.concatenate([cos, cos, ones], axis=1)
    sin_prev = jnp.concatenate([zh, sin, zeros], axis=1)
    sin_next = jnp.concatenate([-sin, zh, zeros], axis=1)
    tables = jnp.stack([jnp.tile(t, (1, LANES // HEAD_DIM)) for t in (cos_t, sin_prev, sin_next)])
    per_group = []
    for d in DILATIONS:
        pos = jnp.arange(seq).reshape(seq // ts, ts // d, d).transpose(0, 2, 1).reshape(seq)
        per_group.append(tables[:, pos, :])
    return jnp.stack(per_group)


def _dft_tables(seq):
    k = jnp.arange(seq, dtype=jnp.int32)
    ang = ((k[:, None] * k[None, :]) % seq).astype(F32) * (2.0 * jnp.pi / seq)
    scale = seq ** -0.5
    cs = jnp.cos(ang) * scale
    ns = -jnp.sin(ang) * scale
    c = jnp.arange(FOURIER_GROUP_DIM, dtype=jnp.int32)
    angc = ((c[:, None] * c[None, :]) % FOURIER_GROUP_DIM).astype(F32) * (2.0 * jnp.pi / FOURIER_GROUP_DIM)
    eye = jnp.eye(ATTN_WIDTH // FOURIER_GROUP_DIM, dtype=F32)
    cscale = FOURIER_GROUP_DIM ** -0.5
    bd = jnp.concatenate([jnp.kron(eye, jnp.cos(angc) * cscale),
                          jnp.kron(eye, jnp.sin(angc) * cscale)], axis=1)
    return bd.astype(BF16), cs.astype(BF16), ns.astype(BF16)


def _trunk(x, c, p):
    b, s, _ = x.shape
    t = b * s
    mod = _ada(c, p["w_ada"], p["b_ada"]).reshape(b, 6, D_MODEL)
    *qkvs, fz = _inproj(x, mod, p["g1"], p["w_qkvf"], p["gains"], p["rope"], p["gmat"])
    os_, lses = zip(*[_attn(qkv, g) for g, qkv in enumerate(qkvs)])
    fm = _fourier(fz, p["bd"], p["cs"], p["ns"])
    h, u2, ridx, rw, cnt = _merge(x, mod, os_, lses, fm, p["g1"], p["g2"], p["wg"], p["wap"], p["wfp"],
                                  p["wout"], p["wrh"], p["wrl"], p["br"], p["expand"], p["ltri"])

    rows = EXPERT_ROWS
    counts = cnt[0, :N_EXPERTS].astype(jnp.int32)
    padded = (counts + rows - 1) // rows * rows
    pend = jnp.cumsum(padded)
    pstart = pend - padded
    idx = ridx[..., :TOP_K].reshape(t, TOP_K)
    rank = ridx[..., TOP_K:2 * TOP_K].reshape(t, TOP_K)
    start_of = jnp.sum(jnp.where(idx[..., None] == jnp.arange(N_EXPERTS), pstart, 0), axis=-1)
    dest = (start_of + rank).reshape(-1)
    n_blocks = -(-(t * TOP_K + N_EXPERTS * (rows - 1)) // rows)
    block_start = jnp.arange(n_blocks, dtype=jnp.int32) * rows
    block_e = jnp.minimum(jnp.sum((pend[None, :] <= block_start[:, None]).astype(jnp.int32), axis=-1),
                          N_EXPERTS - 1)
    nb_used = (pend[-1:] // rows).astype(jnp.int32)

    xs = _scatter(dest, u2.reshape(t, D_MODEL), jnp.zeros((n_blocks * rows, D_MODEL), F32))
    ys = _expert(block_e, nb_used, xs, p["w1"], p["b1"], p["w2"], p["b2"])
    out = _combine(dest, h.reshape(t, D_MODEL), rw.reshape(t, LANES), mod, ys, s)
    return out.reshape(b, s, D_MODEL)


def kernel(x_prompt, x_sample, c_prompt, c_sample, w_ada, b_ada, norm1_g, norm2_g, w_in, q_gain, k_gain,
           w_attn_proj, w_fourier_proj, w_out, w_router, b_router, w1, b1, w2, b2):
    seq = x_prompt.shape[1]
    assert w_ada.shape[0] == 1, "single-layer trunk"
    w_in0 = w_in[0]
    gains = jnp.zeros((8, ATTN_WIDTH), F32)
    for g in range(N_GROUPS):
        gains = gains.at[2 * g].set(jnp.tile(q_gain[0, g], HEADS))
        gains = gains.at[2 * g + 1].set(jnp.tile(k_gain[0, g], HEADS))
    heads = jnp.arange(ATTN_WIDTH) // HEAD_DIM
    gmat = ((heads[:, None] == heads[None, :]).astype(F32) / HEAD_DIM).astype(BF16)
    expand = (jnp.arange(LANES)[:, None] == heads[None, :]).astype(BF16)
    ltri = (jnp.arange(MERGE_ROWS)[:, None] > jnp.arange(MERGE_ROWS)[None, :]).astype(BF16)
    wr = jnp.zeros((D_MODEL, LANES), F32).at[:, :N_EXPERTS].set(w_router[0])
    wrh = wr.astype(BF16)
    wrl = (wr - wrh.astype(F32)).astype(BF16)
    br = jnp.zeros((1, LANES), F32).at[0, :N_EXPERTS].set(b_router[0])
    bd, cs, ns = _dft_tables(seq)
    p = dict(
        w_ada=w_ada[0], b_ada=b_ada[0], g1=norm1_g, g2=norm2_g,
        w_qkvf=w_in0[:, :QKVF_WIDTH].astype(BF16), wg=w_in0[:, QKVF_WIDTH:].astype(BF16),
        gains=gains, rope=_rope_tables(seq, INPROJ_ROWS), gmat=gmat, bd=bd, cs=cs, ns=ns,
        wap=w_attn_proj[0].astype(BF16), wfp=w_fourier_proj[0].astype(BF16), wout=w_out[0].astype(BF16),
        wrh=wrh, wrl=wrl, br=br, expand=expand, ltri=ltri,
        w1=w1[0].astype(BF16), b1=b1[0].reshape(N_EXPERTS, 1, 2 * D_FF),
        w2=w2[0].astype(BF16), b2=b2[0].reshape(N_EXPERTS, 1, D_MODEL),
    )
    return _trunk(x_prompt, c_prompt, p), _trunk(x_sample, c_sample, p)
```

```python
import functools

import jax
import jax.numpy as jnp
from jax import lax
from jax.experimental import pallas as pl
from jax.experimental.pallas import tpu as pltpu

F32 = jnp.float32
BF16 = jnp.bfloat16

D_MODEL = 1024
HEAD_DIM = 64
HEADS = 8
ATTN_WIDTH = HEADS * HEAD_DIM
DILATIONS = (1, 4, 16)
BAND = 64
N_GROUPS = 3
QKVF_WIDTH = 5120
FOURIER_GROUP_DIM = 64
ROPE_DIM = 16
ROPE_THETA = 500000.0
N_EXPERTS = 32
TOP_K = 4
D_FF = 1024
SWIGLU_ALPHA = 1.702
SWIGLU_LIMIT = 7.0
NORM_EPS = 1e-6
NEG_INF = -1e30
LN2 = 0.6931471805599453
LOG2E = 1.4426950408889634

LANES = 128
EXPERT_ROWS = 256
INPROJ_ROWS = 512
MERGE_ROWS = 256
VMEM_LIMIT = 56 * 1024 * 1024


def _dot(a, b):
    return jnp.dot(a, b, preferred_element_type=F32)


def _split_bf16(v):
    hi = v.astype(BF16)
    lo = (v - hi.astype(F32)).astype(BF16)
    return hi, lo


def _rms_mod(x, gain, scale, shift):
    ms = jnp.mean(x * x, axis=-1, keepdims=True)
    return x * lax.rsqrt(ms + NORM_EPS) * gain * (1.0 + scale) + shift


def _const_spec(shape):
    return pl.BlockSpec(shape, lambda *_: (0,) * len(shape), pipeline_mode=pl.Buffered(1))


def _ada_kernel(c_ref, w_ref, b_ref, o_ref):
    c = c_ref[...]
    a = c * jax.nn.sigmoid(c)
    a_hi, a_lo = _split_bf16(a)
    w_hi, w_lo = _split_bf16(w_ref[...])
    o_ref[...] = _dot(a_hi, w_hi) + _dot(a_lo, w_hi) + _dot(a_hi, w_lo) + b_ref[...]


def _ada(c, w_ada, b_ada):
    b = c.shape[0]
    n = w_ada.shape[1]
    tn = D_MODEL
    return pl.pallas_call(
        _ada_kernel,
        grid=(n // tn,),
        in_specs=[pl.BlockSpec((b, D_MODEL), lambda j: (0, 0)),
                  pl.BlockSpec((D_MODEL, tn), lambda j: (0, j)),
                  pl.BlockSpec((1, tn), lambda j: (0, j))],
        out_specs=pl.BlockSpec((b, tn), lambda j: (0, j)),
        out_shape=jax.ShapeDtypeStruct((b, n), F32),
        compiler_params=pltpu.CompilerParams(vmem_limit_bytes=VMEM_LIMIT),
        name="ada",
    )(c, w_ada, b_ada.reshape(1, n))


def _inproj_kernel(x_ref, mod_ref, g1_ref, w_ref, gain_ref, rope_ref, gmat_ref,
                   o0_ref, o1_ref, o2_ref, of_ref, u_scr):
    x = x_ref[0]
    ts = x.shape[0]
    mod = mod_ref[0]
    u = _rms_mod(x, g1_ref[...], mod[1:2], mod[0:1])
    ub = u.astype(BF16)
    for c in range(D_MODEL // LANES):
        u_scr[c] = u[:, c * LANES:(c + 1) * LANES]
    for g, o_ref in enumerate((o0_ref, o1_ref, o2_ref)):
        d = DILATIONS[g]
        n = ts // d
        if d == 1:
            ug = ub
        else:
            ug = jnp.concatenate(
                [jnp.concatenate([u_scr[c, pl.ds(r, n, stride=d), :] for r in range(d)], axis=0)
                 for c in range(D_MODEL // LANES)], axis=1).astype(BF16)
        cos_t, sin_prev, sin_next = rope_ref[g, 0], rope_ref[g, 1], rope_ref[g, 2]
        for j in range(3):
            c0 = (3 * g + j) * ATTN_WIDTH
            z = _dot(ug, w_ref[:, c0:c0 + ATTN_WIDTH])
            if j < 2:
                ms = _dot((z * z).astype(BF16), gmat_ref[...])
                gain = gain_ref[2 * g + j:2 * g + j + 1, :]
                if j == 0:
                    gain = gain * (HEAD_DIM ** -0.5 * LOG2E)
                z = z * lax.rsqrt(ms + NORM_EPS) * gain
                blocks = []
                for blk in range(ATTN_WIDTH // LANES):
                    zb = z[:, blk * LANES:(blk + 1) * LANES]
                    blocks.append(zb * cos_t
                                  + pltpu.roll(zb, ROPE_DIM // 2, 1) * sin_prev
                                  + pltpu.roll(zb, LANES - ROPE_DIM // 2, 1) * sin_next)
                z = jnp.concatenate(blocks, axis=1)
            zb16 = z.astype(BF16)
            for r in range(d):
                o_ref[0, r, :, j * ATTN_WIDTH:(j + 1) * ATTN_WIDTH] = zb16[r * n:(r + 1) * n, :]
    of_ref[0] = _dot(ub, w_ref[:, 9 * ATTN_WIDTH:]).astype(BF16)


def _inproj(x, mod, g1, w_qkvf, gains, rope, gmat, ts=INPROJ_ROWS):
    b, s, _ = x.shape
    qkv_w = 3 * ATTN_WIDTH
    out_specs = [pl.BlockSpec((1, d, ts // d, qkv_w), lambda i, j: (i, 0, j, 0)) for d in DILATIONS]
    out_shape = [jax.ShapeDtypeStruct((b, d, s // d, qkv_w), BF16) for d in DILATIONS]
    return pl.pallas_call(
        _inproj_kernel,
        grid=(b, s // ts),
        in_specs=[pl.BlockSpec((1, ts, D_MODEL), lambda i, j: (i, j, 0)),
                  pl.BlockSpec((1, 6, D_MODEL), lambda i, j: (i, 0, 0)),
                  _const_spec((1, D_MODEL)),
                  _const_spec((D_MODEL, QKVF_WIDTH)),
                  _const_spec((8, ATTN_WIDTH)),
                  pl.BlockSpec((N_GROUPS, 3, ts, LANES), lambda i, j: (0, 0, j, 0)),
                  _const_spec((ATTN_WIDTH, ATTN_WIDTH))],
        out_specs=out_specs + [pl.BlockSpec((1, ts, ATTN_WIDTH), lambda i, j: (i, j, 0))],
        out_shape=out_shape + [jax.ShapeDtypeStruct((b, s, ATTN_WIDTH), BF16)],
        scratch_shapes=[pltpu.VMEM((D_MODEL // LANES, ts, LANES), F32)],
        compiler_params=pltpu.CompilerParams(
            dimension_semantics=("arbitrary", "arbitrary"), vmem_limit_bytes=VMEM_LIMIT),
        name="inproj",
    )(x, mod, g1, w_qkvf, gains, rope, gmat)


def _attn_kernel(qkv_ref, o_ref, lse_ref, s_scr, p_scr, *, length):
    tq = LANES
    tk = min(2 * LANES, length)
    nq = length // tq
    n_pairs = ATTN_WIDTH // LANES
    lane = lax.broadcasted_iota(jnp.int32, (tq, LANES), 1)
    first_head = lane < HEAD_DIM
    first_head_k = lax.broadcasted_iota(jnp.int32, (tk, LANES), 1) < HEAD_DIM
    rel = (lax.broadcasted_iota(jnp.int32, (tq, tk), 1)
           - lax.broadcasted_iota(jnp.int32, (tq, tk), 0))

    def q_block(it, carry):
        r = it // nq
        i = it % nq
        q0 = pl.multiple_of(i * tq, tq)
        ws = pl.multiple_of(jnp.clip(i * tq - BAND, 0, length - tk), BAND)
        valid = jnp.abs(rel + (ws - i * tq)) <= BAND

        for p in range(n_pairs):
            q2 = qkv_ref[0, r, pl.ds(q0, tq), p * LANES:(p + 1) * LANES]
            k2 = qkv_ref[0, r, pl.ds(ws, tk), ATTN_WIDTH + p * LANES:ATTN_WIDTH + (p + 1) * LANES]
            for hh in range(2):
                sel = first_head if hh == 0 else jnp.logical_not(first_head)
                qa = jnp.where(sel, q2, jnp.zeros_like(q2))
                s = lax.dot_general(qa, k2, (((1,), (1,)), ((), ())), preferred_element_type=F32)
                s_scr[2 * p + hh] = jnp.where(valid, s, NEG_INF)

        m_tile = jnp.zeros((tq, LANES), F32)
        for h in range(HEADS):
            s = s_scr[h]
            m = jnp.max(s, axis=-1, keepdims=True)
            p_scr[h] = jnp.exp2(s - m).astype(BF16)
            m_tile = jnp.where(lane == h, m, m_tile)

        den_tile = jnp.ones((tq, LANES), F32)
        for p in range(n_pairs):
            cs = slice(2 * ATTN_WIDTH + p * LANES, 2 * ATTN_WIDTH + (p + 1) * LANES)
            v2 = qkv_ref[0, r, pl.ds(ws, tk), cs]
            one = jnp.ones_like(v2)
            o_a = _dot(p_scr[2 * p], jnp.where(first_head_k, v2, one))
            o_b = _dot(p_scr[2 * p + 1], jnp.where(first_head_k, one, v2))
            den_ba = jnp.where(first_head, o_b, o_a)
            den_ab = pltpu.roll(den_ba, HEAD_DIM, 1)
            o_ref[0, r, pl.ds(q0, tq), p * LANES:(p + 1) * LANES] = (
                jnp.where(first_head, o_a, o_b) / den_ab).astype(BF16)
            den_tile = jnp.where(lane == 2 * p, den_ab, den_tile)
            den_tile = jnp.where(lane == 2 * p + 1, den_ba, den_tile)
        lse_ref[0, r, pl.ds(q0, tq), :] = (m_tile + jnp.log2(den_tile)) * LN2
        return carry

    lax.fori_loop(0, qkv_ref.shape[1] * nq, q_block, 0)


def _attn(qkv, g):
    b, d, length, width = qkv.shape
    tk = min(2 * LANES, length)
    return pl.pallas_call(
        functools.partial(_attn_kernel, length=length),
        grid=(b,),
        in_specs=[pl.BlockSpec((1, d, length, width), lambda i: (i, 0, 0, 0))],
        out_specs=[pl.BlockSpec((1, d, length, ATTN_WIDTH), lambda i: (i, 0, 0, 0)),
                   pl.BlockSpec((1, d, length, LANES), lambda i: (i, 0, 0, 0))],
        out_shape=[jax.ShapeDtypeStruct((b, d, length, ATTN_WIDTH), BF16),
                   jax.ShapeDtypeStruct((b, d, length, LANES), F32)],
        scratch_shapes=[pltpu.VMEM((HEADS, LANES, tk), F32), pltpu.VMEM((HEADS, LANES, tk), BF16)],
        compiler_params=pltpu.CompilerParams(
            dimension_semantics=("arbitrary",), vmem_limit_bytes=VMEM_LIMIT),
        name=f"attn{g}",
    )(qkv)


def _fourier_kernel(x_ref, bd_ref, cs_ref, ns_ref, o_ref, y_ref, *, seq, rows):
    for r in range(seq // rows):
        rs = slice(r * rows, (r + 1) * rows)
        y_ref[rs, :] = _dot(x_ref[0, rs, :], bd_ref[...]).astype(BF16)
    for r in range(seq // rows):
        rs = slice(r * rows, (r + 1) * rows)
        o = _dot(cs_ref[rs, :], y_ref[:, :ATTN_WIDTH]) + _dot(ns_ref[rs, :], y_ref[:, ATTN_WIDTH:])
        o_ref[0, rs, :] = o.astype(BF16)


def _fourier(qkvf, bd, cs, ns):
    b, s, _ = qkvf.shape
    return pl.pallas_call(
        functools.partial(_fourier_kernel, seq=s, rows=256),
        grid=(b,),
        in_specs=[pl.BlockSpec((1, s, ATTN_WIDTH), lambda i: (i, 0, 0)),
                  _const_spec((ATTN_WIDTH, 2 * ATTN_WIDTH)),
                  _const_spec((s, s)),
                  _const_spec((s, s))],
        out_specs=pl.BlockSpec((1, s, ATTN_WIDTH), lambda i: (i, 0, 0)),
        out_shape=jax.ShapeDtypeStruct((b, s, ATTN_WIDTH), BF16),
        scratch_shapes=[pltpu.VMEM((s, 2 * ATTN_WIDTH), BF16)],
        compiler_params=pltpu.CompilerParams(
            dimension_semantics=("arbitrary",), vmem_limit_bytes=VMEM_LIMIT),
        name="fourier",
    )(qkvf, bd, cs, ns)


def _merge_kernel(x_ref, mod_ref, o0_ref, o1_ref, o2_ref, l0_ref, l1_ref, l2_ref, fm_ref,
                  g1_ref, g2_ref, wg_ref, wap_ref, wfp_ref, wout_ref, wrh_ref, wrl_ref, br_ref,
                  expand_ref, ltri_ref, perm1_ref, perm2_ref,
                  h_ref, u2_ref, ridx_ref, rw_ref, cnt_ref, cnt_scr):
    @pl.when((pl.program_id(0) == 0) & (pl.program_id(1) == 0))
    def _():
        cnt_scr[...] = jnp.zeros_like(cnt_scr)

    x = x_ref[0]
    ts = x.shape[0]
    mod = mod_ref[0]
    shift1, scale1, gate1 = mod[0:1], mod[1:2], mod[2:3]
    shift2, scale2 = mod[3:4], mod[4:5]
    u = _rms_mod(x, g1_ref[...], scale1, shift1).astype(BF16)
    gates = _dot(u, wg_ref[...])

    os_ = [o0_ref[0, 0].astype(F32)]
    lses = [l0_ref[0, 0]]
    for o_ref, l_ref, perm_ref in ((o1_ref, l1_ref, perm1_ref), (o2_ref, l2_ref, perm2_ref)):
        perm = perm_ref[...]
        os_.append(_dot(perm, o_ref[0].reshape(ts, ATTN_WIDTH)))
        l_hi, l_lo = _split_bf16(l_ref[0].reshape(ts, LANES))
        lses.append(_dot(perm, l_hi) + _dot(perm, l_lo))

    mx = jnp.maximum(jnp.maximum(lses[0], lses[1]), lses[2])
    es = [jnp.exp(l - mx) for l in lses]
    den = es[0] + es[1] + es[2]
    attn = None
    for e, o in zip(es, os_):
        w_hi, w_lo = _split_bf16(e / den)
        wf = _dot(w_hi, expand_ref[...]) + _dot(w_lo, expand_ref[...])
        term = wf * o
        attn = term if attn is None else attn + term

    a = _dot(attn.astype(BF16), wap_ref[...])
    f = _dot(fm_ref[0], wfp_ref[...])
    merged = jax.nn.sigmoid(gates[:, :D_MODEL]) * a + jax.nn.sigmoid(gates[:, D_MODEL:]) * f
    h = x + gate1 * _dot(merged.astype(BF16), wout_ref[...])
    h_ref[0] = h

    u2 = _rms_mod(h, g2_ref[...], scale2, shift2)
    u2_ref[0] = u2
    u_hi, u_lo = _split_bf16(u2)
    logits = (_dot(u_hi, wrh_ref[...]) + _dot(u_lo, wrh_ref[...]) + _dot(u_hi, wrl_ref[...])
              + br_ref[...])

    lane = lax.broadcasted_iota(jnp.int32, (ts, LANES), 1)
    lg = jnp.where(lane < N_EXPERTS, logits, -jnp.inf)
    vals, ids = [], []
    onehot = jnp.zeros((ts, LANES), F32)
    for _ in range(TOP_K):
        m = jnp.max(lg, axis=-1, keepdims=True)
        idx = jnp.min(jnp.where(lg == m, lane, LANES), axis=-1, keepdims=True)
        hit = lane == idx
        vals.append(m)
        ids.append(idx)
        onehot = jnp.where(hit, 1.0, onehot)
        lg = jnp.where(hit, -jnp.inf, lg)
    exps = [jnp.exp(v - vals[0]) for v in vals]
    esum = exps[0] + exps[1] + exps[2] + exps[3]

    cum = _dot(ltri_ref[...], onehot.astype(BF16)) + cnt_scr[0:1, :]
    ridx = jnp.zeros((ts, LANES), F32)
    rw = jnp.zeros((ts, LANES), F32)
    for k in range(TOP_K):
        rank = jnp.sum(jnp.where(lane == ids[k], cum, 0.0), axis=-1, keepdims=True)
        ridx = jnp.where(lane == k, ids[k].astype(F32), ridx)
        ridx = jnp.where(lane == TOP_K + k, rank, ridx)
        rw = jnp.where(lane == k, exps[k] / esum, rw)
    ridx_ref[0] = ridx.astype(jnp.int32)
    rw_ref[0] = rw
    cnt_scr[...] = cnt_scr[...] + jnp.sum(onehot, axis=0, keepdims=True)
    cnt_ref[...] = cnt_scr[...]


def _merge(x, mod, os_, lses, fm, g1, g2, wg, wap, wfp, wout, wrh, wrl, br, expand, ltri, perms,
           ts=MERGE_ROWS):
    b, s, _ = x.shape
    tile = lambda w: pl.BlockSpec((1, ts, w), lambda i, j: (i, j, 0))
    strided = lambda w: [pl.BlockSpec((1, d, ts // d, w), lambda i, j: (i, 0, j, 0)) for d in DILATIONS]
    consts = (g1, g2, wg, wap, wfp, wout, wrh, wrl, br, expand, ltri) + tuple(perms)
    in_specs = ([tile(D_MODEL), pl.BlockSpec((1, 6, D_MODEL), lambda i, j: (i, 0, 0))]
                + strided(ATTN_WIDTH) + strided(LANES) + [tile(ATTN_WIDTH)]
                + [_const_spec(a.shape) for a in consts])
    return pl.pallas_call(
        _merge_kernel,
        grid=(b, s // ts),
        in_specs=in_specs,
        out_specs=[tile(D_MODEL), tile(D_MODEL), tile(LANES), tile(LANES),
                   pl.BlockSpec((8, LANES), lambda i, j: (0, 0))],
        out_shape=[jax.ShapeDtypeStruct((b, s, D_MODEL), F32),
                   jax.ShapeDtypeStruct((b, s, D_MODEL), F32),
                   jax.ShapeDtypeStruct((b, s, LANES), jnp.int32),
                   jax.ShapeDtypeStruct((b, s, LANES), F32),
                   jax.ShapeDtypeStruct((8, LANES), F32)],
        scratch_shapes=[pltpu.VMEM((8, LANES), F32)],
        compiler_params=pltpu.CompilerParams(
            dimension_semantics=("arbitrary", "arbitrary"), vmem_limit_bytes=VMEM_LIMIT),
        name="merge",
    )(x, mod, *os_, *lses, fm, *consts)


def _row_copy(src_ref, src_row, dst_ref, dst_row, sem):
    return pltpu.make_async_copy(src_ref.at[pl.ds(src_row, 1)], dst_ref.at[pl.ds(dst_row, 1)], sem)


def _scatter_kernel(dest_ref, u2_ref, xs_in_ref, xs_ref, sem, *, tt):
    del xs_in_ref

    def copies(t):
        return [_row_copy(u2_ref, t, xs_ref, dest_ref[t * TOP_K + k], sem) for k in range(TOP_K)]

    def start(t, c):
        for cp in copies(t):
            cp.start()
        return c

    def wait(t, c):
        for cp in copies(t):
            cp.wait()
        return c

    lax.fori_loop(0, tt, start, 0)
    lax.fori_loop(0, tt, wait, 0)


def _scatter(dest, u2, xs_zero, tt=256):
    t = u2.shape[0]
    return pl.pallas_call(
        functools.partial(_scatter_kernel, tt=tt),
        grid=(t // tt,),
        in_specs=[pl.BlockSpec((tt * TOP_K,), lambda i: (i,), memory_space=pltpu.SMEM),
                  pl.BlockSpec((tt, D_MODEL), lambda i: (i, 0)),
                  pl.BlockSpec(memory_space=pl.ANY)],
        out_specs=pl.BlockSpec(memory_space=pl.ANY),
        out_shape=jax.ShapeDtypeStruct(xs_zero.shape, xs_zero.dtype),
        scratch_shapes=[pltpu.SemaphoreType.DMA],
        input_output_aliases={2: 0},
        compiler_params=pltpu.CompilerParams(
            dimension_semantics=("arbitrary",), has_side_effects=True, vmem_limit_bytes=VMEM_LIMIT),
        name="scatter",
    )(dest, u2, xs_zero)


def _expert_kernel(be_ref, nb_ref, xs_ref, w1_ref, b1_ref, w2_ref, b2_ref, ys_ref):
    del be_ref
    i = pl.program_id(0)

    @pl.when(i < nb_ref[0])
    def _():
        hdn = _dot(xs_ref[...].astype(BF16), w1_ref[0]) + b1_ref[0]
        glu = jnp.minimum(hdn[:, :D_FF], SWIGLU_LIMIT)
        lin = jnp.clip(hdn[:, D_FF:], -SWIGLU_LIMIT, SWIGLU_LIMIT)
        act = (lin + 1.0) * (glu * jax.nn.sigmoid(SWIGLU_ALPHA * glu))
        ys_ref[...] = _dot(act.astype(BF16), w2_ref[0]) + b2_ref[0]

    @pl.when(i >= nb_ref[0])
    def _():
        ys_ref[...] = jnp.zeros_like(ys_ref)


def _expert(block_e, nb_used, xs, w1, b1, w2, b2):
    n_blocks = xs.shape[0] // EXPERT_ROWS
    grid_spec = pltpu.PrefetchScalarGridSpec(
        num_scalar_prefetch=2,
        grid=(n_blocks,),
        in_specs=[pl.BlockSpec((EXPERT_ROWS, D_MODEL), lambda i, be, nb: (i, 0)),
                  pl.BlockSpec((1, D_MODEL, 2 * D_FF), lambda i, be, nb: (be[i], 0, 0)),
                  pl.BlockSpec((1, 1, 2 * D_FF), lambda i, be, nb: (be[i], 0, 0)),
                  pl.BlockSpec((1, D_FF, D_MODEL), lambda i, be, nb: (be[i], 0, 0)),
                  pl.BlockSpec((1, 1, D_MODEL), lambda i, be, nb: (be[i], 0, 0))],
        out_specs=pl.BlockSpec((EXPERT_ROWS, D_MODEL), lambda i, be, nb: (i, 0)),
    )
    return pl.pallas_call(
        _expert_kernel,
        grid_spec=grid_spec,
        out_shape=jax.ShapeDtypeStruct(xs.shape, F32),
        compiler_params=pltpu.CompilerParams(
            dimension_semantics=("arbitrary",), vmem_limit_bytes=VMEM_LIMIT),
        name="expert",
    )(block_e, nb_used, xs, w1, b1, w2, b2)


def _combine_kernel(dest_ref, h_ref, rw_ref, mod_ref, ys_ref, o_ref, ybuf, sem, *, tt):
    def copies(t):
        return [pltpu.make_async_copy(ys_ref.at[pl.ds(dest_ref[t * TOP_K + k], 1)],
                                      ybuf.at[k, pl.ds(t, 1)], sem) for k in range(TOP_K)]

    def start(t, c):
        for cp in copies(t):
            cp.start()
        return c

    def wait(t, c):
        for cp in copies(t):
            cp.wait()
        return c

    lax.fori_loop(0, tt, start, 0)
    lax.fori_loop(0, tt, wait, 0)
    rw = rw_ref[...]
    acc = ybuf[0] * rw[:, 0:1]
    for k in range(1, TOP_K):
        acc = acc + ybuf[k] * rw[:, k:k + 1]
    o_ref[...] = h_ref[...] + mod_ref[0][5:6] * acc


def _combine(dest, h, rw, mod, ys, seq, tt=256):
    t = h.shape[0]
    per_seq = seq // tt
    return pl.pallas_call(
        functools.partial(_combine_kernel, tt=tt),
        grid=(t // tt,),
        in_specs=[pl.BlockSpec((tt * TOP_K,), lambda i: (i,), memory_space=pltpu.SMEM),
                  pl.BlockSpec((tt, D_MODEL), lambda i: (i, 0)),
                  pl.BlockSpec((tt, LANES), lambda i: (i, 0)),
                  pl.BlockSpec((1, 6, D_MODEL), lambda i: (i // per_seq, 0, 0)),
                  pl.BlockSpec(memory_space=pl.ANY)],
        out_specs=pl.BlockSpec((tt, D_MODEL), lambda i: (i, 0)),
        out_shape=jax.ShapeDtypeStruct((t, D_MODEL), F32),
        scratch_shapes=[pltpu.VMEM((TOP_K, tt, D_MODEL), F32), pltpu.SemaphoreType.DMA],
        compiler_params=pltpu.CompilerParams(
            dimension_semantics=("arbitrary",), vmem_limit_bytes=VMEM_LIMIT),
        name="combine",
    )(dest, h, rw, mod, ys)


def _rope_tables(seq, ts):
    half = ROPE_DIM // 2
    inv_freq = jnp.power(jnp.float32(ROPE_THETA), -jnp.arange(half, dtype=F32) * 2.0 / ROPE_DIM)
    ang = jnp.arange(seq, dtype=F32)[:, None] * inv_freq[None, :]
    cos, sin = jnp.cos(ang), jnp.sin(ang)
    ones = jnp.ones((seq, HEAD_DIM - ROPE_DIM), F32)
    zeros = jnp.zeros_like(ones)
    zh = jnp.zeros((seq, half), F32)
    cos_t = jnp.concatenate([cos, cos, ones], axis=1)
    sin_prev = jnp.concatenate([zh, sin, zeros], axis=1)
    sin_next = jnp.concatenate([-sin, zh, zeros], axis=1)
    tables = jnp.stack([jnp.tile(t, (1, LANES // HEAD_DIM)) for t in (cos_t, sin_prev, sin_next)])
    per_group = []
    for d in DILATIONS:
        pos = jnp.arange(seq).reshape(seq // ts, ts // d, d).transpose(0, 2, 1).reshape(seq)
        per_group.append(tables[:, pos, :])
    return jnp.stack(per_group)


def _dft_tables(seq):
    k = jnp.arange(seq, dtype=jnp.int32)
    ang = ((k[:, None] * k[None, :]) % seq).astype(F32) * (2.0 * jnp.pi / seq)
    scale = seq ** -0.5
    cs = jnp.cos(ang) * scale
    ns = -jnp.sin(ang) * scale
    c = jnp.arange(FOURIER_GROUP_DIM, dtype=jnp.int32)
    angc = ((c[:, None] * c[None, :]) % FOURIER_GROUP_DIM).astype(F32) * (2.0 * jnp.pi / FOURIER_GROUP_DIM)
    eye = jnp.eye(ATTN_WIDTH // FOURIER_GROUP_DIM, dtype=F32)
    cscale = FOURIER_GROUP_DIM ** -0.5
    bd = jnp.concatenate([jnp.kron(eye, jnp.cos(angc) * cscale),
                          jnp.kron(eye, jnp.sin(angc) * cscale)], axis=1)
    return bd.astype(BF16), cs.astype(BF16), ns.astype(BF16)


def _trunk(x, c, p):
    b, s, _ = x.shape
    t = b * s
    mod = _ada(c, p["w_ada"], p["b_ada"]).reshape(b, 6, D_MODEL)
    *qkvs, fz = _inproj(x, mod, p["g1"], p["w_qkvf"], p["gains"], p["rope"], p["gmat"])
    os_, lses = zip(*[_attn(qkv, g) for g, qkv in enumerate(qkvs)])
    fm = _fourier(fz, p["bd"], p["cs"], p["ns"])
    h, u2, ridx, rw, cnt = _merge(x, mod, os_, lses, fm, p["g1"], p["g2"], p["wg"], p["wap"], p["wfp"],
                                  p["wout"], p["wrh"], p["wrl"], p["br"], p["expand"], p["ltri"], p["perms"])

    rows = EXPERT_ROWS
    counts = cnt[0, :N_EXPERTS].astype(jnp.int32)
    padded = (counts + rows - 1) // rows * rows
    pend = jnp.cumsum(padded)
    pstart = pend - padded
    idx = ridx[..., :TOP_K].reshape(t, TOP_K)
    rank = ridx[..., TOP_K:2 * TOP_K].reshape(t, TOP_K)
    start_of = jnp.sum(jnp.where(idx[..., None] == jnp.arange(N_EXPERTS), pstart, 0), axis=-1)
    dest = (start_of + rank).reshape(-1)
    n_blocks = -(-(t * TOP_K + N_EXPERTS * (rows - 1)) // rows)
    block_start = jnp.arange(n_blocks, dtype=jnp.int32) * rows
    block_e = jnp.minimum(jnp.sum((pend[None, :] <= block_start[:, None]).astype(jnp.int32), axis=-1),
                          N_EXPERTS - 1)
    nb_used = (pend[-1:] // rows).astype(jnp.int32)

    xs = _scatter(dest, u2.reshape(t, D_MODEL), jnp.zeros((n_blocks * rows, D_MODEL), F32))
    ys = _expert(block_e, nb_used, xs, p["w1"], p["b1"], p["w2"], p["b2"])
    out = _combine(dest, h.reshape(t, D_MODEL), rw.reshape(t, LANES), mod, ys, s)
    return out.reshape(b, s, D_MODEL)


def kernel(x_prompt, x_sample, c_prompt, c_sample, w_ada, b_ada, norm1_g, norm2_g, w_in, q_gain, k_gain,
           w_attn_proj, w_fourier_proj, w_out, w_router, b_router, w1, b1, w2, b2):
    seq = x_prompt.shape[1]
    assert w_ada.shape[0] == 1, "single-layer trunk"
    w_in0 = w_in[0]
    gains = jnp.zeros((8, ATTN_WIDTH), F32)
    for g in range(N_GROUPS):
        gains = gains.at[2 * g].set(jnp.tile(q_gain[0, g], HEADS))
        gains = gains.at[2 * g + 1].set(jnp.tile(k_gain[0, g], HEADS))
    heads = jnp.arange(ATTN_WIDTH) // HEAD_DIM
    gmat = ((heads[:, None] == heads[None, :]).astype(F32) / HEAD_DIM).astype(BF16)
    expand = (jnp.arange(LANES)[:, None] == heads[None, :]).astype(BF16)
    ltri = (jnp.arange(MERGE_ROWS)[:, None] > jnp.arange(MERGE_ROWS)[None, :]).astype(BF16)
    perms = []
    for d in DILATIONS[1:]:
        n = MERGE_ROWS // d
        src = (jnp.arange(MERGE_ROWS) % d) * n + jnp.arange(MERGE_ROWS) // d
        perms.append((src[:, None] == jnp.arange(MERGE_ROWS)[None, :]).astype(BF16))
    wr = jnp.zeros((D_MODEL, LANES), F32).at[:, :N_EXPERTS].set(w_router[0])
    wrh = wr.astype(BF16)
    wrl = (wr - wrh.astype(F32)).astype(BF16)
    br = jnp.zeros((1, LANES), F32).at[0, :N_EXPERTS].set(b_router[0])
    bd, cs, ns = _dft_tables(seq)
    p = dict(
        w_ada=w_ada[0], b_ada=b_ada[0], g1=norm1_g, g2=norm2_g,
        w_qkvf=w_in0[:, :QKVF_WIDTH].astype(BF16), wg=w_in0[:, QKVF_WIDTH:].astype(BF16),
        gains=gains, rope=_rope_tables(seq, INPROJ_ROWS), gmat=gmat, bd=bd, cs=cs, ns=ns,
        wap=w_attn_proj[0].astype(BF16), wfp=w_fourier_proj[0].astype(BF16), wout=w_out[0].astype(BF16),
        wrh=wrh, wrl=wrl, br=br, expand=expand, ltri=ltri, perms=perms,
        w1=w1[0].astype(BF16), b1=b1[0].reshape(N_EXPERTS, 1, 2 * D_FF),
        w2=w2[0].astype(BF16), b2=b2[0].reshape(N_EXPERTS, 1, D_MODEL),
    )
    return _trunk(x_prompt, c_prompt, p), _trunk(x_sample, c_sample, p)
```

```python
import functools

import jax
import jax.numpy as jnp
from jax import lax
from jax.experimental import pallas as pl
from jax.experimental.pallas import tpu as pltpu

F32 = jnp.float32
BF16 = jnp.bfloat16

D_MODEL = 1024
HEAD_DIM = 64
HEADS = 8
ATTN_WIDTH = HEADS * HEAD_DIM
DILATIONS = (1, 4, 16)
BAND = 64
N_GROUPS = 3
QKVF_WIDTH = 5120
FOURIER_GROUP_DIM = 64
ROPE_DIM = 16
ROPE_THETA = 500000.0
N_EXPERTS = 32
TOP_K = 4
D_FF = 1024
SWIGLU_ALPHA = 1.702
SWIGLU_LIMIT = 7.0
NORM_EPS = 1e-6
NEG_INF = -1e30
LN2 = 0.6931471805599453
LOG2E = 1.4426950408889634

LANES = 128
ROW_CHUNKS = D_MODEL // LANES
EXPERT_ROWS = 256
SEGMENT_PIECES = (256, 128, 64, 32, 16, 8, 4, 2, 1)
INPROJ_ROWS = 512
MERGE_ROWS = 256
VMEM_LIMIT = 56 * 1024 * 1024


def _dot(a, b):
    return jnp.dot(a, b, preferred_element_type=F32)


def _split_bf16(v):
    hi = v.astype(BF16)
    lo = (v - hi.astype(F32)).astype(BF16)
    return hi, lo


def _rms_mod(x, gain, scale, shift):
    ms = jnp.mean(x * x, axis=-1, keepdims=True)
    return x * lax.rsqrt(ms + NORM_EPS) * gain * (1.0 + scale) + shift


def _const_spec(shape):
    return pl.BlockSpec(shape, lambda *_: (0,) * len(shape), pipeline_mode=pl.Buffered(1))


def _ada_kernel(c_ref, w_ref, b_ref, o_ref):
    c = c_ref[...]
    a = c * jax.nn.sigmoid(c)
    a_hi, a_lo = _split_bf16(a)
    w_hi, w_lo = _split_bf16(w_ref[...])
    o_ref[...] = _dot(a_hi, w_hi) + _dot(a_lo, w_hi) + _dot(a_hi, w_lo) + b_ref[...]


def _ada(c, w_ada, b_ada):
    b = c.shape[0]
    n = w_ada.shape[1]
    tn = D_MODEL
    return pl.pallas_call(
        _ada_kernel,
        grid=(n // tn,),
        in_specs=[pl.BlockSpec((b, D_MODEL), lambda j: (0, 0)),
                  pl.BlockSpec((D_MODEL, tn), lambda j: (0, j)),
                  pl.BlockSpec((1, tn), lambda j: (0, j))],
        out_specs=pl.BlockSpec((b, tn), lambda j: (0, j)),
        out_shape=jax.ShapeDtypeStruct((b, n), F32),
        compiler_params=pltpu.CompilerParams(vmem_limit_bytes=VMEM_LIMIT),
        name="ada",
    )(c, w_ada, b_ada.reshape(1, n))


def _inproj_kernel(x_ref, mod_ref, g1_ref, w_ref, gain_ref, rope_ref, gmat_ref,
                   o0_ref, o1_ref, o2_ref, of_ref, u_scr):
    x = x_ref[0]
    ts = x.shape[0]
    mod = mod_ref[0]
    u = _rms_mod(x, g1_ref[...], mod[1:2], mod[0:1])
    ub = u.astype(BF16)
    for c in range(D_MODEL // LANES):
        u_scr[c] = u[:, c * LANES:(c + 1) * LANES]
    for g, o_ref in enumerate((o0_ref, o1_ref, o2_ref)):
        d = DILATIONS[g]
        n = ts // d
        if d == 1:
            ug = ub
        else:
            ug = jnp.concatenate(
                [jnp.concatenate([u_scr[c, pl.ds(r, n, stride=d), :] for r in range(d)], axis=0)
                 for c in range(D_MODEL // LANES)], axis=1).astype(BF16)
        cos_t, sin_prev, sin_next = rope_ref[g, 0], rope_ref[g, 1], rope_ref[g, 2]
        for j in range(3):
            c0 = (3 * g + j) * ATTN_WIDTH
            z = _dot(ug, w_ref[:, c0:c0 + ATTN_WIDTH])
            if j < 2:
                ms = _dot((z * z).astype(BF16), gmat_ref[...])
                gain = gain_ref[2 * g + j:2 * g + j + 1, :]
                if j == 0:
                    gain = gain * (HEAD_DIM ** -0.5 * LOG2E)
                z = z * lax.rsqrt(ms + NORM_EPS) * gain
                blocks = []
                for blk in range(ATTN_WIDTH // LANES):
                    zb = z[:, blk * LANES:(blk + 1) * LANES]
                    blocks.append(zb * cos_t
                                  + pltpu.roll(zb, ROPE_DIM // 2, 1) * sin_prev
                                  + pltpu.roll(zb, LANES - ROPE_DIM // 2, 1) * sin_next)
                z = jnp.concatenate(blocks, axis=1)
            zb16 = z.astype(BF16)
            for r in range(d):
                o_ref[0, r, :, j * ATTN_WIDTH:(j + 1) * ATTN_WIDTH] = zb16[r * n:(r + 1) * n, :]
    of_ref[0] = _dot(ub, w_ref[:, 9 * ATTN_WIDTH:]).astype(BF16)


def _inproj(x, mod, g1, w_qkvf, gains, rope, gmat, ts=INPROJ_ROWS):
    b, s, _ = x.shape
    qkv_w = 3 * ATTN_WIDTH
    out_specs = [pl.BlockSpec((1, d, ts // d, qkv_w), lambda i, j: (i, 0, j, 0)) for d in DILATIONS]
    out_shape = [jax.ShapeDtypeStruct((b, d, s // d, qkv_w), BF16) for d in DILATIONS]
    return pl.pallas_call(
        _inproj_kernel,
        grid=(b, s // ts),
        in_specs=[pl.BlockSpec((1, ts, D_MODEL), lambda i, j: (i, j, 0)),
                  pl.BlockSpec((1, 6, D_MODEL), lambda i, j: (i, 0, 0)),
                  _const_spec((1, D_MODEL)),
                  _const_spec((D_MODEL, QKVF_WIDTH)),
                  _const_spec((8, ATTN_WIDTH)),
                  pl.BlockSpec((N_GROUPS, 3, ts, LANES), lambda i, j: (0, 0, j, 0)),
                  _const_spec((ATTN_WIDTH, ATTN_WIDTH))],
        out_specs=out_specs + [pl.BlockSpec((1, ts, ATTN_WIDTH), lambda i, j: (i, j, 0))],
        out_shape=out_shape + [jax.ShapeDtypeStruct((b, s, ATTN_WIDTH), BF16)],
        scratch_shapes=[pltpu.VMEM((D_MODEL // LANES, ts, LANES), F32)],
        compiler_params=pltpu.CompilerParams(
            dimension_semantics=("arbitrary", "arbitrary"), vmem_limit_bytes=VMEM_LIMIT),
        name="inproj",
    )(x, mod, g1, w_qkvf, gains, rope, gmat)


def _attn_kernel(qkv_ref, o_ref, lse_ref, s_scr, p_scr, *, length):
    tq = LANES
    tk = min(2 * LANES, length)
    nq = length // tq
    n_pairs = ATTN_WIDTH // LANES
    lane = lax.broadcasted_iota(jnp.int32, (tq, LANES), 1)
    first_head = lane < HEAD_DIM
    first_head_k = lax.broadcasted_iota(jnp.int32, (tk, LANES), 1) < HEAD_DIM
    rel = (lax.broadcasted_iota(jnp.int32, (tq, tk), 1)
           - lax.broadcasted_iota(jnp.int32, (tq, tk), 0))

    def q_block(it, carry):
        r = it // nq
        i = it % nq
        q0 = pl.multiple_of(i * tq, tq)
        ws = pl.multiple_of(jnp.clip(i * tq - BAND, 0, length - tk), BAND)
        valid = jnp.abs(rel + (ws - i * tq)) <= BAND

        for p in range(n_pairs):
            q2 = qkv_ref[0, r, pl.ds(q0, tq), p * LANES:(p + 1) * LANES]
            k2 = qkv_ref[0, r, pl.ds(ws, tk), ATTN_WIDTH + p * LANES:ATTN_WIDTH + (p + 1) * LANES]
            for hh in range(2):
                sel = first_head if hh == 0 else jnp.logical_not(first_head)
                qa = jnp.where(sel, q2, jnp.zeros_like(q2))
                s = lax.dot_general(qa, k2, (((1,), (1,)), ((), ())), preferred_element_type=F32)
                s_scr[2 * p + hh] = jnp.where(valid, s, NEG_INF)

        m_tile = jnp.zeros((tq, LANES), F32)
        for h in range(HEADS):
            s = s_scr[h]
            m = jnp.max(s, axis=-1, keepdims=True)
            p_scr[h] = jnp.exp2(s - m).astype(BF16)
            m_tile = jnp.where(lane == h, m, m_tile)

        den_tile = jnp.ones((tq, LANES), F32)
        for p in range(n_pairs):
            cs = slice(2 * ATTN_WIDTH + p * LANES, 2 * ATTN_WIDTH + (p + 1) * LANES)
            v2 = qkv_ref[0, r, pl.ds(ws, tk), cs]
            one = jnp.ones_like(v2)
            o_a = _dot(p_scr[2 * p], jnp.where(first_head_k, v2, one))
            o_b = _dot(p_scr[2 * p + 1], jnp.where(first_head_k, one, v2))
            den_ba = jnp.where(first_head, o_b, o_a)
            den_ab = pltpu.roll(den_ba, HEAD_DIM, 1)
            o_ref[0, r, pl.ds(q0, tq), p * LANES:(p + 1) * LANES] = (
                jnp.where(first_head, o_a, o_b) / den_ab).astype(BF16)
            den_tile = jnp.where(lane == 2 * p, den_ab, den_tile)
            den_tile = jnp.where(lane == 2 * p + 1, den_ba, den_tile)
        lse_ref[0, r, pl.ds(q0, tq), :] = (m_tile + jnp.log2(den_tile)) * LN2
        return carry

    lax.fori_loop(0, qkv_ref.shape[1] * nq, q_block, 0)


def _attn(qkv, g):
    b, d, length, width = qkv.shape
    tk = min(2 * LANES, length)
    return pl.pallas_call(
        functools.partial(_attn_kernel, length=length),
        grid=(b,),
        in_specs=[pl.BlockSpec((1, d, length, width), lambda i: (i, 0, 0, 0))],
        out_specs=[pl.BlockSpec((1, d, length, ATTN_WIDTH), lambda i: (i, 0, 0, 0)),
                   pl.BlockSpec((1, d, length, LANES), lambda i: (i, 0, 0, 0))],
        out_shape=[jax.ShapeDtypeStruct((b, d, length, ATTN_WIDTH), BF16),
                   jax.ShapeDtypeStruct((b, d, length, LANES), F32)],
        scratch_shapes=[pltpu.VMEM((HEADS, LANES, tk), F32), pltpu.VMEM((HEADS, LANES, tk), BF16)],
        compiler_params=pltpu.CompilerParams(
            dimension_semantics=("arbitrary",), vmem_limit_bytes=VMEM_LIMIT),
        name=f"attn{g}",
    )(qkv)


def _fourier_kernel(x_ref, bd_ref, cs_ref, ns_ref, o_ref, y_ref, *, seq, rows):
    for r in range(seq // rows):
        rs = slice(r * rows, (r + 1) * rows)
        y_ref[rs, :] = _dot(x_ref[0, rs, :], bd_ref[...]).astype(BF16)
    for r in range(seq // rows):
        rs = slice(r * rows, (r + 1) * rows)
        o = _dot(cs_ref[rs, :], y_ref[:, :ATTN_WIDTH]) + _dot(ns_ref[rs, :], y_ref[:, ATTN_WIDTH:])
        o_ref[0, rs, :] = o.astype(BF16)


def _fourier(qkvf, bd, cs, ns):
    b, s, _ = qkvf.shape
    return pl.pallas_call(
        functools.partial(_fourier_kernel, seq=s, rows=256),
        grid=(b,),
        in_specs=[pl.BlockSpec((1, s, ATTN_WIDTH), lambda i: (i, 0, 0)),
                  _const_spec((ATTN_WIDTH, 2 * ATTN_WIDTH)),
                  _const_spec((s, s)),
                  _const_spec((s, s))],
        out_specs=pl.BlockSpec((1, s, ATTN_WIDTH), lambda i: (i, 0, 0)),
        out_shape=jax.ShapeDtypeStruct((b, s, ATTN_WIDTH), BF16),
        scratch_shapes=[pltpu.VMEM((s, 2 * ATTN_WIDTH), BF16)],
        compiler_params=pltpu.CompilerParams(
            dimension_semantics=("arbitrary",), vmem_limit_bytes=VMEM_LIMIT),
        name="fourier",
    )(qkvf, bd, cs, ns)


def _merge_kernel(x_ref, mod_ref, o0_ref, o1_ref, o2_ref, l0_ref, l1_ref, l2_ref, fm_ref,
                  g1_ref, g2_ref, wg_ref, wap_ref, wfp_ref, wout_ref, wrh_ref, wrl_ref, br_ref,
                  expand_ref, ltri_ref, utri_ref, perm1_ref, perm2_ref,
                  h_ref, u2_ref, ridx_ref, rw_ref, tcnt_ref, cnt_ref, cnt_scr):
    @pl.when((pl.program_id(0) == 0) & (pl.program_id(1) == 0))
    def _():
        cnt_scr[...] = jnp.zeros_like(cnt_scr)

    x = x_ref[0]
    ts = x.shape[0]
    mod = mod_ref[0]
    shift1, scale1, gate1 = mod[0:1], mod[1:2], mod[2:3]
    shift2, scale2 = mod[3:4], mod[4:5]
    u = _rms_mod(x, g1_ref[...], scale1, shift1).astype(BF16)
    gates = _dot(u, wg_ref[...])

    os_ = [o0_ref[0, 0].astype(F32)]
    lses = [l0_ref[0, 0]]
    for o_ref, l_ref, perm_ref in ((o1_ref, l1_ref, perm1_ref), (o2_ref, l2_ref, perm2_ref)):
        perm = perm_ref[...]
        os_.append(_dot(perm, o_ref[0].reshape(ts, ATTN_WIDTH)))
        l_hi, l_lo = _split_bf16(l_ref[0].reshape(ts, LANES))
        lses.append(_dot(perm, l_hi) + _dot(perm, l_lo))

    mx = jnp.maximum(jnp.maximum(lses[0], lses[1]), lses[2])
    es = [jnp.exp(l - mx) for l in lses]
    den = es[0] + es[1] + es[2]
    attn = None
    for e, o in zip(es, os_):
        w_hi, w_lo = _split_bf16(e / den)
        wf = _dot(w_hi, expand_ref[...]) + _dot(w_lo, expand_ref[...])
        term = wf * o
        attn = term if attn is None else attn + term

    a = _dot(attn.astype(BF16), wap_ref[...])
    f = _dot(fm_ref[0], wfp_ref[...])
    merged = jax.nn.sigmoid(gates[:, :D_MODEL]) * a + jax.nn.sigmoid(gates[:, D_MODEL:]) * f
    h = x + gate1 * _dot(merged.astype(BF16), wout_ref[...])
    h_ref[0] = h

    u2 = _rms_mod(h, g2_ref[...], scale2, shift2)
    u_hi, u_lo = _split_bf16(u2)
    u2_ref[0] = u_hi
    logits = (_dot(u_hi, wrh_ref[...]) + _dot(u_lo, wrh_ref[...]) + _dot(u_hi, wrl_ref[...])
              + br_ref[...])

    lane = lax.broadcasted_iota(jnp.int32, (ts, LANES), 1)
    lg = jnp.where(lane < N_EXPERTS, logits, -jnp.inf)
    vals, ids = [], []
    onehot = jnp.zeros((ts, LANES), F32)
    for _ in range(TOP_K):
        m = jnp.max(lg, axis=-1, keepdims=True)
        idx = jnp.min(jnp.where(lg == m, lane, LANES), axis=-1, keepdims=True)
        hit = lane == idx
        vals.append(m)
        ids.append(idx)
        onehot = jnp.where(hit, 1.0, onehot)
        lg = jnp.where(hit, -jnp.inf, lg)
    exps = [jnp.exp(v - vals[0]) for v in vals]
    esum = exps[0] + exps[1] + exps[2] + exps[3]

    n_tile = jnp.sum(onehot, axis=0, keepdims=True)
    seg_start = _dot(jnp.broadcast_to(n_tile, (8, LANES)).astype(BF16), utri_ref[...])[0:1, :]
    pos_all = _dot(ltri_ref[...], onehot.astype(BF16)) + seg_start
    ridx = jnp.zeros((ts, LANES), F32)
    rw = jnp.zeros((ts, LANES), F32)
    for k in range(TOP_K):
        pos = jnp.sum(jnp.where(lane == ids[k], pos_all, 0.0), axis=-1, keepdims=True)
        ridx = jnp.where(lane == k, ids[k].astype(F32), ridx)
        ridx = jnp.where(lane == TOP_K + k, pos, ridx)
        rw = jnp.where(lane == k, exps[k] / esum, rw)
    ridx_ref[0] = ridx.astype(jnp.int32)
    rw_ref[0] = rw
    row = lax.broadcasted_iota(jnp.int32, (8, LANES), 0)
    tcnt = jnp.where(row == 0, cnt_scr[...], jnp.where(row == 1, jnp.broadcast_to(n_tile, (8, LANES)), 0.0))
    tcnt_ref[0] = tcnt.astype(jnp.int32)
    cnt_scr[...] = cnt_scr[...] + n_tile
    cnt_ref[...] = cnt_scr[...]


def _merge(x, mod, os_, lses, fm, g1, g2, wg, wap, wfp, wout, wrh, wrl, br, expand, ltri, utri, perms,
           ts=MERGE_ROWS):
    b, s, _ = x.shape
    tile = lambda w: pl.BlockSpec((1, ts, w), lambda i, j: (i, j, 0))
    strided = lambda w: [pl.BlockSpec((1, d, ts // d, w), lambda i, j: (i, 0, j, 0)) for d in DILATIONS]
    consts = (g1, g2, wg, wap, wfp, wout, wrh, wrl, br, expand, ltri, utri) + tuple(perms)
    in_specs = ([tile(D_MODEL), pl.BlockSpec((1, 6, D_MODEL), lambda i, j: (i, 0, 0))]
                + strided(ATTN_WIDTH) + strided(LANES) + [tile(ATTN_WIDTH)]
                + [_const_spec(a.shape) for a in consts])
    per_seq = s // ts
    return pl.pallas_call(
        _merge_kernel,
        grid=(b, per_seq),
        in_specs=in_specs,
        out_specs=[tile(D_MODEL), tile(D_MODEL), tile(LANES), tile(LANES),
                   pl.BlockSpec((1, 8, LANES), lambda i, j: (i * per_seq + j, 0, 0)),
                   pl.BlockSpec((8, LANES), lambda i, j: (0, 0))],
        out_shape=[jax.ShapeDtypeStruct((b, s, D_MODEL), F32),
                   jax.ShapeDtypeStruct((b, s, D_MODEL), BF16),
                   jax.ShapeDtypeStruct((b, s, LANES), jnp.int32),
                   jax.ShapeDtypeStruct((b, s, LANES), F32),
                   jax.ShapeDtypeStruct((b * per_seq, 8, LANES), jnp.int32),
                   jax.ShapeDtypeStruct((8, LANES), F32)],
        scratch_shapes=[pltpu.VMEM((8, LANES), F32)],
        compiler_params=pltpu.CompilerParams(
            dimension_semantics=("arbitrary", "arbitrary"), vmem_limit_bytes=VMEM_LIMIT),
        name="merge",
    )(x, mod, *os_, *lses, fm, *consts)


def _to_row_chunks(ref, mat):
    for c in range(ROW_CHUNKS):
        ref[pl.ds(c, mat.shape[0], stride=ROW_CHUNKS), :] = mat[:, c * LANES:(c + 1) * LANES]


def _from_row_chunks(ref):
    n = ref.shape[0] // ROW_CHUNKS
    return jnp.concatenate([ref[pl.ds(c, n, stride=ROW_CHUNKS), :] for c in range(ROW_CHUNKS)], axis=1)


def _segment_copies(seg_n_ref, seg_row_ref, tile, sorted_ref, hbm_ref, sem, *, to_hbm, start):
    base = tile * N_EXPERTS

    def segment(e, src):
        n = seg_n_ref[base + e]
        dst = seg_row_ref[base + e]
        for bit in SEGMENT_PIECES:
            off = n & (-2 * bit)

            @pl.when((n & bit) != 0)
            def _():
                vm = sorted_ref.at[pl.ds(pl.multiple_of((src + off) * ROW_CHUNKS, ROW_CHUNKS), bit * ROW_CHUNKS)]
                hb = hbm_ref.at[pl.ds(pl.multiple_of((dst + off) * ROW_CHUNKS, ROW_CHUNKS), bit * ROW_CHUNKS)]
                cp = pltpu.make_async_copy(vm, hb, sem) if to_hbm else pltpu.make_async_copy(hb, vm, sem)
                if start:
                    cp.start()
                else:
                    cp.wait()
        return src + n

    lax.fori_loop(0, N_EXPERTS, segment, 0)


def _dispatch_kernel(seg_n_ref, seg_row_ref, ridx_ref, u2_ref, xs_in_ref, xs_ref, sorted_scr, sem):
    del xs_in_ref
    tile = pl.program_id(0)
    tt = u2_ref.shape[0]
    rows = TOP_K * tt
    pos_t = ridx_ref[...].astype(F32).T
    j = lax.broadcasted_iota(jnp.int32, (rows, tt), 0).astype(F32)
    onehot = jnp.zeros((rows, tt), F32)
    for k in range(TOP_K):
        onehot = jnp.where(j == pos_t[TOP_K + k:TOP_K + k + 1, :], 1.0, onehot)
    _to_row_chunks(sorted_scr, _dot(onehot.astype(BF16), u2_ref[...]))
    args = (seg_n_ref, seg_row_ref, tile, sorted_scr, xs_ref, sem)
    _segment_copies(*args, to_hbm=True, start=True)
    _segment_copies(*args, to_hbm=True, start=False)


def _dispatch(seg_n, seg_row, ridx, u2, xs_zero, tt=MERGE_ROWS):
    t = u2.shape[0]
    grid_spec = pltpu.PrefetchScalarGridSpec(
        num_scalar_prefetch=2,
        grid=(t // tt,),
        in_specs=[pl.BlockSpec((tt, LANES), lambda i, *_: (i, 0)),
                  pl.BlockSpec((tt, D_MODEL), lambda i, *_: (i, 0)),
                  pl.BlockSpec(memory_space=pl.ANY)],
        out_specs=pl.BlockSpec(memory_space=pl.ANY),
        scratch_shapes=[pltpu.VMEM((TOP_K * tt * ROW_CHUNKS, LANES), F32), pltpu.SemaphoreType.DMA],
    )
    return pl.pallas_call(
        _dispatch_kernel,
        grid_spec=grid_spec,
        out_shape=jax.ShapeDtypeStruct(xs_zero.shape, xs_zero.dtype),
        input_output_aliases={4: 0},
        compiler_params=pltpu.CompilerParams(
            dimension_semantics=("arbitrary",), has_side_effects=True, vmem_limit_bytes=VMEM_LIMIT),
        name="dispatch",
    )(seg_n, seg_row, ridx, u2, xs_zero)


def _expert_kernel(be_ref, nb_ref, xs_ref, w1_ref, b1_ref, w2_ref, b2_ref, ys_ref):
    del be_ref
    i = pl.program_id(0)

    @pl.when(i < nb_ref[0])
    def _():
        hdn = _dot(_from_row_chunks(xs_ref).astype(BF16), w1_ref[0]) + b1_ref[0]
        glu = jnp.minimum(hdn[:, :D_FF], SWIGLU_LIMIT)
        lin = jnp.clip(hdn[:, D_FF:], -SWIGLU_LIMIT, SWIGLU_LIMIT)
        act = (lin + 1.0) * (glu * jax.nn.sigmoid(SWIGLU_ALPHA * glu))
        _to_row_chunks(ys_ref, _dot(act.astype(BF16), w2_ref[0]) + b2_ref[0])

    @pl.when(i >= nb_ref[0])
    def _():
        ys_ref[...] = jnp.zeros_like(ys_ref)


def _expert(block_e, nb_used, xs, w1, b1, w2, b2):
    block = (EXPERT_ROWS * ROW_CHUNKS, LANES)
    n_blocks = xs.shape[0] // block[0]
    grid_spec = pltpu.PrefetchScalarGridSpec(
        num_scalar_prefetch=2,
        grid=(n_blocks,),
        in_specs=[pl.BlockSpec(block, lambda i, be, nb: (i, 0)),
                  pl.BlockSpec((1, D_MODEL, 2 * D_FF), lambda i, be, nb: (be[i], 0, 0)),
                  pl.BlockSpec((1, 1, 2 * D_FF), lambda i, be, nb: (be[i], 0, 0)),
                  pl.BlockSpec((1, D_FF, D_MODEL), lambda i, be, nb: (be[i], 0, 0)),
                  pl.BlockSpec((1, 1, D_MODEL), lambda i, be, nb: (be[i], 0, 0))],
        out_specs=pl.BlockSpec(block, lambda i, be, nb: (i, 0)),
    )
    return pl.pallas_call(
        _expert_kernel,
        grid_spec=grid_spec,
        out_shape=jax.ShapeDtypeStruct(xs.shape, F32),
        compiler_params=pltpu.CompilerParams(
            dimension_semantics=("arbitrary",), vmem_limit_bytes=VMEM_LIMIT),
        name="expert",
    )(block_e, nb_used, xs, w1, b1, w2, b2)


def _combine_kernel(seg_n_ref, seg_row_ref, ridx_ref, rw_ref, h_ref, mod_ref, ys_ref, o_ref, sorted_scr, sem):
    tile = pl.program_id(0)
    tt = h_ref.shape[0]
    rows = TOP_K * tt
    args = (seg_n_ref, seg_row_ref, tile, sorted_scr, ys_ref, sem)
    _segment_copies(*args, to_hbm=False, start=True)
    ridx = ridx_ref[...].astype(F32)
    rw = rw_ref[...]
    j = lax.broadcasted_iota(jnp.int32, (tt, rows), 1).astype(F32)
    wmat = jnp.zeros((tt, rows), F32)
    for k in range(TOP_K):
        wmat = jnp.where(j == ridx[:, TOP_K + k:TOP_K + k + 1], rw[:, k:k + 1], wmat)
    w_hi, w_lo = _split_bf16(wmat)
    _segment_copies(*args, to_hbm=False, start=False)
    y = _from_row_chunks(sorted_scr).astype(BF16)
    o_ref[...] = h_ref[...] + mod_ref[0][5:6] * (_dot(w_hi, y) + _dot(w_lo, y))


def _combine(seg_n, seg_row, ridx, rw, h, mod, ys, seq, tt=MERGE_ROWS):
    t = h.shape[0]
    per_seq = seq // tt
    grid_spec = pltpu.PrefetchScalarGridSpec(
        num_scalar_prefetch=2,
        grid=(t // tt,),
        in_specs=[pl.BlockSpec((tt, LANES), lambda i, *_: (i, 0)),
                  pl.BlockSpec((tt, LANES), lambda i, *_: (i, 0)),
                  pl.BlockSpec((tt, D_MODEL), lambda i, *_: (i, 0)),
                  pl.BlockSpec((1, 6, D_MODEL), lambda i, *_: (i // per_seq, 0, 0)),
                  pl.BlockSpec(memory_space=pl.ANY)],
        out_specs=pl.BlockSpec((tt, D_MODEL), lambda i, *_: (i, 0)),
        scratch_shapes=[pltpu.VMEM((TOP_K * tt * ROW_CHUNKS, LANES), F32), pltpu.SemaphoreType.DMA],
    )
    return pl.pallas_call(
        _combine_kernel,
        grid_spec=grid_spec,
        out_shape=jax.ShapeDtypeStruct((t, D_MODEL), F32),
        compiler_params=pltpu.CompilerParams(
            dimension_semantics=("arbitrary",), vmem_limit_bytes=VMEM_LIMIT),
        name="combine",
    )(seg_n, seg_row, ridx, rw, h, mod, ys)


def _rope_tables(seq, ts):
    half = ROPE_DIM // 2
    inv_freq = jnp.power(jnp.float32(ROPE_THETA), -jnp.arange(half, dtype=F32) * 2.0 / ROPE_DIM)
    ang = jnp.arange(seq, dtype=F32)[:, None] * inv_freq[None, :]
    cos, sin = jnp.cos(ang), jnp.sin(ang)
    ones = jnp.ones((seq, HEAD_DIM - ROPE_DIM), F32)
    zeros = jnp.zeros_like(ones)
    zh = jnp.zeros((seq, half), F32)
    cos_t = jnp.concatenate([cos, cos, ones], axis=1)
    sin_prev = jnp.concatenate([zh, sin, zeros], axis=1)
    sin_next = jnp.concatenate([-sin, zh, zeros], axis=1)
    tables = jnp.stack([jnp.tile(t, (1, LANES // HEAD_DIM)) for t in (cos_t, sin_prev, sin_next)])
    per_group = []
    for d in DILATIONS:
        pos = jnp.arange(seq).reshape(seq // ts, ts // d, d).transpose(0, 2, 1).reshape(seq)
        per_group.append(tables[:, pos, :])
    return jnp.stack(per_group)


def _dft_tables(seq):
    k = jnp.arange(seq, dtype=jnp.int32)
    ang = ((k[:, None] * k[None, :]) % seq).astype(F32) * (2.0 * jnp.pi / seq)
    scale = seq ** -0.5
    cs = jnp.cos(ang) * scale
    ns = -jnp.sin(ang) * scale
    c = jnp.arange(FOURIER_GROUP_DIM, dtype=jnp.int32)
    angc = ((c[:, None] * c[None, :]) % FOURIER_GROUP_DIM).astype(F32) * (2.0 * jnp.pi / FOURIER_GROUP_DIM)
    eye = jnp.eye(ATTN_WIDTH // FOURIER_GROUP_DIM, dtype=F32)
    cscale = FOURIER_GROUP_DIM ** -0.5
    bd = jnp.concatenate([jnp.kron(eye, jnp.cos(angc) * cscale),
                          jnp.kron(eye, jnp.sin(angc) * cscale)], axis=1)
    return bd.astype(BF16), cs.astype(BF16), ns.astype(BF16)


def _trunk(x, c, p):
    b, s, _ = x.shape
    t = b * s
    mod = _ada(c, p["w_ada"], p["b_ada"]).reshape(b, 6, D_MODEL)
    *qkvs, fz = _inproj(x, mod, p["g1"], p["w_qkvf"], p["gains"], p["rope"], p["gmat"])
    os_, lses = zip(*[_attn(qkv, g) for g, qkv in enumerate(qkvs)])
    fm = _fourier(fz, p["bd"], p["cs"], p["ns"])
    h, u2, ridx, rw, tcnt, cnt = _merge(
        x, mod, os_, lses, fm, p["g1"], p["g2"], p["wg"], p["wap"], p["wfp"], p["wout"], p["wrh"],
        p["wrl"], p["br"], p["expand"], p["ltri"], p["utri"], p["perms"])

    rows = EXPERT_ROWS
    counts = cnt[0, :N_EXPERTS].astype(jnp.int32)
    padded = (counts + rows - 1) // rows * rows
    pend = jnp.cumsum(padded)
    pstart = pend - padded
    seg_row = (pstart[None, :] + tcnt[:, 0, :N_EXPERTS]).reshape(-1)
    seg_n = tcnt[:, 1, :N_EXPERTS].reshape(-1)
    n_blocks = -(-(t * TOP_K + N_EXPERTS * (rows - 1)) // rows)
    block_start = jnp.arange(n_blocks, dtype=jnp.int32) * rows
    block_e = jnp.minimum(jnp.sum((pend[None, :] <= block_start[:, None]).astype(jnp.int32), axis=-1),
                          N_EXPERTS - 1)
    nb_used = (pend[-1:] // rows).astype(jnp.int32)

    ridx = ridx.reshape(t, LANES)
    xs = _dispatch(seg_n, seg_row, ridx, u2.reshape(t, D_MODEL),
                   jnp.zeros((n_blocks * rows * ROW_CHUNKS, LANES), F32))
    ys = _expert(block_e, nb_used, xs, p["w1"], p["b1"], p["w2"], p["b2"])
    out = _combine(seg_n, seg_row, ridx, rw.reshape(t, LANES), h.reshape(t, D_MODEL), mod, ys, s)
    return out.reshape(b, s, D_MODEL)


def kernel(x_prompt, x_sample, c_prompt, c_sample, w_ada, b_ada, norm1_g, norm2_g, w_in, q_gain, k_gain,
           w_attn_proj, w_fourier_proj, w_out, w_router, b_router, w1, b1, w2, b2):
    seq = x_prompt.shape[1]
    assert w_ada.shape[0] == 1, "single-layer trunk"
    w_in0 = w_in[0]
    gains = jnp.zeros((8, ATTN_WIDTH), F32)
    for g in range(N_GROUPS):
        gains = gains.at[2 * g].set(jnp.tile(q_gain[0, g], HEADS))
        gains = gains.at[2 * g + 1].set(jnp.tile(k_gain[0, g], HEADS))
    heads = jnp.arange(ATTN_WIDTH) // HEAD_DIM
    gmat = ((heads[:, None] == heads[None, :]).astype(F32) / HEAD_DIM).astype(BF16)
    expand = (jnp.arange(LANES)[:, None] == heads[None, :]).astype(BF16)
    ltri = (jnp.arange(MERGE_ROWS)[:, None] > jnp.arange(MERGE_ROWS)[None, :]).astype(BF16)
    utri = (jnp.arange(LANES)[:, None] < jnp.arange(LANES)[None, :]).astype(BF16)
    perms = []
    for d in DILATIONS[1:]:
        n = MERGE_ROWS // d
        src = (jnp.arange(MERGE_ROWS) % d) * n + jnp.arange(MERGE_ROWS) // d
        perms.append((src[:, None] == jnp.arange(MERGE_ROWS)[None, :]).astype(BF16))
    wr = jnp.zeros((D_MODEL, LANES), F32).at[:, :N_EXPERTS].set(w_router[0])
    wrh = wr.astype(BF16)
    wrl = (wr - wrh.astype(F32)).astype(BF16)
    br = jnp.zeros((1, LANES), F32).at[0, :N_EXPERTS].set(b_router[0])
    bd, cs, ns = _dft_tables(seq)
    p = dict(
        w_ada=w_ada[0], b_ada=b_ada[0], g1=norm1_g, g2=norm2_g,
        w_qkvf=w_in0[:, :QKVF_WIDTH].astype(BF16), wg=w_in0[:, QKVF_WIDTH:].astype(BF16),
        gains=gains, rope=_rope_tables(seq, INPROJ_ROWS), gmat=gmat, bd=bd, cs=cs, ns=ns,
        wap=w_attn_proj[0].astype(BF16), wfp=w_fourier_proj[0].astype(BF16), wout=w_out[0].astype(BF16),
        wrh=wrh, wrl=wrl, br=br, expand=expand, ltri=ltri, utri=utri, perms=perms,
        w1=w1[0].astype(BF16), b1=b1[0].reshape(N_EXPERTS, 1, 2 * D_FF),
        w2=w2[0].astype(BF16), b2=b2[0].reshape(N_EXPERTS, 1, D_MODEL),
    )
    return _trunk(x_prompt, c_prompt, p), _trunk(x_sample, c_sample, p)
```

```python
import functools

import jax
import jax.numpy as jnp
from jax import lax
from jax.experimental import pallas as pl
from jax.experimental.pallas import tpu as pltpu

F32 = jnp.float32
BF16 = jnp.bfloat16

D_MODEL = 1024
HEAD_DIM = 64
HEADS = 8
ATTN_WIDTH = HEADS * HEAD_DIM
DILATIONS = (1, 4, 16)
BAND = 64
N_GROUPS = 3
QKVF_WIDTH = 5120
FOURIER_GROUP_DIM = 64
ROPE_DIM = 16
ROPE_THETA = 500000.0
N_EXPERTS = 32
TOP_K = 4
D_FF = 1024
SWIGLU_ALPHA = 1.702
SWIGLU_LIMIT = 7.0
NORM_EPS = 1e-6
NEG_INF = -1e30
LN2 = 0.6931471805599453
LOG2E = 1.4426950408889634

LANES = 128
ROW_CHUNKS = D_MODEL // LANES
EXPERT_ROWS = 256
SEGMENT_PIECES = (256, 128, 64, 32, 16, 8, 4, 2, 1)
INPROJ_ROWS = 512
MERGE_ROWS = 256
VMEM_LIMIT = 56 * 1024 * 1024


def _dot(a, b):
    return jnp.dot(a, b, preferred_element_type=F32)


def _split_bf16(v):
    hi = v.astype(BF16)
    lo = (v - hi.astype(F32)).astype(BF16)
    return hi, lo


def _rms_mod(x, gain, scale, shift):
    ms = jnp.mean(x * x, axis=-1, keepdims=True)
    return x * lax.rsqrt(ms + NORM_EPS) * gain * (1.0 + scale) + shift


def _const_spec(shape):
    return pl.BlockSpec(shape, lambda *_: (0,) * len(shape), pipeline_mode=pl.Buffered(1))


def _ada_kernel(c_ref, w_ref, b_ref, o_ref):
    c = c_ref[...]
    a = c * jax.nn.sigmoid(c)
    a_hi, a_lo = _split_bf16(a)
    w_hi, w_lo = _split_bf16(w_ref[...])
    o_ref[...] = _dot(a_hi, w_hi) + _dot(a_lo, w_hi) + _dot(a_hi, w_lo) + b_ref[...]


def _ada(c, w_ada, b_ada):
    b = c.shape[0]
    n = w_ada.shape[1]
    tn = D_MODEL
    return pl.pallas_call(
        _ada_kernel,
        grid=(n // tn,),
        in_specs=[pl.BlockSpec((b, D_MODEL), lambda j: (0, 0)),
                  pl.BlockSpec((D_MODEL, tn), lambda j: (0, j)),
                  pl.BlockSpec((1, tn), lambda j: (0, j))],
        out_specs=pl.BlockSpec((b, tn), lambda j: (0, j)),
        out_shape=jax.ShapeDtypeStruct((b, n), F32),
        compiler_params=pltpu.CompilerParams(vmem_limit_bytes=VMEM_LIMIT),
        name="ada",
    )(c, w_ada, b_ada.reshape(1, n))


def _inproj_kernel(x_ref, mod_ref, g1_ref, w_ref, gain_ref, rope_ref, gmat_ref,
                   o0_ref, o1_ref, o2_ref, of_ref, u_scr):
    x = x_ref[0]
    ts = x.shape[0]
    mod = mod_ref[0]
    u = _rms_mod(x, g1_ref[...], mod[1:2], mod[0:1])
    ub = u.astype(BF16)
    for c in range(D_MODEL // LANES):
        u_scr[c] = u[:, c * LANES:(c + 1) * LANES]
    for g, o_ref in enumerate((o0_ref, o1_ref, o2_ref)):
        d = DILATIONS[g]
        n = ts // d
        if d == 1:
            ug = ub
        else:
            ug = jnp.concatenate(
                [jnp.concatenate([u_scr[c, pl.ds(r, n, stride=d), :] for r in range(d)], axis=0)
                 for c in range(D_MODEL // LANES)], axis=1).astype(BF16)
        cos_t, sin_prev, sin_next = rope_ref[g, 0], rope_ref[g, 1], rope_ref[g, 2]
        for j in range(3):
            c0 = (3 * g + j) * ATTN_WIDTH
            z = _dot(ug, w_ref[:, c0:c0 + ATTN_WIDTH])
            if j < 2:
                ms = _dot((z * z).astype(BF16), gmat_ref[...])
                gain = gain_ref[2 * g + j:2 * g + j + 1, :]
                if j == 0:
                    gain = gain * (HEAD_DIM ** -0.5 * LOG2E)
                z = z * lax.rsqrt(ms + NORM_EPS) * gain
                blocks = []
                for blk in range(ATTN_WIDTH // LANES):
                    zb = z[:, blk * LANES:(blk + 1) * LANES]
                    blocks.append(zb * cos_t
                                  + pltpu.roll(zb, ROPE_DIM // 2, 1) * sin_prev
                                  + pltpu.roll(zb, LANES - ROPE_DIM // 2, 1) * sin_next)
                z = jnp.concatenate(blocks, axis=1)
            zb16 = z.astype(BF16)
            for r in range(d):
                o_ref[0, r, :, j * ATTN_WIDTH:(j + 1) * ATTN_WIDTH] = zb16[r * n:(r + 1) * n, :]
    of_ref[0] = _dot(ub, w_ref[:, 9 * ATTN_WIDTH:]).astype(BF16)


def _inproj(x, mod, g1, w_qkvf, gains, rope, gmat, ts=INPROJ_ROWS):
    b, s, _ = x.shape
    qkv_w = 3 * ATTN_WIDTH
    out_specs = [pl.BlockSpec((1, d, ts // d, qkv_w), lambda i, j: (i, 0, j, 0)) for d in DILATIONS]
    out_shape = [jax.ShapeDtypeStruct((b, d, s // d, qkv_w), BF16) for d in DILATIONS]
    return pl.pallas_call(
        _inproj_kernel,
        grid=(b, s // ts),
        in_specs=[pl.BlockSpec((1, ts, D_MODEL), lambda i, j: (i, j, 0)),
                  pl.BlockSpec((1, 6, D_MODEL), lambda i, j: (i, 0, 0)),
                  _const_spec((1, D_MODEL)),
                  _const_spec((D_MODEL, QKVF_WIDTH)),
                  _const_spec((8, ATTN_WIDTH)),
                  pl.BlockSpec((N_GROUPS, 3, ts, LANES), lambda i, j: (0, 0, j, 0)),
                  _const_spec((ATTN_WIDTH, ATTN_WIDTH))],
        out_specs=out_specs + [pl.BlockSpec((1, ts, ATTN_WIDTH), lambda i, j: (i, j, 0))],
        out_shape=out_shape + [jax.ShapeDtypeStruct((b, s, ATTN_WIDTH), BF16)],
        scratch_shapes=[pltpu.VMEM((D_MODEL // LANES, ts, LANES), F32)],
        compiler_params=pltpu.CompilerParams(
            dimension_semantics=("arbitrary", "arbitrary"), vmem_limit_bytes=VMEM_LIMIT),
        name="inproj",
    )(x, mod, g1, w_qkvf, gains, rope, gmat)


def _attn_kernel(qkv_ref, o_ref, lse_ref, s_scr, p_scr, *, length):
    tq = LANES
    tk = min(2 * LANES, length)
    nq = length // tq
    n_pairs = ATTN_WIDTH // LANES
    lane = lax.broadcasted_iota(jnp.int32, (tq, LANES), 1)
    first_head = lane < HEAD_DIM
    first_head_k = lax.broadcasted_iota(jnp.int32, (tk, LANES), 1) < HEAD_DIM
    rel = (lax.broadcasted_iota(jnp.int32, (tq, tk), 1)
           - lax.broadcasted_iota(jnp.int32, (tq, tk), 0))

    def q_block(it, carry):
        r = it // nq
        i = it % nq
        q0 = pl.multiple_of(i * tq, tq)
        ws = pl.multiple_of(jnp.clip(i * tq - BAND, 0, length - tk), BAND)
        valid = jnp.abs(rel + (ws - i * tq)) <= BAND

        for p in range(n_pairs):
            q2 = qkv_ref[0, r, pl.ds(q0, tq), p * LANES:(p + 1) * LANES]
            k2 = qkv_ref[0, r, pl.ds(ws, tk), ATTN_WIDTH + p * LANES:ATTN_WIDTH + (p + 1) * LANES]
            for hh in range(2):
                sel = first_head if hh == 0 else jnp.logical_not(first_head)
                qa = jnp.where(sel, q2, jnp.zeros_like(q2))
                s = lax.dot_general(qa, k2, (((1,), (1,)), ((), ())), preferred_element_type=F32)
                s_scr[2 * p + hh] = jnp.where(valid, s, NEG_INF)

        m_tile = jnp.zeros((tq, LANES), F32)
        for h in range(HEADS):
            s = s_scr[h]
            m = jnp.max(s, axis=-1, keepdims=True)
            p_scr[h] = jnp.exp2(s - m).astype(BF16)
            m_tile = jnp.where(lane == h, m, m_tile)

        den_tile = jnp.ones((tq, LANES), F32)
        for p in range(n_pairs):
            cs = slice(2 * ATTN_WIDTH + p * LANES, 2 * ATTN_WIDTH + (p + 1) * LANES)
            v2 = qkv_ref[0, r, pl.ds(ws, tk), cs]
            one = jnp.ones_like(v2)
            o_a = _dot(p_scr[2 * p], jnp.where(first_head_k, v2, one))
            o_b = _dot(p_scr[2 * p + 1], jnp.where(first_head_k, one, v2))
            den_ba = jnp.where(first_head, o_b, o_a)
            den_ab = pltpu.roll(den_ba, HEAD_DIM, 1)
            o_ref[0, r, pl.ds(q0, tq), p * LANES:(p + 1) * LANES] = (
                jnp.where(first_head, o_a, o_b) / den_ab).astype(BF16)
            den_tile = jnp.where(lane == 2 * p, den_ab, den_tile)
            den_tile = jnp.where(lane == 2 * p + 1, den_ba, den_tile)
        lse_ref[0, r, pl.ds(q0, tq), :] = (m_tile + jnp.log2(den_tile)) * LN2
        return carry

    lax.fori_loop(0, qkv_ref.shape[1] * nq, q_block, 0)


def _attn(qkv, g):
    b, d, length, width = qkv.shape
    tk = min(2 * LANES, length)
    return pl.pallas_call(
        functools.partial(_attn_kernel, length=length),
        grid=(b,),
        in_specs=[pl.BlockSpec((1, d, length, width), lambda i: (i, 0, 0, 0))],
        out_specs=[pl.BlockSpec((1, d, length, ATTN_WIDTH), lambda i: (i, 0, 0, 0)),
                   pl.BlockSpec((1, d, length, LANES), lambda i: (i, 0, 0, 0))],
        out_shape=[jax.ShapeDtypeStruct((b, d, length, ATTN_WIDTH), BF16),
                   jax.ShapeDtypeStruct((b, d, length, LANES), F32)],
        scratch_shapes=[pltpu.VMEM((HEADS, LANES, tk), F32), pltpu.VMEM((HEADS, LANES, tk), BF16)],
        compiler_params=pltpu.CompilerParams(
            dimension_semantics=("arbitrary",), vmem_limit_bytes=VMEM_LIMIT),
        name=f"attn{g}",
    )(qkv)


def _fourier_kernel(x_ref, bd_ref, cs_ref, ns_ref, o_ref, y_ref, *, seq, rows):
    for r in range(seq // rows):
        rs = slice(r * rows, (r + 1) * rows)
        y_ref[rs, :] = _dot(x_ref[0, rs, :], bd_ref[...]).astype(BF16)
    for r in range(seq // rows):
        rs = slice(r * rows, (r + 1) * rows)
        o = _dot(cs_ref[rs, :], y_ref[:, :ATTN_WIDTH]) + _dot(ns_ref[rs, :], y_ref[:, ATTN_WIDTH:])
        o_ref[0, rs, :] = o.astype(BF16)


def _fourier(qkvf, bd, cs, ns):
    b, s, _ = qkvf.shape
    return pl.pallas_call(
        functools.partial(_fourier_kernel, seq=s, rows=256),
        grid=(b,),
        in_specs=[pl.BlockSpec((1, s, ATTN_WIDTH), lambda i: (i, 0, 0)),
                  _const_spec((ATTN_WIDTH, 2 * ATTN_WIDTH)),
                  _const_spec((s, s)),
                  _const_spec((s, s))],
        out_specs=pl.BlockSpec((1, s, ATTN_WIDTH), lambda i: (i, 0, 0)),
        out_shape=jax.ShapeDtypeStruct((b, s, ATTN_WIDTH), BF16),
        scratch_shapes=[pltpu.VMEM((s, 2 * ATTN_WIDTH), BF16)],
        compiler_params=pltpu.CompilerParams(
            dimension_semantics=("arbitrary",), vmem_limit_bytes=VMEM_LIMIT),
        name="fourier",
    )(qkvf, bd, cs, ns)


def _merge_kernel(x_ref, mod_ref, o0_ref, o1_ref, o2_ref, l0_ref, l1_ref, l2_ref, fm_ref,
                  g1_ref, g2_ref, wg_ref, wap_ref, wfp_ref, wout_ref, wrh_ref, wrl_ref, br_ref,
                  expand_ref, ltri_ref, utri_ref, perm1_ref, perm2_ref,
                  h_ref, u2_ref, ridx_ref, rw_ref, tcnt_ref, cnt_ref, cnt_scr):
    @pl.when((pl.program_id(0) == 0) & (pl.program_id(1) == 0))
    def _():
        cnt_scr[...] = jnp.zeros_like(cnt_scr)

    x = x_ref[0]
    ts = x.shape[0]
    mod = mod_ref[0]
    shift1, scale1, gate1 = mod[0:1], mod[1:2], mod[2:3]
    shift2, scale2 = mod[3:4], mod[4:5]
    u = _rms_mod(x, g1_ref[...], scale1, shift1).astype(BF16)
    gates = _dot(u, wg_ref[...])

    os_ = [o0_ref[0, 0].astype(F32)]
    lses = [l0_ref[0, 0]]
    for o_ref, l_ref, perm_ref in ((o1_ref, l1_ref, perm1_ref), (o2_ref, l2_ref, perm2_ref)):
        perm = perm_ref[...]
        os_.append(_dot(perm, o_ref[0].reshape(ts, ATTN_WIDTH)))
        l_hi, l_lo = _split_bf16(l_ref[0].reshape(ts, LANES))
        lses.append(_dot(perm, l_hi) + _dot(perm, l_lo))

    mx = jnp.maximum(jnp.maximum(lses[0], lses[1]), lses[2])
    es = [jnp.exp(l - mx) for l in lses]
    den = es[0] + es[1] + es[2]
    attn = None
    for e, o in zip(es, os_):
        w_hi, w_lo = _split_bf16(e / den)
        wf = _dot(w_hi, expand_ref[...]) + _dot(w_lo, expand_ref[...])
        term = wf * o
        attn = term if attn is None else attn + term

    a = _dot(attn.astype(BF16), wap_ref[...])
    f = _dot(fm_ref[0], wfp_ref[...])
    merged = jax.nn.sigmoid(gates[:, :D_MODEL]) * a + jax.nn.sigmoid(gates[:, D_MODEL:]) * f
    h = x + gate1 * _dot(merged.astype(BF16), wout_ref[...])
    h_ref[0] = h

    u2 = _rms_mod(h, g2_ref[...], scale2, shift2)
    u_hi, u_lo = _split_bf16(u2)
    u2_ref[0] = u_hi
    logits = (_dot(u_hi, wrh_ref[...]) + _dot(u_lo, wrh_ref[...]) + _dot(u_hi, wrl_ref[...])
              + br_ref[...])

    lane = lax.broadcasted_iota(jnp.int32, (ts, LANES), 1)
    lg = jnp.where(lane < N_EXPERTS, logits, -jnp.inf)
    vals, ids = [], []
    onehot = jnp.zeros((ts, LANES), F32)
    for _ in range(TOP_K):
        m = jnp.max(lg, axis=-1, keepdims=True)
        idx = jnp.min(jnp.where(lg == m, lane, LANES), axis=-1, keepdims=True)
        hit = lane == idx
        vals.append(m)
        ids.append(idx)
        onehot = jnp.where(hit, 1.0, onehot)
        lg = jnp.where(hit, -jnp.inf, lg)
    exps = [jnp.exp(v - vals[0]) for v in vals]
    esum = exps[0] + exps[1] + exps[2] + exps[3]

    n_tile = jnp.sum(onehot, axis=0, keepdims=True)
    seg_start = _dot(jnp.broadcast_to(n_tile, (8, LANES)).astype(BF16), utri_ref[...])[0:1, :]
    pos_all = _dot(ltri_ref[...], onehot.astype(BF16)) + seg_start
    ridx = jnp.zeros((ts, LANES), F32)
    rw = jnp.zeros((ts, LANES), F32)
    for k in range(TOP_K):
        pos = jnp.sum(jnp.where(lane == ids[k], pos_all, 0.0), axis=-1, keepdims=True)
        ridx = jnp.where(lane == k, ids[k].astype(F32), ridx)
        ridx = jnp.where(lane == TOP_K + k, pos, ridx)
        rw = jnp.where(lane == k, exps[k] / esum, rw)
    ridx_ref[0] = ridx.astype(jnp.int32)
    rw_ref[0] = rw
    row = lax.broadcasted_iota(jnp.int32, (8, LANES), 0)
    tcnt = jnp.where(row == 0, cnt_scr[...], jnp.where(row == 1, jnp.broadcast_to(n_tile, (8, LANES)), 0.0))
    tcnt_ref[0] = tcnt.astype(jnp.int32)
    cnt_scr[...] = cnt_scr[...] + n_tile
    cnt_ref[...] = cnt_scr[...]


def _merge(x, mod, os_, lses, fm, g1, g2, wg, wap, wfp, wout, wrh, wrl, br, expand, ltri, utri, perms,
           ts=MERGE_ROWS):
    b, s, _ = x.shape
    tile = lambda w: pl.BlockSpec((1, ts, w), lambda i, j: (i, j, 0))
    strided = lambda w: [pl.BlockSpec((1, d, ts // d, w), lambda i, j: (i, 0, j, 0)) for d in DILATIONS]
    consts = (g1, g2, wg, wap, wfp, wout, wrh, wrl, br, expand, ltri, utri) + tuple(perms)
    in_specs = ([tile(D_MODEL), pl.BlockSpec((1, 6, D_MODEL), lambda i, j: (i, 0, 0))]
                + strided(ATTN_WIDTH) + strided(LANES) + [tile(ATTN_WIDTH)]
                + [_const_spec(a.shape) for a in consts])
    per_seq = s // ts
    return pl.pallas_call(
        _merge_kernel,
        grid=(b, per_seq),
        in_specs=in_specs,
        out_specs=[tile(D_MODEL), tile(D_MODEL), tile(LANES), tile(LANES),
                   pl.BlockSpec((1, 8, LANES), lambda i, j: (i * per_seq + j, 0, 0)),
                   pl.BlockSpec((8, LANES), lambda i, j: (0, 0))],
        out_shape=[jax.ShapeDtypeStruct((b, s, D_MODEL), F32),
                   jax.ShapeDtypeStruct((b, s, D_MODEL), BF16),
                   jax.ShapeDtypeStruct((b, s, LANES), jnp.int32),
                   jax.ShapeDtypeStruct((b, s, LANES), F32),
                   jax.ShapeDtypeStruct((b * per_seq, 8, LANES), jnp.int32),
                   jax.ShapeDtypeStruct((8, LANES), F32)],
        scratch_shapes=[pltpu.VMEM((8, LANES), F32)],
        compiler_params=pltpu.CompilerParams(
            dimension_semantics=("arbitrary", "arbitrary"), vmem_limit_bytes=VMEM_LIMIT),
        name="merge",
    )(x, mod, *os_, *lses, fm, *consts)


def _to_row_chunks(ref, mat):
    for c in range(ROW_CHUNKS):
        ref[pl.ds(c, mat.shape[0], stride=ROW_CHUNKS), :] = mat[:, c * LANES:(c + 1) * LANES]


def _from_row_chunks(ref):
    n = ref.shape[0] // ROW_CHUNKS
    return jnp.concatenate([ref[pl.ds(c, n, stride=ROW_CHUNKS), :] for c in range(ROW_CHUNKS)], axis=1)


def _segment_copies(seg_n_ref, seg_row_ref, tile, sorted_ref, hbm_ref, sem, *, to_hbm, start, consecutive=True):
    base = tile * N_EXPERTS

    def segment(e, src):
        n = seg_n_ref[base + e]
        dst = seg_row_ref[base + e]
        for bit in SEGMENT_PIECES:
            off = n & (-2 * bit)

            @pl.when((n & bit) != 0)
            def _():
                vm_row = (src + off) if consecutive else 0
                vm = sorted_ref.at[pl.ds(pl.multiple_of(vm_row * ROW_CHUNKS, ROW_CHUNKS), bit * ROW_CHUNKS)]
                hb = hbm_ref.at[pl.ds(pl.multiple_of((dst + off) * ROW_CHUNKS, ROW_CHUNKS), bit * ROW_CHUNKS)]
                cp = pltpu.make_async_copy(vm, hb, sem) if to_hbm else pltpu.make_async_copy(hb, vm, sem)
                if start:
                    cp.start()
                else:
                    cp.wait()
        return src + n

    lax.fori_loop(0, N_EXPERTS, segment, 0)


def _dispatch_kernel(seg_n_ref, seg_row_ref, ridx_ref, u2_ref, xs_ref, sorted_scr, sems):
    tile = pl.program_id(0)
    n_tiles = pl.num_programs(0)
    slot = tile % 2
    tt = u2_ref.shape[0]
    rows = TOP_K * tt

    def copies(tile_, slot_, start):
        _segment_copies(seg_n_ref, seg_row_ref, tile_, sorted_scr.at[slot_], xs_ref, sems.at[slot_],
                        to_hbm=True, start=start)

    @pl.when(tile == 0)
    def _():
        sorted_scr[1, 0:max(SEGMENT_PIECES) * ROW_CHUNKS, :] = jnp.zeros(
            (max(SEGMENT_PIECES) * ROW_CHUNKS, LANES), F32)
        for start in (True, False):
            _segment_copies(seg_n_ref, seg_row_ref, n_tiles, sorted_scr.at[1], xs_ref, sems.at[1],
                            to_hbm=True, start=start, consecutive=False)

    pos_t = ridx_ref[...].astype(F32).T
    j = lax.broadcasted_iota(jnp.int32, (rows, tt), 0).astype(F32)
    onehot = jnp.zeros((rows, tt), F32)
    for k in range(TOP_K):
        onehot = jnp.where(j == pos_t[TOP_K + k:TOP_K + k + 1, :], 1.0, onehot)
    _to_row_chunks(sorted_scr.at[slot], _dot(onehot.astype(BF16), u2_ref[...]))
    copies(tile, slot, True)

    @pl.when(tile > 0)
    def _():
        copies(tile - 1, 1 - slot, False)

    @pl.when(tile == n_tiles - 1)
    def _():
        copies(tile, slot, False)


def _dispatch(seg_n, seg_row, ridx, u2, n_rows, tt=MERGE_ROWS):
    t = u2.shape[0]
    grid_spec = pltpu.PrefetchScalarGridSpec(
        num_scalar_prefetch=2,
        grid=(t // tt,),
        in_specs=[pl.BlockSpec((tt, LANES), lambda i, *_: (i, 0)),
                  pl.BlockSpec((tt, D_MODEL), lambda i, *_: (i, 0))],
        out_specs=pl.BlockSpec(memory_space=pl.ANY),
        scratch_shapes=[pltpu.VMEM((2, TOP_K * tt * ROW_CHUNKS, LANES), F32), pltpu.SemaphoreType.DMA((2,))],
    )
    return pl.pallas_call(
        _dispatch_kernel,
        grid_spec=grid_spec,
        out_shape=jax.ShapeDtypeStruct((n_rows * ROW_CHUNKS, LANES), F32),
        compiler_params=pltpu.CompilerParams(
            dimension_semantics=("arbitrary",), has_side_effects=True, vmem_limit_bytes=VMEM_LIMIT),
        name="dispatch",
    )(seg_n, seg_row, ridx, u2)


def _expert_kernel(be_ref, nb_ref, xs_ref, w1_ref, b1_ref, w2_ref, b2_ref, ys_ref):
    del be_ref
    i = pl.program_id(0)

    @pl.when(i < nb_ref[0])
    def _():
        hdn = _dot(_from_row_chunks(xs_ref).astype(BF16), w1_ref[0]) + b1_ref[0]
        glu = jnp.minimum(hdn[:, :D_FF], SWIGLU_LIMIT)
        lin = jnp.clip(hdn[:, D_FF:], -SWIGLU_LIMIT, SWIGLU_LIMIT)
        act = (lin + 1.0) * (glu * jax.nn.sigmoid(SWIGLU_ALPHA * glu))
        _to_row_chunks(ys_ref, _dot(act.astype(BF16), w2_ref[0]) + b2_ref[0])

    @pl.when(i >= nb_ref[0])
    def _():
        ys_ref[...] = jnp.zeros_like(ys_ref)


def _expert(block_e, nb_used, xs, w1, b1, w2, b2):
    block = (EXPERT_ROWS * ROW_CHUNKS, LANES)
    n_blocks = xs.shape[0] // block[0]
    grid_spec = pltpu.PrefetchScalarGridSpec(
        num_scalar_prefetch=2,
        grid=(n_blocks,),
        in_specs=[pl.BlockSpec(block, lambda i, be, nb: (jnp.minimum(i, nb[0] - 1), 0)),
                  pl.BlockSpec((1, D_MODEL, 2 * D_FF), lambda i, be, nb: (be[i], 0, 0)),
                  pl.BlockSpec((1, 1, 2 * D_FF), lambda i, be, nb: (be[i], 0, 0)),
                  pl.BlockSpec((1, D_FF, D_MODEL), lambda i, be, nb: (be[i], 0, 0)),
                  pl.BlockSpec((1, 1, D_MODEL), lambda i, be, nb: (be[i], 0, 0))],
        out_specs=pl.BlockSpec(block, lambda i, be, nb: (i, 0)),
    )
    return pl.pallas_call(
        _expert_kernel,
        grid_spec=grid_spec,
        out_shape=jax.ShapeDtypeStruct(xs.shape, F32),
        compiler_params=pltpu.CompilerParams(
            dimension_semantics=("arbitrary",), vmem_limit_bytes=VMEM_LIMIT),
        name="expert",
    )(block_e, nb_used, xs, w1, b1, w2, b2)


def _combine_kernel(seg_n_ref, seg_row_ref, ridx_ref, rw_ref, h_ref, mod_ref, ys_ref, o_ref, sorted_scr, sems):
    tile = pl.program_id(0)
    n_tiles = pl.num_programs(0)
    slot = tile % 2
    tt = h_ref.shape[0]
    rows = TOP_K * tt

    def copies(tile_, slot_, start):
        _segment_copies(seg_n_ref, seg_row_ref, tile_, sorted_scr.at[slot_], ys_ref, sems.at[slot_],
                        to_hbm=False, start=start)

    @pl.when(tile == 0)
    def _():
        copies(tile, slot, True)

    @pl.when(tile + 1 < n_tiles)
    def _():
        copies(tile + 1, 1 - slot, True)

    ridx = ridx_ref[...].astype(F32)
    rw = rw_ref[...]
    j = lax.broadcasted_iota(jnp.int32, (tt, rows), 1).astype(F32)
    wmat = jnp.zeros((tt, rows), F32)
    for k in range(TOP_K):
        wmat = jnp.where(j == ridx[:, TOP_K + k:TOP_K + k + 1], rw[:, k:k + 1], wmat)
    w_hi, w_lo = _split_bf16(wmat)
    copies(tile, slot, False)
    y = _from_row_chunks(sorted_scr.at[slot]).astype(BF16)
    o_ref[...] = h_ref[...] + mod_ref[0][5:6] * (_dot(w_hi, y) + _dot(w_lo, y))


def _combine(seg_n, seg_row, ridx, rw, h, mod, ys, seq, tt=MERGE_ROWS):
    t = h.shape[0]
    per_seq = seq // tt
    grid_spec = pltpu.PrefetchScalarGridSpec(
        num_scalar_prefetch=2,
        grid=(t // tt,),
        in_specs=[pl.BlockSpec((tt, LANES), lambda i, *_: (i, 0)),
                  pl.BlockSpec((tt, LANES), lambda i, *_: (i, 0)),
                  pl.BlockSpec((tt, D_MODEL), lambda i, *_: (i, 0)),
                  pl.BlockSpec((1, 6, D_MODEL), lambda i, *_: (i // per_seq, 0, 0)),
                  pl.BlockSpec(memory_space=pl.ANY)],
        out_specs=pl.BlockSpec((tt, D_MODEL), lambda i, *_: (i, 0)),
        scratch_shapes=[pltpu.VMEM((2, TOP_K * tt * ROW_CHUNKS, LANES), F32), pltpu.SemaphoreType.DMA((2,))],
    )
    return pl.pallas_call(
        _combine_kernel,
        grid_spec=grid_spec,
        out_shape=jax.ShapeDtypeStruct((t, D_MODEL), F32),
        compiler_params=pltpu.CompilerParams(
            dimension_semantics=("arbitrary",), vmem_limit_bytes=VMEM_LIMIT),
        name="combine",
    )(seg_n, seg_row, ridx, rw, h, mod, ys)


def _rope_tables(seq, ts):
    half = ROPE_DIM // 2
    inv_freq = jnp.power(jnp.float32(ROPE_THETA), -jnp.arange(half, dtype=F32) * 2.0 / ROPE_DIM)
    ang = jnp.arange(seq, dtype=F32)[:, None] * inv_freq[None, :]
    cos, sin = jnp.cos(ang), jnp.sin(ang)
    ones = jnp.ones((seq, HEAD_DIM - ROPE_DIM), F32)
    zeros = jnp.zeros_like(ones)
    zh = jnp.zeros((seq, half), F32)
    cos_t = jnp.concatenate([cos, cos, ones], axis=1)
    sin_prev = jnp.concatenate([zh, sin, zeros], axis=1)
    sin_next = jnp.concatenate([-sin, zh, zeros], axis=1)
    tables = jnp.stack([jnp.tile(t, (1, LANES // HEAD_DIM)) for t in (cos_t, sin_prev, sin_next)])
    per_group = []
    for d in DILATIONS:
        pos = jnp.arange(seq).reshape(seq // ts, ts // d, d).transpose(0, 2, 1).reshape(seq)
        per_group.append(tables[:, pos, :])
    return jnp.stack(per_group)


def _dft_tables(seq):
    k = jnp.arange(seq, dtype=jnp.int32)
    ang = ((k[:, None] * k[None, :]) % seq).astype(F32) * (2.0 * jnp.pi / seq)
    scale = seq ** -0.5
    cs = jnp.cos(ang) * scale
    ns = -jnp.sin(ang) * scale
    c = jnp.arange(FOURIER_GROUP_DIM, dtype=jnp.int32)
    angc = ((c[:, None] * c[None, :]) % FOURIER_GROUP_DIM).astype(F32) * (2.0 * jnp.pi / FOURIER_GROUP_DIM)
    eye = jnp.eye(ATTN_WIDTH // FOURIER_GROUP_DIM, dtype=F32)
    cscale = FOURIER_GROUP_DIM ** -0.5
    bd = jnp.concatenate([jnp.kron(eye, jnp.cos(angc) * cscale),
                          jnp.kron(eye, jnp.sin(angc) * cscale)], axis=1)
    return bd.astype(BF16), cs.astype(BF16), ns.astype(BF16)


def _trunk(x, c, p):
    b, s, _ = x.shape
    t = b * s
    mod = _ada(c, p["w_ada"], p["b_ada"]).reshape(b, 6, D_MODEL)
    *qkvs, fz = _inproj(x, mod, p["g1"], p["w_qkvf"], p["gains"], p["rope"], p["gmat"])
    os_, lses = zip(*[_attn(qkv, g) for g, qkv in enumerate(qkvs)])
    fm = _fourier(fz, p["bd"], p["cs"], p["ns"])
    h, u2, ridx, rw, tcnt, cnt = _merge(
        x, mod, os_, lses, fm, p["g1"], p["g2"], p["wg"], p["wap"], p["wfp"], p["wout"], p["wrh"],
        p["wrl"], p["br"], p["expand"], p["ltri"], p["utri"], p["perms"])

    rows = EXPERT_ROWS
    counts = cnt[0, :N_EXPERTS].astype(jnp.int32)
    padded = (counts + rows - 1) // rows * rows
    pend = jnp.cumsum(padded)
    pstart = pend - padded
    seg_row = jnp.concatenate([pstart[None, :] + tcnt[:, 0, :N_EXPERTS], (pstart + counts)[None, :]]).reshape(-1)
    seg_n = jnp.concatenate([tcnt[:, 1, :N_EXPERTS], (padded - counts)[None, :]]).reshape(-1)
    n_blocks = -(-(t * TOP_K + N_EXPERTS * (rows - 1)) // rows)
    block_start = jnp.arange(n_blocks, dtype=jnp.int32) * rows
    block_e = jnp.minimum(jnp.sum((pend[None, :] <= block_start[:, None]).astype(jnp.int32), axis=-1),
                          N_EXPERTS - 1)
    nb_used = (pend[-1:] // rows).astype(jnp.int32)

    ridx = ridx.reshape(t, LANES)
    xs = _dispatch(seg_n, seg_row, ridx, u2.reshape(t, D_MODEL), n_blocks * rows)
    ys = _expert(block_e, nb_used, xs, p["w1"], p["b1"], p["w2"], p["b2"])
    out = _combine(seg_n, seg_row, ridx, rw.reshape(t, LANES), h.reshape(t, D_MODEL), mod, ys, s)
    return out.reshape(b, s, D_MODEL)


def kernel(x_prompt, x_sample, c_prompt, c_sample, w_ada, b_ada, norm1_g, norm2_g, w_in, q_gain, k_gain,
           w_attn_proj, w_fourier_proj, w_out, w_router, b_router, w1, b1, w2, b2):
    seq = x_prompt.shape[1]
    assert w_ada.shape[0] == 1, "single-layer trunk"
    w_in0 = w_in[0]
    gains = jnp.zeros((8, ATTN_WIDTH), F32)
    for g in range(N_GROUPS):
        gains = gains.at[2 * g].set(jnp.tile(q_gain[0, g], HEADS))
        gains = gains.at[2 * g + 1].set(jnp.tile(k_gain[0, g], HEADS))
    heads = jnp.arange(ATTN_WIDTH) // HEAD_DIM
    gmat = ((heads[:, None] == heads[None, :]).astype(F32) / HEAD_DIM).astype(BF16)
    expand = (jnp.arange(LANES)[:, None] == heads[None, :]).astype(BF16)
    ltri = (jnp.arange(MERGE_ROWS)[:, None] > jnp.arange(MERGE_ROWS)[None, :]).astype(BF16)
    utri = (jnp.arange(LANES)[:, None] < jnp.arange(LANES)[None, :]).astype(BF16)
    perms = []
    for d in DILATIONS[1:]:
        n = MERGE_ROWS // d
        src = (jnp.arange(MERGE_ROWS) % d) * n + jnp.arange(MERGE_ROWS) // d
        perms.append((src[:, None] == jnp.arange(MERGE_ROWS)[None, :]).astype(BF16))
    wr = jnp.zeros((D_MODEL, LANES), F32).at[:, :N_EXPERTS].set(w_router[0])
    wrh = wr.astype(BF16)
    wrl = (wr - wrh.astype(F32)).astype(BF16)
    br = jnp.zeros((1, LANES), F32).at[0, :N_EXPERTS].set(b_router[0])
    bd, cs, ns = _dft_tables(seq)
    p = dict(
        w_ada=w_ada[0], b_ada=b_ada[0], g1=norm1_g, g2=norm2_g,
        w_qkvf=w_in0[:, :QKVF_WIDTH].astype(BF16), wg=w_in0[:, QKVF_WIDTH:].astype(BF16),
        gains=gains, rope=_rope_tables(seq, INPROJ_ROWS), gmat=gmat, bd=bd, cs=cs, ns=ns,
        wap=w_attn_proj[0].astype(BF16), wfp=w_fourier_proj[0].astype(BF16), wout=w_out[0].astype(BF16),
        wrh=wrh, wrl=wrl, br=br, expand=expand, ltri=ltri, utri=utri, perms=perms,
        w1=w1[0].astype(BF16), b1=b1[0].reshape(N_EXPERTS, 1, 2 * D_FF),
        w2=w2[0].astype(BF16), b2=b2[0].reshape(N_EXPERTS, 1, D_MODEL),
    )
    return _trunk(x_prompt, c_prompt, p), _trunk(x_sample, c_sample, p)
```

```python
import functools

import jax
import jax.numpy as jnp
from jax import lax
from jax.experimental import pallas as pl
from jax.experimental.pallas import tpu as pltpu

F32 = jnp.float32
BF16 = jnp.bfloat16

D_MODEL = 1024
HEAD_DIM = 64
HEADS = 8
ATTN_WIDTH = HEADS * HEAD_DIM
DILATIONS = (1, 4, 16)
BAND = 64
N_GROUPS = 3
QKVF_WIDTH = 5120
FOURIER_GROUP_DIM = 64
ROPE_DIM = 16
ROPE_THETA = 500000.0
N_EXPERTS = 32
TOP_K = 4
D_FF = 1024
SWIGLU_ALPHA = 1.702
SWIGLU_LIMIT = 7.0
NORM_EPS = 1e-6
NEG_INF = -1e30
LN2 = 0.6931471805599453
LOG2E = 1.4426950408889634

LANES = 128
ROW_WORDS = D_MODEL // LANES // 2
SORTED_ROWS = 1088
EXPERT_ROWS = 256
SEGMENT_PIECES = (256, 128, 64, 32, 16, 8, 4, 2)
INPROJ_ROWS = 512
MERGE_ROWS = 256
VMEM_LIMIT = 56 * 1024 * 1024


def _dot(a, b):
    return jnp.dot(a, b, preferred_element_type=F32)


def _split_bf16(v):
    hi = v.astype(BF16)
    lo = (v - hi.astype(F32)).astype(BF16)
    return hi, lo


def _rms_mod(x, gain, scale, shift):
    ms = jnp.mean(x * x, axis=-1, keepdims=True)
    return x * lax.rsqrt(ms + NORM_EPS) * gain * (1.0 + scale) + shift


def _const_spec(shape):
    return pl.BlockSpec(shape, lambda *_: (0,) * len(shape), pipeline_mode=pl.Buffered(1))


def _ada_kernel(c_ref, w_ref, b_ref, o_ref):
    c = c_ref[...]
    a = c * jax.nn.sigmoid(c)
    a_hi, a_lo = _split_bf16(a)
    w_hi, w_lo = _split_bf16(w_ref[...])
    o_ref[...] = _dot(a_hi, w_hi) + _dot(a_lo, w_hi) + _dot(a_hi, w_lo) + b_ref[...]


def _ada(c, w_ada, b_ada):
    b = c.shape[0]
    n = w_ada.shape[1]
    tn = D_MODEL
    return pl.pallas_call(
        _ada_kernel,
        grid=(n // tn,),
        in_specs=[pl.BlockSpec((b, D_MODEL), lambda j: (0, 0)),
                  pl.BlockSpec((D_MODEL, tn), lambda j: (0, j)),
                  pl.BlockSpec((1, tn), lambda j: (0, j))],
        out_specs=pl.BlockSpec((b, tn), lambda j: (0, j)),
        out_shape=jax.ShapeDtypeStruct((b, n), F32),
        compiler_params=pltpu.CompilerParams(vmem_limit_bytes=VMEM_LIMIT),
        name="ada",
    )(c, w_ada, b_ada.reshape(1, n))


def _inproj_kernel(x_ref, mod_ref, g1_ref, w_ref, gain_ref, rope_ref, gmat_ref,
                   o0_ref, o1_ref, o2_ref, of_ref, u_scr):
    x = x_ref[0]
    ts = x.shape[0]
    mod = mod_ref[0]
    u = _rms_mod(x, g1_ref[...], mod[1:2], mod[0:1])
    ub = u.astype(BF16)
    for c in range(D_MODEL // LANES):
        u_scr[c] = u[:, c * LANES:(c + 1) * LANES]
    for g, o_ref in enumerate((o0_ref, o1_ref, o2_ref)):
        d = DILATIONS[g]
        n = ts // d
        if d == 1:
            ug = ub
        else:
            ug = jnp.concatenate(
                [jnp.concatenate([u_scr[c, pl.ds(r, n, stride=d), :] for r in range(d)], axis=0)
                 for c in range(D_MODEL // LANES)], axis=1).astype(BF16)
        cos_t, sin_prev, sin_next = rope_ref[g, 0], rope_ref[g, 1], rope_ref[g, 2]
        for j in range(3):
            c0 = (3 * g + j) * ATTN_WIDTH
            z = _dot(ug, w_ref[:, c0:c0 + ATTN_WIDTH])
            if j < 2:
                ms = _dot((z * z).astype(BF16), gmat_ref[...])
                gain = gain_ref[2 * g + j:2 * g + j + 1, :]
                if j == 0:
                    gain = gain * (HEAD_DIM ** -0.5 * LOG2E)
                z = z * lax.rsqrt(ms + NORM_EPS) * gain
                blocks = []
                for blk in range(ATTN_WIDTH // LANES):
                    zb = z[:, blk * LANES:(blk + 1) * LANES]
                    blocks.append(zb * cos_t
                                  + pltpu.roll(zb, ROPE_DIM // 2, 1) * sin_prev
                                  + pltpu.roll(zb, LANES - ROPE_DIM // 2, 1) * sin_next)
                z = jnp.concatenate(blocks, axis=1)
            zb16 = z.astype(BF16)
            for r in range(d):
                o_ref[0, r, :, j * ATTN_WIDTH:(j + 1) * ATTN_WIDTH] = zb16[r * n:(r + 1) * n, :]
    of_ref[0] = _dot(ub, w_ref[:, 9 * ATTN_WIDTH:]).astype(BF16)


def _inproj(x, mod, g1, w_qkvf, gains, rope, gmat, ts=INPROJ_ROWS):
    b, s, _ = x.shape
    qkv_w = 3 * ATTN_WIDTH
    out_specs = [pl.BlockSpec((1, d, ts // d, qkv_w), lambda i, j: (i, 0, j, 0)) for d in DILATIONS]
    out_shape = [jax.ShapeDtypeStruct((b, d, s // d, qkv_w), BF16) for d in DILATIONS]
    return pl.pallas_call(
        _inproj_kernel,
        grid=(b, s // ts),
        in_specs=[pl.BlockSpec((1, ts, D_MODEL), lambda i, j: (i, j, 0)),
                  pl.BlockSpec((1, 6, D_MODEL), lambda i, j: (i, 0, 0)),
                  _const_spec((1, D_MODEL)),
                  _const_spec((D_MODEL, QKVF_WIDTH)),
                  _const_spec((8, ATTN_WIDTH)),
                  pl.BlockSpec((N_GROUPS, 3, ts, LANES), lambda i, j: (0, 0, j, 0)),
                  _const_spec((ATTN_WIDTH, ATTN_WIDTH))],
        out_specs=out_specs + [pl.BlockSpec((1, ts, ATTN_WIDTH), lambda i, j: (i, j, 0))],
        out_shape=out_shape + [jax.ShapeDtypeStruct((b, s, ATTN_WIDTH), BF16)],
        scratch_shapes=[pltpu.VMEM((D_MODEL // LANES, ts, LANES), F32)],
        compiler_params=pltpu.CompilerParams(
            dimension_semantics=("arbitrary", "arbitrary"), vmem_limit_bytes=VMEM_LIMIT),
        name="inproj",
    )(x, mod, g1, w_qkvf, gains, rope, gmat)


def _attn_kernel(qkv_ref, o_ref, lse_ref, s_scr, p_scr, *, length):
    tq = LANES
    tk = min(2 * LANES, length)
    nq = length // tq
    n_pairs = ATTN_WIDTH // LANES
    lane = lax.broadcasted_iota(jnp.int32, (tq, LANES), 1)
    first_head = lane < HEAD_DIM
    first_head_k = lax.broadcasted_iota(jnp.int32, (tk, LANES), 1) < HEAD_DIM
    rel = (lax.broadcasted_iota(jnp.int32, (tq, tk), 1)
           - lax.broadcasted_iota(jnp.int32, (tq, tk), 0))

    def q_block(it, carry):
        r = it // nq
        i = it % nq
        q0 = pl.multiple_of(i * tq, tq)
        ws = pl.multiple_of(jnp.clip(i * tq - BAND, 0, length - tk), BAND)
        valid = jnp.abs(rel + (ws - i * tq)) <= BAND

        for p in range(n_pairs):
            q2 = qkv_ref[0, r, pl.ds(q0, tq), p * LANES:(p + 1) * LANES]
            k2 = qkv_ref[0, r, pl.ds(ws, tk), ATTN_WIDTH + p * LANES:ATTN_WIDTH + (p + 1) * LANES]
            for hh in range(2):
                sel = first_head if hh == 0 else jnp.logical_not(first_head)
                qa = jnp.where(sel, q2, jnp.zeros_like(q2))
                s = lax.dot_general(qa, k2, (((1,), (1,)), ((), ())), preferred_element_type=F32)
                s_scr[2 * p + hh] = jnp.where(valid, s, NEG_INF)

        m_tile = jnp.zeros((tq, LANES), F32)
        for h in range(HEADS):
            s = s_scr[h]
            m = jnp.max(s, axis=-1, keepdims=True)
            p_scr[h] = jnp.exp2(s - m).astype(BF16)
            m_tile = jnp.where(lane == h, m, m_tile)

        den_tile = jnp.ones((tq, LANES), F32)
        for p in range(n_pairs):
            cs = slice(2 * ATTN_WIDTH + p * LANES, 2 * ATTN_WIDTH + (p + 1) * LANES)
            v2 = qkv_ref[0, r, pl.ds(ws, tk), cs]
            one = jnp.ones_like(v2)
            o_a = _dot(p_scr[2 * p], jnp.where(first_head_k, v2, one))
            o_b = _dot(p_scr[2 * p + 1], jnp.where(first_head_k, one, v2))
            den_ba = jnp.where(first_head, o_b, o_a)
            den_ab = pltpu.roll(den_ba, HEAD_DIM, 1)
            o_ref[0, r, pl.ds(q0, tq), p * LANES:(p + 1) * LANES] = (
                jnp.where(first_head, o_a, o_b) / den_ab).astype(BF16)
            den_tile = jnp.where(lane == 2 * p, den_ab, den_tile)
            den_tile = jnp.where(lane == 2 * p + 1, den_ba, den_tile)
        lse_ref[0, r, pl.ds(q0, tq), :] = (m_tile + jnp.log2(den_tile)) * LN2
        return carry

    lax.fori_loop(0, qkv_ref.shape[1] * nq, q_block, 0)


def _attn(qkv, g):
    b, d, length, width = qkv.shape
    tk = min(2 * LANES, length)
    return pl.pallas_call(
        functools.partial(_attn_kernel, length=length),
        grid=(b,),
        in_specs=[pl.BlockSpec((1, d, length, width), lambda i: (i, 0, 0, 0))],
        out_specs=[pl.BlockSpec((1, d, length, ATTN_WIDTH), lambda i: (i, 0, 0, 0)),
                   pl.BlockSpec((1, d, length, LANES), lambda i: (i, 0, 0, 0))],
        out_shape=[jax.ShapeDtypeStruct((b, d, length, ATTN_WIDTH), BF16),
                   jax.ShapeDtypeStruct((b, d, length, LANES), F32)],
        scratch_shapes=[pltpu.VMEM((HEADS, LANES, tk), F32), pltpu.VMEM((HEADS, LANES, tk), BF16)],
        compiler_params=pltpu.CompilerParams(
            dimension_semantics=("arbitrary",), vmem_limit_bytes=VMEM_LIMIT),
        name=f"attn{g}",
    )(qkv)


def _fourier_kernel(x_ref, bd_ref, cs_ref, ns_ref, o_ref, y_ref, *, seq, rows):
    for r in range(seq // rows):
        rs = slice(r * rows, (r + 1) * rows)
        y_ref[rs, :] = _dot(x_ref[0, rs, :], bd_ref[...]).astype(BF16)
    for r in range(seq // rows):
        rs = slice(r * rows, (r + 1) * rows)
        o = _dot(cs_ref[rs, :], y_ref[:, :ATTN_WIDTH]) + _dot(ns_ref[rs, :], y_ref[:, ATTN_WIDTH:])
        o_ref[0, rs, :] = o.astype(BF16)


def _fourier(qkvf, bd, cs, ns):
    b, s, _ = qkvf.shape
    return pl.pallas_call(
        functools.partial(_fourier_kernel, seq=s, rows=256),
        grid=(b,),
        in_specs=[pl.BlockSpec((1, s, ATTN_WIDTH), lambda i: (i, 0, 0)),
                  _const_spec((ATTN_WIDTH, 2 * ATTN_WIDTH)),
                  _const_spec((s, s)),
                  _const_spec((s, s))],
        out_specs=pl.BlockSpec((1, s, ATTN_WIDTH), lambda i: (i, 0, 0)),
        out_shape=jax.ShapeDtypeStruct((b, s, ATTN_WIDTH), BF16),
        scratch_shapes=[pltpu.VMEM((s, 2 * ATTN_WIDTH), BF16)],
        compiler_params=pltpu.CompilerParams(
            dimension_semantics=("arbitrary",), vmem_limit_bytes=VMEM_LIMIT),
        name="fourier",
    )(qkvf, bd, cs, ns)


def _merge_kernel(x_ref, mod_ref, o0_ref, o1_ref, o2_ref, l0_ref, l1_ref, l2_ref, fm_ref,
                  g1_ref, g2_ref, wg_ref, wap_ref, wfp_ref, wout_ref, wrh_ref, wrl_ref, br_ref,
                  expand_ref, ltri_ref, utri_ref, perm1_ref, perm2_ref,
                  h_ref, u2_ref, ridx_ref, rw_ref, tcnt_ref, cnt_ref, cnt_scr):
    @pl.when((pl.program_id(0) == 0) & (pl.program_id(1) == 0))
    def _():
        cnt_scr[...] = jnp.zeros_like(cnt_scr)

    x = x_ref[0]
    ts = x.shape[0]
    mod = mod_ref[0]
    shift1, scale1, gate1 = mod[0:1], mod[1:2], mod[2:3]
    shift2, scale2 = mod[3:4], mod[4:5]
    u = _rms_mod(x, g1_ref[...], scale1, shift1).astype(BF16)
    gates = _dot(u, wg_ref[...])

    os_ = [o0_ref[0, 0].astype(F32)]
    lses = [l0_ref[0, 0]]
    for o_ref, l_ref, perm_ref in ((o1_ref, l1_ref, perm1_ref), (o2_ref, l2_ref, perm2_ref)):
        perm = perm_ref[...]
        os_.append(_dot(perm, o_ref[0].reshape(ts, ATTN_WIDTH)))
        l_hi, l_lo = _split_bf16(l_ref[0].reshape(ts, LANES))
        lses.append(_dot(perm, l_hi) + _dot(perm, l_lo))

    mx = jnp.maximum(jnp.maximum(lses[0], lses[1]), lses[2])
    es = [jnp.exp(l - mx) for l in lses]
    den = es[0] + es[1] + es[2]
    attn = None
    for e, o in zip(es, os_):
        w_hi, w_lo = _split_bf16(e / den)
        wf = _dot(w_hi, expand_ref[...]) + _dot(w_lo, expand_ref[...])
        term = wf * o
        attn = term if attn is None else attn + term

    a = _dot(attn.astype(BF16), wap_ref[...])
    f = _dot(fm_ref[0], wfp_ref[...])
    merged = jax.nn.sigmoid(gates[:, :D_MODEL]) * a + jax.nn.sigmoid(gates[:, D_MODEL:]) * f
    h = x + gate1 * _dot(merged.astype(BF16), wout_ref[...])
    h_ref[0] = h

    u2 = _rms_mod(h, g2_ref[...], scale2, shift2)
    u_hi, u_lo = _split_bf16(u2)
    u2_ref[0] = u_hi
    logits = (_dot(u_hi, wrh_ref[...]) + _dot(u_lo, wrh_ref[...]) + _dot(u_hi, wrl_ref[...])
              + br_ref[...])

    lane = lax.broadcasted_iota(jnp.int32, (ts, LANES), 1)
    lg = jnp.where(lane < N_EXPERTS, logits, -jnp.inf)
    vals, ids = [], []
    onehot = jnp.zeros((ts, LANES), F32)
    for _ in range(TOP_K):
        m = jnp.max(lg, axis=-1, keepdims=True)
        idx = jnp.min(jnp.where(lg == m, lane, LANES), axis=-1, keepdims=True)
        hit = lane == idx
        vals.append(m)
        ids.append(idx)
        onehot = jnp.where(hit, 1.0, onehot)
        lg = jnp.where(hit, -jnp.inf, lg)
    exps = [jnp.exp(v - vals[0]) for v in vals]
    esum = exps[0] + exps[1] + exps[2] + exps[3]

    n_tile = jnp.sum(onehot, axis=0, keepdims=True)
    n_tile = n_tile + (n_tile - 2.0 * jnp.floor(0.5 * n_tile))
    seg_start = _dot(jnp.broadcast_to(n_tile, (8, LANES)).astype(BF16), utri_ref[...])[0:1, :]
    pos_all = _dot(ltri_ref[...], onehot.astype(BF16)) + seg_start
    ridx = jnp.zeros((ts, LANES), F32)
    rw = jnp.zeros((ts, LANES), F32)
    for k in range(TOP_K):
        pos = jnp.sum(jnp.where(lane == ids[k], pos_all, 0.0), axis=-1, keepdims=True)
        ridx = jnp.where(lane == k, ids[k].astype(F32), ridx)
        ridx = jnp.where(lane == TOP_K + k, pos, ridx)
        rw = jnp.where(lane == k, exps[k] / esum, rw)
    ridx_ref[0] = ridx.astype(jnp.int32)
    rw_ref[0] = rw
    row = lax.broadcasted_iota(jnp.int32, (8, LANES), 0)
    tcnt = jnp.where(row == 0, cnt_scr[...], jnp.where(row == 1, jnp.broadcast_to(n_tile, (8, LANES)), 0.0))
    tcnt_ref[0] = tcnt.astype(jnp.int32)
    cnt_scr[...] = cnt_scr[...] + n_tile
    cnt_ref[...] = cnt_scr[...]


def _merge(x, mod, os_, lses, fm, g1, g2, wg, wap, wfp, wout, wrh, wrl, br, expand, ltri, utri, perms,
           ts=MERGE_ROWS):
    b, s, _ = x.shape
    tile = lambda w: pl.BlockSpec((1, ts, w), lambda i, j: (i, j, 0))
    strided = lambda w: [pl.BlockSpec((1, d, ts // d, w), lambda i, j: (i, 0, j, 0)) for d in DILATIONS]
    consts = (g1, g2, wg, wap, wfp, wout, wrh, wrl, br, expand, ltri, utri) + tuple(perms)
    in_specs = ([tile(D_MODEL), pl.BlockSpec((1, 6, D_MODEL), lambda i, j: (i, 0, 0))]
                + strided(ATTN_WIDTH) + strided(LANES) + [tile(ATTN_WIDTH)]
                + [_const_spec(a.shape) for a in consts])
    per_seq = s // ts
    return pl.pallas_call(
        _merge_kernel,
        grid=(b, per_seq),
        in_specs=in_specs,
        out_specs=[tile(D_MODEL), tile(D_MODEL), tile(LANES), tile(LANES),
                   pl.BlockSpec((1, 8, LANES), lambda i, j: (i * per_seq + j, 0, 0)),
                   pl.BlockSpec((8, LANES), lambda i, j: (0, 0))],
        out_shape=[jax.ShapeDtypeStruct((b, s, D_MODEL), F32),
                   jax.ShapeDtypeStruct((b, s, D_MODEL), BF16),
                   jax.ShapeDtypeStruct((b, s, LANES), jnp.int32),
                   jax.ShapeDtypeStruct((b, s, LANES), F32),
                   jax.ShapeDtypeStruct((b * per_seq, 8, LANES), jnp.int32),
                   jax.ShapeDtypeStruct((8, LANES), F32)],
        scratch_shapes=[pltpu.VMEM((8, LANES), F32)],
        compiler_params=pltpu.CompilerParams(
            dimension_semantics=("arbitrary", "arbitrary"), vmem_limit_bytes=VMEM_LIMIT),
        name="merge",
    )(x, mod, *os_, *lses, fm, *consts)


def _to_packed_rows(ref, mat):
    bits = lax.bitcast_convert_type(mat, jnp.uint32)
    for c in range(ROW_WORDS):
        lo = bits[:, c * LANES:(c + 1) * LANES] >> 16
        hi = bits[:, (c + ROW_WORDS) * LANES:(c + ROW_WORDS + 1) * LANES] & jnp.uint32(0xFFFF0000)
        ref[pl.ds(c, mat.shape[0], stride=ROW_WORDS), :] = lo | hi


def _from_packed_rows(ref):
    n = ref.shape[0] // ROW_WORDS
    words = [ref[pl.ds(c, n, stride=ROW_WORDS), :] for c in range(ROW_WORDS)]
    lo = [lax.bitcast_convert_type(w << 16, F32) for w in words]
    hi = [lax.bitcast_convert_type(w & jnp.uint32(0xFFFF0000), F32) for w in words]
    return jnp.concatenate(lo + hi, axis=1)


def _segment_copies(seg_n_ref, seg_row_ref, tile, sorted_ref, hbm_ref, sem, *, to_hbm, start, consecutive=True):
    base = tile * N_EXPERTS

    def segment(e, src):
        n = seg_n_ref[base + e]
        dst = seg_row_ref[base + e]
        for bit in SEGMENT_PIECES:
            off = n & (-2 * bit)

            @pl.when((n & bit) != 0)
            def _():
                vm_row = (src + off) if consecutive else 0
                vm = sorted_ref.at[pl.ds(pl.multiple_of(vm_row * ROW_WORDS, 2 * ROW_WORDS), bit * ROW_WORDS)]
                hb = hbm_ref.at[pl.ds(pl.multiple_of((dst + off) * ROW_WORDS, 2 * ROW_WORDS), bit * ROW_WORDS)]
                cp = pltpu.make_async_copy(vm, hb, sem) if to_hbm else pltpu.make_async_copy(hb, vm, sem)
                if start:
                    cp.start()
                else:
                    cp.wait()
        return src + n

    lax.fori_loop(0, N_EXPERTS, segment, 0)


def _dispatch_kernel(seg_n_ref, seg_row_ref, ridx_ref, u2_ref, xs_ref, sorted_scr, sems):
    tile = pl.program_id(0)
    n_tiles = pl.num_programs(0)
    slot = tile % 2
    tt = u2_ref.shape[0]
    rows = SORTED_ROWS

    def copies(tile_, slot_, start):
        _segment_copies(seg_n_ref, seg_row_ref, tile_, sorted_scr.at[slot_], xs_ref, sems.at[slot_],
                        to_hbm=True, start=start)

    @pl.when(tile == 0)
    def _():
        sorted_scr[1, 0:max(SEGMENT_PIECES) * ROW_WORDS, :] = jnp.zeros(
            (max(SEGMENT_PIECES) * ROW_WORDS, LANES), jnp.uint32)
        for start in (True, False):
            _segment_copies(seg_n_ref, seg_row_ref, n_tiles, sorted_scr.at[1], xs_ref, sems.at[1],
                            to_hbm=True, start=start, consecutive=False)

    pos_t = ridx_ref[...].astype(F32).T
    j = lax.broadcasted_iota(jnp.int32, (rows, tt), 0).astype(F32)
    onehot = jnp.zeros((rows, tt), F32)
    for k in range(TOP_K):
        onehot = jnp.where(j == pos_t[TOP_K + k:TOP_K + k + 1, :], 1.0, onehot)
    _to_packed_rows(sorted_scr.at[slot], _dot(onehot.astype(BF16), u2_ref[...]))
    copies(tile, slot, True)

    @pl.when(tile > 0)
    def _():
        copies(tile - 1, 1 - slot, False)

    @pl.when(tile == n_tiles - 1)
    def _():
        copies(tile, slot, False)


def _dispatch(seg_n, seg_row, ridx, u2, n_rows, tt=MERGE_ROWS):
    t = u2.shape[0]
    grid_spec = pltpu.PrefetchScalarGridSpec(
        num_scalar_prefetch=2,
        grid=(t // tt,),
        in_specs=[pl.BlockSpec((tt, LANES), lambda i, *_: (i, 0)),
                  pl.BlockSpec((tt, D_MODEL), lambda i, *_: (i, 0))],
        out_specs=pl.BlockSpec(memory_space=pl.ANY),
        scratch_shapes=[pltpu.VMEM((2, SORTED_ROWS * ROW_WORDS, LANES), jnp.uint32),
                        pltpu.SemaphoreType.DMA((2,))],
    )
    return pl.pallas_call(
        _dispatch_kernel,
        grid_spec=grid_spec,
        out_shape=jax.ShapeDtypeStruct((n_rows * ROW_WORDS, LANES), jnp.uint32),
        compiler_params=pltpu.CompilerParams(
            dimension_semantics=("arbitrary",), has_side_effects=True, vmem_limit_bytes=VMEM_LIMIT),
        name="dispatch",
    )(seg_n, seg_row, ridx, u2)


def _expert_kernel(be_ref, nb_ref, xs_ref, w1_ref, b1_ref, w2_ref, b2_ref, ys_ref):
    del be_ref
    i = pl.program_id(0)

    @pl.when(i < nb_ref[0])
    def _():
        hdn = _dot(_from_packed_rows(xs_ref).astype(BF16), w1_ref[0]) + b1_ref[0]
        glu = jnp.minimum(hdn[:, :D_FF], SWIGLU_LIMIT)
        lin = jnp.clip(hdn[:, D_FF:], -SWIGLU_LIMIT, SWIGLU_LIMIT)
        act = (lin + 1.0) * (glu * jax.nn.sigmoid(SWIGLU_ALPHA * glu))
        y = _dot(act.astype(BF16), w2_ref[0]) + b2_ref[0]
        _to_packed_rows(ys_ref, y.astype(BF16).astype(F32))

    @pl.when(i >= nb_ref[0])
    def _():
        ys_ref[...] = jnp.zeros_like(ys_ref)


def _expert(block_e, nb_used, xs, w1, b1, w2, b2):
    block = (EXPERT_ROWS * ROW_WORDS, LANES)
    n_blocks = xs.shape[0] // block[0]
    grid_spec = pltpu.PrefetchScalarGridSpec(
        num_scalar_prefetch=2,
        grid=(n_blocks,),
        in_specs=[pl.BlockSpec(block, lambda i, be, nb: (jnp.minimum(i, nb[0] - 1), 0)),
                  pl.BlockSpec((1, D_MODEL, 2 * D_FF), lambda i, be, nb: (be[i], 0, 0)),
                  pl.BlockSpec((1, 1, 2 * D_FF), lambda i, be, nb: (be[i], 0, 0)),
                  pl.BlockSpec((1, D_FF, D_MODEL), lambda i, be, nb: (be[i], 0, 0)),
                  pl.BlockSpec((1, 1, D_MODEL), lambda i, be, nb: (be[i], 0, 0))],
        out_specs=pl.BlockSpec(block, lambda i, be, nb: (i, 0)),
    )
    return pl.pallas_call(
        _expert_kernel,
        grid_spec=grid_spec,
        out_shape=jax.ShapeDtypeStruct(xs.shape, xs.dtype),
        compiler_params=pltpu.CompilerParams(
            dimension_semantics=("arbitrary",), vmem_limit_bytes=VMEM_LIMIT),
        name="expert",
    )(block_e, nb_used, xs, w1, b1, w2, b2)


def _combine_kernel(seg_n_ref, seg_row_ref, ridx_ref, rw_ref, h_ref, mod_ref, ys_ref, o_ref, sorted_scr, sems):
    tile = pl.program_id(0)
    n_tiles = pl.num_programs(0)
    slot = tile % 2
    tt = h_ref.shape[0]
    rows = SORTED_ROWS

    def copies(tile_, slot_, start):
        _segment_copies(seg_n_ref, seg_row_ref, tile_, sorted_scr.at[slot_], ys_ref, sems.at[slot_],
                        to_hbm=False, start=start)

    @pl.when(tile == 0)
    def _():
        sorted_scr[...] = jnp.zeros_like(sorted_scr)
        copies(tile, slot, True)

    @pl.when(tile + 1 < n_tiles)
    def _():
        copies(tile + 1, 1 - slot, True)

    ridx = ridx_ref[...].astype(F32)
    rw = rw_ref[...]
    j = lax.broadcasted_iota(jnp.int32, (tt, rows), 1).astype(F32)
    wmat = jnp.zeros((tt, rows), F32)
    for k in range(TOP_K):
        wmat = jnp.where(j == ridx[:, TOP_K + k:TOP_K + k + 1], rw[:, k:k + 1], wmat)
    w_hi, w_lo = _split_bf16(wmat)
    copies(tile, slot, False)
    y = _from_packed_rows(sorted_scr.at[slot]).astype(BF16)
    o_ref[...] = h_ref[...] + mod_ref[0][5:6] * (_dot(w_hi, y) + _dot(w_lo, y))


def _combine(seg_n, seg_row, ridx, rw, h, mod, ys, seq, tt=MERGE_ROWS):
    t = h.shape[0]
    per_seq = seq // tt
    grid_spec = pltpu.PrefetchScalarGridSpec(
        num_scalar_prefetch=2,
        grid=(t // tt,),
        in_specs=[pl.BlockSpec((tt, LANES), lambda i, *_: (i, 0)),
                  pl.BlockSpec((tt, LANES), lambda i, *_: (i, 0)),
                  pl.BlockSpec((tt, D_MODEL), lambda i, *_: (i, 0)),
                  pl.BlockSpec((1, 6, D_MODEL), lambda i, *_: (i // per_seq, 0, 0)),
                  pl.BlockSpec(memory_space=pl.ANY)],
        out_specs=pl.BlockSpec((tt, D_MODEL), lambda i, *_: (i, 0)),
        scratch_shapes=[pltpu.VMEM((2, SORTED_ROWS * ROW_WORDS, LANES), jnp.uint32),
                        pltpu.SemaphoreType.DMA((2,))],
    )
    return pl.pallas_call(
        _combine_kernel,
        grid_spec=grid_spec,
        out_shape=jax.ShapeDtypeStruct((t, D_MODEL), F32),
        compiler_params=pltpu.CompilerParams(
            dimension_semantics=("arbitrary",), vmem_limit_bytes=VMEM_LIMIT),
        name="combine",
    )(seg_n, seg_row, ridx, rw, h, mod, ys)


def _rope_tables(seq, ts):
    half = ROPE_DIM // 2
    inv_freq = jnp.power(jnp.float32(ROPE_THETA), -jnp.arange(half, dtype=F32) * 2.0 / ROPE_DIM)
    ang = jnp.arange(seq, dtype=F32)[:, None] * inv_freq[None, :]
    cos, sin = jnp.cos(ang), jnp.sin(ang)
    ones = jnp.ones((seq, HEAD_DIM - ROPE_DIM), F32)
    zeros = jnp.zeros_like(ones)
    zh = jnp.zeros((seq, half), F32)
    cos_t = jnp.concatenate([cos, cos, ones], axis=1)
    sin_prev = jnp.concatenate([zh, sin, zeros], axis=1)
    sin_next = jnp.concatenate([-sin, zh, zeros], axis=1)
    tables = jnp.stack([jnp.tile(t, (1, LANES // HEAD_DIM)) for t in (cos_t, sin_prev, sin_next)])
    per_group = []
    for d in DILATIONS:
        pos = jnp.arange(seq).reshape(seq // ts, ts // d, d).transpose(0, 2, 1).reshape(seq)
        per_group.append(tables[:, pos, :])
    return jnp.stack(per_group)


def _dft_tables(seq):
    k = jnp.arange(seq, dtype=jnp.int32)
    ang = ((k[:, None] * k[None, :]) % seq).astype(F32) * (2.0 * jnp.pi / seq)
    scale = seq ** -0.5
    cs = jnp.cos(ang) * scale
    ns = -jnp.sin(ang) * scale
    c = jnp.arange(FOURIER_GROUP_DIM, dtype=jnp.int32)
    angc = ((c[:, None] * c[None, :]) % FOURIER_GROUP_DIM).astype(F32) * (2.0 * jnp.pi / FOURIER_GROUP_DIM)
    eye = jnp.eye(ATTN_WIDTH // FOURIER_GROUP_DIM, dtype=F32)
    cscale = FOURIER_GROUP_DIM ** -0.5
    bd = jnp.concatenate([jnp.kron(eye, jnp.cos(angc) * cscale),
                          jnp.kron(eye, jnp.sin(angc) * cscale)], axis=1)
    return bd.astype(BF16), cs.astype(BF16), ns.astype(BF16)


def _trunk(x, c, p):
    b, s, _ = x.shape
    t = b * s
    mod = _ada(c, p["w_ada"], p["b_ada"]).reshape(b, 6, D_MODEL)
    *qkvs, fz = _inproj(x, mod, p["g1"], p["w_qkvf"], p["gains"], p["rope"], p["gmat"])
    os_, lses = zip(*[_attn(qkv, g) for g, qkv in enumerate(qkvs)])
    fm = _fourier(fz, p["bd"], p["cs"], p["ns"])
    h, u2, ridx, rw, tcnt, cnt = _merge(
        x, mod, os_, lses, fm, p["g1"], p["g2"], p["wg"], p["wap"], p["wfp"], p["wout"], p["wrh"],
        p["wrl"], p["br"], p["expand"], p["ltri"], p["utri"], p["perms"])

    rows = EXPERT_ROWS
    counts = cnt[0, :N_EXPERTS].astype(jnp.int32)
    padded = (counts + rows - 1) // rows * rows
    pend = jnp.cumsum(padded)
    pstart = pend - padded
    seg_row = jnp.concatenate([pstart[None, :] + tcnt[:, 0, :N_EXPERTS], (pstart + counts)[None, :]]).reshape(-1)
    seg_n = jnp.concatenate([tcnt[:, 1, :N_EXPERTS], (padded - counts)[None, :]]).reshape(-1)
    n_blocks = -(-(t * TOP_K + N_EXPERTS * (t // MERGE_ROWS + rows - 1)) // rows)
    block_start = jnp.arange(n_blocks, dtype=jnp.int32) * rows
    block_e = jnp.minimum(jnp.sum((pend[None, :] <= block_start[:, None]).astype(jnp.int32), axis=-1),
                          N_EXPERTS - 1)
    nb_used = (pend[-1:] // rows).astype(jnp.int32)

    ridx = ridx.reshape(t, LANES)
    xs = _dispatch(seg_n, seg_row, ridx, u2.reshape(t, D_MODEL), n_blocks * rows)
    ys = _expert(block_e, nb_used, xs, p["w1"], p["b1"], p["w2"], p["b2"])
    out = _combine(seg_n, seg_row, ridx, rw.reshape(t, LANES), h.reshape(t, D_MODEL), mod, ys, s)
    return out.reshape(b, s, D_MODEL)


def kernel(x_prompt, x_sample, c_prompt, c_sample, w_ada, b_ada, norm1_g, norm2_g, w_in, q_gain, k_gain,
           w_attn_proj, w_fourier_proj, w_out, w_router, b_router, w1, b1, w2, b2):
    seq = x_prompt.shape[1]
    assert w_ada.shape[0] == 1, "single-layer trunk"
    w_in0 = w_in[0]
    gains = jnp.zeros((8, ATTN_WIDTH), F32)
    for g in range(N_GROUPS):
        gains = gains.at[2 * g].set(jnp.tile(q_gain[0, g], HEADS))
        gains = gains.at[2 * g + 1].set(jnp.tile(k_gain[0, g], HEADS))
    heads = jnp.arange(ATTN_WIDTH) // HEAD_DIM
    gmat = ((heads[:, None] == heads[None, :]).astype(F32) / HEAD_DIM).astype(BF16)
    expand = (jnp.arange(LANES)[:, None] == heads[None, :]).astype(BF16)
    ltri = (jnp.arange(MERGE_ROWS)[:, None] > jnp.arange(MERGE_ROWS)[None, :]).astype(BF16)
    utri = (jnp.arange(LANES)[:, None] < jnp.arange(LANES)[None, :]).astype(BF16)
    perms = []
    for d in DILATIONS[1:]:
        n = MERGE_ROWS // d
        src = (jnp.arange(MERGE_ROWS) % d) * n + jnp.arange(MERGE_ROWS) // d
        perms.append((src[:, None] == jnp.arange(MERGE_ROWS)[None, :]).astype(BF16))
    wr = jnp.zeros((D_MODEL, LANES), F32).at[:, :N_EXPERTS].set(w_router[0])
    wrh = wr.astype(BF16)
    wrl = (wr - wrh.astype(F32)).astype(BF16)
    br = jnp.zeros((1, LANES), F32).at[0, :N_EXPERTS].set(b_router[0])
    bd, cs, ns = _dft_tables(seq)
    p = dict(
        w_ada=w_ada[0], b_ada=b_ada[0], g1=norm1_g, g2=norm2_g,
        w_qkvf=w_in0[:, :QKVF_WIDTH].astype(BF16), wg=w_in0[:, QKVF_WIDTH:].astype(BF16),
        gains=gains, rope=_rope_tables(seq, INPROJ_ROWS), gmat=gmat, bd=bd, cs=cs, ns=ns,
        wap=w_attn_proj[0].astype(BF16), wfp=w_fourier_proj[0].astype(BF16), wout=w_out[0].astype(BF16),
        wrh=wrh, wrl=wrl, br=br, expand=expand, ltri=ltri, utri=utri, perms=perms,
        w1=w1[0].astype(BF16), b1=b1[0].reshape(N_EXPERTS, 1, 2 * D_FF),
        w2=w2[0].astype(BF16), b2=b2[0].reshape(N_EXPERTS, 1, D_MODEL),
    )
    return _trunk(x_prompt, c_prompt, p), _trunk(x_sample, c_sample, p)
```

```python
import functools

import jax
import jax.numpy as jnp
from jax import lax
from jax.experimental import pallas as pl
from jax.experimental.pallas import tpu as pltpu

F32 = jnp.float32
BF16 = jnp.bfloat16

D_MODEL = 1024
HEAD_DIM = 64
HEADS = 8
ATTN_WIDTH = HEADS * HEAD_DIM
DILATIONS = (1, 4, 16)
BAND = 64
N_GROUPS = 3
QKVF_WIDTH = 5120
FOURIER_GROUP_DIM = 64
ROPE_DIM = 16
ROPE_THETA = 500000.0
N_EXPERTS = 32
TOP_K = 4
D_FF = 1024
SWIGLU_ALPHA = 1.702
SWIGLU_LIMIT = 7.0
NORM_EPS = 1e-6
NEG_INF = -1e30
LN2 = 0.6931471805599453
LOG2E = 1.4426950408889634

LANES = 128
ROW_WORDS = D_MODEL // LANES // 2
SORTED_ROWS = 1088
EXPERT_ROWS = 512
LONG_PIECE = 64
SHORT_PIECES = (32, 16, 8, 4, 2)
INPROJ_ROWS = 512
MERGE_ROWS = 256
VMEM_LIMIT = 56 * 1024 * 1024


def _dot(a, b):
    return jnp.dot(a, b, preferred_element_type=F32)


def _split_bf16(v):
    hi = v.astype(BF16)
    lo = (v - hi.astype(F32)).astype(BF16)
    return hi, lo


def _rms_mod(x, gain, scale, shift):
    ms = jnp.mean(x * x, axis=-1, keepdims=True)
    return x * lax.rsqrt(ms + NORM_EPS) * gain * (1.0 + scale) + shift


def _const_spec(shape):
    return pl.BlockSpec(shape, lambda *_: (0,) * len(shape), pipeline_mode=pl.Buffered(1))


def _ada_kernel(c_ref, w_ref, b_ref, o_ref):
    c = c_ref[...]
    a = c * jax.nn.sigmoid(c)
    a_hi, a_lo = _split_bf16(a)
    w_hi, w_lo = _split_bf16(w_ref[...])
    o_ref[...] = _dot(a_hi, w_hi) + _dot(a_lo, w_hi) + _dot(a_hi, w_lo) + b_ref[...]


def _ada(c, w_ada, b_ada):
    b = c.shape[0]
    n = w_ada.shape[1]
    tn = D_MODEL
    return pl.pallas_call(
        _ada_kernel,
        grid=(n // tn,),
        in_specs=[pl.BlockSpec((b, D_MODEL), lambda j: (0, 0)),
                  pl.BlockSpec((D_MODEL, tn), lambda j: (0, j)),
                  pl.BlockSpec((1, tn), lambda j: (0, j))],
        out_specs=pl.BlockSpec((b, tn), lambda j: (0, j)),
        out_shape=jax.ShapeDtypeStruct((b, n), F32),
        compiler_params=pltpu.CompilerParams(vmem_limit_bytes=VMEM_LIMIT),
        name="ada",
    )(c, w_ada, b_ada.reshape(1, n))


def _inproj_kernel(x_ref, mod_ref, g1_ref, w_ref, gain_ref, rope_ref, gmat_ref,
                   o0_ref, o1_ref, o2_ref, of_ref, u_scr):
    x = x_ref[0]
    ts = x.shape[0]
    mod = mod_ref[0]
    u = _rms_mod(x, g1_ref[...], mod[1:2], mod[0:1])
    ub = u.astype(BF16)
    for c in range(D_MODEL // LANES):
        u_scr[c] = u[:, c * LANES:(c + 1) * LANES]
    for g, o_ref in enumerate((o0_ref, o1_ref, o2_ref)):
        d = DILATIONS[g]
        n = ts // d
        if d == 1:
            ug = ub
        else:
            ug = jnp.concatenate(
                [jnp.concatenate([u_scr[c, pl.ds(r, n, stride=d), :] for r in range(d)], axis=0)
                 for c in range(D_MODEL // LANES)], axis=1).astype(BF16)
        cos_t, sin_prev, sin_next = rope_ref[g, 0], rope_ref[g, 1], rope_ref[g, 2]
        for j in range(3):
            c0 = (3 * g + j) * ATTN_WIDTH
            z = _dot(ug, w_ref[:, c0:c0 + ATTN_WIDTH])
            if j < 2:
                ms = _dot((z * z).astype(BF16), gmat_ref[...])
                gain = gain_ref[2 * g + j:2 * g + j + 1, :]
                if j == 0:
                    gain = gain * (HEAD_DIM ** -0.5 * LOG2E)
                z = z * lax.rsqrt(ms + NORM_EPS) * gain
                blocks = []
                for blk in range(ATTN_WIDTH // LANES):
                    zb = z[:, blk * LANES:(blk + 1) * LANES]
                    blocks.append(zb * cos_t
                                  + pltpu.roll(zb, ROPE_DIM // 2, 1) * sin_prev
                                  + pltpu.roll(zb, LANES - ROPE_DIM // 2, 1) * sin_next)
                z = jnp.concatenate(blocks, axis=1)
            zb16 = z.astype(BF16)
            for r in range(d):
                o_ref[0, r, :, j * ATTN_WIDTH:(j + 1) * ATTN_WIDTH] = zb16[r * n:(r + 1) * n, :]
    of_ref[0] = _dot(ub, w_ref[:, 9 * ATTN_WIDTH:]).astype(BF16)


def _inproj(x, mod, g1, w_qkvf, gains, rope, gmat, ts=INPROJ_ROWS):
    b, s, _ = x.shape
    qkv_w = 3 * ATTN_WIDTH
    out_specs = [pl.BlockSpec((1, d, ts // d, qkv_w), lambda i, j: (i, 0, j, 0)) for d in DILATIONS]
    out_shape = [jax.ShapeDtypeStruct((b, d, s // d, qkv_w), BF16) for d in DILATIONS]
    return pl.pallas_call(
        _inproj_kernel,
        grid=(b, s // ts),
        in_specs=[pl.BlockSpec((1, ts, D_MODEL), lambda i, j: (i, j, 0)),
                  pl.BlockSpec((1, 6, D_MODEL), lambda i, j: (i, 0, 0)),
                  _const_spec((1, D_MODEL)),
                  _const_spec((D_MODEL, QKVF_WIDTH)),
                  _const_spec((8, ATTN_WIDTH)),
                  pl.BlockSpec((N_GROUPS, 3, ts, LANES), lambda i, j: (0, 0, j, 0)),
                  _const_spec((ATTN_WIDTH, ATTN_WIDTH))],
        out_specs=out_specs + [pl.BlockSpec((1, ts, ATTN_WIDTH), lambda i, j: (i, j, 0))],
        out_shape=out_shape + [jax.ShapeDtypeStruct((b, s, ATTN_WIDTH), BF16)],
        scratch_shapes=[pltpu.VMEM((D_MODEL // LANES, ts, LANES), F32)],
        compiler_params=pltpu.CompilerParams(
            dimension_semantics=("arbitrary", "arbitrary"), vmem_limit_bytes=VMEM_LIMIT),
        name="inproj",
    )(x, mod, g1, w_qkvf, gains, rope, gmat)


def _attn_kernel(qkv_ref, o_ref, lse_ref, s_scr, p_scr, *, length):
    tq = LANES
    tk = min(2 * LANES, length)
    nq = length // tq
    n_pairs = ATTN_WIDTH // LANES
    lane = lax.broadcasted_iota(jnp.int32, (tq, LANES), 1)
    first_head = lane < HEAD_DIM
    first_head_k = lax.broadcasted_iota(jnp.int32, (tk, LANES), 1) < HEAD_DIM
    rel = (lax.broadcasted_iota(jnp.int32, (tq, tk), 1)
           - lax.broadcasted_iota(jnp.int32, (tq, tk), 0))

    def q_block(it, carry):
        r = it // nq
        i = it % nq
        q0 = pl.multiple_of(i * tq, tq)
        ws = pl.multiple_of(jnp.clip(i * tq - BAND, 0, length - tk), BAND)
        valid = jnp.abs(rel + (ws - i * tq)) <= BAND

        for p in range(n_pairs):
            q2 = qkv_ref[0, r, pl.ds(q0, tq), p * LANES:(p + 1) * LANES]
            k2 = qkv_ref[0, r, pl.ds(ws, tk), ATTN_WIDTH + p * LANES:ATTN_WIDTH + (p + 1) * LANES]
            for hh in range(2):
                sel = first_head if hh == 0 else jnp.logical_not(first_head)
                qa = jnp.where(sel, q2, jnp.zeros_like(q2))
                s = lax.dot_general(qa, k2, (((1,), (1,)), ((), ())), preferred_element_type=F32)
                s_scr[2 * p + hh] = jnp.where(valid, s, NEG_INF)

        m_tile = jnp.zeros((tq, LANES), F32)
        for h in range(HEADS):
            s = s_scr[h]
            m = jnp.max(s, axis=-1, keepdims=True)
            p_scr[h] = jnp.exp2(s - m).astype(BF16)
            m_tile = jnp.where(lane == h, m, m_tile)

        den_tile = jnp.ones((tq, LANES), F32)
        for p in range(n_pairs):
            cs = slice(2 * ATTN_WIDTH + p * LANES, 2 * ATTN_WIDTH + (p + 1) * LANES)
            v2 = qkv_ref[0, r, pl.ds(ws, tk), cs]
            one = jnp.ones_like(v2)
            o_a = _dot(p_scr[2 * p], jnp.where(first_head_k, v2, one))
            o_b = _dot(p_scr[2 * p + 1], jnp.where(first_head_k, one, v2))
            den_ba = jnp.where(first_head, o_b, o_a)
            den_ab = pltpu.roll(den_ba, HEAD_DIM, 1)
            o_ref[0, r, pl.ds(q0, tq), p * LANES:(p + 1) * LANES] = (
                jnp.where(first_head, o_a, o_b) / den_ab).astype(BF16)
            den_tile = jnp.where(lane == 2 * p, den_ab, den_tile)
            den_tile = jnp.where(lane == 2 * p + 1, den_ba, den_tile)
        lse_ref[0, r, pl.ds(q0, tq), :] = (m_tile + jnp.log2(den_tile)) * LN2
        return carry

    lax.fori_loop(0, qkv_ref.shape[1] * nq, q_block, 0)


def _attn(qkv, g):
    b, d, length, width = qkv.shape
    tk = min(2 * LANES, length)
    return pl.pallas_call(
        functools.partial(_attn_kernel, length=length),
        grid=(b,),
        in_specs=[pl.BlockSpec((1, d, length, width), lambda i: (i, 0, 0, 0))],
        out_specs=[pl.BlockSpec((1, d, length, ATTN_WIDTH), lambda i: (i, 0, 0, 0)),
                   pl.BlockSpec((1, d, length, LANES), lambda i: (i, 0, 0, 0))],
        out_shape=[jax.ShapeDtypeStruct((b, d, length, ATTN_WIDTH), BF16),
                   jax.ShapeDtypeStruct((b, d, length, LANES), F32)],
        scratch_shapes=[pltpu.VMEM((HEADS, LANES, tk), F32), pltpu.VMEM((HEADS, LANES, tk), BF16)],
        compiler_params=pltpu.CompilerParams(
            dimension_semantics=("arbitrary",), vmem_limit_bytes=VMEM_LIMIT),
        name=f"attn{g}",
    )(qkv)


def _fourier_kernel(x_ref, bd_ref, cs_ref, ns_ref, o_ref, y_ref, *, seq, rows):
    for r in range(seq // rows):
        rs = slice(r * rows, (r + 1) * rows)
        y_ref[rs, :] = _dot(x_ref[0, rs, :], bd_ref[...]).astype(BF16)
    for r in range(seq // rows):
        rs = slice(r * rows, (r + 1) * rows)
        o = _dot(cs_ref[rs, :], y_ref[:, :ATTN_WIDTH]) + _dot(ns_ref[rs, :], y_ref[:, ATTN_WIDTH:])
        o_ref[0, rs, :] = o.astype(BF16)


def _fourier(qkvf, bd, cs, ns):
    b, s, _ = qkvf.shape
    return pl.pallas_call(
        functools.partial(_fourier_kernel, seq=s, rows=256),
        grid=(b,),
        in_specs=[pl.BlockSpec((1, s, ATTN_WIDTH), lambda i: (i, 0, 0)),
                  _const_spec((ATTN_WIDTH, 2 * ATTN_WIDTH)),
                  _const_spec((s, s)),
                  _const_spec((s, s))],
        out_specs=pl.BlockSpec((1, s, ATTN_WIDTH), lambda i: (i, 0, 0)),
        out_shape=jax.ShapeDtypeStruct((b, s, ATTN_WIDTH), BF16),
        scratch_shapes=[pltpu.VMEM((s, 2 * ATTN_WIDTH), BF16)],
        compiler_params=pltpu.CompilerParams(
            dimension_semantics=("arbitrary",), vmem_limit_bytes=VMEM_LIMIT),
        name="fourier",
    )(qkvf, bd, cs, ns)


def _merge_kernel(x_ref, mod_ref, o0_ref, o1_ref, o2_ref, l0_ref, l1_ref, l2_ref, fm_ref,
                  g1_ref, g2_ref, wg_ref, wap_ref, wfp_ref, wout_ref, wrh_ref, wrl_ref, br_ref,
                  expand_ref, ltri_ref, utri_ref, perm1_ref, perm2_ref,
                  h_ref, u2_ref, ridx_ref, rw_ref, tcnt_ref, cnt_ref, cnt_scr):
    @pl.when((pl.program_id(0) == 0) & (pl.program_id(1) == 0))
    def _():
        cnt_scr[...] = jnp.zeros_like(cnt_scr)

    x = x_ref[0]
    ts = x.shape[0]
    mod = mod_ref[0]
    shift1, scale1, gate1 = mod[0:1], mod[1:2], mod[2:3]
    shift2, scale2 = mod[3:4], mod[4:5]
    u = _rms_mod(x, g1_ref[...], scale1, shift1).astype(BF16)
    gates = _dot(u, wg_ref[...])

    os_ = [o0_ref[0, 0].astype(F32)]
    lses = [l0_ref[0, 0]]
    for o_ref, l_ref, perm_ref in ((o1_ref, l1_ref, perm1_ref), (o2_ref, l2_ref, perm2_ref)):
        perm = perm_ref[...]
        os_.append(_dot(perm, o_ref[0].reshape(ts, ATTN_WIDTH)))
        l_hi, l_lo = _split_bf16(l_ref[0].reshape(ts, LANES))
        lses.append(_dot(perm, l_hi) + _dot(perm, l_lo))

    mx = jnp.maximum(jnp.maximum(lses[0], lses[1]), lses[2])
    es = [jnp.exp(l - mx) for l in lses]
    den = es[0] + es[1] + es[2]
    attn = None
    for e, o in zip(es, os_):
        w_hi, w_lo = _split_bf16(e / den)
        wf = _dot(w_hi, expand_ref[...]) + _dot(w_lo, expand_ref[...])
        term = wf * o
        attn = term if attn is None else attn + term

    a = _dot(attn.astype(BF16), wap_ref[...])
    f = _dot(fm_ref[0], wfp_ref[...])
    merged = jax.nn.sigmoid(gates[:, :D_MODEL]) * a + jax.nn.sigmoid(gates[:, D_MODEL:]) * f
    h = x + gate1 * _dot(merged.astype(BF16), wout_ref[...])
    h_ref[0] = h

    u2 = _rms_mod(h, g2_ref[...], scale2, shift2)
    u_hi, u_lo = _split_bf16(u2)
    u2_ref[0] = u_hi
    logits = (_dot(u_hi, wrh_ref[...]) + _dot(u_lo, wrh_ref[...]) + _dot(u_hi, wrl_ref[...])
              + br_ref[...])

    lane = lax.broadcasted_iota(jnp.int32, (ts, LANES), 1)
    lg = jnp.where(lane < N_EXPERTS, logits, -jnp.inf)
    vals, ids = [], []
    onehot = jnp.zeros((ts, LANES), F32)
    for _ in range(TOP_K):
        m = jnp.max(lg, axis=-1, keepdims=True)
        idx = jnp.min(jnp.where(lg == m, lane, LANES), axis=-1, keepdims=True)
        hit = lane == idx
        vals.append(m)
        ids.append(idx)
        onehot = jnp.where(hit, 1.0, onehot)
        lg = jnp.where(hit, -jnp.inf, lg)
    exps = [jnp.exp(v - vals[0]) for v in vals]
    esum = exps[0] + exps[1] + exps[2] + exps[3]

    n_tile = jnp.sum(onehot, axis=0, keepdims=True)
    n_tile = n_tile + (n_tile - 2.0 * jnp.floor(0.5 * n_tile))
    seg_start = _dot(jnp.broadcast_to(n_tile, (8, LANES)).astype(BF16), utri_ref[...])[0:1, :]
    pos_all = _dot(ltri_ref[...], onehot.astype(BF16)) + seg_start
    ridx = jnp.zeros((ts, LANES), F32)
    rw = jnp.zeros((ts, LANES), F32)
    for k in range(TOP_K):
        pos = jnp.sum(jnp.where(lane == ids[k], pos_all, 0.0), axis=-1, keepdims=True)
        ridx = jnp.where(lane == k, ids[k].astype(F32), ridx)
        ridx = jnp.where(lane == TOP_K + k, pos, ridx)
        rw = jnp.where(lane == k, exps[k] / esum, rw)
    ridx_ref[0] = ridx.astype(jnp.int32)
    rw_ref[0] = rw
    row = lax.broadcasted_iota(jnp.int32, (8, LANES), 0)
    tcnt = jnp.where(row == 0, cnt_scr[...], jnp.where(row == 1, jnp.broadcast_to(n_tile, (8, LANES)), 0.0))
    tcnt_ref[0] = tcnt.astype(jnp.int32)
    cnt_scr[...] = cnt_scr[...] + n_tile
    cnt_ref[...] = cnt_scr[...]


def _merge(x, mod, os_, lses, fm, g1, g2, wg, wap, wfp, wout, wrh, wrl, br, expand, ltri, utri, perms,
           ts=MERGE_ROWS):
    b, s, _ = x.shape
    tile = lambda w: pl.BlockSpec((1, ts, w), lambda i, j: (i, j, 0))
    strided = lambda w: [pl.BlockSpec((1, d, ts // d, w), lambda i, j: (i, 0, j, 0)) for d in DILATIONS]
    consts = (g1, g2, wg, wap, wfp, wout, wrh, wrl, br, expand, ltri, utri) + tuple(perms)
    in_specs = ([tile(D_MODEL), pl.BlockSpec((1, 6, D_MODEL), lambda i, j: (i, 0, 0))]
                + strided(ATTN_WIDTH) + strided(LANES) + [tile(ATTN_WIDTH)]
                + [_const_spec(a.shape) for a in consts])
    per_seq = s // ts
    return pl.pallas_call(
        _merge_kernel,
        grid=(b, per_seq),
        in_specs=in_specs,
        out_specs=[tile(D_MODEL), tile(D_MODEL), tile(LANES), tile(LANES),
                   pl.BlockSpec((1, 8, LANES), lambda i, j: (i * per_seq + j, 0, 0)),
                   pl.BlockSpec((8, LANES), lambda i, j: (0, 0))],
        out_shape=[jax.ShapeDtypeStruct((b, s, D_MODEL), F32),
                   jax.ShapeDtypeStruct((b, s, D_MODEL), BF16),
                   jax.ShapeDtypeStruct((b, s, LANES), jnp.int32),
                   jax.ShapeDtypeStruct((b, s, LANES), F32),
                   jax.ShapeDtypeStruct((b * per_seq, 8, LANES), jnp.int32),
                   jax.ShapeDtypeStruct((8, LANES), F32)],
        scratch_shapes=[pltpu.VMEM((8, LANES), F32)],
        compiler_params=pltpu.CompilerParams(
            dimension_semantics=("arbitrary", "arbitrary"), vmem_limit_bytes=VMEM_LIMIT),
        name="merge",
    )(x, mod, *os_, *lses, fm, *consts)


def _to_packed_rows(ref, mat):
    bits = lax.bitcast_convert_type(mat, jnp.uint32)
    for c in range(ROW_WORDS):
        lo = bits[:, c * LANES:(c + 1) * LANES] >> 16
        hi = bits[:, (c + ROW_WORDS) * LANES:(c + ROW_WORDS + 1) * LANES] & jnp.uint32(0xFFFF0000)
        ref[pl.ds(c, mat.shape[0], stride=ROW_WORDS), :] = lo | hi


def _from_packed_rows(ref):
    n = ref.shape[0] // ROW_WORDS
    words = [ref[pl.ds(c, n, stride=ROW_WORDS), :] for c in range(ROW_WORDS)]
    lo = [lax.bitcast_convert_type(w << 16, F32) for w in words]
    hi = [lax.bitcast_convert_type(w & jnp.uint32(0xFFFF0000), F32) for w in words]
    return jnp.concatenate(lo + hi, axis=1)


def _segment_copies(seg_n_ref, seg_row_ref, tile, sorted_ref, hbm_ref, sem, *, to_hbm, start, consecutive=True):
    base = tile * N_EXPERTS

    def copy(src, dst, off, size):
        vm_row = (src + off) if consecutive else 0
        vm = sorted_ref.at[pl.ds(pl.multiple_of(vm_row * ROW_WORDS, 2 * ROW_WORDS), size * ROW_WORDS)]
        hb = hbm_ref.at[pl.ds(pl.multiple_of((dst + off) * ROW_WORDS, 2 * ROW_WORDS), size * ROW_WORDS)]
        cp = pltpu.make_async_copy(vm, hb, sem) if to_hbm else pltpu.make_async_copy(hb, vm, sem)
        if start:
            cp.start()
        else:
            cp.wait()

    def segment(e, src):
        n = seg_n_ref[base + e]
        dst = seg_row_ref[base + e]
        n_long = n >> (LONG_PIECE.bit_length() - 1)

        def long_piece(i, c):
            copy(src, dst, i * LONG_PIECE, LONG_PIECE)
            return c

        lax.fori_loop(0, n_long, long_piece, 0)
        for bit in SHORT_PIECES:
            @pl.when((n & bit) != 0)
            def _():
                copy(src, dst, n & (-2 * bit), bit)
        return src + n

    lax.fori_loop(0, N_EXPERTS, segment, 0)


def _dispatch_kernel(seg_n_ref, seg_row_ref, ridx_ref, u2_ref, xs_ref, sorted_scr, sems):
    tile = pl.program_id(0)
    n_tiles = pl.num_programs(0)
    slot = tile % 2
    tt = u2_ref.shape[0]
    rows = SORTED_ROWS

    def copies(tile_, slot_, start):
        _segment_copies(seg_n_ref, seg_row_ref, tile_, sorted_scr.at[slot_], xs_ref, sems.at[slot_],
                        to_hbm=True, start=start)

    @pl.when(tile == 0)
    def _():
        sorted_scr[1, 0:LONG_PIECE * ROW_WORDS, :] = jnp.zeros((LONG_PIECE * ROW_WORDS, LANES), jnp.uint32)
        for start in (True, False):
            _segment_copies(seg_n_ref, seg_row_ref, n_tiles, sorted_scr.at[1], xs_ref, sems.at[1],
                            to_hbm=True, start=start, consecutive=False)

    pos_t = ridx_ref[...].astype(F32).T
    j = lax.broadcasted_iota(jnp.int32, (rows, tt), 0).astype(F32)
    onehot = jnp.zeros((rows, tt), F32)
    for k in range(TOP_K):
        onehot = jnp.where(j == pos_t[TOP_K + k:TOP_K + k + 1, :], 1.0, onehot)
    _to_packed_rows(sorted_scr.at[slot], _dot(onehot.astype(BF16), u2_ref[...]))
    copies(tile, slot, True)

    @pl.when(tile > 0)
    def _():
        copies(tile - 1, 1 - slot, False)

    @pl.when(tile == n_tiles - 1)
    def _():
        copies(tile, slot, False)


def _dispatch(seg_n, seg_row, ridx, u2, n_rows, tt=MERGE_ROWS):
    t = u2.shape[0]
    grid_spec = pltpu.PrefetchScalarGridSpec(
        num_scalar_prefetch=2,
        grid=(t // tt,),
        in_specs=[pl.BlockSpec((tt, LANES), lambda i, *_: (i, 0)),
                  pl.BlockSpec((tt, D_MODEL), lambda i, *_: (i, 0))],
        out_specs=pl.BlockSpec(memory_space=pl.ANY),
        scratch_shapes=[pltpu.VMEM((2, SORTED_ROWS * ROW_WORDS, LANES), jnp.uint32),
                        pltpu.SemaphoreType.DMA((2,))],
    )
    return pl.pallas_call(
        _dispatch_kernel,
        grid_spec=grid_spec,
        out_shape=jax.ShapeDtypeStruct((n_rows * ROW_WORDS, LANES), jnp.uint32),
        compiler_params=pltpu.CompilerParams(
            dimension_semantics=("arbitrary",), has_side_effects=True, vmem_limit_bytes=VMEM_LIMIT),
        name="dispatch",
    )(seg_n, seg_row, ridx, u2)


def _expert_kernel(be_ref, nb_ref, xs_ref, w1_ref, b1_ref, w2_ref, b2_ref, ys_ref):
    del be_ref
    i = pl.program_id(0)

    @pl.when(i < nb_ref[0])
    def _():
        hdn = _dot(_from_packed_rows(xs_ref).astype(BF16), w1_ref[0]) + b1_ref[0]
        glu = jnp.minimum(hdn[:, :D_FF], SWIGLU_LIMIT)
        lin = jnp.clip(hdn[:, D_FF:], -SWIGLU_LIMIT, SWIGLU_LIMIT)
        act = (lin + 1.0) * (glu * jax.nn.sigmoid(SWIGLU_ALPHA * glu))
        y = _dot(act.astype(BF16), w2_ref[0]) + b2_ref[0]
        _to_packed_rows(ys_ref, y.astype(BF16).astype(F32))

    @pl.when(i >= nb_ref[0])
    def _():
        ys_ref[...] = jnp.zeros_like(ys_ref)


def _expert(block_e, nb_used, xs, w1, b1, w2, b2):
    block = (EXPERT_ROWS * ROW_WORDS, LANES)
    n_blocks = xs.shape[0] // block[0]
    grid_spec = pltpu.PrefetchScalarGridSpec(
        num_scalar_prefetch=2,
        grid=(n_blocks,),
        in_specs=[pl.BlockSpec(block, lambda i, be, nb: (jnp.minimum(i, nb[0] - 1), 0)),
                  pl.BlockSpec((1, D_MODEL, 2 * D_FF), lambda i, be, nb: (be[i], 0, 0)),
                  pl.BlockSpec((1, 1, 2 * D_FF), lambda i, be, nb: (be[i], 0, 0)),
                  pl.BlockSpec((1, D_FF, D_MODEL), lambda i, be, nb: (be[i], 0, 0)),
                  pl.BlockSpec((1, 1, D_MODEL), lambda i, be, nb: (be[i], 0, 0))],
        out_specs=pl.BlockSpec(block, lambda i, be, nb: (i, 0)),
    )
    return pl.pallas_call(
        _expert_kernel,
        grid_spec=grid_spec,
        out_shape=jax.ShapeDtypeStruct(xs.shape, xs.dtype),
        compiler_params=pltpu.CompilerParams(
            dimension_semantics=("arbitrary",), vmem_limit_bytes=VMEM_LIMIT),
        name="expert",
    )(block_e, nb_used, xs, w1, b1, w2, b2)


def _combine_kernel(seg_n_ref, seg_row_ref, ridx_ref, rw_ref, h_ref, mod_ref, ys_ref, o_ref, sorted_scr, sems):
    tile = pl.program_id(0)
    n_tiles = pl.num_programs(0)
    slot = tile % 2
    tt = h_ref.shape[0]
    rows = SORTED_ROWS

    def copies(tile_, slot_, start):
        _segment_copies(seg_n_ref, seg_row_ref, tile_, sorted_scr.at[slot_], ys_ref, sems.at[slot_],
                        to_hbm=False, start=start)

    @pl.when(tile == 0)
    def _():
        sorted_scr[...] = jnp.zeros_like(sorted_scr)
        copies(tile, slot, True)

    @pl.when(tile + 1 < n_tiles)
    def _():
        copies(tile + 1, 1 - slot, True)

    ridx = ridx_ref[...].astype(F32)
    rw = rw_ref[...]
    j = lax.broadcasted_iota(jnp.int32, (tt, rows), 1).astype(F32)
    wmat = jnp.zeros((tt, rows), F32)
    for k in range(TOP_K):
        wmat = jnp.where(j == ridx[:, TOP_K + k:TOP_K + k + 1], rw[:, k:k + 1], wmat)
    w_hi, w_lo = _split_bf16(wmat)
    copies(tile, slot, False)
    y = _from_packed_rows(sorted_scr.at[slot]).astype(BF16)
    o_ref[...] = h_ref[...] + mod_ref[0][5:6] * (_dot(w_hi, y) + _dot(w_lo, y))


def _combine(seg_n, seg_row, ridx, rw, h, mod, ys, seq, tt=MERGE_ROWS):
    t = h.shape[0]
    per_seq = seq // tt
    grid_spec = pltpu.PrefetchScalarGridSpec(
        num_scalar_prefetch=2,
        grid=(t // tt,),
        in_specs=[pl.BlockSpec((tt, LANES), lambda i, *_: (i, 0)),
                  pl.BlockSpec((tt, LANES), lambda i, *_: (i, 0)),
                  pl.BlockSpec((tt, D_MODEL), lambda i, *_: (i, 0)),
                  pl.BlockSpec((1, 6, D_MODEL), lambda i, *_: (i // per_seq, 0, 0)),
                  pl.BlockSpec(memory_space=pl.ANY)],
        out_specs=pl.BlockSpec((tt, D_MODEL), lambda i, *_: (i, 0)),
        scratch_shapes=[pltpu.VMEM((2, SORTED_ROWS * ROW_WORDS, LANES), jnp.uint32),
                        pltpu.SemaphoreType.DMA((2,))],
    )
    return pl.pallas_call(
        _combine_kernel,
        grid_spec=grid_spec,
        out_shape=jax.ShapeDtypeStruct((t, D_MODEL), F32),
        compiler_params=pltpu.CompilerParams(
            dimension_semantics=("arbitrary",), vmem_limit_bytes=VMEM_LIMIT),
        name="combine",
    )(seg_n, seg_row, ridx, rw, h, mod, ys)


def _rope_tables(seq, ts):
    half = ROPE_DIM // 2
    inv_freq = jnp.power(jnp.float32(ROPE_THETA), -jnp.arange(half, dtype=F32) * 2.0 / ROPE_DIM)
    ang = jnp.arange(seq, dtype=F32)[:, None] * inv_freq[None, :]
    cos, sin = jnp.cos(ang), jnp.sin(ang)
    ones = jnp.ones((seq, HEAD_DIM - ROPE_DIM), F32)
    zeros = jnp.zeros_like(ones)
    zh = jnp.zeros((seq, half), F32)
    cos_t = jnp.concatenate([cos, cos, ones], axis=1)
    sin_prev = jnp.concatenate([zh, sin, zeros], axis=1)
    sin_next = jnp.concatenate([-sin, zh, zeros], axis=1)
    tables = jnp.stack([jnp.tile(t, (1, LANES // HEAD_DIM)) for t in (cos_t, sin_prev, sin_next)])
    per_group = []
    for d in DILATIONS:
        pos = jnp.arange(seq).reshape(seq // ts, ts // d, d).transpose(0, 2, 1).reshape(seq)
        per_group.append(tables[:, pos, :])
    return jnp.stack(per_group)


def _dft_tables(seq):
    k = jnp.arange(seq, dtype=jnp.int32)
    ang = ((k[:, None] * k[None, :]) % seq).astype(F32) * (2.0 * jnp.pi / seq)
    scale = seq ** -0.5
    cs = jnp.cos(ang) * scale
    ns = -jnp.sin(ang) * scale
    c = jnp.arange(FOURIER_GROUP_DIM, dtype=jnp.int32)
    angc = ((c[:, None] * c[None, :]) % FOURIER_GROUP_DIM).astype(F32) * (2.0 * jnp.pi / FOURIER_GROUP_DIM)
    eye = jnp.eye(ATTN_WIDTH // FOURIER_GROUP_DIM, dtype=F32)
    cscale = FOURIER_GROUP_DIM ** -0.5
    bd = jnp.concatenate([jnp.kron(eye, jnp.cos(angc) * cscale),
                          jnp.kron(eye, jnp.sin(angc) * cscale)], axis=1)
    return bd.astype(BF16), cs.astype(BF16), ns.astype(BF16)


def _trunk(x, c, p):
    b, s, _ = x.shape
    t = b * s
    mod = _ada(c, p["w_ada"], p["b_ada"]).reshape(b, 6, D_MODEL)
    *qkvs, fz = _inproj(x, mod, p["g1"], p["w_qkvf"], p["gains"], p["rope"], p["gmat"])
    os_, lses = zip(*[_attn(qkv, g) for g, qkv in enumerate(qkvs)])
    fm = _fourier(fz, p["bd"], p["cs"], p["ns"])
    h, u2, ridx, rw, tcnt, cnt = _merge(
        x, mod, os_, lses, fm, p["g1"], p["g2"], p["wg"], p["wap"], p["wfp"], p["wout"], p["wrh"],
        p["wrl"], p["br"], p["expand"], p["ltri"], p["utri"], p["perms"])

    rows = EXPERT_ROWS
    counts = cnt[0, :N_EXPERTS].astype(jnp.int32)
    padded = (counts + rows - 1) // rows * rows
    pend = jnp.cumsum(padded)
    pstart = pend - padded
    seg_row = jnp.concatenate([pstart[None, :] + tcnt[:, 0, :N_EXPERTS], (pstart + counts)[None, :]]).reshape(-1)
    seg_n = jnp.concatenate([tcnt[:, 1, :N_EXPERTS], (padded - counts)[None, :]]).reshape(-1)
    n_blocks = -(-(t * TOP_K + N_EXPERTS * (t // MERGE_ROWS + rows - 1)) // rows)
    block_start = jnp.arange(n_blocks, dtype=jnp.int32) * rows
    block_e = jnp.minimum(jnp.sum((pend[None, :] <= block_start[:, None]).astype(jnp.int32), axis=-1),
                          N_EXPERTS - 1)
    nb_used = (pend[-1:] // rows).astype(jnp.int32)

    ridx = ridx.reshape(t, LANES)
    xs = _dispatch(seg_n, seg_row, ridx, u2.reshape(t, D_MODEL), n_blocks * rows)
    ys = _expert(block_e, nb_used, xs, p["w1"], p["b1"], p["w2"], p["b2"])
    out = _combine(seg_n, seg_row, ridx, rw.reshape(t, LANES), h.reshape(t, D_MODEL), mod, ys, s)
    return out.reshape(b, s, D_MODEL)


def kernel(x_prompt, x_sample, c_prompt, c_sample, w_ada, b_ada, norm1_g, norm2_g, w_in, q_gain, k_gain,
           w_attn_proj, w_fourier_proj, w_out, w_router, b_router, w1, b1, w2, b2):
    seq = x_prompt.shape[1]
    assert w_ada.shape[0] == 1, "single-layer trunk"
    w_in0 = w_in[0]
    gains = jnp.zeros((8, ATTN_WIDTH), F32)
    for g in range(N_GROUPS):
        gains = gains.at[2 * g].set(jnp.tile(q_gain[0, g], HEADS))
        gains = gains.at[2 * g + 1].set(jnp.tile(k_gain[0, g], HEADS))
    heads = jnp.arange(ATTN_WIDTH) // HEAD_DIM
    gmat = ((heads[:, None] == heads[None, :]).astype(F32) / HEAD_DIM).astype(BF16)
    expand = (jnp.arange(LANES)[:, None] == heads[None, :]).astype(BF16)
    ltri = (jnp.arange(MERGE_ROWS)[:, None] > jnp.arange(MERGE_ROWS)[None, :]).astype(BF16)
    utri = (jnp.arange(LANES)[:, None] < jnp.arange(LANES)[None, :]).astype(BF16)
    perms = []
    for d in DILATIONS[1:]:
        n = MERGE_ROWS // d
        src = (jnp.arange(MERGE_ROWS) % d) * n + jnp.arange(MERGE_ROWS) // d
        perms.append((src[:, None] == jnp.arange(MERGE_ROWS)[None, :]).astype(BF16))
    wr = jnp.zeros((D_MODEL, LANES), F32).at[:, :N_EXPERTS].set(w_router[0])
    wrh = wr.astype(BF16)
    wrl = (wr - wrh.astype(F32)).astype(BF16)
    br = jnp.zeros((1, LANES), F32).at[0, :N_EXPERTS].set(b_router[0])
    bd, cs, ns = _dft_tables(seq)
    p = dict(
        w_ada=w_ada[0], b_ada=b_ada[0], g1=norm1_g, g2=norm2_g,
        w_qkvf=w_in0[:, :QKVF_WIDTH].astype(BF16), wg=w_in0[:, QKVF_WIDTH:].astype(BF16),
        gains=gains, rope=_rope_tables(seq, INPROJ_ROWS), gmat=gmat, bd=bd, cs=cs, ns=ns,
        wap=w_attn_proj[0].astype(BF16), wfp=w_fourier_proj[0].astype(BF16), wout=w_out[0].astype(BF16),
        wrh=wrh, wrl=wrl, br=br, expand=expand, ltri=ltri, utri=utri, perms=perms,
        w1=w1[0].astype(BF16), b1=b1[0].reshape(N_EXPERTS, 1, 2 * D_FF),
        w2=w2[0].astype(BF16), b2=b2[0].reshape(N_EXPERTS, 1, D_MODEL),
    )
    return _trunk(x_prompt, c_prompt, p), _trunk(x_sample, c_sample, p)
```

```python
import functools

import jax
import jax.numpy as jnp
from jax import lax
from jax.experimental import pallas as pl
from jax.experimental.pallas import tpu as pltpu

F32 = jnp.float32
BF16 = jnp.bfloat16

D_MODEL = 1024
HEAD_DIM = 64
HEADS = 8
ATTN_WIDTH = HEADS * HEAD_DIM
DILATIONS = (1, 4, 16)
BAND = 64
N_GROUPS = 3
QKVF_WIDTH = 5120
FOURIER_GROUP_DIM = 64
ROPE_DIM = 16
ROPE_THETA = 500000.0
N_EXPERTS = 32
TOP_K = 4
D_FF = 1024
SWIGLU_ALPHA = 1.702
SWIGLU_LIMIT = 7.0
NORM_EPS = 1e-6
NEG_INF = -1e30
LN2 = 0.6931471805599453
LOG2E = 1.4426950408889634

LANES = 128
ROW_WORDS = D_MODEL // LANES // 2
SORTED_ROWS = 1088
EXPERT_ROWS = 512
LONG_PIECE = 64
SHORT_PIECES = (32, 16, 8, 4, 2)
INPROJ_ROWS = 512
MERGE_ROWS = 256
VMEM_LIMIT = 56 * 1024 * 1024


def _dot(a, b):
    return jnp.dot(a, b, preferred_element_type=F32)


def _split_bf16(v):
    hi = v.astype(BF16)
    lo = (v - hi.astype(F32)).astype(BF16)
    return hi, lo


def _rms_mod(x, gain, scale, shift):
    ms = jnp.mean(x * x, axis=-1, keepdims=True)
    return x * lax.rsqrt(ms + NORM_EPS) * gain * (1.0 + scale) + shift


def _const_spec(shape):
    return pl.BlockSpec(shape, lambda *_: (0,) * len(shape), pipeline_mode=pl.Buffered(1))


def _ada_kernel(c_ref, w_ref, b_ref, o_ref):
    c = c_ref[...]
    a = c * jax.nn.sigmoid(c)
    a_hi, a_lo = _split_bf16(a)
    w_hi, w_lo = _split_bf16(w_ref[...])
    o_ref[...] = _dot(a_hi, w_hi) + _dot(a_lo, w_hi) + _dot(a_hi, w_lo) + b_ref[...]


def _ada(c, w_ada, b_ada):
    b = c.shape[0]
    n = w_ada.shape[1]
    tn = D_MODEL
    return pl.pallas_call(
        _ada_kernel,
        grid=(n // tn,),
        in_specs=[pl.BlockSpec((b, D_MODEL), lambda j: (0, 0)),
                  pl.BlockSpec((D_MODEL, tn), lambda j: (0, j)),
                  pl.BlockSpec((1, tn), lambda j: (0, j))],
        out_specs=pl.BlockSpec((b, tn), lambda j: (0, j)),
        out_shape=jax.ShapeDtypeStruct((b, n), F32),
        compiler_params=pltpu.CompilerParams(vmem_limit_bytes=VMEM_LIMIT),
        name="ada",
    )(c, w_ada, b_ada.reshape(1, n))


def _inproj_kernel(x_ref, mod_ref, g1_ref, w_ref, gain_ref, rope_ref, gmat_ref,
                   o0_ref, o1_ref, o2_ref, of_ref, u_scr):
    x = x_ref[0]
    ts = x.shape[0]
    mod = mod_ref[0]
    u = _rms_mod(x, g1_ref[...], mod[1:2], mod[0:1])
    ub = u.astype(BF16)
    for c in range(D_MODEL // LANES):
        u_scr[c] = u[:, c * LANES:(c + 1) * LANES]
    for g, o_ref in enumerate((o0_ref, o1_ref, o2_ref)):
        d = DILATIONS[g]
        n = ts // d
        if d == 1:
            ug = ub
        else:
            ug = jnp.concatenate(
                [jnp.concatenate([u_scr[c, pl.ds(r, n, stride=d), :] for r in range(d)], axis=0)
                 for c in range(D_MODEL // LANES)], axis=1).astype(BF16)
        cos_t, sin_prev, sin_next = rope_ref[g, 0], rope_ref[g, 1], rope_ref[g, 2]
        for j in range(3):
            c0 = (3 * g + j) * ATTN_WIDTH
            z = _dot(ug, w_ref[:, c0:c0 + ATTN_WIDTH])
            if j < 2:
                ms = _dot((z * z).astype(BF16), gmat_ref[...])
                gain = gain_ref[2 * g + j:2 * g + j + 1, :]
                if j == 0:
                    gain = gain * (HEAD_DIM ** -0.5 * LOG2E)
                z = z * lax.rsqrt(ms + NORM_EPS) * gain
                blocks = []
                for blk in range(ATTN_WIDTH // LANES):
                    zb = z[:, blk * LANES:(blk + 1) * LANES]
                    blocks.append(zb * cos_t
                                  + pltpu.roll(zb, ROPE_DIM // 2, 1) * sin_prev
                                  + pltpu.roll(zb, LANES - ROPE_DIM // 2, 1) * sin_next)
                z = jnp.concatenate(blocks, axis=1)
            zb16 = z.astype(BF16)
            for r in range(d):
                o_ref[0, r, :, j * ATTN_WIDTH:(j + 1) * ATTN_WIDTH] = zb16[r * n:(r + 1) * n, :]
    of_ref[0] = _dot(ub, w_ref[:, 9 * ATTN_WIDTH:]).astype(BF16)


def _inproj(x, mod, g1, w_qkvf, gains, rope, gmat, ts=INPROJ_ROWS):
    b, s, _ = x.shape
    qkv_w = 3 * ATTN_WIDTH
    out_specs = [pl.BlockSpec((1, d, ts // d, qkv_w), lambda i, j: (i, 0, j, 0)) for d in DILATIONS]
    out_shape = [jax.ShapeDtypeStruct((b, d, s // d, qkv_w), BF16) for d in DILATIONS]
    return pl.pallas_call(
        _inproj_kernel,
        grid=(b, s // ts),
        in_specs=[pl.BlockSpec((1, ts, D_MODEL), lambda i, j: (i, j, 0)),
                  pl.BlockSpec((1, 6, D_MODEL), lambda i, j: (i, 0, 0)),
                  _const_spec((1, D_MODEL)),
                  _const_spec((D_MODEL, QKVF_WIDTH)),
                  _const_spec((8, ATTN_WIDTH)),
                  pl.BlockSpec((N_GROUPS, 3, ts, LANES), lambda i, j: (0, 0, j, 0)),
                  _const_spec((ATTN_WIDTH, ATTN_WIDTH))],
        out_specs=out_specs + [pl.BlockSpec((1, ts, ATTN_WIDTH), lambda i, j: (i, j, 0))],
        out_shape=out_shape + [jax.ShapeDtypeStruct((b, s, ATTN_WIDTH), BF16)],
        scratch_shapes=[pltpu.VMEM((D_MODEL // LANES, ts, LANES), F32)],
        compiler_params=pltpu.CompilerParams(
            dimension_semantics=("arbitrary", "arbitrary"), vmem_limit_bytes=VMEM_LIMIT),
        name="inproj",
    )(x, mod, g1, w_qkvf, gains, rope, gmat)


def _attn_kernel(qkv_ref, o_ref, lse_ref, s_scr, p_scr, *, length):
    tq = LANES
    tk = min(2 * LANES, length)
    nq = length // tq
    n_pairs = ATTN_WIDTH // LANES
    lane = lax.broadcasted_iota(jnp.int32, (tq, LANES), 1)
    first_head = lane < HEAD_DIM
    first_head_k = lax.broadcasted_iota(jnp.int32, (tk, LANES), 1) < HEAD_DIM
    rel = (lax.broadcasted_iota(jnp.int32, (tq, tk), 1)
           - lax.broadcasted_iota(jnp.int32, (tq, tk), 0))

    def q_block(it, carry):
        r = it // nq
        i = it % nq
        q0 = pl.multiple_of(i * tq, tq)
        ws = pl.multiple_of(jnp.clip(i * tq - BAND, 0, length - tk), BAND)
        valid = jnp.abs(rel + (ws - i * tq)) <= BAND

        for p in range(n_pairs):
            q2 = qkv_ref[0, r, pl.ds(q0, tq), p * LANES:(p + 1) * LANES]
            k2 = qkv_ref[0, r, pl.ds(ws, tk), ATTN_WIDTH + p * LANES:ATTN_WIDTH + (p + 1) * LANES]
            for hh in range(2):
                sel = first_head if hh == 0 else jnp.logical_not(first_head)
                qa = jnp.where(sel, q2, jnp.zeros_like(q2))
                s = lax.dot_general(qa, k2, (((1,), (1,)), ((), ())), preferred_element_type=F32)
                s_scr[2 * p + hh] = jnp.where(valid, s, NEG_INF)

        m_tile = jnp.zeros((tq, LANES), F32)
        for h in range(HEADS):
            s = s_scr[h]
            m = jnp.max(s, axis=-1, keepdims=True)
            p_scr[h] = jnp.exp2(s - m).astype(BF16)
            m_tile = jnp.where(lane == h, m, m_tile)

        den_tile = jnp.ones((tq, LANES), F32)
        for p in range(n_pairs):
            cs = slice(2 * ATTN_WIDTH + p * LANES, 2 * ATTN_WIDTH + (p + 1) * LANES)
            v2 = qkv_ref[0, r, pl.ds(ws, tk), cs]
            one = jnp.ones_like(v2)
            o_a = _dot(p_scr[2 * p], jnp.where(first_head_k, v2, one))
            o_b = _dot(p_scr[2 * p + 1], jnp.where(first_head_k, one, v2))
            den_ba = jnp.where(first_head, o_b, o_a)
            den_ab = pltpu.roll(den_ba, HEAD_DIM, 1)
            o_ref[0, r, pl.ds(q0, tq), p * LANES:(p + 1) * LANES] = (
                jnp.where(first_head, o_a, o_b) / den_ab).astype(BF16)
            den_tile = jnp.where(lane == 2 * p, den_ab, den_tile)
            den_tile = jnp.where(lane == 2 * p + 1, den_ba, den_tile)
        lse_ref[0, r, pl.ds(q0, tq), :] = (m_tile + jnp.log2(den_tile)) * LN2
        return carry

    lax.fori_loop(0, qkv_ref.shape[1] * nq, q_block, 0)


def _attn(qkv, g):
    b, d, length, width = qkv.shape
    tk = min(2 * LANES, length)
    return pl.pallas_call(
        functools.partial(_attn_kernel, length=length),
        grid=(b,),
        in_specs=[pl.BlockSpec((1, d, length, width), lambda i: (i, 0, 0, 0))],
        out_specs=[pl.BlockSpec((1, d, length, ATTN_WIDTH), lambda i: (i, 0, 0, 0)),
                   pl.BlockSpec((1, d, length, LANES), lambda i: (i, 0, 0, 0))],
        out_shape=[jax.ShapeDtypeStruct((b, d, length, ATTN_WIDTH), BF16),
                   jax.ShapeDtypeStruct((b, d, length, LANES), F32)],
        scratch_shapes=[pltpu.VMEM((HEADS, LANES, tk), F32), pltpu.VMEM((HEADS, LANES, tk), BF16)],
        compiler_params=pltpu.CompilerParams(
            dimension_semantics=("arbitrary",), vmem_limit_bytes=VMEM_LIMIT),
        name=f"attn{g}",
    )(qkv)


def _fourier_kernel(x_ref, bd_ref, cs_ref, ns_ref, o_ref, y_ref, *, seq, rows):
    for r in range(seq // rows):
        rs = slice(r * rows, (r + 1) * rows)
        y_ref[rs, :] = _dot(x_ref[0, rs, :], bd_ref[...]).astype(BF16)
    for r in range(seq // rows):
        rs = slice(r * rows, (r + 1) * rows)
        o = _dot(cs_ref[rs, :], y_ref[:, :ATTN_WIDTH]) + _dot(ns_ref[rs, :], y_ref[:, ATTN_WIDTH:])
        o_ref[0, rs, :] = o.astype(BF16)


def _fourier(qkvf, bd, cs, ns):
    b, s, _ = qkvf.shape
    return pl.pallas_call(
        functools.partial(_fourier_kernel, seq=s, rows=256),
        grid=(b,),
        in_specs=[pl.BlockSpec((1, s, ATTN_WIDTH), lambda i: (i, 0, 0)),
                  _const_spec((ATTN_WIDTH, 2 * ATTN_WIDTH)),
                  _const_spec((s, s)),
                  _const_spec((s, s))],
        out_specs=pl.BlockSpec((1, s, ATTN_WIDTH), lambda i: (i, 0, 0)),
        out_shape=jax.ShapeDtypeStruct((b, s, ATTN_WIDTH), BF16),
        scratch_shapes=[pltpu.VMEM((s, 2 * ATTN_WIDTH), BF16)],
        compiler_params=pltpu.CompilerParams(
            dimension_semantics=("arbitrary",), vmem_limit_bytes=VMEM_LIMIT),
        name="fourier",
    )(qkvf, bd, cs, ns)


def _merge_kernel(x_ref, mod_ref, o0_ref, o1_ref, o2_ref, l0_ref, l1_ref, l2_ref, fm_ref,
                  g1_ref, g2_ref, wg_ref, wap_ref, wfp_ref, wout_ref, wrh_ref, wrl_ref, br_ref,
                  expand_ref, ltri_ref, utri_ref, perm1_ref, perm2_ref,
                  h_ref, u2_ref, ridx_ref, rw_ref, tcnt_ref, cnt_ref, cnt_scr):
    @pl.when((pl.program_id(0) == 0) & (pl.program_id(1) == 0))
    def _():
        cnt_scr[...] = jnp.zeros_like(cnt_scr)

    x = x_ref[0]
    ts = x.shape[0]
    mod = mod_ref[0]
    shift1, scale1, gate1 = mod[0:1], mod[1:2], mod[2:3]
    shift2, scale2 = mod[3:4], mod[4:5]
    u = _rms_mod(x, g1_ref[...], scale1, shift1).astype(BF16)
    gates = _dot(u, wg_ref[...])

    os_ = [o0_ref[0, 0].astype(F32)]
    lses = [l0_ref[0, 0]]
    for o_ref, l_ref, perm_ref in ((o1_ref, l1_ref, perm1_ref), (o2_ref, l2_ref, perm2_ref)):
        perm = perm_ref[...]
        os_.append(_dot(perm, o_ref[0].reshape(ts, ATTN_WIDTH)))
        l_hi, l_lo = _split_bf16(l_ref[0].reshape(ts, LANES))
        lses.append(_dot(perm, l_hi) + _dot(perm, l_lo))

    mx = jnp.maximum(jnp.maximum(lses[0], lses[1]), lses[2])
    es = [jnp.exp(l - mx) for l in lses]
    den = es[0] + es[1] + es[2]
    attn = None
    for e, o in zip(es, os_):
        w_hi, w_lo = _split_bf16(e / den)
        wf = _dot(w_hi, expand_ref[...]) + _dot(w_lo, expand_ref[...])
        term = wf * o
        attn = term if attn is None else attn + term

    a = _dot(attn.astype(BF16), wap_ref[...])
    f = _dot(fm_ref[0], wfp_ref[...])
    merged = jax.nn.sigmoid(gates[:, :D_MODEL]) * a + jax.nn.sigmoid(gates[:, D_MODEL:]) * f
    h = x + gate1 * _dot(merged.astype(BF16), wout_ref[...])
    h_ref[0] = h

    u2 = _rms_mod(h, g2_ref[...], scale2, shift2)
    u_hi, u_lo = _split_bf16(u2)
    u2_ref[0] = u_hi
    logits = (_dot(u_hi, wrh_ref[...]) + _dot(u_lo, wrh_ref[...]) + _dot(u_hi, wrl_ref[...])
              + br_ref[...])

    lane = lax.broadcasted_iota(jnp.int32, (ts, LANES), 1)
    lg = jnp.where(lane < N_EXPERTS, logits, -jnp.inf)
    vals, ids = [], []
    onehot = jnp.zeros((ts, LANES), F32)
    for _ in range(TOP_K):
        m = jnp.max(lg, axis=-1, keepdims=True)
        idx = jnp.min(jnp.where(lg == m, lane, LANES), axis=-1, keepdims=True)
        hit = lane == idx
        vals.append(m)
        ids.append(idx)
        onehot = jnp.where(hit, 1.0, onehot)
        lg = jnp.where(hit, -jnp.inf, lg)
    exps = [jnp.exp(v - vals[0]) for v in vals]
    esum = exps[0] + exps[1] + exps[2] + exps[3]

    n_tile = jnp.sum(onehot, axis=0, keepdims=True)
    n_tile = n_tile + (n_tile - 2.0 * jnp.floor(0.5 * n_tile))
    seg_start = _dot(jnp.broadcast_to(n_tile, (8, LANES)).astype(BF16), utri_ref[...])[0:1, :]
    pos_all = _dot(ltri_ref[...], onehot.astype(BF16)) + seg_start
    ridx = jnp.zeros((ts, LANES), F32)
    rw = jnp.zeros((ts, LANES), F32)
    for k in range(TOP_K):
        pos = jnp.sum(jnp.where(lane == ids[k], pos_all, 0.0), axis=-1, keepdims=True)
        ridx = jnp.where(lane == k, ids[k].astype(F32), ridx)
        ridx = jnp.where(lane == TOP_K + k, pos, ridx)
        rw = jnp.where(lane == k, exps[k] / esum, rw)
    ridx_ref[0] = ridx.astype(jnp.int32)
    rw_ref[0] = rw
    row = lax.broadcasted_iota(jnp.int32, (8, LANES), 0)
    tcnt = jnp.where(row == 0, cnt_scr[...], jnp.where(row == 1, jnp.broadcast_to(n_tile, (8, LANES)), 0.0))
    tcnt_ref[0] = tcnt.astype(jnp.int32)
    cnt_scr[...] = cnt_scr[...] + n_tile
    cnt_ref[...] = cnt_scr[...]


def _merge(x, mod, os_, lses, fm, g1, g2, wg, wap, wfp, wout, wrh, wrl, br, expand, ltri, utri, perms,
           ts=MERGE_ROWS):
    b, s, _ = x.shape
    tile = lambda w: pl.BlockSpec((1, ts, w), lambda i, j: (i, j, 0))
    strided = lambda w: [pl.BlockSpec((1, d, ts // d, w), lambda i, j: (i, 0, j, 0)) for d in DILATIONS]
    consts = (g1, g2, wg, wap, wfp, wout, wrh, wrl, br, expand, ltri, utri) + tuple(perms)
    in_specs = ([tile(D_MODEL), pl.BlockSpec((1, 6, D_MODEL), lambda i, j: (i, 0, 0))]
                + strided(ATTN_WIDTH) + strided(LANES) + [tile(ATTN_WIDTH)]
                + [_const_spec(a.shape) for a in consts])
    per_seq = s // ts
    return pl.pallas_call(
        _merge_kernel,
        grid=(b, per_seq),
        in_specs=in_specs,
        out_specs=[tile(D_MODEL), tile(D_MODEL), tile(LANES), tile(LANES),
                   pl.BlockSpec((1, 8, LANES), lambda i, j: (i * per_seq + j, 0, 0)),
                   pl.BlockSpec((8, LANES), lambda i, j: (0, 0))],
        out_shape=[jax.ShapeDtypeStruct((b, s, D_MODEL), F32),
                   jax.ShapeDtypeStruct((b, s, D_MODEL), BF16),
                   jax.ShapeDtypeStruct((b, s, LANES), jnp.int32),
                   jax.ShapeDtypeStruct((b, s, LANES), F32),
                   jax.ShapeDtypeStruct((b * per_seq, 8, LANES), jnp.int32),
                   jax.ShapeDtypeStruct((8, LANES), F32)],
        scratch_shapes=[pltpu.VMEM((8, LANES), F32)],
        compiler_params=pltpu.CompilerParams(
            dimension_semantics=("arbitrary", "arbitrary"), vmem_limit_bytes=VMEM_LIMIT),
        name="merge",
    )(x, mod, *os_, *lses, fm, *consts)


def _to_packed_rows(ref, mat):
    bits = lax.bitcast_convert_type(mat, jnp.uint32)
    for c in range(ROW_WORDS):
        lo = bits[:, c * LANES:(c + 1) * LANES] >> 16
        hi = bits[:, (c + ROW_WORDS) * LANES:(c + ROW_WORDS + 1) * LANES] & jnp.uint32(0xFFFF0000)
        ref[pl.ds(c, mat.shape[0], stride=ROW_WORDS), :] = lo | hi


def _from_packed_rows(ref):
    n = ref.shape[0] // ROW_WORDS
    words = [ref[pl.ds(c, n, stride=ROW_WORDS), :] for c in range(ROW_WORDS)]
    lo = [lax.bitcast_convert_type(w << 16, F32) for w in words]
    hi = [lax.bitcast_convert_type(w & jnp.uint32(0xFFFF0000), F32) for w in words]
    return jnp.concatenate(lo + hi, axis=1)


def _segment_copies(seg_n_ref, seg_row_ref, tile, sorted_ref, hbm_ref, sem, *, to_hbm, start, consecutive=True):
    base = tile * N_EXPERTS

    def copy(src, dst, off, size):
        vm_row = (src + off) if consecutive else 0
        vm = sorted_ref.at[pl.ds(pl.multiple_of(vm_row * ROW_WORDS, 2 * ROW_WORDS), size * ROW_WORDS)]
        hb = hbm_ref.at[pl.ds(pl.multiple_of((dst + off) * ROW_WORDS, 2 * ROW_WORDS), size * ROW_WORDS)]
        cp = pltpu.make_async_copy(vm, hb, sem) if to_hbm else pltpu.make_async_copy(hb, vm, sem)
        if start:
            cp.start()
        else:
            cp.wait()

    def segment(e, src):
        n = seg_n_ref[base + e]
        dst = seg_row_ref[base + e]
        n_long = n >> (LONG_PIECE.bit_length() - 1)

        def long_piece(i, c):
            copy(src, dst, i * LONG_PIECE, LONG_PIECE)
            return c

        lax.fori_loop(0, n_long, long_piece, 0)
        for bit in SHORT_PIECES:
            @pl.when((n & bit) != 0)
            def _():
                copy(src, dst, n & (-2 * bit), bit)
        return src + n

    lax.fori_loop(0, N_EXPERTS, segment, 0)


def _wait_tile_copies(seg_n_ref, tile, sorted_ref, hbm_ref, sem, *, to_hbm):
    base = tile * N_EXPERTS
    total = lax.fori_loop(0, N_EXPERTS, lambda e, acc: acc + seg_n_ref[base + e], 0)

    def wait(size):
        vm = sorted_ref.at[pl.ds(0, size * ROW_WORDS)]
        hb = hbm_ref.at[pl.ds(0, size * ROW_WORDS)]
        (pltpu.make_async_copy(vm, hb, sem) if to_hbm else pltpu.make_async_copy(hb, vm, sem)).wait()

    def long_wait(i, c):
        wait(LONG_PIECE)
        return c

    lax.fori_loop(0, total >> (LONG_PIECE.bit_length() - 1), long_wait, 0)
    for bit in SHORT_PIECES:
        @pl.when((total & bit) != 0)
        def _():
            wait(bit)


def _dispatch_kernel(seg_n_ref, seg_row_ref, ridx_ref, u2_ref, xs_ref, sorted_scr, sems):
    tile = pl.program_id(0)
    n_tiles = pl.num_programs(0)
    slot = tile % 2
    tt = u2_ref.shape[0]
    rows = SORTED_ROWS

    def copies(tile_, slot_, start):
        if start:
            _segment_copies(seg_n_ref, seg_row_ref, tile_, sorted_scr.at[slot_], xs_ref, sems.at[slot_],
                            to_hbm=True, start=True)
        else:
            _wait_tile_copies(seg_n_ref, tile_, sorted_scr.at[slot_], xs_ref, sems.at[slot_], to_hbm=True)

    @pl.when(tile == 0)
    def _():
        sorted_scr[1, 0:LONG_PIECE * ROW_WORDS, :] = jnp.zeros((LONG_PIECE * ROW_WORDS, LANES), jnp.uint32)
        for start in (True, False):
            _segment_copies(seg_n_ref, seg_row_ref, n_tiles, sorted_scr.at[1], xs_ref, sems.at[1],
                            to_hbm=True, start=start, consecutive=False)

    pos_t = ridx_ref[...].astype(F32).T
    j = lax.broadcasted_iota(jnp.int32, (rows, tt), 0).astype(F32)
    onehot = jnp.zeros((rows, tt), F32)
    for k in range(TOP_K):
        onehot = jnp.where(j == pos_t[TOP_K + k:TOP_K + k + 1, :], 1.0, onehot)
    _to_packed_rows(sorted_scr.at[slot], _dot(onehot.astype(BF16), u2_ref[...]))
    copies(tile, slot, True)

    @pl.when(tile > 0)
    def _():
        copies(tile - 1, 1 - slot, False)

    @pl.when(tile == n_tiles - 1)
    def _():
        copies(tile, slot, False)


def _dispatch(seg_n, seg_row, ridx, u2, n_rows, tt=MERGE_ROWS):
    t = u2.shape[0]
    grid_spec = pltpu.PrefetchScalarGridSpec(
        num_scalar_prefetch=2,
        grid=(t // tt,),
        in_specs=[pl.BlockSpec((tt, LANES), lambda i, *_: (i, 0)),
                  pl.BlockSpec((tt, D_MODEL), lambda i, *_: (i, 0))],
        out_specs=pl.BlockSpec(memory_space=pl.ANY),
        scratch_shapes=[pltpu.VMEM((2, SORTED_ROWS * ROW_WORDS, LANES), jnp.uint32),
                        pltpu.SemaphoreType.DMA((2,))],
    )
    return pl.pallas_call(
        _dispatch_kernel,
        grid_spec=grid_spec,
        out_shape=jax.ShapeDtypeStruct((n_rows * ROW_WORDS, LANES), jnp.uint32),
        compiler_params=pltpu.CompilerParams(
            dimension_semantics=("arbitrary",), has_side_effects=True, vmem_limit_bytes=VMEM_LIMIT),
        name="dispatch",
    )(seg_n, seg_row, ridx, u2)


def _expert_kernel(be_ref, nb_ref, xs_ref, w1_ref, b1_ref, w2_ref, b2_ref, ys_ref):
    del be_ref
    i = pl.program_id(0)

    @pl.when(i < nb_ref[0])
    def _():
        hdn = _dot(_from_packed_rows(xs_ref).astype(BF16), w1_ref[0]) + b1_ref[0]
        glu = jnp.minimum(hdn[:, :D_FF], SWIGLU_LIMIT)
        lin = jnp.clip(hdn[:, D_FF:], -SWIGLU_LIMIT, SWIGLU_LIMIT)
        act = (lin + 1.0) * (glu * jax.nn.sigmoid(SWIGLU_ALPHA * glu))
        y = _dot(act.astype(BF16), w2_ref[0]) + b2_ref[0]
        _to_packed_rows(ys_ref, y.astype(BF16).astype(F32))

    @pl.when(i >= nb_ref[0])
    def _():
        ys_ref[...] = jnp.zeros_like(ys_ref)


def _expert(block_e, nb_used, xs, w1, b1, w2, b2):
    block = (EXPERT_ROWS * ROW_WORDS, LANES)
    n_blocks = xs.shape[0] // block[0]
    grid_spec = pltpu.PrefetchScalarGridSpec(
        num_scalar_prefetch=2,
        grid=(n_blocks,),
        in_specs=[pl.BlockSpec(block, lambda i, be, nb: (jnp.minimum(i, nb[0] - 1), 0)),
                  pl.BlockSpec((1, D_MODEL, 2 * D_FF), lambda i, be, nb: (be[i], 0, 0)),
                  pl.BlockSpec((1, 1, 2 * D_FF), lambda i, be, nb: (be[i], 0, 0)),
                  pl.BlockSpec((1, D_FF, D_MODEL), lambda i, be, nb: (be[i], 0, 0)),
                  pl.BlockSpec((1, 1, D_MODEL), lambda i, be, nb: (be[i], 0, 0))],
        out_specs=pl.BlockSpec(block, lambda i, be, nb: (i, 0)),
    )
    return pl.pallas_call(
        _expert_kernel,
        grid_spec=grid_spec,
        out_shape=jax.ShapeDtypeStruct(xs.shape, xs.dtype),
        compiler_params=pltpu.CompilerParams(
            dimension_semantics=("arbitrary",), vmem_limit_bytes=VMEM_LIMIT),
        name="expert",
    )(block_e, nb_used, xs, w1, b1, w2, b2)


def _combine_kernel(seg_n_ref, seg_row_ref, ridx_ref, rw_ref, h_ref, mod_ref, ys_ref, o_ref, sorted_scr, sems):
    tile = pl.program_id(0)
    n_tiles = pl.num_programs(0)
    slot = tile % 2
    tt = h_ref.shape[0]
    rows = SORTED_ROWS

    def copies(tile_, slot_, start):
        if start:
            _segment_copies(seg_n_ref, seg_row_ref, tile_, sorted_scr.at[slot_], ys_ref, sems.at[slot_],
                            to_hbm=False, start=True)
        else:
            _wait_tile_copies(seg_n_ref, tile_, sorted_scr.at[slot_], ys_ref, sems.at[slot_], to_hbm=False)

    @pl.when(tile == 0)
    def _():
        sorted_scr[...] = jnp.zeros_like(sorted_scr)
        copies(tile, slot, True)

    @pl.when(tile + 1 < n_tiles)
    def _():
        copies(tile + 1, 1 - slot, True)

    ridx = ridx_ref[...].astype(F32)
    rw = rw_ref[...]
    j = lax.broadcasted_iota(jnp.int32, (tt, rows), 1).astype(F32)
    wmat = jnp.zeros((tt, rows), F32)
    for k in range(TOP_K):
        wmat = jnp.where(j == ridx[:, TOP_K + k:TOP_K + k + 1], rw[:, k:k + 1], wmat)
    w_hi, w_lo = _split_bf16(wmat)
    copies(tile, slot, False)
    y = _from_packed_rows(sorted_scr.at[slot]).astype(BF16)
    o_ref[...] = h_ref[...] + mod_ref[0][5:6] * (_dot(w_hi, y) + _dot(w_lo, y))


def _combine(seg_n, seg_row, ridx, rw, h, mod, ys, seq, tt=MERGE_ROWS):
    t = h.shape[0]
    per_seq = seq // tt
    grid_spec = pltpu.PrefetchScalarGridSpec(
        num_scalar_prefetch=2,
        grid=(t // tt,),
        in_specs=[pl.BlockSpec((tt, LANES), lambda i, *_: (i, 0)),
                  pl.BlockSpec((tt, LANES), lambda i, *_: (i, 0)),
                  pl.BlockSpec((tt, D_MODEL), lambda i, *_: (i, 0)),
                  pl.BlockSpec((1, 6, D_MODEL), lambda i, *_: (i // per_seq, 0, 0)),
                  pl.BlockSpec(memory_space=pl.ANY)],
        out_specs=pl.BlockSpec((tt, D_MODEL), lambda i, *_: (i, 0)),
        scratch_shapes=[pltpu.VMEM((2, SORTED_ROWS * ROW_WORDS, LANES), jnp.uint32),
                        pltpu.SemaphoreType.DMA((2,))],
    )
    return pl.pallas_call(
        _combine_kernel,
        grid_spec=grid_spec,
        out_shape=jax.ShapeDtypeStruct((t, D_MODEL), F32),
        compiler_params=pltpu.CompilerParams(
            dimension_semantics=("arbitrary",), vmem_limit_bytes=VMEM_LIMIT),
        name="combine",
    )(seg_n, seg_row, ridx, rw, h, mod, ys)


def _rope_tables(seq, ts):
    half = ROPE_DIM // 2
    inv_freq = jnp.power(jnp.float32(ROPE_THETA), -jnp.arange(half, dtype=F32) * 2.0 / ROPE_DIM)
    ang = jnp.arange(seq, dtype=F32)[:, None] * inv_freq[None, :]
    cos, sin = jnp.cos(ang), jnp.sin(ang)
    ones = jnp.ones((seq, HEAD_DIM - ROPE_DIM), F32)
    zeros = jnp.zeros_like(ones)
    zh = jnp.zeros((seq, half), F32)
    cos_t = jnp.concatenate([cos, cos, ones], axis=1)
    sin_prev = jnp.concatenate([zh, sin, zeros], axis=1)
    sin_next = jnp.concatenate([-sin, zh, zeros], axis=1)
    tables = jnp.stack([jnp.tile(t, (1, LANES // HEAD_DIM)) for t in (cos_t, sin_prev, sin_next)])
    per_group = []
    for d in DILATIONS:
        pos = jnp.arange(seq).reshape(seq // ts, ts // d, d).transpose(0, 2, 1).reshape(seq)
        per_group.append(tables[:, pos, :])
    return jnp.stack(per_group)


def _dft_tables(seq):
    k = jnp.arange(seq, dtype=jnp.int32)
    ang = ((k[:, None] * k[None, :]) % seq).astype(F32) * (2.0 * jnp.pi / seq)
    scale = seq ** -0.5
    cs = jnp.cos(ang) * scale
    ns = -jnp.sin(ang) * scale
    c = jnp.arange(FOURIER_GROUP_DIM, dtype=jnp.int32)
    angc = ((c[:, None] * c[None, :]) % FOURIER_GROUP_DIM).astype(F32) * (2.0 * jnp.pi / FOURIER_GROUP_DIM)
    eye = jnp.eye(ATTN_WIDTH // FOURIER_GROUP_DIM, dtype=F32)
    cscale = FOURIER_GROUP_DIM ** -0.5
    bd = jnp.concatenate([jnp.kron(eye, jnp.cos(angc) * cscale),
                          jnp.kron(eye, jnp.sin(angc) * cscale)], axis=1)
    return bd.astype(BF16), cs.astype(BF16), ns.astype(BF16)


def _trunk(x, c, p):
    b, s, _ = x.shape
    t = b * s
    mod = _ada(c, p["w_ada"], p["b_ada"]).reshape(b, 6, D_MODEL)
    *qkvs, fz = _inproj(x, mod, p["g1"], p["w_qkvf"], p["gains"], p["rope"], p["gmat"])
    os_, lses = zip(*[_attn(qkv, g) for g, qkv in enumerate(qkvs)])
    fm = _fourier(fz, p["bd"], p["cs"], p["ns"])
    h, u2, ridx, rw, tcnt, cnt = _merge(
        x, mod, os_, lses, fm, p["g1"], p["g2"], p["wg"], p["wap"], p["wfp"], p["wout"], p["wrh"],
        p["wrl"], p["br"], p["expand"], p["ltri"], p["utri"], p["perms"])

    rows = EXPERT_ROWS
    counts = cnt[0, :N_EXPERTS].astype(jnp.int32)
    padded = (counts + rows - 1) // rows * rows
    pend = jnp.cumsum(padded)
    pstart = pend - padded
    seg_row = jnp.concatenate([pstart[None, :] + tcnt[:, 0, :N_EXPERTS], (pstart + counts)[None, :]]).reshape(-1)
    seg_n = jnp.concatenate([tcnt[:, 1, :N_EXPERTS], (padded - counts)[None, :]]).reshape(-1)
    n_blocks = -(-(t * TOP_K + N_EXPERTS * (t // MERGE_ROWS + rows - 1)) // rows)
    block_start = jnp.arange(n_blocks, dtype=jnp.int32) * rows
    block_e = jnp.minimum(jnp.sum((pend[None, :] <= block_start[:, None]).astype(jnp.int32), axis=-1),
                          N_EXPERTS - 1)
    nb_used = (pend[-1:] // rows).astype(jnp.int32)

    ridx = ridx.reshape(t, LANES)
    xs = _dispatch(seg_n, seg_row, ridx, u2.reshape(t, D_MODEL), n_blocks * rows)
    ys = _expert(block_e, nb_used, xs, p["w1"], p["b1"], p["w2"], p["b2"])
    out = _combine(seg_n, seg_row, ridx, rw.reshape(t, LANES), h.reshape(t, D_MODEL), mod, ys, s)
    return out.reshape(b, s, D_MODEL)


def kernel(x_prompt, x_sample, c_prompt, c_sample, w_ada, b_ada, norm1_g, norm2_g, w_in, q_gain, k_gain,
           w_attn_proj, w_fourier_proj, w_out, w_router, b_router, w1, b1, w2, b2):
    seq = x_prompt.shape[1]
    assert w_ada.shape[0] == 1, "single-layer trunk"
    w_in0 = w_in[0]
    gains = jnp.zeros((8, ATTN_WIDTH), F32)
    for g in range(N_GROUPS):
        gains = gains.at[2 * g].set(jnp.tile(q_gain[0, g], HEADS))
        gains = gains.at[2 * g + 1].set(jnp.tile(k_gain[0, g], HEADS))
    heads = jnp.arange(ATTN_WIDTH) // HEAD_DIM
    gmat = ((heads[:, None] == heads[None, :]).astype(F32) / HEAD_DIM).astype(BF16)
    expand = (jnp.arange(LANES)[:, None] == heads[None, :]).astype(BF16)
    ltri = (jnp.arange(MERGE_ROWS)[:, None] > jnp.arange(MERGE_ROWS)[None, :]).astype(BF16)
    utri = (jnp.arange(LANES)[:, None] < jnp.arange(LANES)[None, :]).astype(BF16)
    perms = []
    for d in DILATIONS[1:]:
        n = MERGE_ROWS // d
        src = (jnp.arange(MERGE_ROWS) % d) * n + jnp.arange(MERGE_ROWS) // d
        perms.append((src[:, None] == jnp.arange(MERGE_ROWS)[None, :]).astype(BF16))
    wr = jnp.zeros((D_MODEL, LANES), F32).at[:, :N_EXPERTS].set(w_router[0])
    wrh = wr.astype(BF16)
    wrl = (wr - wrh.astype(F32)).astype(BF16)
    br = jnp.zeros((1, LANES), F32).at[0, :N_EXPERTS].set(b_router[0])
    bd, cs, ns = _dft_tables(seq)
    p = dict(
        w_ada=w_ada[0], b_ada=b_ada[0], g1=norm1_g, g2=norm2_g,
        w_qkvf=w_in0[:, :QKVF_WIDTH].astype(BF16), wg=w_in0[:, QKVF_WIDTH:].astype(BF16),
        gains=gains, rope=_rope_tables(seq, INPROJ_ROWS), gmat=gmat, bd=bd, cs=cs, ns=ns,
        wap=w_attn_proj[0].astype(BF16), wfp=w_fourier_proj[0].astype(BF16), wout=w_out[0].astype(BF16),
        wrh=wrh, wrl=wrl, br=br, expand=expand, ltri=ltri, utri=utri, perms=perms,
        w1=w1[0].astype(BF16), b1=b1[0].reshape(N_EXPERTS, 1, 2 * D_FF),
        w2=w2[0].astype(BF16), b2=b2[0].reshape(N_EXPERTS, 1, D_MODEL),
    )
    return _trunk(x_prompt, c_prompt, p), _trunk(x_sample, c_sample, p)
```

```python
import functools

import jax
import jax.numpy as jnp
from jax import lax
from jax.experimental import pallas as pl
from jax.experimental.pallas import tpu as pltpu

F32 = jnp.float32
BF16 = jnp.bfloat16

D_MODEL = 1024
HEAD_DIM = 64
HEADS = 8
ATTN_WIDTH = HEADS * HEAD_DIM
DILATIONS = (1, 4, 16)
BAND = 64
N_GROUPS = 3
QKVF_WIDTH = 5120
FOURIER_GROUP_DIM = 64
ROPE_DIM = 16
ROPE_THETA = 500000.0
N_EXPERTS = 32
TOP_K = 4
D_FF = 1024
SWIGLU_ALPHA = 1.702
SWIGLU_LIMIT = 7.0
NORM_EPS = 1e-6
NEG_INF = -1e30
LN2 = 0.6931471805599453
LOG2E = 1.4426950408889634

LANES = 128
ROW_WORDS = D_MODEL // LANES // 2
SORTED_ROWS = 1088
EXPERT_ROWS = 512
LONG_PIECE = 64
SHORT_PIECES = (32, 16, 8, 4, 2)
INPROJ_ROWS = 512
MERGE_ROWS = 256
MERGE_TILES_PER_STEP = 2
VMEM_LIMIT = 56 * 1024 * 1024


def _dot(a, b):
    return jnp.dot(a, b, preferred_element_type=F32)


def _split_bf16(v):
    hi = v.astype(BF16)
    lo = (v - hi.astype(F32)).astype(BF16)
    return hi, lo


def _rms_mod(x, gain, scale, shift):
    ms = jnp.mean(x * x, axis=-1, keepdims=True)
    return x * lax.rsqrt(ms + NORM_EPS) * gain * (1.0 + scale) + shift


def _const_spec(shape):
    return pl.BlockSpec(shape, lambda *_: (0,) * len(shape), pipeline_mode=pl.Buffered(1))


def _ada_kernel(c_ref, w_ref, b_ref, o_ref):
    c = c_ref[...]
    a = c * jax.nn.sigmoid(c)
    a_hi, a_lo = _split_bf16(a)
    w_hi, w_lo = _split_bf16(w_ref[...])
    o_ref[...] = _dot(a_hi, w_hi) + _dot(a_lo, w_hi) + _dot(a_hi, w_lo) + b_ref[...]


def _ada(c, w_ada, b_ada):
    b = c.shape[0]
    n = w_ada.shape[1]
    tn = D_MODEL
    return pl.pallas_call(
        _ada_kernel,
        grid=(n // tn,),
        in_specs=[pl.BlockSpec((b, D_MODEL), lambda j: (0, 0)),
                  pl.BlockSpec((D_MODEL, tn), lambda j: (0, j)),
                  pl.BlockSpec((1, tn), lambda j: (0, j))],
        out_specs=pl.BlockSpec((b, tn), lambda j: (0, j)),
        out_shape=jax.ShapeDtypeStruct((b, n), F32),
        compiler_params=pltpu.CompilerParams(vmem_limit_bytes=VMEM_LIMIT),
        name="ada",
    )(c, w_ada, b_ada.reshape(1, n))


def _inproj_kernel(x_ref, mod_ref, g1_ref, w_ref, gain_ref, rope_ref, gmat_ref,
                   o0_ref, o1_ref, o2_ref, of_ref, u_scr):
    x = x_ref[0]
    ts = x.shape[0]
    mod = mod_ref[0]
    u = _rms_mod(x, g1_ref[...], mod[1:2], mod[0:1])
    ub = u.astype(BF16)
    for c in range(D_MODEL // LANES):
        u_scr[c] = u[:, c * LANES:(c + 1) * LANES]
    for g, o_ref in enumerate((o0_ref, o1_ref, o2_ref)):
        d = DILATIONS[g]
        n = ts // d
        if d == 1:
            ug = ub
        else:
            ug = jnp.concatenate(
                [jnp.concatenate([u_scr[c, pl.ds(r, n, stride=d), :] for r in range(d)], axis=0)
                 for c in range(D_MODEL // LANES)], axis=1).astype(BF16)
        cos_t, sin_prev, sin_next = rope_ref[g, 0], rope_ref[g, 1], rope_ref[g, 2]
        for j in range(3):
            c0 = (3 * g + j) * ATTN_WIDTH
            z = _dot(ug, w_ref[:, c0:c0 + ATTN_WIDTH])
            if j < 2:
                ms = _dot((z * z).astype(BF16), gmat_ref[...])
                gain = gain_ref[2 * g + j:2 * g + j + 1, :]
                if j == 0:
                    gain = gain * (HEAD_DIM ** -0.5 * LOG2E)
                z = z * lax.rsqrt(ms + NORM_EPS) * gain
                blocks = []
                for blk in range(ATTN_WIDTH // LANES):
                    zb = z[:, blk * LANES:(blk + 1) * LANES]
                    blocks.append(zb * cos_t
                                  + pltpu.roll(zb, ROPE_DIM // 2, 1) * sin_prev
                                  + pltpu.roll(zb, LANES - ROPE_DIM // 2, 1) * sin_next)
                z = jnp.concatenate(blocks, axis=1)
            zb16 = z.astype(BF16)
            for r in range(d):
                o_ref[0, r, :, j * ATTN_WIDTH:(j + 1) * ATTN_WIDTH] = zb16[r * n:(r + 1) * n, :]
    of_ref[0] = _dot(ub, w_ref[:, 9 * ATTN_WIDTH:]).astype(BF16)


def _inproj(x, mod, g1, w_qkvf, gains, rope, gmat, ts=INPROJ_ROWS):
    b, s, _ = x.shape
    qkv_w = 3 * ATTN_WIDTH
    out_specs = [pl.BlockSpec((1, d, ts // d, qkv_w), lambda i, j: (i, 0, j, 0)) for d in DILATIONS]
    out_shape = [jax.ShapeDtypeStruct((b, d, s // d, qkv_w), BF16) for d in DILATIONS]
    return pl.pallas_call(
        _inproj_kernel,
        grid=(b, s // ts),
        in_specs=[pl.BlockSpec((1, ts, D_MODEL), lambda i, j: (i, j, 0)),
                  pl.BlockSpec((1, 6, D_MODEL), lambda i, j: (i, 0, 0)),
                  _const_spec((1, D_MODEL)),
                  _const_spec((D_MODEL, QKVF_WIDTH)),
                  _const_spec((8, ATTN_WIDTH)),
                  pl.BlockSpec((N_GROUPS, 3, ts, LANES), lambda i, j: (0, 0, j, 0)),
                  _const_spec((ATTN_WIDTH, ATTN_WIDTH))],
        out_specs=out_specs + [pl.BlockSpec((1, ts, ATTN_WIDTH), lambda i, j: (i, j, 0))],
        out_shape=out_shape + [jax.ShapeDtypeStruct((b, s, ATTN_WIDTH), BF16)],
        scratch_shapes=[pltpu.VMEM((D_MODEL // LANES, ts, LANES), F32)],
        compiler_params=pltpu.CompilerParams(
            dimension_semantics=("arbitrary", "arbitrary"), vmem_limit_bytes=VMEM_LIMIT),
        name="inproj",
    )(x, mod, g1, w_qkvf, gains, rope, gmat)


def _attn_kernel(qkv_ref, o_ref, lse_ref, s_scr, p_scr, *, length):
    tq = LANES
    tk = min(2 * LANES, length)
    nq = length // tq
    n_pairs = ATTN_WIDTH // LANES
    lane = lax.broadcasted_iota(jnp.int32, (tq, LANES), 1)
    first_head = lane < HEAD_DIM
    first_head_k = lax.broadcasted_iota(jnp.int32, (tk, LANES), 1) < HEAD_DIM
    rel = (lax.broadcasted_iota(jnp.int32, (tq, tk), 1)
           - lax.broadcasted_iota(jnp.int32, (tq, tk), 0))

    def q_block(it, carry):
        r = it // nq
        i = it % nq
        q0 = pl.multiple_of(i * tq, tq)
        ws = pl.multiple_of(jnp.clip(i * tq - BAND, 0, length - tk), BAND)
        valid = jnp.abs(rel + (ws - i * tq)) <= BAND

        for p in range(n_pairs):
            q2 = qkv_ref[0, r, pl.ds(q0, tq), p * LANES:(p + 1) * LANES]
            k2 = qkv_ref[0, r, pl.ds(ws, tk), ATTN_WIDTH + p * LANES:ATTN_WIDTH + (p + 1) * LANES]
            for hh in range(2):
                sel = first_head if hh == 0 else jnp.logical_not(first_head)
                qa = jnp.where(sel, q2, jnp.zeros_like(q2))
                s = lax.dot_general(qa, k2, (((1,), (1,)), ((), ())), preferred_element_type=F32)
                s_scr[2 * p + hh] = jnp.where(valid, s, NEG_INF)

        m_tile = jnp.zeros((tq, LANES), F32)
        for h in range(HEADS):
            s = s_scr[h]
            m = jnp.max(s, axis=-1, keepdims=True)
            p_scr[h] = jnp.exp2(s - m).astype(BF16)
            m_tile = jnp.where(lane == h, m, m_tile)

        den_tile = jnp.ones((tq, LANES), F32)
        for p in range(n_pairs):
            cs = slice(2 * ATTN_WIDTH + p * LANES, 2 * ATTN_WIDTH + (p + 1) * LANES)
            v2 = qkv_ref[0, r, pl.ds(ws, tk), cs]
            one = jnp.ones_like(v2)
            o_a = _dot(p_scr[2 * p], jnp.where(first_head_k, v2, one))
            o_b = _dot(p_scr[2 * p + 1], jnp.where(first_head_k, one, v2))
            den_ba = jnp.where(first_head, o_b, o_a)
            den_ab = pltpu.roll(den_ba, HEAD_DIM, 1)
            o_ref[0, r, pl.ds(q0, tq), p * LANES:(p + 1) * LANES] = (
                jnp.where(first_head, o_a, o_b) / den_ab).astype(BF16)
            den_tile = jnp.where(lane == 2 * p, den_ab, den_tile)
            den_tile = jnp.where(lane == 2 * p + 1, den_ba, den_tile)
        lse_ref[0, r, pl.ds(q0, tq), :] = (m_tile + jnp.log2(den_tile)) * LN2
        return carry

    lax.fori_loop(0, qkv_ref.shape[1] * nq, q_block, 0)


def _attn(qkv, g):
    b, d, length, width = qkv.shape
    tk = min(2 * LANES, length)
    return pl.pallas_call(
        functools.partial(_attn_kernel, length=length),
        grid=(b,),
        in_specs=[pl.BlockSpec((1, d, length, width), lambda i: (i, 0, 0, 0))],
        out_specs=[pl.BlockSpec((1, d, length, ATTN_WIDTH), lambda i: (i, 0, 0, 0)),
                   pl.BlockSpec((1, d, length, LANES), lambda i: (i, 0, 0, 0))],
        out_shape=[jax.ShapeDtypeStruct((b, d, length, ATTN_WIDTH), BF16),
                   jax.ShapeDtypeStruct((b, d, length, LANES), F32)],
        scratch_shapes=[pltpu.VMEM((HEADS, LANES, tk), F32), pltpu.VMEM((HEADS, LANES, tk), BF16)],
        compiler_params=pltpu.CompilerParams(
            dimension_semantics=("arbitrary",), vmem_limit_bytes=VMEM_LIMIT),
        name=f"attn{g}",
    )(qkv)


def _fourier_kernel(x_ref, bd_ref, cs_ref, ns_ref, o_ref, y_ref, *, seq, rows):
    for r in range(seq // rows):
        rs = slice(r * rows, (r + 1) * rows)
        y_ref[rs, :] = _dot(x_ref[0, rs, :], bd_ref[...]).astype(BF16)
    for r in range(seq // rows):
        rs = slice(r * rows, (r + 1) * rows)
        o = _dot(cs_ref[rs, :], y_ref[:, :ATTN_WIDTH]) + _dot(ns_ref[rs, :], y_ref[:, ATTN_WIDTH:])
        o_ref[0, rs, :] = o.astype(BF16)


def _fourier(qkvf, bd, cs, ns):
    b, s, _ = qkvf.shape
    return pl.pallas_call(
        functools.partial(_fourier_kernel, seq=s, rows=256),
        grid=(b,),
        in_specs=[pl.BlockSpec((1, s, ATTN_WIDTH), lambda i: (i, 0, 0)),
                  _const_spec((ATTN_WIDTH, 2 * ATTN_WIDTH)),
                  _const_spec((s, s)),
                  _const_spec((s, s))],
        out_specs=pl.BlockSpec((1, s, ATTN_WIDTH), lambda i: (i, 0, 0)),
        out_shape=jax.ShapeDtypeStruct((b, s, ATTN_WIDTH), BF16),
        scratch_shapes=[pltpu.VMEM((s, 2 * ATTN_WIDTH), BF16)],
        compiler_params=pltpu.CompilerParams(
            dimension_semantics=("arbitrary",), vmem_limit_bytes=VMEM_LIMIT),
        name="fourier",
    )(qkvf, bd, cs, ns)


def _merge_kernel(x_ref, mod_ref, o0_ref, o1_ref, o2_ref, l0_ref, l1_ref, l2_ref, fm_ref,
                  g1_ref, g2_ref, wg_ref, wap_ref, wfp_ref, wout_ref, wrh_ref, wrl_ref, br_ref,
                  expand_ref, ltri_ref, etri_ref, perm1_ref, perm2_ref,
                  h_ref, u2_ref, route_ref, routet_ref, tcnt_ref, cnt_ref, cnt_scr):
    @pl.when((pl.program_id(0) == 0) & (pl.program_id(1) == 0))
    def _():
        cnt_scr[...] = jnp.zeros_like(cnt_scr)

    refs = (x_ref, mod_ref, o0_ref, o1_ref, o2_ref, l0_ref, l1_ref, l2_ref, fm_ref,
            g1_ref, g2_ref, wg_ref, wap_ref, wfp_ref, wout_ref, wrh_ref, wrl_ref, br_ref,
            expand_ref, ltri_ref, etri_ref, perm1_ref, perm2_ref,
            h_ref, u2_ref, route_ref, routet_ref, tcnt_ref, cnt_ref, cnt_scr)
    for sub in range(x_ref.shape[1] // MERGE_ROWS):
        _merge_tile(sub, *refs)


def _merge_tile(sub, x_ref, mod_ref, o0_ref, o1_ref, o2_ref, l0_ref, l1_ref, l2_ref, fm_ref,
                g1_ref, g2_ref, wg_ref, wap_ref, wfp_ref, wout_ref, wrh_ref, wrl_ref, br_ref,
                expand_ref, ltri_ref, etri_ref, perm1_ref, perm2_ref,
                h_ref, u2_ref, route_ref, routet_ref, tcnt_ref, cnt_ref, cnt_scr):
    ts = MERGE_ROWS
    rs = slice(sub * ts, (sub + 1) * ts)
    x = x_ref[0, rs, :]
    mod = mod_ref[0]
    shift1, scale1, gate1 = mod[0:1], mod[1:2], mod[2:3]
    shift2, scale2 = mod[3:4], mod[4:5]
    u = _rms_mod(x, g1_ref[...], scale1, shift1).astype(BF16)
    gates = _dot(u, wg_ref[...])

    os_ = [o0_ref[0, 0, rs, :].astype(F32)]
    lses = [l0_ref[0, 0, rs, :]]
    for d, o_ref, l_ref, perm_ref in ((DILATIONS[1], o1_ref, l1_ref, perm1_ref),
                                      (DILATIONS[2], o2_ref, l2_ref, perm2_ref)):
        perm = perm_ref[...]
        ms = slice(sub * ts // d, (sub + 1) * ts // d)
        os_.append(_dot(perm, o_ref[0, :, ms, :].reshape(ts, ATTN_WIDTH)))
        l_hi, l_lo = _split_bf16(l_ref[0, :, ms, :].reshape(ts, LANES))
        lses.append(_dot(perm, l_hi) + _dot(perm, l_lo))

    mx = jnp.maximum(jnp.maximum(lses[0], lses[1]), lses[2])
    es = [jnp.exp(l - mx) for l in lses]
    den = es[0] + es[1] + es[2]
    attn = None
    for e, o in zip(es, os_):
        w_hi, w_lo = _split_bf16(e / den)
        wf = _dot(w_hi, expand_ref[...]) + _dot(w_lo, expand_ref[...])
        term = wf * o
        attn = term if attn is None else attn + term

    a = _dot(attn.astype(BF16), wap_ref[...])
    f = _dot(fm_ref[0, rs, :], wfp_ref[...])
    merged = jax.nn.sigmoid(gates[:, :D_MODEL]) * a + jax.nn.sigmoid(gates[:, D_MODEL:]) * f
    h = x + gate1 * _dot(merged.astype(BF16), wout_ref[...])
    h_ref[0, rs, :] = h

    u2 = _rms_mod(h, g2_ref[...], scale2, shift2)
    u_hi, u_lo = _split_bf16(u2)
    u2_ref[0, rs, :] = u_hi
    nt = (((1,), (1,)), ((), ()))
    logits = (lax.dot_general(wrh_ref[...], u_hi, nt, preferred_element_type=F32)
              + lax.dot_general(wrh_ref[...], u_lo, nt, preferred_element_type=F32)
              + lax.dot_general(wrl_ref[...], u_hi, nt, preferred_element_type=F32) + br_ref[...])
    erow = lax.broadcasted_iota(jnp.int32, (LANES, ts), 0)
    lg = jnp.where(erow < N_EXPERTS, logits, -jnp.inf)
    vals, ids = [], []
    onehot = jnp.zeros((LANES, ts), F32)
    for _ in range(TOP_K):
        m = jnp.max(lg, axis=0, keepdims=True)
        idx = jnp.min(jnp.where(lg == m, erow, LANES), axis=0, keepdims=True)
        hit = erow == idx
        vals.append(m)
        ids.append(idx)
        onehot = jnp.where(hit, 1.0, onehot)
        lg = jnp.where(hit, -jnp.inf, lg)
    exps = [jnp.exp(v - vals[0]) for v in vals]
    esum = exps[0] + exps[1] + exps[2] + exps[3]

    n_tile = jnp.sum(onehot, axis=1, keepdims=True)
    n_tile = n_tile + (n_tile - 2.0 * jnp.floor(0.5 * n_tile))
    n_wide = jnp.broadcast_to(n_tile, (LANES, LANES))
    seg_start = _dot(etri_ref[...], n_wide.astype(BF16))[:, 0:1]
    pos_all = lax.dot_general(onehot.astype(BF16), ltri_ref[...], nt, preferred_element_type=F32) + seg_start
    route = jnp.zeros((LANES, ts), F32)
    for k in range(TOP_K):
        pos = jnp.sum(jnp.where(erow == ids[k], pos_all, 0.0), axis=0, keepdims=True)
        route = jnp.where(erow == k, ids[k].astype(F32), route)
        route = jnp.where(erow == TOP_K + k, pos, route)
        route = jnp.where(erow == 2 * TOP_K + k, exps[k] / esum, route)
    route_ref[0, rs, :] = route.T
    routet_ref[sub] = route[0:8, :]
    lane = lax.broadcasted_iota(jnp.int32, (LANES, LANES), 1)
    tcnt = jnp.where(lane == 0, cnt_scr[...], jnp.where(lane == 1, n_wide, 0.0)).T
    tcnt_ref[sub] = tcnt[0:8, :].astype(jnp.int32)
    cnt_scr[...] = cnt_scr[...] + n_tile
    cnt_ref[...] = cnt_scr[...].T[0:8, :]


def _merge(x, mod, os_, lses, fm, g1, g2, wg, wap, wfp, wout, wrh, wrl, br, expand, ltri, etri, perms):
    b, s, _ = x.shape
    ts = MERGE_ROWS * MERGE_TILES_PER_STEP
    tile = lambda w: pl.BlockSpec((1, ts, w), lambda i, j: (i, j, 0))
    strided = lambda w: [pl.BlockSpec((1, d, ts // d, w), lambda i, j: (i, 0, j, 0)) for d in DILATIONS]
    consts = (g1, g2, wg, wap, wfp, wout, wrh, wrl, br, expand, ltri, etri) + tuple(perms)
    per_tile = lambda rows, w: pl.BlockSpec((MERGE_TILES_PER_STEP, rows, w), lambda i, j: (i * per_seq + j, 0, 0))
    n_tiles = b * s // MERGE_ROWS
    in_specs = ([tile(D_MODEL), pl.BlockSpec((1, 6, D_MODEL), lambda i, j: (i, 0, 0))]
                + strided(ATTN_WIDTH) + strided(LANES) + [tile(ATTN_WIDTH)]
                + [_const_spec(a.shape) for a in consts])
    per_seq = s // ts
    return pl.pallas_call(
        _merge_kernel,
        grid=(b, per_seq),
        in_specs=in_specs,
        out_specs=[tile(D_MODEL), tile(D_MODEL), tile(LANES), per_tile(8, MERGE_ROWS), per_tile(8, LANES),
                   pl.BlockSpec((8, LANES), lambda i, j: (0, 0))],
        out_shape=[jax.ShapeDtypeStruct((b, s, D_MODEL), F32),
                   jax.ShapeDtypeStruct((b, s, D_MODEL), BF16),
                   jax.ShapeDtypeStruct((b, s, LANES), F32),
                   jax.ShapeDtypeStruct((n_tiles, 8, MERGE_ROWS), F32),
                   jax.ShapeDtypeStruct((n_tiles, 8, LANES), jnp.int32),
                   jax.ShapeDtypeStruct((8, LANES), F32)],
        scratch_shapes=[pltpu.VMEM((LANES, LANES), F32)],
        compiler_params=pltpu.CompilerParams(
            dimension_semantics=("arbitrary", "arbitrary"), vmem_limit_bytes=VMEM_LIMIT),
        name="merge",
    )(x, mod, *os_, *lses, fm, *consts)


def _to_packed_rows(ref, mat):
    bits = lax.bitcast_convert_type(mat, jnp.uint32)
    for c in range(ROW_WORDS):
        lo = bits[:, c * LANES:(c + 1) * LANES] >> 16
        hi = bits[:, (c + ROW_WORDS) * LANES:(c + ROW_WORDS + 1) * LANES] & jnp.uint32(0xFFFF0000)
        ref[pl.ds(c, mat.shape[0], stride=ROW_WORDS), :] = lo | hi


def _from_packed_rows(ref):
    n = ref.shape[0] // ROW_WORDS
    words = [ref[pl.ds(c, n, stride=ROW_WORDS), :] for c in range(ROW_WORDS)]
    lo = [lax.bitcast_convert_type(w << 16, F32) for w in words]
    hi = [lax.bitcast_convert_type(w & jnp.uint32(0xFFFF0000), F32) for w in words]
    return jnp.concatenate(lo + hi, axis=1)


def _segment_copies(seg_n_ref, seg_row_ref, tile, sorted_ref, hbm_ref, sem, *, to_hbm, start, consecutive=True):
    base = tile * N_EXPERTS

    def copy(src, dst, off, size):
        vm_row = (src + off) if consecutive else 0
        vm = sorted_ref.at[pl.ds(pl.multiple_of(vm_row * ROW_WORDS, 2 * ROW_WORDS), size * ROW_WORDS)]
        hb = hbm_ref.at[pl.ds(pl.multiple_of((dst + off) * ROW_WORDS, 2 * ROW_WORDS), size * ROW_WORDS)]
        cp = pltpu.make_async_copy(vm, hb, sem) if to_hbm else pltpu.make_async_copy(hb, vm, sem)
        if start:
            cp.start()
        else:
            cp.wait()

    def segment(e, src):
        n = seg_n_ref[base + e]
        dst = seg_row_ref[base + e]
        n_long = n >> (LONG_PIECE.bit_length() - 1)

        def long_piece(i, c):
            copy(src, dst, i * LONG_PIECE, LONG_PIECE)
            return c

        lax.fori_loop(0, n_long, long_piece, 0)
        for bit in SHORT_PIECES:
            @pl.when((n & bit) != 0)
            def _():
                copy(src, dst, n & (-2 * bit), bit)
        return src + n

    lax.fori_loop(0, N_EXPERTS, segment, 0)


def _wait_tile_copies(seg_n_ref, tile, sorted_ref, hbm_ref, sem, *, to_hbm):
    base = tile * N_EXPERTS
    total = lax.fori_loop(0, N_EXPERTS, lambda e, acc: acc + seg_n_ref[base + e], 0)

    def wait(size):
        vm = sorted_ref.at[pl.ds(0, size * ROW_WORDS)]
        hb = hbm_ref.at[pl.ds(0, size * ROW_WORDS)]
        (pltpu.make_async_copy(vm, hb, sem) if to_hbm else pltpu.make_async_copy(hb, vm, sem)).wait()

    def long_wait(i, c):
        wait(LONG_PIECE)
        return c

    lax.fori_loop(0, total >> (LONG_PIECE.bit_length() - 1), long_wait, 0)
    for bit in SHORT_PIECES:
        @pl.when((total & bit) != 0)
        def _():
            wait(bit)


def _dispatch_kernel(seg_n_ref, seg_row_ref, routet_ref, u2_ref, xs_ref, sorted_scr, sems):
    tile = pl.program_id(0)
    n_tiles = pl.num_programs(0)
    slot = tile % 2
    tt = u2_ref.shape[0]
    rows = SORTED_ROWS

    def copies(tile_, slot_, start):
        if start:
            _segment_copies(seg_n_ref, seg_row_ref, tile_, sorted_scr.at[slot_], xs_ref, sems.at[slot_],
                            to_hbm=True, start=True)
        else:
            _wait_tile_copies(seg_n_ref, tile_, sorted_scr.at[slot_], xs_ref, sems.at[slot_], to_hbm=True)

    @pl.when(tile == 0)
    def _():
        sorted_scr[1, 0:LONG_PIECE * ROW_WORDS, :] = jnp.zeros((LONG_PIECE * ROW_WORDS, LANES), jnp.uint32)
        for start in (True, False):
            _segment_copies(seg_n_ref, seg_row_ref, n_tiles, sorted_scr.at[1], xs_ref, sems.at[1],
                            to_hbm=True, start=start, consecutive=False)

    pos_t = routet_ref[0]
    j = lax.broadcasted_iota(jnp.int32, (rows, tt), 0).astype(F32)
    onehot = jnp.zeros((rows, tt), F32)
    for k in range(TOP_K):
        onehot = jnp.where(j == pos_t[TOP_K + k:TOP_K + k + 1, :], 1.0, onehot)
    _to_packed_rows(sorted_scr.at[slot], _dot(onehot.astype(BF16), u2_ref[...]))
    copies(tile, slot, True)

    @pl.when(tile > 0)
    def _():
        copies(tile - 1, 1 - slot, False)

    @pl.when(tile == n_tiles - 1)
    def _():
        copies(tile, slot, False)


def _dispatch(seg_n, seg_row, route_t, u2, n_rows, tt=MERGE_ROWS):
    t = u2.shape[0]
    grid_spec = pltpu.PrefetchScalarGridSpec(
        num_scalar_prefetch=2,
        grid=(t // tt,),
        in_specs=[pl.BlockSpec((1, 8, tt), lambda i, *_: (i, 0, 0)),
                  pl.BlockSpec((tt, D_MODEL), lambda i, *_: (i, 0))],
        out_specs=pl.BlockSpec(memory_space=pl.ANY),
        scratch_shapes=[pltpu.VMEM((2, SORTED_ROWS * ROW_WORDS, LANES), jnp.uint32),
                        pltpu.SemaphoreType.DMA((2,))],
    )
    return pl.pallas_call(
        _dispatch_kernel,
        grid_spec=grid_spec,
        out_shape=jax.ShapeDtypeStruct((n_rows * ROW_WORDS, LANES), jnp.uint32),
        compiler_params=pltpu.CompilerParams(
            dimension_semantics=("arbitrary",), has_side_effects=True, vmem_limit_bytes=VMEM_LIMIT),
        name="dispatch",
    )(seg_n, seg_row, route_t, u2)


def _expert_kernel(be_ref, nb_ref, xs_ref, w1_ref, b1_ref, w2_ref, b2_ref, ys_ref):
    del be_ref
    i = pl.program_id(0)

    @pl.when(i < nb_ref[0])
    def _():
        hdn = _dot(_from_packed_rows(xs_ref).astype(BF16), w1_ref[0]) + b1_ref[0]
        glu = jnp.minimum(hdn[:, :D_FF], SWIGLU_LIMIT)
        lin = jnp.clip(hdn[:, D_FF:], -SWIGLU_LIMIT, SWIGLU_LIMIT)
        act = (lin + 1.0) * (glu * jax.nn.sigmoid(SWIGLU_ALPHA * glu))
        y = _dot(act.astype(BF16), w2_ref[0]) + b2_ref[0]
        _to_packed_rows(ys_ref, y.astype(BF16).astype(F32))

    @pl.when(i >= nb_ref[0])
    def _():
        ys_ref[...] = jnp.zeros_like(ys_ref)


def _expert(block_e, nb_used, xs, w1, b1, w2, b2):
    block = (EXPERT_ROWS * ROW_WORDS, LANES)
    n_blocks = xs.shape[0] // block[0]
    grid_spec = pltpu.PrefetchScalarGridSpec(
        num_scalar_prefetch=2,
        grid=(n_blocks,),
        in_specs=[pl.BlockSpec(block, lambda i, be, nb: (jnp.minimum(i, nb[0] - 1), 0)),
                  pl.BlockSpec((1, D_MODEL, 2 * D_FF), lambda i, be, nb: (be[i], 0, 0)),
                  pl.BlockSpec((1, 1, 2 * D_FF), lambda i, be, nb: (be[i], 0, 0)),
                  pl.BlockSpec((1, D_FF, D_MODEL), lambda i, be, nb: (be[i], 0, 0)),
                  pl.BlockSpec((1, 1, D_MODEL), lambda i, be, nb: (be[i], 0, 0))],
        out_specs=pl.BlockSpec(block, lambda i, be, nb: (i, 0)),
    )
    return pl.pallas_call(
        _expert_kernel,
        grid_spec=grid_spec,
        out_shape=jax.ShapeDtypeStruct(xs.shape, xs.dtype),
        compiler_params=pltpu.CompilerParams(
            dimension_semantics=("arbitrary",), vmem_limit_bytes=VMEM_LIMIT),
        name="expert",
    )(block_e, nb_used, xs, w1, b1, w2, b2)


def _combine_kernel(seg_n_ref, seg_row_ref, route_ref, h_ref, mod_ref, ys_ref, o_ref, sorted_scr, sems):
    tile = pl.program_id(0)
    n_tiles = pl.num_programs(0)
    slot = tile % 2
    tt = h_ref.shape[0]
    rows = SORTED_ROWS

    def copies(tile_, slot_, start):
        if start:
            _segment_copies(seg_n_ref, seg_row_ref, tile_, sorted_scr.at[slot_], ys_ref, sems.at[slot_],
                            to_hbm=False, start=True)
        else:
            _wait_tile_copies(seg_n_ref, tile_, sorted_scr.at[slot_], ys_ref, sems.at[slot_], to_hbm=False)

    @pl.when(tile == 0)
    def _():
        sorted_scr[...] = jnp.zeros_like(sorted_scr)
        copies(tile, slot, True)

    @pl.when(tile + 1 < n_tiles)
    def _():
        copies(tile + 1, 1 - slot, True)

    route = route_ref[...]
    j = lax.broadcasted_iota(jnp.int32, (tt, rows), 1).astype(F32)
    wmat = jnp.zeros((tt, rows), F32)
    for k in range(TOP_K):
        wmat = jnp.where(j == route[:, TOP_K + k:TOP_K + k + 1], route[:, 2 * TOP_K + k:2 * TOP_K + k + 1], wmat)
    w_hi, w_lo = _split_bf16(wmat)
    copies(tile, slot, False)
    y = _from_packed_rows(sorted_scr.at[slot]).astype(BF16)
    o_ref[...] = h_ref[...] + mod_ref[0][5:6] * (_dot(w_hi, y) + _dot(w_lo, y))


def _combine(seg_n, seg_row, route, h, mod, ys, seq, tt=MERGE_ROWS):
    t = h.shape[0]
    per_seq = seq // tt
    grid_spec = pltpu.PrefetchScalarGridSpec(
        num_scalar_prefetch=2,
        grid=(t // tt,),
        in_specs=[pl.BlockSpec((tt, LANES), lambda i, *_: (i, 0)),
                  pl.BlockSpec((tt, D_MODEL), lambda i, *_: (i, 0)),
                  pl.BlockSpec((1, 6, D_MODEL), lambda i, *_: (i // per_seq, 0, 0)),
                  pl.BlockSpec(memory_space=pl.ANY)],
        out_specs=pl.BlockSpec((tt, D_MODEL), lambda i, *_: (i, 0)),
        scratch_shapes=[pltpu.VMEM((2, SORTED_ROWS * ROW_WORDS, LANES), jnp.uint32),
                        pltpu.SemaphoreType.DMA((2,))],
    )
    return pl.pallas_call(
        _combine_kernel,
        grid_spec=grid_spec,
        out_shape=jax.ShapeDtypeStruct((t, D_MODEL), F32),
        compiler_params=pltpu.CompilerParams(
            dimension_semantics=("arbitrary",), vmem_limit_bytes=VMEM_LIMIT),
        name="combine",
    )(seg_n, seg_row, route, h, mod, ys)


def _rope_tables(seq, ts):
    half = ROPE_DIM // 2
    inv_freq = jnp.power(jnp.float32(ROPE_THETA), -jnp.arange(half, dtype=F32) * 2.0 / ROPE_DIM)
    ang = jnp.arange(seq, dtype=F32)[:, None] * inv_freq[None, :]
    cos, sin = jnp.cos(ang), jnp.sin(ang)
    ones = jnp.ones((seq, HEAD_DIM - ROPE_DIM), F32)
    zeros = jnp.zeros_like(ones)
    zh = jnp.zeros((seq, half), F32)
    cos_t = jnp.concatenate([cos, cos, ones], axis=1)
    sin_prev = jnp.concatenate([zh, sin, zeros], axis=1)
    sin_next = jnp.concatenate([-sin, zh, zeros], axis=1)
    tables = jnp.stack([jnp.tile(t, (1, LANES // HEAD_DIM)) for t in (cos_t, sin_prev, sin_next)])
    per_group = []
    for d in DILATIONS:
        pos = jnp.arange(seq).reshape(seq // ts, ts // d, d).transpose(0, 2, 1).reshape(seq)
        per_group.append(tables[:, pos, :])
    return jnp.stack(per_group)


def _dft_tables(seq):
    k = jnp.arange(seq, dtype=jnp.int32)
    ang = ((k[:, None] * k[None, :]) % seq).astype(F32) * (2.0 * jnp.pi / seq)
    scale = seq ** -0.5
    cs = jnp.cos(ang) * scale
    ns = -jnp.sin(ang) * scale
    c = jnp.arange(FOURIER_GROUP_DIM, dtype=jnp.int32)
    angc = ((c[:, None] * c[None, :]) % FOURIER_GROUP_DIM).astype(F32) * (2.0 * jnp.pi / FOURIER_GROUP_DIM)
    eye = jnp.eye(ATTN_WIDTH // FOURIER_GROUP_DIM, dtype=F32)
    cscale = FOURIER_GROUP_DIM ** -0.5
    bd = jnp.concatenate([jnp.kron(eye, jnp.cos(angc) * cscale),
                          jnp.kron(eye, jnp.sin(angc) * cscale)], axis=1)
    return bd.astype(BF16), cs.astype(BF16), ns.astype(BF16)


def _trunk(x, c, p):
    b, s, _ = x.shape
    t = b * s
    mod = _ada(c, p["w_ada"], p["b_ada"]).reshape(b, 6, D_MODEL)
    *qkvs, fz = _inproj(x, mod, p["g1"], p["w_qkvf"], p["gains"], p["rope"], p["gmat"])
    os_, lses = zip(*[_attn(qkv, g) for g, qkv in enumerate(qkvs)])
    fm = _fourier(fz, p["bd"], p["cs"], p["ns"])
    h, u2, route, route_t, tcnt, cnt = _merge(
        x, mod, os_, lses, fm, p["g1"], p["g2"], p["wg"], p["wap"], p["wfp"], p["wout"], p["wrh"],
        p["wrl"], p["br"], p["expand"], p["ltri"], p["etri"], p["perms"])

    rows = EXPERT_ROWS
    counts = cnt[0, :N_EXPERTS].astype(jnp.int32)
    padded = (counts + rows - 1) // rows * rows
    pend = jnp.cumsum(padded)
    pstart = pend - padded
    seg_row = jnp.concatenate([pstart[None, :] + tcnt[:, 0, :N_EXPERTS], (pstart + counts)[None, :]]).reshape(-1)
    seg_n = jnp.concatenate([tcnt[:, 1, :N_EXPERTS], (padded - counts)[None, :]]).reshape(-1)
    n_blocks = -(-(t * TOP_K + N_EXPERTS * (t // MERGE_ROWS + rows - 1)) // rows)
    block_start = jnp.arange(n_blocks, dtype=jnp.int32) * rows
    block_e = jnp.minimum(jnp.sum((pend[None, :] <= block_start[:, None]).astype(jnp.int32), axis=-1),
                          N_EXPERTS - 1)
    nb_used = (pend[-1:] // rows).astype(jnp.int32)

    xs = _dispatch(seg_n, seg_row, route_t, u2.reshape(t, D_MODEL), n_blocks * rows)
    ys = _expert(block_e, nb_used, xs, p["w1"], p["b1"], p["w2"], p["b2"])
    out = _combine(seg_n, seg_row, route.reshape(t, LANES), h.reshape(t, D_MODEL), mod, ys, s)
    return out.reshape(b, s, D_MODEL)


def kernel(x_prompt, x_sample, c_prompt, c_sample, w_ada, b_ada, norm1_g, norm2_g, w_in, q_gain, k_gain,
           w_attn_proj, w_fourier_proj, w_out, w_router, b_router, w1, b1, w2, b2):
    seq = x_prompt.shape[1]
    assert w_ada.shape[0] == 1, "single-layer trunk"
    w_in0 = w_in[0]
    gains = jnp.zeros((8, ATTN_WIDTH), F32)
    for g in range(N_GROUPS):
        gains = gains.at[2 * g].set(jnp.tile(q_gain[0, g], HEADS))
        gains = gains.at[2 * g + 1].set(jnp.tile(k_gain[0, g], HEADS))
    heads = jnp.arange(ATTN_WIDTH) // HEAD_DIM
    gmat = ((heads[:, None] == heads[None, :]).astype(F32) / HEAD_DIM).astype(BF16)
    expand = (jnp.arange(LANES)[:, None] == heads[None, :]).astype(BF16)
    ltri = (jnp.arange(MERGE_ROWS)[:, None] > jnp.arange(MERGE_ROWS)[None, :]).astype(BF16)
    etri = (jnp.arange(LANES)[:, None] > jnp.arange(LANES)[None, :]).astype(BF16)
    perms = []
    for d in DILATIONS[1:]:
        n = MERGE_ROWS // d
        src = (jnp.arange(MERGE_ROWS) % d) * n + jnp.arange(MERGE_ROWS) // d
        perms.append((src[:, None] == jnp.arange(MERGE_ROWS)[None, :]).astype(BF16))
    wr = jnp.zeros((LANES, D_MODEL), F32).at[:N_EXPERTS].set(w_router[0].T)
    wrh = wr.astype(BF16)
    wrl = (wr - wrh.astype(F32)).astype(BF16)
    br = jnp.zeros((LANES, MERGE_ROWS), F32).at[:N_EXPERTS].set(b_router[0][:, None])
    bd, cs, ns = _dft_tables(seq)
    p = dict(
        w_ada=w_ada[0], b_ada=b_ada[0], g1=norm1_g, g2=norm2_g,
        w_qkvf=w_in0[:, :QKVF_WIDTH].astype(BF16), wg=w_in0[:, QKVF_WIDTH:].astype(BF16),
        gains=gains, rope=_rope_tables(seq, INPROJ_ROWS), gmat=gmat, bd=bd, cs=cs, ns=ns,
        wap=w_attn_proj[0].astype(BF16), wfp=w_fourier_proj[0].astype(BF16), wout=w_out[0].astype(BF16),
        wrh=wrh, wrl=wrl, br=br, expand=expand, ltri=ltri, etri=etri, perms=perms,
        w1=w1[0].astype(BF16), b1=b1[0].reshape(N_EXPERTS, 1, 2 * D_FF),
        w2=w2[0].astype(BF16), b2=b2[0].reshape(N_EXPERTS, 1, D_MODEL),
    )
    return _trunk(x_prompt, c_prompt, p), _trunk(x_sample, c_sample, p)
```

```python
import functools

import jax
import jax.numpy as jnp
from jax import lax
from jax.experimental import pallas as pl
from jax.experimental.pallas import tpu as pltpu

F32 = jnp.float32
BF16 = jnp.bfloat16

D_MODEL = 1024
HEAD_DIM = 64
HEADS = 8
ATTN_WIDTH = HEADS * HEAD_DIM
DILATIONS = (1, 4, 16)
BAND = 64
N_GROUPS = 3
QKVF_WIDTH = 5120
FOURIER_GROUP_DIM = 64
ROPE_DIM = 16
ROPE_THETA = 500000.0
N_EXPERTS = 32
TOP_K = 4
D_FF = 1024
SWIGLU_ALPHA = 1.702
SWIGLU_LIMIT = 7.0
NORM_EPS = 1e-6
NEG_INF = -1e30
LN2 = 0.6931471805599453
LOG2E = 1.4426950408889634

LANES = 128
ROW_WORDS = D_MODEL // LANES // 2
SORTED_ROWS = 1088
EXPERT_ROWS = 512
LONG_PIECE = 64
SHORT_PIECES = (32, 16, 8, 4, 2)
INPROJ_ROWS = 512
MERGE_ROWS = 256
MERGE_TILES_PER_STEP = 2
VMEM_LIMIT = 56 * 1024 * 1024


def _dot(a, b):
    return jnp.dot(a, b, preferred_element_type=F32)


def _split_bf16(v):
    hi = v.astype(BF16)
    lo = (v - hi.astype(F32)).astype(BF16)
    return hi, lo


def _rms_mod(x, gain, scale, shift):
    ms = jnp.mean(x * x, axis=-1, keepdims=True)
    return x * lax.rsqrt(ms + NORM_EPS) * gain * (1.0 + scale) + shift


def _const_spec(shape):
    return pl.BlockSpec(shape, lambda *_: (0,) * len(shape), pipeline_mode=pl.Buffered(1))


def _ada_kernel(c_ref, w_ref, b_ref, o_ref):
    c = c_ref[...]
    a = c * jax.nn.sigmoid(c)
    a_hi, a_lo = _split_bf16(a)
    w_hi, w_lo = _split_bf16(w_ref[...])
    o_ref[...] = _dot(a_hi, w_hi) + _dot(a_lo, w_hi) + _dot(a_hi, w_lo) + b_ref[...]


def _ada(c, w_ada, b_ada):
    b = c.shape[0]
    n = w_ada.shape[1]
    tn = D_MODEL
    return pl.pallas_call(
        _ada_kernel,
        grid=(n // tn,),
        in_specs=[pl.BlockSpec((b, D_MODEL), lambda j: (0, 0)),
                  pl.BlockSpec((D_MODEL, tn), lambda j: (0, j)),
                  pl.BlockSpec((1, tn), lambda j: (0, j))],
        out_specs=pl.BlockSpec((b, tn), lambda j: (0, j)),
        out_shape=jax.ShapeDtypeStruct((b, n), F32),
        compiler_params=pltpu.CompilerParams(vmem_limit_bytes=VMEM_LIMIT),
        name="ada",
    )(c, w_ada, b_ada.reshape(1, n))


def _inproj_kernel(x_ref, mod_ref, g1_ref, w_ref, gain_ref, rope_ref, gmat_ref,
                   o0_ref, o1_ref, o2_ref, of_ref, u_scr):
    x = x_ref[0]
    ts = x.shape[0]
    mod = mod_ref[0]
    u = _rms_mod(x, g1_ref[...], mod[1:2], mod[0:1])
    ub = u.astype(BF16)
    for c in range(D_MODEL // LANES):
        u_scr[c] = u[:, c * LANES:(c + 1) * LANES]
    for g, o_ref in enumerate((o0_ref, o1_ref, o2_ref)):
        d = DILATIONS[g]
        n = ts // d
        if d == 1:
            ug = ub
        else:
            ug = jnp.concatenate(
                [jnp.concatenate([u_scr[c, pl.ds(r, n, stride=d), :] for r in range(d)], axis=0)
                 for c in range(D_MODEL // LANES)], axis=1).astype(BF16)
        cos_t, sin_prev, sin_next = rope_ref[g, 0], rope_ref[g, 1], rope_ref[g, 2]
        for j in range(3):
            c0 = (3 * g + j) * ATTN_WIDTH
            z = _dot(ug, w_ref[:, c0:c0 + ATTN_WIDTH])
            if j < 2:
                ms = _dot((z * z).astype(BF16), gmat_ref[...])
                gain = gain_ref[2 * g + j:2 * g + j + 1, :]
                if j == 0:
                    gain = gain * (HEAD_DIM ** -0.5 * LOG2E)
                z = z * lax.rsqrt(ms + NORM_EPS) * gain
                blocks = []
                for blk in range(ATTN_WIDTH // LANES):
                    zb = z[:, blk * LANES:(blk + 1) * LANES]
                    blocks.append(zb * cos_t
                                  + pltpu.roll(zb, ROPE_DIM // 2, 1) * sin_prev
                                  + pltpu.roll(zb, LANES - ROPE_DIM // 2, 1) * sin_next)
                z = jnp.concatenate(blocks, axis=1)
            zb16 = z.astype(BF16)
            for r in range(d):
                o_ref[0, r, :, j * ATTN_WIDTH:(j + 1) * ATTN_WIDTH] = zb16[r * n:(r + 1) * n, :]
    of_ref[0] = _dot(ub, w_ref[:, 9 * ATTN_WIDTH:]).astype(BF16)


def _inproj(x, mod, g1, w_qkvf, gains, rope, gmat, ts=INPROJ_ROWS):
    b, s, _ = x.shape
    qkv_w = 3 * ATTN_WIDTH
    out_specs = [pl.BlockSpec((1, d, ts // d, qkv_w), lambda i, j: (i, 0, j, 0)) for d in DILATIONS]
    out_shape = [jax.ShapeDtypeStruct((b, d, s // d, qkv_w), BF16) for d in DILATIONS]
    return pl.pallas_call(
        _inproj_kernel,
        grid=(b, s // ts),
        in_specs=[pl.BlockSpec((1, ts, D_MODEL), lambda i, j: (i, j, 0)),
                  pl.BlockSpec((1, 6, D_MODEL), lambda i, j: (i, 0, 0)),
                  _const_spec((1, D_MODEL)),
                  _const_spec((D_MODEL, QKVF_WIDTH)),
                  _const_spec((8, ATTN_WIDTH)),
                  pl.BlockSpec((N_GROUPS, 3, ts, LANES), lambda i, j: (0, 0, j, 0)),
                  _const_spec((ATTN_WIDTH, ATTN_WIDTH))],
        out_specs=out_specs + [pl.BlockSpec((1, ts, ATTN_WIDTH), lambda i, j: (i, j, 0))],
        out_shape=out_shape + [jax.ShapeDtypeStruct((b, s, ATTN_WIDTH), BF16)],
        scratch_shapes=[pltpu.VMEM((D_MODEL // LANES, ts, LANES), F32)],
        compiler_params=pltpu.CompilerParams(
            dimension_semantics=("arbitrary", "arbitrary"), vmem_limit_bytes=VMEM_LIMIT),
        name="inproj",
    )(x, mod, g1, w_qkvf, gains, rope, gmat)


def _attn_kernel(qkv_ref, o_ref, lse_ref, s_scr, p_scr, *, length):
    tq = LANES
    tk = min(2 * LANES, length)
    nq = length // tq
    n_pairs = ATTN_WIDTH // LANES
    lane = lax.broadcasted_iota(jnp.int32, (tq, LANES), 1)
    first_head = lane < HEAD_DIM
    first_head_k = lax.broadcasted_iota(jnp.int32, (tk, LANES), 1) < HEAD_DIM
    rel = (lax.broadcasted_iota(jnp.int32, (tq, tk), 1)
           - lax.broadcasted_iota(jnp.int32, (tq, tk), 0))

    def q_block(it, carry):
        r = it // nq
        i = it % nq
        q0 = pl.multiple_of(i * tq, tq)
        ws = pl.multiple_of(jnp.clip(i * tq - BAND, 0, length - tk), BAND)
        valid = jnp.abs(rel + (ws - i * tq)) <= BAND

        for p in range(n_pairs):
            q2 = qkv_ref[0, r, pl.ds(q0, tq), p * LANES:(p + 1) * LANES]
            k2 = qkv_ref[0, r, pl.ds(ws, tk), ATTN_WIDTH + p * LANES:ATTN_WIDTH + (p + 1) * LANES]
            for hh in range(2):
                sel = first_head if hh == 0 else jnp.logical_not(first_head)
                qa = jnp.where(sel, q2, jnp.zeros_like(q2))
                s = lax.dot_general(qa, k2, (((1,), (1,)), ((), ())), preferred_element_type=F32)
                s_scr[2 * p + hh] = jnp.where(valid, s, NEG_INF)

        m_tile = jnp.zeros((tq, LANES), F32)
        for h in range(HEADS):
            s = s_scr[h]
            m = jnp.max(s, axis=-1, keepdims=True)
            p_scr[h] = jnp.exp2(s - m).astype(BF16)
            m_tile = jnp.where(lane == h, m, m_tile)

        den_tile = jnp.ones((tq, LANES), F32)
        for p in range(n_pairs):
            cs = slice(2 * ATTN_WIDTH + p * LANES, 2 * ATTN_WIDTH + (p + 1) * LANES)
            v2 = qkv_ref[0, r, pl.ds(ws, tk), cs]
            one = jnp.ones_like(v2)
            o_a = _dot(p_scr[2 * p], jnp.where(first_head_k, v2, one))
            o_b = _dot(p_scr[2 * p + 1], jnp.where(first_head_k, one, v2))
            den_ba = jnp.where(first_head, o_b, o_a)
            den_ab = pltpu.roll(den_ba, HEAD_DIM, 1)
            o_ref[0, r, pl.ds(q0, tq), p * LANES:(p + 1) * LANES] = (
                jnp.where(first_head, o_a, o_b) / den_ab).astype(BF16)
            den_tile = jnp.where(lane == 2 * p, den_ab, den_tile)
            den_tile = jnp.where(lane == 2 * p + 1, den_ba, den_tile)
        lse_ref[0, r, pl.ds(q0, tq), :] = (m_tile + jnp.log2(den_tile)) * LN2
        return carry

    lax.fori_loop(0, qkv_ref.shape[1] * nq, q_block, 0)


def _attn(qkv, g):
    b, d, length, width = qkv.shape
    tk = min(2 * LANES, length)
    return pl.pallas_call(
        functools.partial(_attn_kernel, length=length),
        grid=(b,),
        in_specs=[pl.BlockSpec((1, d, length, width), lambda i: (i, 0, 0, 0))],
        out_specs=[pl.BlockSpec((1, d, length, ATTN_WIDTH), lambda i: (i, 0, 0, 0)),
                   pl.BlockSpec((1, d, length, LANES), lambda i: (i, 0, 0, 0))],
        out_shape=[jax.ShapeDtypeStruct((b, d, length, ATTN_WIDTH), BF16),
                   jax.ShapeDtypeStruct((b, d, length, LANES), F32)],
        scratch_shapes=[pltpu.VMEM((HEADS, LANES, tk), F32), pltpu.VMEM((HEADS, LANES, tk), BF16)],
        compiler_params=pltpu.CompilerParams(
            dimension_semantics=("arbitrary",), vmem_limit_bytes=VMEM_LIMIT),
        name=f"attn{g}",
    )(qkv)


def _fourier_kernel(x_ref, bd_ref, cs_ref, ns_ref, o_ref, y_ref, *, seq, rows):
    for r in range(seq // rows):
        rs = slice(r * rows, (r + 1) * rows)
        y_ref[rs, :] = _dot(x_ref[0, rs, :], bd_ref[...]).astype(BF16)
    for r in range(seq // rows):
        rs = slice(r * rows, (r + 1) * rows)
        o = _dot(cs_ref[rs, :], y_ref[:, :ATTN_WIDTH]) + _dot(ns_ref[rs, :], y_ref[:, ATTN_WIDTH:])
        o_ref[0, rs, :] = o.astype(BF16)


def _fourier(qkvf, bd, cs, ns):
    b, s, _ = qkvf.shape
    return pl.pallas_call(
        functools.partial(_fourier_kernel, seq=s, rows=256),
        grid=(b,),
        in_specs=[pl.BlockSpec((1, s, ATTN_WIDTH), lambda i: (i, 0, 0)),
                  _const_spec((ATTN_WIDTH, 2 * ATTN_WIDTH)),
                  _const_spec((s, s)),
                  _const_spec((s, s))],
        out_specs=pl.BlockSpec((1, s, ATTN_WIDTH), lambda i: (i, 0, 0)),
        out_shape=jax.ShapeDtypeStruct((b, s, ATTN_WIDTH), BF16),
        scratch_shapes=[pltpu.VMEM((s, 2 * ATTN_WIDTH), BF16)],
        compiler_params=pltpu.CompilerParams(
            dimension_semantics=("arbitrary",), vmem_limit_bytes=VMEM_LIMIT),
        name="fourier",
    )(qkvf, bd, cs, ns)


def _merge_kernel(x_ref, mod_ref, o0_ref, o1_ref, o2_ref, l0_ref, l1_ref, l2_ref, fm_ref,
                  g1_ref, g2_ref, wg_ref, wap_ref, wfp_ref, wout_ref, wrh_ref, wrl_ref, br_ref,
                  expand_ref, ltri_ref, etri_ref, perm1_ref, perm2_ref,
                  h_ref, u2_ref, route_ref, routet_ref, tcnt_ref, cnt_ref, cnt_scr):
    @pl.when((pl.program_id(0) == 0) & (pl.program_id(1) == 0))
    def _():
        cnt_scr[...] = jnp.zeros_like(cnt_scr)

    refs = (x_ref, mod_ref, o0_ref, o1_ref, o2_ref, l0_ref, l1_ref, l2_ref, fm_ref,
            g1_ref, g2_ref, wg_ref, wap_ref, wfp_ref, wout_ref, wrh_ref, wrl_ref, br_ref,
            expand_ref, ltri_ref, etri_ref, perm1_ref, perm2_ref,
            h_ref, u2_ref, route_ref, routet_ref, tcnt_ref, cnt_ref, cnt_scr)
    for sub in range(x_ref.shape[1] // MERGE_ROWS):
        _merge_tile(sub, *refs)


def _merge_tile(sub, x_ref, mod_ref, o0_ref, o1_ref, o2_ref, l0_ref, l1_ref, l2_ref, fm_ref,
                g1_ref, g2_ref, wg_ref, wap_ref, wfp_ref, wout_ref, wrh_ref, wrl_ref, br_ref,
                expand_ref, ltri_ref, etri_ref, perm1_ref, perm2_ref,
                h_ref, u2_ref, route_ref, routet_ref, tcnt_ref, cnt_ref, cnt_scr):
    ts = MERGE_ROWS
    rs = slice(sub * ts, (sub + 1) * ts)
    x = x_ref[0, rs, :]
    mod = mod_ref[0]
    shift1, scale1, gate1 = mod[0:1], mod[1:2], mod[2:3]
    shift2, scale2 = mod[3:4], mod[4:5]
    u = _rms_mod(x, g1_ref[...], scale1, shift1).astype(BF16)
    gates = _dot(u, wg_ref[...])

    os_ = [o0_ref[0, 0, rs, :].astype(F32)]
    lses = [l0_ref[0, 0, rs, :]]
    for d, o_ref, l_ref, perm_ref in ((DILATIONS[1], o1_ref, l1_ref, perm1_ref),
                                      (DILATIONS[2], o2_ref, l2_ref, perm2_ref)):
        perm = perm_ref[...]
        ms = slice(sub * ts // d, (sub + 1) * ts // d)
        os_.append(_dot(perm, o_ref[0, :, ms, :].reshape(ts, ATTN_WIDTH)))
        l_hi, l_lo = _split_bf16(l_ref[0, :, ms, :].reshape(ts, LANES))
        lses.append(_dot(perm, l_hi) + _dot(perm, l_lo))

    mx = jnp.maximum(jnp.maximum(lses[0], lses[1]), lses[2])
    es = [jnp.exp(l - mx) for l in lses]
    den = es[0] + es[1] + es[2]
    attn = None
    for e, o in zip(es, os_):
        w_hi, w_lo = _split_bf16(e / den)
        wf = _dot(w_hi, expand_ref[...]) + _dot(w_lo, expand_ref[...])
        term = wf * o
        attn = term if attn is None else attn + term

    a = _dot(attn.astype(BF16), wap_ref[...])
    f = _dot(fm_ref[0, rs, :], wfp_ref[...])
    merged = jax.nn.sigmoid(gates[:, :D_MODEL]) * a + jax.nn.sigmoid(gates[:, D_MODEL:]) * f
    h = x + gate1 * _dot(merged.astype(BF16), wout_ref[...])
    h_ref[0, rs, :] = h

    u2 = _rms_mod(h, g2_ref[...], scale2, shift2)
    u_hi, u_lo = _split_bf16(u2)
    u2_ref[0, rs, :] = u_hi
    nt = (((1,), (1,)), ((), ()))
    logits = (lax.dot_general(wrh_ref[...], u_hi, nt, preferred_element_type=F32)
              + lax.dot_general(wrh_ref[...], u_lo, nt, preferred_element_type=F32)
              + lax.dot_general(wrl_ref[...], u_hi, nt, preferred_element_type=F32) + br_ref[...])
    erow = lax.broadcasted_iota(jnp.int32, (LANES, ts), 0)
    lg = jnp.where(erow < N_EXPERTS, logits, -jnp.inf)
    vals, ids = [], []
    onehot = jnp.zeros((LANES, ts), F32)
    for _ in range(TOP_K):
        m = jnp.max(lg, axis=0, keepdims=True)
        idx = jnp.min(jnp.where(lg == m, erow, LANES), axis=0, keepdims=True)
        hit = erow == idx
        vals.append(m)
        ids.append(idx)
        onehot = jnp.where(hit, 1.0, onehot)
        lg = jnp.where(hit, -jnp.inf, lg)
    exps = [jnp.exp(v - vals[0]) for v in vals]
    esum = exps[0] + exps[1] + exps[2] + exps[3]

    n_tile = jnp.sum(onehot, axis=1, keepdims=True)
    n_tile = n_tile + (n_tile - 2.0 * jnp.floor(0.5 * n_tile))
    n_wide = jnp.broadcast_to(n_tile, (LANES, LANES))
    seg_start = _dot(etri_ref[...], n_wide.astype(BF16))[:, 0:1]
    pos_all = lax.dot_general(onehot.astype(BF16), ltri_ref[...], nt, preferred_element_type=F32) + seg_start
    route = jnp.zeros((LANES, ts), F32)
    for k in range(TOP_K):
        pos = jnp.sum(jnp.where(erow == ids[k], pos_all, 0.0), axis=0, keepdims=True)
        route = jnp.where(erow == k, ids[k].astype(F32), route)
        route = jnp.where(erow == TOP_K + k, pos, route)
        route = jnp.where(erow == 2 * TOP_K + k, exps[k] / esum, route)
    route_ref[0, rs, :] = route.T
    routet_ref[sub] = route[0:8, :]
    lane = lax.broadcasted_iota(jnp.int32, (LANES, LANES), 1)
    tcnt = jnp.where(lane == 0, cnt_scr[...], jnp.where(lane == 1, n_wide, 0.0)).T
    tcnt_ref[sub] = tcnt[0:8, :].astype(jnp.int32)
    cnt_scr[...] = cnt_scr[...] + n_tile
    cnt_ref[...] = cnt_scr[...].T[0:8, :]


def _merge(x, mod, os_, lses, fm, g1, g2, wg, wap, wfp, wout, wrh, wrl, br, expand, ltri, etri, perms):
    b, s, _ = x.shape
    ts = MERGE_ROWS * MERGE_TILES_PER_STEP
    tile = lambda w: pl.BlockSpec((1, ts, w), lambda i, j: (i, j, 0))
    strided = lambda w: [pl.BlockSpec((1, d, ts // d, w), lambda i, j: (i, 0, j, 0)) for d in DILATIONS]
    consts = (g1, g2, wg, wap, wfp, wout, wrh, wrl, br, expand, ltri, etri) + tuple(perms)
    per_tile = lambda rows, w: pl.BlockSpec((MERGE_TILES_PER_STEP, rows, w), lambda i, j: (i * per_seq + j, 0, 0))
    n_tiles = b * s // MERGE_ROWS
    in_specs = ([tile(D_MODEL), pl.BlockSpec((1, 6, D_MODEL), lambda i, j: (i, 0, 0))]
                + strided(ATTN_WIDTH) + strided(LANES) + [tile(ATTN_WIDTH)]
                + [_const_spec(a.shape) for a in consts])
    per_seq = s // ts
    return pl.pallas_call(
        _merge_kernel,
        grid=(b, per_seq),
        in_specs=in_specs,
        out_specs=[tile(D_MODEL), tile(D_MODEL), tile(LANES), per_tile(8, MERGE_ROWS), per_tile(8, LANES),
                   pl.BlockSpec((8, LANES), lambda i, j: (0, 0))],
        out_shape=[jax.ShapeDtypeStruct((b, s, D_MODEL), F32),
                   jax.ShapeDtypeStruct((b, s, D_MODEL), BF16),
                   jax.ShapeDtypeStruct((b, s, LANES), F32),
                   jax.ShapeDtypeStruct((n_tiles, 8, MERGE_ROWS), F32),
                   jax.ShapeDtypeStruct((n_tiles, 8, LANES), jnp.int32),
                   jax.ShapeDtypeStruct((8, LANES), F32)],
        scratch_shapes=[pltpu.VMEM((LANES, LANES), F32)],
        compiler_params=pltpu.CompilerParams(
            dimension_semantics=("arbitrary", "arbitrary"), vmem_limit_bytes=VMEM_LIMIT),
        name="merge",
    )(x, mod, *os_, *lses, fm, *consts)


def _to_packed_rows(ref, mat):
    bits = lax.bitcast_convert_type(mat, jnp.uint32)
    for c in range(ROW_WORDS):
        lo = bits[:, c * LANES:(c + 1) * LANES] >> 16
        hi = bits[:, (c + ROW_WORDS) * LANES:(c + ROW_WORDS + 1) * LANES] & jnp.uint32(0xFFFF0000)
        ref[pl.ds(c, mat.shape[0], stride=ROW_WORDS), :] = lo | hi


def _from_packed_rows(ref):
    n = ref.shape[0] // ROW_WORDS
    words = [ref[pl.ds(c, n, stride=ROW_WORDS), :] for c in range(ROW_WORDS)]
    lo = [lax.bitcast_convert_type(w << 16, F32) for w in words]
    hi = [lax.bitcast_convert_type(w & jnp.uint32(0xFFFF0000), F32) for w in words]
    return jnp.concatenate(lo + hi, axis=1)


def _segment_copies(seg_n_ref, seg_row_ref, tile, sorted_ref, hbm_ref, sem, *, to_hbm, start, consecutive=True):
    base = tile * N_EXPERTS

    def copy(src, dst, off, size):
        vm_row = (src + off) if consecutive else 0
        vm = sorted_ref.at[pl.ds(pl.multiple_of(vm_row * ROW_WORDS, 2 * ROW_WORDS), size * ROW_WORDS)]
        hb = hbm_ref.at[pl.ds(pl.multiple_of((dst + off) * ROW_WORDS, 2 * ROW_WORDS), size * ROW_WORDS)]
        cp = pltpu.make_async_copy(vm, hb, sem) if to_hbm else pltpu.make_async_copy(hb, vm, sem)
        if start:
            cp.start()
        else:
            cp.wait()

    def segment(e, src):
        n = seg_n_ref[base + e]
        dst = seg_row_ref[base + e]
        n_long = n >> (LONG_PIECE.bit_length() - 1)

        def long_piece(i, c):
            copy(src, dst, i * LONG_PIECE, LONG_PIECE)
            return c

        lax.fori_loop(0, n_long, long_piece, 0)
        for bit in SHORT_PIECES:
            @pl.when((n & bit) != 0)
            def _():
                copy(src, dst, n & (-2 * bit), bit)
        return src + n

    lax.fori_loop(0, N_EXPERTS, segment, 0)


PIECE_SIZES = (LONG_PIECE,) + SHORT_PIECES
PIECE_SLOTS = 32
PIECE_TABLE = 512


def _piece_tables(n, src0, dst0):
    slots = jnp.arange(PIECE_SLOTS, dtype=jnp.int32)

    def compact(mask, src, dst):
        rank = jnp.cumsum(mask.astype(jnp.int32), axis=1) - 1
        sel = mask[:, :, None] & (rank[:, :, None] == slots)
        pick = lambda v: jnp.sum(jnp.where(sel, v[:, :, None], 0), axis=1)
        return jnp.concatenate([pick(src), pick(dst)], axis=1), jnp.sum(mask.astype(jnp.int32), axis=1)

    n_long = n >> (LONG_PIECE.bit_length() - 1)
    j = jnp.arange(MERGE_ROWS // LONG_PIECE, dtype=jnp.int32)
    offs = (j * LONG_PIECE)[None, :, None]
    flat = lambda v: v.reshape(v.shape[0], -1)
    lists, counts = [], []
    lst, cnt = compact(flat(n_long[:, None, :] > j[None, :, None]), flat(src0[:, None, :] + offs),
                       flat(dst0[:, None, :] + offs))
    lists.append(lst)
    counts.append(cnt)
    for bit in SHORT_PIECES:
        off = n & (-2 * bit)
        lst, cnt = compact((n & bit) != 0, src0 + off, dst0 + off)
        lists.append(lst)
        counts.append(cnt)
    head = jnp.stack(counts, axis=1)
    head = jnp.pad(head, ((0, 0), (0, 2 * PIECE_SLOTS - len(PIECE_SIZES))))
    table = jnp.concatenate([head] + lists, axis=1)
    table = jnp.pad(table, ((0, 0), (0, PIECE_TABLE - table.shape[1])))
    return table.reshape(-1, 1, PIECE_TABLE)


def _start_listed_copies(table_ref, sorted_ref, hbm_ref, sem, *, to_hbm):
    for c, size in enumerate(PIECE_SIZES):
        base = 2 * PIECE_SLOTS * (c + 1)

        def piece(i, carry):
            src = table_ref[0, 0, base + i]
            dst = table_ref[0, 0, base + PIECE_SLOTS + i]
            vm = sorted_ref.at[pl.ds(pl.multiple_of(src * ROW_WORDS, 2 * ROW_WORDS), size * ROW_WORDS)]
            hb = hbm_ref.at[pl.ds(pl.multiple_of(dst * ROW_WORDS, 2 * ROW_WORDS), size * ROW_WORDS)]
            (pltpu.make_async_copy(vm, hb, sem) if to_hbm else pltpu.make_async_copy(hb, vm, sem)).start()
            return carry

        lax.fori_loop(0, table_ref[0, 0, c], piece, 0)


def _wait_tile_copies(seg_n_ref, tile, sorted_ref, hbm_ref, sem, *, to_hbm):
    base = tile * N_EXPERTS
    total = lax.fori_loop(0, N_EXPERTS, lambda e, acc: acc + seg_n_ref[base + e], 0)

    def wait(size):
        vm = sorted_ref.at[pl.ds(0, size * ROW_WORDS)]
        hb = hbm_ref.at[pl.ds(0, size * ROW_WORDS)]
        (pltpu.make_async_copy(vm, hb, sem) if to_hbm else pltpu.make_async_copy(hb, vm, sem)).wait()

    def long_wait(i, c):
        wait(LONG_PIECE)
        return c

    lax.fori_loop(0, total >> (LONG_PIECE.bit_length() - 1), long_wait, 0)
    for bit in SHORT_PIECES:
        @pl.when((total & bit) != 0)
        def _():
            wait(bit)


def _dispatch_kernel(seg_n_ref, seg_row_ref, table_ref, routet_ref, u2_ref, xs_ref, sorted_scr, sems):
    tile = pl.program_id(0)
    n_tiles = pl.num_programs(0)
    slot = tile % 2
    tt = u2_ref.shape[0]
    rows = SORTED_ROWS

    def wait_copies(tile_, slot_):
        _wait_tile_copies(seg_n_ref, tile_, sorted_scr.at[slot_], xs_ref, sems.at[slot_], to_hbm=True)

    @pl.when(tile == 0)
    def _():
        sorted_scr[1, 0:LONG_PIECE * ROW_WORDS, :] = jnp.zeros((LONG_PIECE * ROW_WORDS, LANES), jnp.uint32)
        for start in (True, False):
            _segment_copies(seg_n_ref, seg_row_ref, n_tiles, sorted_scr.at[1], xs_ref, sems.at[1],
                            to_hbm=True, start=start, consecutive=False)

    pos_t = routet_ref[0]
    j = lax.broadcasted_iota(jnp.int32, (rows, tt), 0).astype(F32)
    onehot = jnp.zeros((rows, tt), F32)
    for k in range(TOP_K):
        onehot = jnp.where(j == pos_t[TOP_K + k:TOP_K + k + 1, :], 1.0, onehot)
    _to_packed_rows(sorted_scr.at[slot], _dot(onehot.astype(BF16), u2_ref[...]))
    _start_listed_copies(table_ref, sorted_scr.at[slot], xs_ref, sems.at[slot], to_hbm=True)

    @pl.when(tile > 0)
    def _():
        wait_copies(tile - 1, 1 - slot)

    @pl.when(tile == n_tiles - 1)
    def _():
        wait_copies(tile, slot)


def _dispatch(seg_n, seg_row, table, route_t, u2, n_rows, tt=MERGE_ROWS):
    t = u2.shape[0]
    grid_spec = pltpu.PrefetchScalarGridSpec(
        num_scalar_prefetch=2,
        grid=(t // tt,),
        in_specs=[pl.BlockSpec((1, 1, PIECE_TABLE), lambda i, *_: (i, 0, 0), memory_space=pltpu.SMEM),
                  pl.BlockSpec((1, 8, tt), lambda i, *_: (i, 0, 0)),
                  pl.BlockSpec((tt, D_MODEL), lambda i, *_: (i, 0))],
        out_specs=pl.BlockSpec(memory_space=pl.ANY),
        scratch_shapes=[pltpu.VMEM((2, SORTED_ROWS * ROW_WORDS, LANES), jnp.uint32),
                        pltpu.SemaphoreType.DMA((2,))],
    )
    return pl.pallas_call(
        _dispatch_kernel,
        grid_spec=grid_spec,
        out_shape=jax.ShapeDtypeStruct((n_rows * ROW_WORDS, LANES), jnp.uint32),
        compiler_params=pltpu.CompilerParams(
            dimension_semantics=("arbitrary",), has_side_effects=True, vmem_limit_bytes=VMEM_LIMIT),
        name="dispatch",
    )(seg_n, seg_row, table, route_t, u2)


def _expert_kernel(be_ref, nb_ref, xs_ref, w1_ref, b1_ref, w2_ref, b2_ref, ys_ref):
    del be_ref
    i = pl.program_id(0)

    @pl.when(i < nb_ref[0])
    def _():
        hdn = _dot(_from_packed_rows(xs_ref).astype(BF16), w1_ref[0]) + b1_ref[0]
        glu = jnp.minimum(hdn[:, :D_FF], SWIGLU_LIMIT)
        lin = jnp.clip(hdn[:, D_FF:], -SWIGLU_LIMIT, SWIGLU_LIMIT)
        act = (lin + 1.0) * (glu * jax.nn.sigmoid(SWIGLU_ALPHA * glu))
        y = _dot(act.astype(BF16), w2_ref[0]) + b2_ref[0]
        _to_packed_rows(ys_ref, y.astype(BF16).astype(F32))

    @pl.when(i >= nb_ref[0])
    def _():
        ys_ref[...] = jnp.zeros_like(ys_ref)


def _expert(block_e, nb_used, xs, w1, b1, w2, b2):
    block = (EXPERT_ROWS * ROW_WORDS, LANES)
    n_blocks = xs.shape[0] // block[0]
    grid_spec = pltpu.PrefetchScalarGridSpec(
        num_scalar_prefetch=2,
        grid=(n_blocks,),
        in_specs=[pl.BlockSpec(block, lambda i, be, nb: (jnp.minimum(i, nb[0] - 1), 0)),
                  pl.BlockSpec((1, D_MODEL, 2 * D_FF), lambda i, be, nb: (be[i], 0, 0)),
                  pl.BlockSpec((1, 1, 2 * D_FF), lambda i, be, nb: (be[i], 0, 0)),
                  pl.BlockSpec((1, D_FF, D_MODEL), lambda i, be, nb: (be[i], 0, 0)),
                  pl.BlockSpec((1, 1, D_MODEL), lambda i, be, nb: (be[i], 0, 0))],
        out_specs=pl.BlockSpec(block, lambda i, be, nb: (i, 0)),
    )
    return pl.pallas_call(
        _expert_kernel,
        grid_spec=grid_spec,
        out_shape=jax.ShapeDtypeStruct(xs.shape, xs.dtype),
        compiler_params=pltpu.CompilerParams(
            dimension_semantics=("arbitrary",), vmem_limit_bytes=VMEM_LIMIT),
        name="expert",
    )(block_e, nb_used, xs, w1, b1, w2, b2)


def _combine_kernel(seg_n_ref, table_ref, next_table_ref, route_ref, h_ref, mod_ref, ys_ref, o_ref,
                    sorted_scr, sems):
    tile = pl.program_id(0)
    n_tiles = pl.num_programs(0)
    slot = tile % 2
    tt = h_ref.shape[0]
    rows = SORTED_ROWS

    @pl.when(tile == 0)
    def _():
        sorted_scr[...] = jnp.zeros_like(sorted_scr)
        _start_listed_copies(table_ref, sorted_scr.at[slot], ys_ref, sems.at[slot], to_hbm=False)

    @pl.when(tile + 1 < n_tiles)
    def _():
        _start_listed_copies(next_table_ref, sorted_scr.at[1 - slot], ys_ref, sems.at[1 - slot], to_hbm=False)

    route = route_ref[...]
    j = lax.broadcasted_iota(jnp.int32, (tt, rows), 1).astype(F32)
    wmat = jnp.zeros((tt, rows), F32)
    for k in range(TOP_K):
        wmat = jnp.where(j == route[:, TOP_K + k:TOP_K + k + 1], route[:, 2 * TOP_K + k:2 * TOP_K + k + 1], wmat)
    _wait_tile_copies(seg_n_ref, tile, sorted_scr.at[slot], ys_ref, sems.at[slot], to_hbm=False)
    y = _from_packed_rows(sorted_scr.at[slot]).astype(BF16)
    o_ref[...] = h_ref[...] + mod_ref[0][5:6] * _dot(wmat.astype(BF16), y)


def _combine(seg_n, table, route, h, mod, ys, seq, tt=MERGE_ROWS):
    t = h.shape[0]
    n_tiles = t // tt
    per_seq = seq // tt
    table_spec = lambda ahead: pl.BlockSpec(
        (1, 1, PIECE_TABLE), lambda i, *_: (jnp.minimum(i + ahead, n_tiles - 1), 0, 0), memory_space=pltpu.SMEM)
    grid_spec = pltpu.PrefetchScalarGridSpec(
        num_scalar_prefetch=1,
        grid=(n_tiles,),
        in_specs=[table_spec(0), table_spec(1),
                  pl.BlockSpec((tt, LANES), lambda i, *_: (i, 0)),
                  pl.BlockSpec((tt, D_MODEL), lambda i, *_: (i, 0)),
                  pl.BlockSpec((1, 6, D_MODEL), lambda i, *_: (i // per_seq, 0, 0)),
                  pl.BlockSpec(memory_space=pl.ANY)],
        out_specs=pl.BlockSpec((tt, D_MODEL), lambda i, *_: (i, 0)),
        scratch_shapes=[pltpu.VMEM((2, SORTED_ROWS * ROW_WORDS, LANES), jnp.uint32),
                        pltpu.SemaphoreType.DMA((2,))],
    )
    return pl.pallas_call(
        _combine_kernel,
        grid_spec=grid_spec,
        out_shape=jax.ShapeDtypeStruct((t, D_MODEL), F32),
        compiler_params=pltpu.CompilerParams(
            dimension_semantics=("arbitrary",), vmem_limit_bytes=VMEM_LIMIT),
        name="combine",
    )(seg_n, table, table, route, h, mod, ys)


def _rope_tables(seq, ts):
    half = ROPE_DIM // 2
    inv_freq = jnp.power(jnp.float32(ROPE_THETA), -jnp.arange(half, dtype=F32) * 2.0 / ROPE_DIM)
    ang = jnp.arange(seq, dtype=F32)[:, None] * inv_freq[None, :]
    cos, sin = jnp.cos(ang), jnp.sin(ang)
    ones = jnp.ones((seq, HEAD_DIM - ROPE_DIM), F32)
    zeros = jnp.zeros_like(ones)
    zh = jnp.zeros((seq, half), F32)
    cos_t = jnp.concatenate([cos, cos, ones], axis=1)
    sin_prev = jnp.concatenate([zh, sin, zeros], axis=1)
    sin_next = jnp.concatenate([-sin, zh, zeros], axis=1)
    tables = jnp.stack([jnp.tile(t, (1, LANES // HEAD_DIM)) for t in (cos_t, sin_prev, sin_next)])
    per_group = []
    for d in DILATIONS:
        pos = jnp.arange(seq).reshape(seq // ts, ts // d, d).transpose(0, 2, 1).reshape(seq)
        per_group.append(tables[:, pos, :])
    return jnp.stack(per_group)


def _dft_tables(seq):
    k = jnp.arange(seq, dtype=jnp.int32)
    ang = ((k[:, None] * k[None, :]) % seq).astype(F32) * (2.0 * jnp.pi / seq)
    scale = seq ** -0.5
    cs = jnp.cos(ang) * scale
    ns = -jnp.sin(ang) * scale
    c = jnp.arange(FOURIER_GROUP_DIM, dtype=jnp.int32)
    angc = ((c[:, None] * c[None, :]) % FOURIER_GROUP_DIM).astype(F32) * (2.0 * jnp.pi / FOURIER_GROUP_DIM)
    eye = jnp.eye(ATTN_WIDTH // FOURIER_GROUP_DIM, dtype=F32)
    cscale = FOURIER_GROUP_DIM ** -0.5
    bd = jnp.concatenate([jnp.kron(eye, jnp.cos(angc) * cscale),
                          jnp.kron(eye, jnp.sin(angc) * cscale)], axis=1)
    return bd.astype(BF16), cs.astype(BF16), ns.astype(BF16)


def _trunk(x, c, p):
    b, s, _ = x.shape
    t = b * s
    mod = _ada(c, p["w_ada"], p["b_ada"]).reshape(b, 6, D_MODEL)
    *qkvs, fz = _inproj(x, mod, p["g1"], p["w_qkvf"], p["gains"], p["rope"], p["gmat"])
    os_, lses = zip(*[_attn(qkv, g) for g, qkv in enumerate(qkvs)])
    fm = _fourier(fz, p["bd"], p["cs"], p["ns"])
    h, u2, route, route_t, tcnt, cnt = _merge(
        x, mod, os_, lses, fm, p["g1"], p["g2"], p["wg"], p["wap"], p["wfp"], p["wout"], p["wrh"],
        p["wrl"], p["br"], p["expand"], p["ltri"], p["etri"], p["perms"])

    rows = EXPERT_ROWS
    counts = cnt[0, :N_EXPERTS].astype(jnp.int32)
    padded = (counts + rows - 1) // rows * rows
    pend = jnp.cumsum(padded)
    pstart = pend - padded
    seg_row = jnp.concatenate([pstart[None, :] + tcnt[:, 0, :N_EXPERTS], (pstart + counts)[None, :]]).reshape(-1)
    seg_n = jnp.concatenate([tcnt[:, 1, :N_EXPERTS], (padded - counts)[None, :]]).reshape(-1)
    n_blocks = -(-(t * TOP_K + N_EXPERTS * (t // MERGE_ROWS + rows - 1)) // rows)
    block_start = jnp.arange(n_blocks, dtype=jnp.int32) * rows
    block_e = jnp.minimum(jnp.sum((pend[None, :] <= block_start[:, None]).astype(jnp.int32), axis=-1),
                          N_EXPERTS - 1)
    nb_used = (pend[-1:] // rows).astype(jnp.int32)

    n_tile = tcnt[:, 1, :N_EXPERTS]
    table = _piece_tables(n_tile, jnp.cumsum(n_tile, axis=1) - n_tile, pstart[None, :] + tcnt[:, 0, :N_EXPERTS])
    xs = _dispatch(seg_n, seg_row, table, route_t, u2.reshape(t, D_MODEL), n_blocks * rows)
    ys = _expert(block_e, nb_used, xs, p["w1"], p["b1"], p["w2"], p["b2"])
    out = _combine(seg_n, table, route.reshape(t, LANES), h.reshape(t, D_MODEL), mod, ys, s)
    return out.reshape(b, s, D_MODEL)


def kernel(x_prompt, x_sample, c_prompt, c_sample, w_ada, b_ada, norm1_g, norm2_g, w_in, q_gain, k_gain,
           w_attn_proj, w_fourier_proj, w_out, w_router, b_router, w1, b1, w2, b2):
    seq = x_prompt.shape[1]
    assert w_ada.shape[0] == 1, "single-layer trunk"
    w_in0 = w_in[0]
    gains = jnp.zeros((8, ATTN_WIDTH), F32)
    for g in range(N_GROUPS):
        gains = gains.at[2 * g].set(jnp.tile(q_gain[0, g], HEADS))
        gains = gains.at[2 * g + 1].set(jnp.tile(k_gain[0, g], HEADS))
    heads = jnp.arange(ATTN_WIDTH) // HEAD_DIM
    gmat = ((heads[:, None] == heads[None, :]).astype(F32) / HEAD_DIM).astype(BF16)
    expand = (jnp.arange(LANES)[:, None] == heads[None, :]).astype(BF16)
    ltri = (jnp.arange(MERGE_ROWS)[:, None] > jnp.arange(MERGE_ROWS)[None, :]).astype(BF16)
    etri = (jnp.arange(LANES)[:, None] > jnp.arange(LANES)[None, :]).astype(BF16)
    perms = []
    for d in DILATIONS[1:]:
        n = MERGE_ROWS // d
        src = (jnp.arange(MERGE_ROWS) % d) * n + jnp.arange(MERGE_ROWS) // d
        perms.append((src[:, None] == jnp.arange(MERGE_ROWS)[None, :]).astype(BF16))
    wr = jnp.zeros((LANES, D_MODEL), F32).at[:N_EXPERTS].set(w_router[0].T)
    wrh = wr.astype(BF16)
    wrl = (wr - wrh.astype(F32)).astype(BF16)
    br = jnp.zeros((LANES, MERGE_ROWS), F32).at[:N_EXPERTS].set(b_router[0][:, None])
    bd, cs, ns = _dft_tables(seq)
    p = dict(
        w_ada=w_ada[0], b_ada=b_ada[0], g1=norm1_g, g2=norm2_g,
        w_qkvf=w_in0[:, :QKVF_WIDTH].astype(BF16), wg=w_in0[:, QKVF_WIDTH:].astype(BF16),
        gains=gains, rope=_rope_tables(seq, INPROJ_ROWS), gmat=gmat, bd=bd, cs=cs, ns=ns,
        wap=w_attn_proj[0].astype(BF16), wfp=w_fourier_proj[0].astype(BF16), wout=w_out[0].astype(BF16),
        wrh=wrh, wrl=wrl, br=br, expand=expand, ltri=ltri, etri=etri, perms=perms,
        w1=w1[0].astype(BF16), b1=b1[0].reshape(N_EXPERTS, 1, 2 * D_FF),
        w2=w2[0].astype(BF16), b2=b2[0].reshape(N_EXPERTS, 1, D_MODEL),
    )
    return _trunk(x_prompt, c_prompt, p), _trunk(x_sample, c_sample, p)
```

```python
import functools

import jax
import jax.numpy as jnp
from jax import lax
from jax.experimental import pallas as pl
from jax.experimental.pallas import tpu as pltpu

F32 = jnp.float32
BF16 = jnp.bfloat16

D_MODEL = 1024
HEAD_DIM = 64
HEADS = 8
ATTN_WIDTH = HEADS * HEAD_DIM
DILATIONS = (1, 4, 16)
BAND = 64
N_GROUPS = 3
QKVF_WIDTH = 5120
FOURIER_GROUP_DIM = 64
ROPE_DIM = 16
ROPE_THETA = 500000.0
N_EXPERTS = 32
TOP_K = 4
D_FF = 1024
SWIGLU_ALPHA = 1.702
SWIGLU_LIMIT = 7.0
NORM_EPS = 1e-6
NEG_INF = -1e30
LN2 = 0.6931471805599453
LOG2E = 1.4426950408889634

LANES = 128
ROW_WORDS = D_MODEL // LANES // 2
SORTED_ROWS = 1088
EXPERT_ROWS = 512
LONG_PIECE = 64
SHORT_PIECES = (32, 16, 8, 4, 2)
INPROJ_ROWS = 512
MERGE_ROWS = 256
MERGE_TILES_PER_STEP = 2
VMEM_LIMIT = 56 * 1024 * 1024


def _dot(a, b):
    return jnp.dot(a, b, preferred_element_type=F32)


def _split_bf16(v):
    hi = v.astype(BF16)
    lo = (v - hi.astype(F32)).astype(BF16)
    return hi, lo


def _rms_mod(x, gain, scale, shift):
    ms = jnp.mean(x * x, axis=-1, keepdims=True)
    return x * lax.rsqrt(ms + NORM_EPS) * gain * (1.0 + scale) + shift


def _const_spec(shape):
    return pl.BlockSpec(shape, lambda *_: (0,) * len(shape), pipeline_mode=pl.Buffered(1))


def _ada_kernel(c_ref, w_ref, b_ref, o_ref):
    c = c_ref[...]
    a = c * jax.nn.sigmoid(c)
    a_hi, a_lo = _split_bf16(a)
    w_hi, w_lo = _split_bf16(w_ref[...])
    o_ref[...] = _dot(a_hi, w_hi) + _dot(a_lo, w_hi) + _dot(a_hi, w_lo) + b_ref[...]


def _ada(c, w_ada, b_ada):
    b = c.shape[0]
    n = w_ada.shape[1]
    tn = D_MODEL
    return pl.pallas_call(
        _ada_kernel,
        grid=(n // tn,),
        in_specs=[pl.BlockSpec((b, D_MODEL), lambda j: (0, 0)),
                  pl.BlockSpec((D_MODEL, tn), lambda j: (0, j)),
                  pl.BlockSpec((1, tn), lambda j: (0, j))],
        out_specs=pl.BlockSpec((b, tn), lambda j: (0, j)),
        out_shape=jax.ShapeDtypeStruct((b, n), F32),
        compiler_params=pltpu.CompilerParams(vmem_limit_bytes=VMEM_LIMIT),
        name="ada",
    )(c, w_ada, b_ada.reshape(1, n))


def _inproj_kernel(x_ref, mod_ref, g1_ref, w_ref, gain_ref, rope_ref, gmat_ref,
                   o0_ref, o1_ref, o2_ref, of_ref, u_scr):
    x = x_ref[0]
    ts = x.shape[0]
    mod = mod_ref[0]
    u = _rms_mod(x, g1_ref[...], mod[1:2], mod[0:1])
    ub = u.astype(BF16)
    for c in range(D_MODEL // LANES):
        u_scr[c] = u[:, c * LANES:(c + 1) * LANES]
    for g, o_ref in enumerate((o0_ref, o1_ref, o2_ref)):
        d = DILATIONS[g]
        n = ts // d
        if d == 1:
            ug = ub
        else:
            ug = jnp.concatenate(
                [jnp.concatenate([u_scr[c, pl.ds(r, n, stride=d), :] for r in range(d)], axis=0)
                 for c in range(D_MODEL // LANES)], axis=1).astype(BF16)
        cos_t, sin_prev, sin_next = rope_ref[g, 0], rope_ref[g, 1], rope_ref[g, 2]
        for j in range(3):
            c0 = (3 * g + j) * ATTN_WIDTH
            z = _dot(ug, w_ref[:, c0:c0 + ATTN_WIDTH])
            if j < 2:
                ms = _dot((z * z).astype(BF16), gmat_ref[...])
                gain = gain_ref[2 * g + j:2 * g + j + 1, :]
                if j == 0:
                    gain = gain * (HEAD_DIM ** -0.5 * LOG2E)
                z = z * lax.rsqrt(ms + NORM_EPS) * gain
                blocks = []
                for blk in range(ATTN_WIDTH // LANES):
                    zb = z[:, blk * LANES:(blk + 1) * LANES]
                    blocks.append(zb * cos_t
                                  + pltpu.roll(zb, ROPE_DIM // 2, 1) * sin_prev
                                  + pltpu.roll(zb, LANES - ROPE_DIM // 2, 1) * sin_next)
                z = jnp.concatenate(blocks, axis=1)
            zb16 = z.astype(BF16)
            for r in range(d):
                o_ref[0, r, :, j * ATTN_WIDTH:(j + 1) * ATTN_WIDTH] = zb16[r * n:(r + 1) * n, :]
    of_ref[0] = _dot(ub, w_ref[:, 9 * ATTN_WIDTH:]).astype(BF16)


def _inproj(x, mod, g1, w_qkvf, gains, rope, gmat, ts=INPROJ_ROWS):
    b, s, _ = x.shape
    qkv_w = 3 * ATTN_WIDTH
    out_specs = [pl.BlockSpec((1, d, ts // d, qkv_w), lambda i, j: (i, 0, j, 0)) for d in DILATIONS]
    out_shape = [jax.ShapeDtypeStruct((b, d, s // d, qkv_w), BF16) for d in DILATIONS]
    return pl.pallas_call(
        _inproj_kernel,
        grid=(b, s // ts),
        in_specs=[pl.BlockSpec((1, ts, D_MODEL), lambda i, j: (i, j, 0)),
                  pl.BlockSpec((1, 6, D_MODEL), lambda i, j: (i, 0, 0)),
                  _const_spec((1, D_MODEL)),
                  _const_spec((D_MODEL, QKVF_WIDTH)),
                  _const_spec((8, ATTN_WIDTH)),
                  pl.BlockSpec((N_GROUPS, 3, ts, LANES), lambda i, j: (0, 0, j, 0)),
                  _const_spec((ATTN_WIDTH, ATTN_WIDTH))],
        out_specs=out_specs + [pl.BlockSpec((1, ts, ATTN_WIDTH), lambda i, j: (i, j, 0))],
        out_shape=out_shape + [jax.ShapeDtypeStruct((b, s, ATTN_WIDTH), BF16)],
        scratch_shapes=[pltpu.VMEM((D_MODEL // LANES, ts, LANES), F32)],
        compiler_params=pltpu.CompilerParams(
            dimension_semantics=("arbitrary", "arbitrary"), vmem_limit_bytes=VMEM_LIMIT),
        name="inproj",
    )(x, mod, g1, w_qkvf, gains, rope, gmat)


def _attn_kernel(qkv_ref, o_ref, lse_ref, s_scr, p_scr, *, length):
    tq = LANES
    tk = min(2 * LANES, length)
    nq = length // tq
    n_pairs = ATTN_WIDTH // LANES
    lane = lax.broadcasted_iota(jnp.int32, (tq, LANES), 1)
    first_head = lane < HEAD_DIM
    first_head_k = lax.broadcasted_iota(jnp.int32, (tk, LANES), 1) < HEAD_DIM
    rel = (lax.broadcasted_iota(jnp.int32, (tq, tk), 1)
           - lax.broadcasted_iota(jnp.int32, (tq, tk), 0))

    def q_block(it, carry):
        r = it // nq
        i = it % nq
        q0 = pl.multiple_of(i * tq, tq)
        ws = pl.multiple_of(jnp.clip(i * tq - BAND, 0, length - tk), BAND)
        valid = jnp.abs(rel + (ws - i * tq)) <= BAND

        for p in range(n_pairs):
            q2 = qkv_ref[0, r, pl.ds(q0, tq), p * LANES:(p + 1) * LANES]
            k2 = qkv_ref[0, r, pl.ds(ws, tk), ATTN_WIDTH + p * LANES:ATTN_WIDTH + (p + 1) * LANES]
            for hh in range(2):
                sel = first_head if hh == 0 else jnp.logical_not(first_head)
                qa = jnp.where(sel, q2, jnp.zeros_like(q2))
                s = lax.dot_general(qa, k2, (((1,), (1,)), ((), ())), preferred_element_type=F32)
                s_scr[2 * p + hh] = jnp.where(valid, s, NEG_INF)

        m_tile = jnp.zeros((tq, LANES), F32)
        for h in range(HEADS):
            s = s_scr[h]
            m = jnp.max(s, axis=-1, keepdims=True)
            p_scr[h] = jnp.exp2(s - m).astype(BF16)
            m_tile = jnp.where(lane == h, m, m_tile)

        den_tile = jnp.ones((tq, LANES), F32)
        for p in range(n_pairs):
            cs = slice(2 * ATTN_WIDTH + p * LANES, 2 * ATTN_WIDTH + (p + 1) * LANES)
            v2 = qkv_ref[0, r, pl.ds(ws, tk), cs]
            one = jnp.ones_like(v2)
            o_a = _dot(p_scr[2 * p], jnp.where(first_head_k, v2, one))
            o_b = _dot(p_scr[2 * p + 1], jnp.where(first_head_k, one, v2))
            den_ba = jnp.where(first_head, o_b, o_a)
            den_ab = pltpu.roll(den_ba, HEAD_DIM, 1)
            o_ref[0, r, pl.ds(q0, tq), p * LANES:(p + 1) * LANES] = (
                jnp.where(first_head, o_a, o_b) / den_ab).astype(BF16)
            den_tile = jnp.where(lane == 2 * p, den_ab, den_tile)
            den_tile = jnp.where(lane == 2 * p + 1, den_ba, den_tile)
        lse_ref[0, r, pl.ds(q0, tq), :] = (m_tile + jnp.log2(den_tile)) * LN2
        return carry

    lax.fori_loop(0, qkv_ref.shape[1] * nq, q_block, 0)


def _attn(qkv, g):
    b, d, length, width = qkv.shape
    tk = min(2 * LANES, length)
    return pl.pallas_call(
        functools.partial(_attn_kernel, length=length),
        grid=(b,),
        in_specs=[pl.BlockSpec((1, d, length, width), lambda i: (i, 0, 0, 0))],
        out_specs=[pl.BlockSpec((1, d, length, ATTN_WIDTH), lambda i: (i, 0, 0, 0)),
                   pl.BlockSpec((1, d, length, LANES), lambda i: (i, 0, 0, 0))],
        out_shape=[jax.ShapeDtypeStruct((b, d, length, ATTN_WIDTH), BF16),
                   jax.ShapeDtypeStruct((b, d, length, LANES), F32)],
        scratch_shapes=[pltpu.VMEM((HEADS, LANES, tk), F32), pltpu.VMEM((HEADS, LANES, tk), BF16)],
        compiler_params=pltpu.CompilerParams(
            dimension_semantics=("arbitrary",), vmem_limit_bytes=VMEM_LIMIT),
        name=f"attn{g}",
    )(qkv)


def _fourier_kernel(x_ref, bd_ref, cs_ref, ns_ref, o_ref, y_ref, *, seq, rows):
    for r in range(seq // rows):
        rs = slice(r * rows, (r + 1) * rows)
        y_ref[rs, :] = _dot(x_ref[0, rs, :], bd_ref[...]).astype(BF16)
    for r in range(seq // rows):
        rs = slice(r * rows, (r + 1) * rows)
        o = _dot(cs_ref[rs, :], y_ref[:, :ATTN_WIDTH]) + _dot(ns_ref[rs, :], y_ref[:, ATTN_WIDTH:])
        o_ref[0, rs, :] = o.astype(BF16)


def _fourier(qkvf, bd, cs, ns):
    b, s, _ = qkvf.shape
    return pl.pallas_call(
        functools.partial(_fourier_kernel, seq=s, rows=256),
        grid=(b,),
        in_specs=[pl.BlockSpec((1, s, ATTN_WIDTH), lambda i: (i, 0, 0)),
                  _const_spec((ATTN_WIDTH, 2 * ATTN_WIDTH)),
                  _const_spec((s, s)),
                  _const_spec((s, s))],
        out_specs=pl.BlockSpec((1, s, ATTN_WIDTH), lambda i: (i, 0, 0)),
        out_shape=jax.ShapeDtypeStruct((b, s, ATTN_WIDTH), BF16),
        scratch_shapes=[pltpu.VMEM((s, 2 * ATTN_WIDTH), BF16)],
        compiler_params=pltpu.CompilerParams(
            dimension_semantics=("arbitrary",), vmem_limit_bytes=VMEM_LIMIT),
        name="fourier",
    )(qkvf, bd, cs, ns)


def _merge_kernel(x_ref, mod_ref, o0_ref, o1_ref, o2_ref, l0_ref, l1_ref, l2_ref, fm_ref,
                  g1_ref, g2_ref, wg_ref, wap_ref, wfp_ref, wout_ref, wrh_ref, wrl_ref, br_ref,
                  expand_ref, ltri_ref, etri_ref, perm1_ref, perm2_ref,
                  h_ref, u2_ref, route_ref, routet_ref, tcnt_ref, cnt_ref, cnt_scr):
    @pl.when((pl.program_id(0) == 0) & (pl.program_id(1) == 0))
    def _():
        cnt_scr[...] = jnp.zeros_like(cnt_scr)

    refs = (x_ref, mod_ref, o0_ref, o1_ref, o2_ref, l0_ref, l1_ref, l2_ref, fm_ref,
            g1_ref, g2_ref, wg_ref, wap_ref, wfp_ref, wout_ref, wrh_ref, wrl_ref, br_ref,
            expand_ref, ltri_ref, etri_ref, perm1_ref, perm2_ref,
            h_ref, u2_ref, route_ref, routet_ref, tcnt_ref, cnt_ref, cnt_scr)
    for sub in range(x_ref.shape[1] // MERGE_ROWS):
        _merge_tile(sub, *refs)


def _merge_tile(sub, x_ref, mod_ref, o0_ref, o1_ref, o2_ref, l0_ref, l1_ref, l2_ref, fm_ref,
                g1_ref, g2_ref, wg_ref, wap_ref, wfp_ref, wout_ref, wrh_ref, wrl_ref, br_ref,
                expand_ref, ltri_ref, etri_ref, perm1_ref, perm2_ref,
                h_ref, u2_ref, route_ref, routet_ref, tcnt_ref, cnt_ref, cnt_scr):
    ts = MERGE_ROWS
    rs = slice(sub * ts, (sub + 1) * ts)
    x = x_ref[0, rs, :]
    mod = mod_ref[0]
    shift1, scale1, gate1 = mod[0:1], mod[1:2], mod[2:3]
    shift2, scale2 = mod[3:4], mod[4:5]
    u = _rms_mod(x, g1_ref[...], scale1, shift1).astype(BF16)
    gates = _dot(u, wg_ref[...])

    os_ = [o0_ref[0, 0, rs, :].astype(F32)]
    lses = [l0_ref[0, 0, rs, :]]
    for d, o_ref, l_ref, perm_ref in ((DILATIONS[1], o1_ref, l1_ref, perm1_ref),
                                      (DILATIONS[2], o2_ref, l2_ref, perm2_ref)):
        perm = perm_ref[...]
        ms = slice(sub * ts // d, (sub + 1) * ts // d)
        os_.append(_dot(perm, o_ref[0, :, ms, :].reshape(ts, ATTN_WIDTH)))
        l_hi, l_lo = _split_bf16(l_ref[0, :, ms, :].reshape(ts, LANES))
        lses.append(_dot(perm, l_hi) + _dot(perm, l_lo))

    mx = jnp.maximum(jnp.maximum(lses[0], lses[1]), lses[2])
    es = [jnp.exp(l - mx) for l in lses]
    den = es[0] + es[1] + es[2]
    attn = None
    for e, o in zip(es, os_):
        w_hi, w_lo = _split_bf16(e / den)
        wf = _dot(w_hi, expand_ref[...]) + _dot(w_lo, expand_ref[...])
        term = wf * o
        attn = term if attn is None else attn + term

    a = _dot(attn.astype(BF16), wap_ref[...])
    f = _dot(fm_ref[0, rs, :], wfp_ref[...])
    merged = jax.nn.sigmoid(gates[:, :D_MODEL]) * a + jax.nn.sigmoid(gates[:, D_MODEL:]) * f
    h = x + gate1 * _dot(merged.astype(BF16), wout_ref[...])
    h_ref[0, rs, :] = h

    u2 = _rms_mod(h, g2_ref[...], scale2, shift2)
    u_hi, u_lo = _split_bf16(u2)
    u2_ref[0, rs, :] = u_hi
    nt = (((1,), (1,)), ((), ()))
    logits = (lax.dot_general(wrh_ref[...], u_hi, nt, preferred_element_type=F32)
              + lax.dot_general(wrh_ref[...], u_lo, nt, preferred_element_type=F32)
              + lax.dot_general(wrl_ref[...], u_hi, nt, preferred_element_type=F32) + br_ref[...])
    erow = lax.broadcasted_iota(jnp.int32, (LANES, ts), 0)
    lg = jnp.where(erow < N_EXPERTS, logits, -jnp.inf)
    vals, ids = [], []
    onehot = jnp.zeros((LANES, ts), F32)
    for _ in range(TOP_K):
        m = jnp.max(lg, axis=0, keepdims=True)
        idx = jnp.min(jnp.where(lg == m, erow, LANES), axis=0, keepdims=True)
        hit = erow == idx
        vals.append(m)
        ids.append(idx)
        onehot = jnp.where(hit, 1.0, onehot)
        lg = jnp.where(hit, -jnp.inf, lg)
    exps = [jnp.exp(v - vals[0]) for v in vals]
    esum = exps[0] + exps[1] + exps[2] + exps[3]

    n_tile = jnp.sum(onehot, axis=1, keepdims=True)
    n_tile = n_tile + (n_tile - 2.0 * jnp.floor(0.5 * n_tile))
    n_wide = jnp.broadcast_to(n_tile, (LANES, LANES))
    seg_start = _dot(etri_ref[...], n_wide.astype(BF16))[:, 0:1]
    pos_all = lax.dot_general(onehot.astype(BF16), ltri_ref[...], nt, preferred_element_type=F32) + seg_start
    route = jnp.zeros((LANES, ts), F32)
    for k in range(TOP_K):
        pos = jnp.sum(jnp.where(erow == ids[k], pos_all, 0.0), axis=0, keepdims=True)
        route = jnp.where(erow == k, ids[k].astype(F32), route)
        route = jnp.where(erow == TOP_K + k, pos, route)
        route = jnp.where(erow == 2 * TOP_K + k, exps[k] / esum, route)
    route_ref[0, rs, :] = route.T
    routet_ref[sub] = route[0:8, :]
    lane = lax.broadcasted_iota(jnp.int32, (LANES, LANES), 1)
    tcnt = jnp.where(lane == 0, cnt_scr[...], jnp.where(lane == 1, n_wide, 0.0)).T
    tcnt_ref[sub] = tcnt[0:8, :].astype(jnp.int32)
    cnt_scr[...] = cnt_scr[...] + n_tile
    cnt_ref[...] = cnt_scr[...].T[0:8, :]


def _merge(x, mod, os_, lses, fm, g1, g2, wg, wap, wfp, wout, wrh, wrl, br, expand, ltri, etri, perms):
    b, s, _ = x.shape
    ts = MERGE_ROWS * MERGE_TILES_PER_STEP
    tile = lambda w: pl.BlockSpec((1, ts, w), lambda i, j: (i, j, 0))
    strided = lambda w: [pl.BlockSpec((1, d, ts // d, w), lambda i, j: (i, 0, j, 0)) for d in DILATIONS]
    consts = (g1, g2, wg, wap, wfp, wout, wrh, wrl, br, expand, ltri, etri) + tuple(perms)
    per_tile = lambda rows, w: pl.BlockSpec((MERGE_TILES_PER_STEP, rows, w), lambda i, j: (i * per_seq + j, 0, 0))
    n_tiles = b * s // MERGE_ROWS
    in_specs = ([tile(D_MODEL), pl.BlockSpec((1, 6, D_MODEL), lambda i, j: (i, 0, 0))]
                + strided(ATTN_WIDTH) + strided(LANES) + [tile(ATTN_WIDTH)]
                + [_const_spec(a.shape) for a in consts])
    per_seq = s // ts
    return pl.pallas_call(
        _merge_kernel,
        grid=(b, per_seq),
        in_specs=in_specs,
        out_specs=[tile(D_MODEL), tile(D_MODEL), tile(LANES), per_tile(8, MERGE_ROWS), per_tile(8, LANES),
                   pl.BlockSpec((8, LANES), lambda i, j: (0, 0))],
        out_shape=[jax.ShapeDtypeStruct((b, s, D_MODEL), F32),
                   jax.ShapeDtypeStruct((b, s, D_MODEL), BF16),
                   jax.ShapeDtypeStruct((b, s, LANES), F32),
                   jax.ShapeDtypeStruct((n_tiles, 8, MERGE_ROWS), F32),
                   jax.ShapeDtypeStruct((n_tiles, 8, LANES), jnp.int32),
                   jax.ShapeDtypeStruct((8, LANES), F32)],
        scratch_shapes=[pltpu.VMEM((LANES, LANES), F32)],
        compiler_params=pltpu.CompilerParams(
            dimension_semantics=("arbitrary", "arbitrary"), vmem_limit_bytes=VMEM_LIMIT),
        name="merge",
    )(x, mod, *os_, *lses, fm, *consts)


def _to_packed_rows(ref, mat):
    bits = lax.bitcast_convert_type(mat, jnp.uint32)
    for c in range(ROW_WORDS):
        lo = bits[:, c * LANES:(c + 1) * LANES] >> 16
        hi = bits[:, (c + ROW_WORDS) * LANES:(c + ROW_WORDS + 1) * LANES] & jnp.uint32(0xFFFF0000)
        ref[pl.ds(c, mat.shape[0], stride=ROW_WORDS), :] = lo | hi


def _from_packed_rows(ref):
    n = ref.shape[0] // ROW_WORDS
    words = [ref[pl.ds(c, n, stride=ROW_WORDS), :] for c in range(ROW_WORDS)]
    lo = [lax.bitcast_convert_type(w << 16, F32) for w in words]
    hi = [lax.bitcast_convert_type(w & jnp.uint32(0xFFFF0000), F32) for w in words]
    return jnp.concatenate(lo + hi, axis=1)


def _segment_copies(seg_n_ref, seg_row_ref, tile, sorted_ref, hbm_ref, sem, *, to_hbm, start, consecutive=True):
    base = tile * N_EXPERTS

    def copy(src, dst, off, size):
        vm_row = (src + off) if consecutive else 0
        vm = sorted_ref.at[pl.ds(pl.multiple_of(vm_row * ROW_WORDS, 2 * ROW_WORDS), size * ROW_WORDS)]
        hb = hbm_ref.at[pl.ds(pl.multiple_of((dst + off) * ROW_WORDS, 2 * ROW_WORDS), size * ROW_WORDS)]
        cp = pltpu.make_async_copy(vm, hb, sem) if to_hbm else pltpu.make_async_copy(hb, vm, sem)
        if start:
            cp.start()
        else:
            cp.wait()

    def segment(e, src):
        n = seg_n_ref[base + e]
        dst = seg_row_ref[base + e]
        n_long = n >> (LONG_PIECE.bit_length() - 1)

        def long_piece(i, c):
            copy(src, dst, i * LONG_PIECE, LONG_PIECE)
            return c

        lax.fori_loop(0, n_long, long_piece, 0)
        for bit in SHORT_PIECES:
            @pl.when((n & bit) != 0)
            def _():
                copy(src, dst, n & (-2 * bit), bit)
        return src + n

    lax.fori_loop(0, N_EXPERTS, segment, 0)


PIECE_SIZES = (LONG_PIECE,) + SHORT_PIECES
PIECE_SLOTS = 32
PIECE_TABLE = 512


def _piece_tables(n, src0, dst0):
    slots = jnp.arange(PIECE_SLOTS, dtype=jnp.int32)

    def compact(mask, src, dst):
        rank = jnp.cumsum(mask.astype(jnp.int32), axis=1) - 1
        sel = mask[:, :, None] & (rank[:, :, None] == slots)
        pick = lambda v: jnp.sum(jnp.where(sel, v[:, :, None], 0), axis=1)
        return jnp.concatenate([pick(src), pick(dst)], axis=1), jnp.sum(mask.astype(jnp.int32), axis=1)

    n_long = n >> (LONG_PIECE.bit_length() - 1)
    j = jnp.arange(MERGE_ROWS // LONG_PIECE, dtype=jnp.int32)
    offs = (j * LONG_PIECE)[None, :, None]
    flat = lambda v: v.reshape(v.shape[0], -1)
    lists, counts = [], []
    lst, cnt = compact(flat(n_long[:, None, :] > j[None, :, None]), flat(src0[:, None, :] + offs),
                       flat(dst0[:, None, :] + offs))
    lists.append(lst)
    counts.append(cnt)
    for bit in SHORT_PIECES:
        off = n & (-2 * bit)
        lst, cnt = compact((n & bit) != 0, src0 + off, dst0 + off)
        lists.append(lst)
        counts.append(cnt)
    head = jnp.stack(counts, axis=1)
    head = jnp.pad(head, ((0, 0), (0, 2 * PIECE_SLOTS - len(PIECE_SIZES))))
    table = jnp.concatenate([head] + lists, axis=1)
    table = jnp.pad(table, ((0, 0), (0, PIECE_TABLE - table.shape[1])))
    return table.reshape(-1, 1, PIECE_TABLE)


def _start_listed_copies(table_ref, sorted_ref, hbm_ref, sem, *, to_hbm):
    for c, size in enumerate(PIECE_SIZES):
        base = 2 * PIECE_SLOTS * (c + 1)

        def piece(i, carry):
            src = table_ref[0, 0, base + i]
            dst = table_ref[0, 0, base + PIECE_SLOTS + i]
            vm = sorted_ref.at[pl.ds(pl.multiple_of(src * ROW_WORDS, 2 * ROW_WORDS), size * ROW_WORDS)]
            hb = hbm_ref.at[pl.ds(pl.multiple_of(dst * ROW_WORDS, 2 * ROW_WORDS), size * ROW_WORDS)]
            (pltpu.make_async_copy(vm, hb, sem) if to_hbm else pltpu.make_async_copy(hb, vm, sem)).start()
            return carry

        lax.fori_loop(0, table_ref[0, 0, c], piece, 0)


def _wait_tile_copies(seg_n_ref, tile, sorted_ref, hbm_ref, sem, *, to_hbm):
    base = tile * N_EXPERTS
    total = lax.fori_loop(0, N_EXPERTS, lambda e, acc: acc + seg_n_ref[base + e], 0)

    def wait(size):
        vm = sorted_ref.at[pl.ds(0, size * ROW_WORDS)]
        hb = hbm_ref.at[pl.ds(0, size * ROW_WORDS)]
        (pltpu.make_async_copy(vm, hb, sem) if to_hbm else pltpu.make_async_copy(hb, vm, sem)).wait()

    def long_wait(i, c):
        wait(LONG_PIECE)
        return c

    lax.fori_loop(0, total >> (LONG_PIECE.bit_length() - 1), long_wait, 0)
    for bit in SHORT_PIECES:
        @pl.when((total & bit) != 0)
        def _():
            wait(bit)


def _dispatch_kernel(seg_n_ref, seg_row_ref, table_ref, routet_ref, u2_ref, xs_ref, sorted_scr, sems):
    tile = pl.program_id(0)
    n_tiles = pl.num_programs(0)
    slot = tile % 2
    tt = u2_ref.shape[0]
    rows = SORTED_ROWS

    def wait_copies(tile_, slot_):
        _wait_tile_copies(seg_n_ref, tile_, sorted_scr.at[slot_], xs_ref, sems.at[slot_], to_hbm=True)

    @pl.when(tile == 0)
    def _():
        sorted_scr[1, 0:LONG_PIECE * ROW_WORDS, :] = jnp.zeros((LONG_PIECE * ROW_WORDS, LANES), jnp.uint32)
        for start in (True, False):
            _segment_copies(seg_n_ref, seg_row_ref, n_tiles, sorted_scr.at[1], xs_ref, sems.at[1],
                            to_hbm=True, start=start, consecutive=False)

    pos_t = routet_ref[0]
    j = lax.broadcasted_iota(jnp.int32, (rows, tt), 0).astype(F32)
    onehot = jnp.zeros((rows, tt), F32)
    for k in range(TOP_K):
        onehot = jnp.where(j == pos_t[TOP_K + k:TOP_K + k + 1, :], 1.0, onehot)
    _to_packed_rows(sorted_scr.at[slot], _dot(onehot.astype(BF16), u2_ref[...]))
    _start_listed_copies(table_ref, sorted_scr.at[slot], xs_ref, sems.at[slot], to_hbm=True)

    @pl.when(tile > 0)
    def _():
        wait_copies(tile - 1, 1 - slot)

    @pl.when(tile == n_tiles - 1)
    def _():
        wait_copies(tile, slot)


def _dispatch(seg_n, seg_row, table, route_t, u2, n_rows, tt=MERGE_ROWS):
    t = u2.shape[0]
    grid_spec = pltpu.PrefetchScalarGridSpec(
        num_scalar_prefetch=2,
        grid=(t // tt,),
        in_specs=[pl.BlockSpec((1, 1, PIECE_TABLE), lambda i, *_: (i, 0, 0), memory_space=pltpu.SMEM),
                  pl.BlockSpec((1, 8, tt), lambda i, *_: (i, 0, 0)),
                  pl.BlockSpec((tt, D_MODEL), lambda i, *_: (i, 0))],
        out_specs=pl.BlockSpec(memory_space=pl.ANY),
        scratch_shapes=[pltpu.VMEM((2, SORTED_ROWS * ROW_WORDS, LANES), jnp.uint32),
                        pltpu.SemaphoreType.DMA((2,))],
    )
    return pl.pallas_call(
        _dispatch_kernel,
        grid_spec=grid_spec,
        out_shape=jax.ShapeDtypeStruct((n_rows * ROW_WORDS, LANES), jnp.uint32),
        compiler_params=pltpu.CompilerParams(
            dimension_semantics=("arbitrary",), has_side_effects=True, vmem_limit_bytes=VMEM_LIMIT),
        name="dispatch",
    )(seg_n, seg_row, table, route_t, u2)


def _expert_kernel(be_ref, nb_ref, xs_ref, w1_ref, b1_ref, w2_ref, b2_ref, ys_ref, w1_scr, w2_scr):
    i = pl.program_id(0)

    @pl.when((i == 0) | (be_ref[i] != be_ref[jnp.maximum(i - 1, 0)]))
    def _():
        chunk = 256
        for r in range(0, D_MODEL, chunk):
            w1_scr[r:r + chunk, :] = w1_ref[0, r:r + chunk, :].astype(BF16)
        for r in range(0, D_FF, chunk):
            w2_scr[r:r + chunk, :] = w2_ref[0, r:r + chunk, :].astype(BF16)

    @pl.when(i < nb_ref[0])
    def _():
        hdn = _dot(_from_packed_rows(xs_ref).astype(BF16), w1_scr[...]) + b1_ref[0]
        glu = jnp.minimum(hdn[:, :D_FF], SWIGLU_LIMIT)
        lin = jnp.clip(hdn[:, D_FF:], -SWIGLU_LIMIT, SWIGLU_LIMIT)
        act = (lin + 1.0) * (glu * jax.nn.sigmoid(SWIGLU_ALPHA * glu))
        y = _dot(act.astype(BF16), w2_scr[...]) + b2_ref[0]
        _to_packed_rows(ys_ref, y.astype(BF16).astype(F32))

    @pl.when(i >= nb_ref[0])
    def _():
        ys_ref[...] = jnp.zeros_like(ys_ref)


def _expert(block_e, nb_used, xs, w1, b1, w2, b2):
    block = (EXPERT_ROWS * ROW_WORDS, LANES)
    n_blocks = xs.shape[0] // block[0]
    grid_spec = pltpu.PrefetchScalarGridSpec(
        num_scalar_prefetch=2,
        grid=(n_blocks,),
        in_specs=[pl.BlockSpec(block, lambda i, be, nb: (jnp.minimum(i, nb[0] - 1), 0)),
                  pl.BlockSpec((1, D_MODEL, 2 * D_FF), lambda i, be, nb: (be[i], 0, 0)),
                  pl.BlockSpec((1, 1, 2 * D_FF), lambda i, be, nb: (be[i], 0, 0)),
                  pl.BlockSpec((1, D_FF, D_MODEL), lambda i, be, nb: (be[i], 0, 0)),
                  pl.BlockSpec((1, 1, D_MODEL), lambda i, be, nb: (be[i], 0, 0))],
        out_specs=pl.BlockSpec(block, lambda i, be, nb: (i, 0)),
        scratch_shapes=[pltpu.VMEM((D_MODEL, 2 * D_FF), BF16), pltpu.VMEM((D_FF, D_MODEL), BF16)],
    )
    return pl.pallas_call(
        _expert_kernel,
        grid_spec=grid_spec,
        out_shape=jax.ShapeDtypeStruct(xs.shape, xs.dtype),
        compiler_params=pltpu.CompilerParams(
            dimension_semantics=("arbitrary",), vmem_limit_bytes=VMEM_LIMIT),
        name="expert",
    )(block_e, nb_used, xs, w1, b1, w2, b2)


def _combine_kernel(seg_n_ref, table_ref, next_table_ref, route_ref, h_ref, mod_ref, ys_ref, o_ref,
                    sorted_scr, sems):
    tile = pl.program_id(0)
    n_tiles = pl.num_programs(0)
    slot = tile % 2
    tt = h_ref.shape[0]
    rows = SORTED_ROWS

    @pl.when(tile == 0)
    def _():
        sorted_scr[...] = jnp.zeros_like(sorted_scr)
        _start_listed_copies(table_ref, sorted_scr.at[slot], ys_ref, sems.at[slot], to_hbm=False)

    @pl.when(tile + 1 < n_tiles)
    def _():
        _start_listed_copies(next_table_ref, sorted_scr.at[1 - slot], ys_ref, sems.at[1 - slot], to_hbm=False)

    route = route_ref[...]
    j = lax.broadcasted_iota(jnp.int32, (tt, rows), 1).astype(F32)
    wmat = jnp.zeros((tt, rows), F32)
    for k in range(TOP_K):
        wmat = jnp.where(j == route[:, TOP_K + k:TOP_K + k + 1], route[:, 2 * TOP_K + k:2 * TOP_K + k + 1], wmat)
    _wait_tile_copies(seg_n_ref, tile, sorted_scr.at[slot], ys_ref, sems.at[slot], to_hbm=False)
    y = _from_packed_rows(sorted_scr.at[slot]).astype(BF16)
    o_ref[...] = h_ref[...] + mod_ref[0][5:6] * _dot(wmat.astype(BF16), y)


def _combine(seg_n, table, route, h, mod, ys, seq, tt=MERGE_ROWS):
    t = h.shape[0]
    n_tiles = t // tt
    per_seq = seq // tt
    table_spec = lambda ahead: pl.BlockSpec(
        (1, 1, PIECE_TABLE), lambda i, *_: (jnp.minimum(i + ahead, n_tiles - 1), 0, 0), memory_space=pltpu.SMEM)
    grid_spec = pltpu.PrefetchScalarGridSpec(
        num_scalar_prefetch=1,
        grid=(n_tiles,),
        in_specs=[table_spec(0), table_spec(1),
                  pl.BlockSpec((tt, LANES), lambda i, *_: (i, 0)),
                  pl.BlockSpec((tt, D_MODEL), lambda i, *_: (i, 0)),
                  pl.BlockSpec((1, 6, D_MODEL), lambda i, *_: (i // per_seq, 0, 0)),
                  pl.BlockSpec(memory_space=pl.ANY)],
        out_specs=pl.BlockSpec((tt, D_MODEL), lambda i, *_: (i, 0)),
        scratch_shapes=[pltpu.VMEM((2, SORTED_ROWS * ROW_WORDS, LANES), jnp.uint32),
                        pltpu.SemaphoreType.DMA((2,))],
    )
    return pl.pallas_call(
        _combine_kernel,
        grid_spec=grid_spec,
        out_shape=jax.ShapeDtypeStruct((t, D_MODEL), F32),
        compiler_params=pltpu.CompilerParams(
            dimension_semantics=("arbitrary",), vmem_limit_bytes=VMEM_LIMIT),
        name="combine",
    )(seg_n, table, table, route, h, mod, ys)


def _rope_tables(seq, ts):
    half = ROPE_DIM // 2
    inv_freq = jnp.power(jnp.float32(ROPE_THETA), -jnp.arange(half, dtype=F32) * 2.0 / ROPE_DIM)
    ang = jnp.arange(seq, dtype=F32)[:, None] * inv_freq[None, :]
    cos, sin = jnp.cos(ang), jnp.sin(ang)
    ones = jnp.ones((seq, HEAD_DIM - ROPE_DIM), F32)
    zeros = jnp.zeros_like(ones)
    zh = jnp.zeros((seq, half), F32)
    cos_t = jnp.concatenate([cos, cos, ones], axis=1)
    sin_prev = jnp.concatenate([zh, sin, zeros], axis=1)
    sin_next = jnp.concatenate([-sin, zh, zeros], axis=1)
    tables = jnp.stack([jnp.tile(t, (1, LANES // HEAD_DIM)) for t in (cos_t, sin_prev, sin_next)])
    per_group = []
    for d in DILATIONS:
        pos = jnp.arange(seq).reshape(seq // ts, ts // d, d).transpose(0, 2, 1).reshape(seq)
        per_group.append(tables[:, pos, :])
    return jnp.stack(per_group)


def _dft_tables(seq):
    k = jnp.arange(seq, dtype=jnp.int32)
    ang = ((k[:, None] * k[None, :]) % seq).astype(F32) * (2.0 * jnp.pi / seq)
    scale = seq ** -0.5
    cs = jnp.cos(ang) * scale
    ns = -jnp.sin(ang) * scale
    c = jnp.arange(FOURIER_GROUP_DIM, dtype=jnp.int32)
    angc = ((c[:, None] * c[None, :]) % FOURIER_GROUP_DIM).astype(F32) * (2.0 * jnp.pi / FOURIER_GROUP_DIM)
    eye = jnp.eye(ATTN_WIDTH // FOURIER_GROUP_DIM, dtype=F32)
    cscale = FOURIER_GROUP_DIM ** -0.5
    bd = jnp.concatenate([jnp.kron(eye, jnp.cos(angc) * cscale),
                          jnp.kron(eye, jnp.sin(angc) * cscale)], axis=1)
    return bd.astype(BF16), cs.astype(BF16), ns.astype(BF16)


def _trunk(x, c, p):
    b, s, _ = x.shape
    t = b * s
    mod = _ada(c, p["w_ada"], p["b_ada"]).reshape(b, 6, D_MODEL)
    *qkvs, fz = _inproj(x, mod, p["g1"], p["w_qkvf"], p["gains"], p["rope"], p["gmat"])
    os_, lses = zip(*[_attn(qkv, g) for g, qkv in enumerate(qkvs)])
    fm = _fourier(fz, p["bd"], p["cs"], p["ns"])
    h, u2, route, route_t, tcnt, cnt = _merge(
        x, mod, os_, lses, fm, p["g1"], p["g2"], p["wg"], p["wap"], p["wfp"], p["wout"], p["wrh"],
        p["wrl"], p["br"], p["expand"], p["ltri"], p["etri"], p["perms"])

    rows = EXPERT_ROWS
    counts = cnt[0, :N_EXPERTS].astype(jnp.int32)
    padded = (counts + rows - 1) // rows * rows
    pend = jnp.cumsum(padded)
    pstart = pend - padded
    seg_row = jnp.concatenate([pstart[None, :] + tcnt[:, 0, :N_EXPERTS], (pstart + counts)[None, :]]).reshape(-1)
    seg_n = jnp.concatenate([tcnt[:, 1, :N_EXPERTS], (padded - counts)[None, :]]).reshape(-1)
    n_blocks = -(-(t * TOP_K + N_EXPERTS * (t // MERGE_ROWS + rows - 1)) // rows)
    block_start = jnp.arange(n_blocks, dtype=jnp.int32) * rows
    block_e = jnp.minimum(jnp.sum((pend[None, :] <= block_start[:, None]).astype(jnp.int32), axis=-1),
                          N_EXPERTS - 1)
    nb_used = (pend[-1:] // rows).astype(jnp.int32)

    n_tile = tcnt[:, 1, :N_EXPERTS]
    table = _piece_tables(n_tile, jnp.cumsum(n_tile, axis=1) - n_tile, pstart[None, :] + tcnt[:, 0, :N_EXPERTS])
    xs = _dispatch(seg_n, seg_row, table, route_t, u2.reshape(t, D_MODEL), n_blocks * rows)
    ys = _expert(block_e, nb_used, xs, p["w1"], p["b1"], p["w2"], p["b2"])
    out = _combine(seg_n, table, route.reshape(t, LANES), h.reshape(t, D_MODEL), mod, ys, s)
    return out.reshape(b, s, D_MODEL)


def kernel(x_prompt, x_sample, c_prompt, c_sample, w_ada, b_ada, norm1_g, norm2_g, w_in, q_gain, k_gain,
           w_attn_proj, w_fourier_proj, w_out, w_router, b_router, w1, b1, w2, b2):
    seq = x_prompt.shape[1]
    assert w_ada.shape[0] == 1, "single-layer trunk"
    w_in0 = w_in[0]
    gains = jnp.zeros((8, ATTN_WIDTH), F32)
    for g in range(N_GROUPS):
        gains = gains.at[2 * g].set(jnp.tile(q_gain[0, g], HEADS))
        gains = gains.at[2 * g + 1].set(jnp.tile(k_gain[0, g], HEADS))
    heads = jnp.arange(ATTN_WIDTH) // HEAD_DIM
    gmat = ((heads[:, None] == heads[None, :]).astype(F32) / HEAD_DIM).astype(BF16)
    expand = (jnp.arange(LANES)[:, None] == heads[None, :]).astype(BF16)
    ltri = (jnp.arange(MERGE_ROWS)[:, None] > jnp.arange(MERGE_ROWS)[None, :]).astype(BF16)
    etri = (jnp.arange(LANES)[:, None] > jnp.arange(LANES)[None, :]).astype(BF16)
    perms = []
    for d in DILATIONS[1:]:
        n = MERGE_ROWS // d
        src = (jnp.arange(MERGE_ROWS) % d) * n + jnp.arange(MERGE_ROWS) // d
        perms.append((src[:, None] == jnp.arange(MERGE_ROWS)[None, :]).astype(BF16))
    wr = jnp.zeros((LANES, D_MODEL), F32).at[:N_EXPERTS].set(w_router[0].T)
    wrh = wr.astype(BF16)
    wrl = (wr - wrh.astype(F32)).astype(BF16)
    br = jnp.zeros((LANES, MERGE_ROWS), F32).at[:N_EXPERTS].set(b_router[0][:, None])
    bd, cs, ns = _dft_tables(seq)
    p = dict(
        w_ada=w_ada[0], b_ada=b_ada[0], g1=norm1_g, g2=norm2_g,
        w_qkvf=w_in0[:, :QKVF_WIDTH].astype(BF16), wg=w_in0[:, QKVF_WIDTH:].astype(BF16),
        gains=gains, rope=_rope_tables(seq, INPROJ_ROWS), gmat=gmat, bd=bd, cs=cs, ns=ns,
        wap=w_attn_proj[0].astype(BF16), wfp=w_fourier_proj[0].astype(BF16), wout=w_out[0].astype(BF16),
        wrh=wrh, wrl=wrl, br=br, expand=expand, ltri=ltri, etri=etri, perms=perms,
        w1=w1[0], b1=b1[0].reshape(N_EXPERTS, 1, 2 * D_FF),
        w2=w2[0], b2=b2[0].reshape(N_EXPERTS, 1, D_MODEL),
    )
    return _trunk(x_prompt, c_prompt, p), _trunk(x_sample, c_sample, p)
```

```python
import functools

import jax
import jax.numpy as jnp
from jax import lax
from jax.experimental import pallas as pl
from jax.experimental.pallas import tpu as pltpu

F32 = jnp.float32
BF16 = jnp.bfloat16

D_MODEL = 1024
HEAD_DIM = 64
HEADS = 8
ATTN_WIDTH = HEADS * HEAD_DIM
DILATIONS = (1, 4, 16)
BAND = 64
N_GROUPS = 3
QKVF_WIDTH = 5120
FOURIER_GROUP_DIM = 64
ROPE_DIM = 16
ROPE_THETA = 500000.0
N_EXPERTS = 32
TOP_K = 4
D_FF = 1024
SWIGLU_ALPHA = 1.702
SWIGLU_LIMIT = 7.0
NORM_EPS = 1e-6
NEG_INF = -1e30
LN2 = 0.6931471805599453
LOG2E = 1.4426950408889634

LANES = 128
ROW_WORDS = D_MODEL // LANES // 2
SORTED_ROWS = 1088
EXPERT_ROWS = 512
LONG_PIECE = 64
SHORT_PIECES = (32, 16, 8, 4, 2)
INPROJ_ROWS = 512
MERGE_ROWS = 256
MERGE_TILES_PER_STEP = 4
MERGE_STAGE_SKEW = 0
VMEM_LIMIT = 56 * 1024 * 1024


def _dot(a, b):
    return jnp.dot(a, b, preferred_element_type=F32)


def _split_bf16(v):
    hi = v.astype(BF16)
    lo = (v - hi.astype(F32)).astype(BF16)
    return hi, lo


def _rms_mod(x, gain, scale, shift):
    ms = jnp.mean(x * x, axis=-1, keepdims=True)
    return x * lax.rsqrt(ms + NORM_EPS) * gain * (1.0 + scale) + shift


def _const_spec(shape):
    return pl.BlockSpec(shape, lambda *_: (0,) * len(shape), pipeline_mode=pl.Buffered(1))


def _ada_kernel(c_ref, w_ref, b_ref, o_ref):
    c = c_ref[...]
    a = c * jax.nn.sigmoid(c)
    a_hi, a_lo = _split_bf16(a)
    w_hi, w_lo = _split_bf16(w_ref[...])
    o_ref[...] = _dot(a_hi, w_hi) + _dot(a_lo, w_hi) + _dot(a_hi, w_lo) + b_ref[...]


def _ada(c, w_ada, b_ada):
    b = c.shape[0]
    n = w_ada.shape[1]
    tn = D_MODEL
    return pl.pallas_call(
        _ada_kernel,
        grid=(n // tn,),
        in_specs=[pl.BlockSpec((b, D_MODEL), lambda j: (0, 0)),
                  pl.BlockSpec((D_MODEL, tn), lambda j: (0, j)),
                  pl.BlockSpec((1, tn), lambda j: (0, j))],
        out_specs=pl.BlockSpec((b, tn), lambda j: (0, j)),
        out_shape=jax.ShapeDtypeStruct((b, n), F32),
        compiler_params=pltpu.CompilerParams(vmem_limit_bytes=VMEM_LIMIT),
        name="ada",
    )(c, w_ada, b_ada.reshape(1, n))


def _inproj_kernel(x_ref, mod_ref, g1_ref, w_ref, gain_ref, rope_ref, gmat_ref,
                   o0_ref, o1_ref, o2_ref, of_ref, u_scr):
    x = x_ref[0]
    ts = x.shape[0]
    mod = mod_ref[0]
    u = _rms_mod(x, g1_ref[...], mod[1:2], mod[0:1])
    ub = u.astype(BF16)
    for c in range(D_MODEL // LANES):
        u_scr[c] = u[:, c * LANES:(c + 1) * LANES]
    for g, o_ref in enumerate((o0_ref, o1_ref, o2_ref)):
        d = DILATIONS[g]
        n = ts // d
        if d == 1:
            ug = ub
        else:
            ug = jnp.concatenate(
                [jnp.concatenate([u_scr[c, pl.ds(r, n, stride=d), :] for r in range(d)], axis=0)
                 for c in range(D_MODEL // LANES)], axis=1).astype(BF16)
        cos_t, sin_prev, sin_next = rope_ref[g, 0], rope_ref[g, 1], rope_ref[g, 2]
        for j in range(3):
            c0 = (3 * g + j) * ATTN_WIDTH
            z = _dot(ug, w_ref[:, c0:c0 + ATTN_WIDTH])
            if j < 2:
                ms = _dot((z * z).astype(BF16), gmat_ref[...])
                gain = gain_ref[2 * g + j:2 * g + j + 1, :]
                if j == 0:
                    gain = gain * (HEAD_DIM ** -0.5 * LOG2E)
                z = z * lax.rsqrt(ms + NORM_EPS) * gain
                blocks = []
                for blk in range(ATTN_WIDTH // LANES):
                    zb = z[:, blk * LANES:(blk + 1) * LANES]
                    blocks.append(zb * cos_t
                                  + pltpu.roll(zb, ROPE_DIM // 2, 1) * sin_prev
                                  + pltpu.roll(zb, LANES - ROPE_DIM // 2, 1) * sin_next)
                z = jnp.concatenate(blocks, axis=1)
            zb16 = z.astype(BF16)
            for r in range(d):
                o_ref[0, r, :, j * ATTN_WIDTH:(j + 1) * ATTN_WIDTH] = zb16[r * n:(r + 1) * n, :]
    of_ref[0] = _dot(ub, w_ref[:, 9 * ATTN_WIDTH:]).astype(BF16)


def _inproj(x, mod, g1, w_qkvf, gains, rope, gmat, ts=INPROJ_ROWS):
    b, s, _ = x.shape
    qkv_w = 3 * ATTN_WIDTH
    out_specs = [pl.BlockSpec((1, d, ts // d, qkv_w), lambda i, j: (i, 0, j, 0)) for d in DILATIONS]
    out_shape = [jax.ShapeDtypeStruct((b, d, s // d, qkv_w), BF16) for d in DILATIONS]
    return pl.pallas_call(
        _inproj_kernel,
        grid=(b, s // ts),
        in_specs=[pl.BlockSpec((1, ts, D_MODEL), lambda i, j: (i, j, 0)),
                  pl.BlockSpec((1, 6, D_MODEL), lambda i, j: (i, 0, 0)),
                  _const_spec((1, D_MODEL)),
                  _const_spec((D_MODEL, QKVF_WIDTH)),
                  _const_spec((8, ATTN_WIDTH)),
                  pl.BlockSpec((N_GROUPS, 3, ts, LANES), lambda i, j: (0, 0, j, 0)),
                  _const_spec((ATTN_WIDTH, ATTN_WIDTH))],
        out_specs=out_specs + [pl.BlockSpec((1, ts, ATTN_WIDTH), lambda i, j: (i, j, 0))],
        out_shape=out_shape + [jax.ShapeDtypeStruct((b, s, ATTN_WIDTH), BF16)],
        scratch_shapes=[pltpu.VMEM((D_MODEL // LANES, ts, LANES), F32)],
        compiler_params=pltpu.CompilerParams(
            dimension_semantics=("arbitrary", "arbitrary"), vmem_limit_bytes=VMEM_LIMIT),
        name="inproj",
    )(x, mod, g1, w_qkvf, gains, rope, gmat)


def _attn_kernel(qkv_ref, o_ref, lse_ref, s_scr, p_scr, *, length):
    tq = LANES
    tk = min(2 * LANES, length)
    nq = length // tq
    n_pairs = ATTN_WIDTH // LANES
    lane = lax.broadcasted_iota(jnp.int32, (tq, LANES), 1)
    first_head = lane < HEAD_DIM
    first_head_k = lax.broadcasted_iota(jnp.int32, (tk, LANES), 1) < HEAD_DIM
    rel = (lax.broadcasted_iota(jnp.int32, (tq, tk), 1)
           - lax.broadcasted_iota(jnp.int32, (tq, tk), 0))

    def q_block(it, carry):
        r = it // nq
        i = it % nq
        q0 = pl.multiple_of(i * tq, tq)
        ws = pl.multiple_of(jnp.clip(i * tq - BAND, 0, length - tk), BAND)
        valid = jnp.abs(rel + (ws - i * tq)) <= BAND

        for p in range(n_pairs):
            q2 = qkv_ref[0, r, pl.ds(q0, tq), p * LANES:(p + 1) * LANES]
            k2 = qkv_ref[0, r, pl.ds(ws, tk), ATTN_WIDTH + p * LANES:ATTN_WIDTH + (p + 1) * LANES]
            for hh in range(2):
                sel = first_head if hh == 0 else jnp.logical_not(first_head)
                qa = jnp.where(sel, q2, jnp.zeros_like(q2))
                s = lax.dot_general(qa, k2, (((1,), (1,)), ((), ())), preferred_element_type=F32)
                s_scr[2 * p + hh] = jnp.where(valid, s, NEG_INF)

        m_tile = jnp.zeros((tq, LANES), F32)
        for h in range(HEADS):
            s = s_scr[h]
            m = jnp.max(s, axis=-1, keepdims=True)
            p_scr[h] = jnp.exp2(s - m).astype(BF16)
            m_tile = jnp.where(lane == h, m, m_tile)

        den_tile = jnp.ones((tq, LANES), F32)
        for p in range(n_pairs):
            cs = slice(2 * ATTN_WIDTH + p * LANES, 2 * ATTN_WIDTH + (p + 1) * LANES)
            v2 = qkv_ref[0, r, pl.ds(ws, tk), cs]
            one = jnp.ones_like(v2)
            o_a = _dot(p_scr[2 * p], jnp.where(first_head_k, v2, one))
            o_b = _dot(p_scr[2 * p + 1], jnp.where(first_head_k, one, v2))
            den_ba = jnp.where(first_head, o_b, o_a)
            den_ab = pltpu.roll(den_ba, HEAD_DIM, 1)
            o_ref[0, r, pl.ds(q0, tq), p * LANES:(p + 1) * LANES] = (
                jnp.where(first_head, o_a, o_b) / den_ab).astype(BF16)
            den_tile = jnp.where(lane == 2 * p, den_ab, den_tile)
            den_tile = jnp.where(lane == 2 * p + 1, den_ba, den_tile)
        lse_ref[0, r, pl.ds(q0, tq), :] = (m_tile + jnp.log2(den_tile)) * LN2
        return carry

    lax.fori_loop(0, qkv_ref.shape[1] * nq, q_block, 0)


def _attn(qkv, g):
    b, d, length, width = qkv.shape
    tk = min(2 * LANES, length)
    return pl.pallas_call(
        functools.partial(_attn_kernel, length=length),
        grid=(b,),
        in_specs=[pl.BlockSpec((1, d, length, width), lambda i: (i, 0, 0, 0))],
        out_specs=[pl.BlockSpec((1, d, length, ATTN_WIDTH), lambda i: (i, 0, 0, 0)),
                   pl.BlockSpec((1, d, length, LANES), lambda i: (i, 0, 0, 0))],
        out_shape=[jax.ShapeDtypeStruct((b, d, length, ATTN_WIDTH), BF16),
                   jax.ShapeDtypeStruct((b, d, length, LANES), F32)],
        scratch_shapes=[pltpu.VMEM((HEADS, LANES, tk), F32), pltpu.VMEM((HEADS, LANES, tk), BF16)],
        compiler_params=pltpu.CompilerParams(
            dimension_semantics=("arbitrary",), vmem_limit_bytes=VMEM_LIMIT),
        name=f"attn{g}",
    )(qkv)


def _fourier_kernel(x_ref, bd_ref, cs_ref, ns_ref, o_ref, y_ref, *, seq, rows):
    for r in range(seq // rows):
        rs = slice(r * rows, (r + 1) * rows)
        y_ref[rs, :] = _dot(x_ref[0, rs, :], bd_ref[...]).astype(BF16)
    for r in range(seq // rows):
        rs = slice(r * rows, (r + 1) * rows)
        o = _dot(cs_ref[rs, :], y_ref[:, :ATTN_WIDTH]) + _dot(ns_ref[rs, :], y_ref[:, ATTN_WIDTH:])
        o_ref[0, rs, :] = o.astype(BF16)


def _fourier(qkvf, bd, cs, ns):
    b, s, _ = qkvf.shape
    return pl.pallas_call(
        functools.partial(_fourier_kernel, seq=s, rows=256),
        grid=(b,),
        in_specs=[pl.BlockSpec((1, s, ATTN_WIDTH), lambda i: (i, 0, 0)),
                  _const_spec((ATTN_WIDTH, 2 * ATTN_WIDTH)),
                  _const_spec((s, s)),
                  _const_spec((s, s))],
        out_specs=pl.BlockSpec((1, s, ATTN_WIDTH), lambda i: (i, 0, 0)),
        out_shape=jax.ShapeDtypeStruct((b, s, ATTN_WIDTH), BF16),
        scratch_shapes=[pltpu.VMEM((s, 2 * ATTN_WIDTH), BF16)],
        compiler_params=pltpu.CompilerParams(
            dimension_semantics=("arbitrary",), vmem_limit_bytes=VMEM_LIMIT),
        name="fourier",
    )(qkvf, bd, cs, ns)


def _merge_kernel(x_ref, mod_ref, o0_ref, o1_ref, o2_ref, l0_ref, l1_ref, l2_ref, fm_ref,
                  g1_ref, g2_ref, wg_ref, wap_ref, wfp_ref, wout_ref, wrh_ref, wrl_ref, br_ref,
                  expand_ref, ltri_ref, etri_ref, perm1_ref, perm2_ref,
                  h_ref, u2_ref, route_ref, routet_ref, tcnt_ref, cnt_ref, cnt_scr):
    @pl.when((pl.program_id(0) == 0) & (pl.program_id(1) == 0))
    def _():
        cnt_scr[...] = jnp.zeros_like(cnt_scr)

    refs = (x_ref, mod_ref, o0_ref, o1_ref, o2_ref, l0_ref, l1_ref, l2_ref, fm_ref,
            g1_ref, g2_ref, wg_ref, wap_ref, wfp_ref, wout_ref, wrh_ref, wrl_ref, br_ref,
            expand_ref, ltri_ref, etri_ref, perm1_ref, perm2_ref,
            h_ref, u2_ref, route_ref, routet_ref, tcnt_ref, cnt_ref, cnt_scr)
    tiles = [_merge_tile(sub, *refs) for sub in range(x_ref.shape[1] // MERGE_ROWS)]
    live = set(range(len(tiles)))
    step = 0
    while live:
        for sub, tile in enumerate(tiles):
            if sub in live and step >= sub * MERGE_STAGE_SKEW:
                if next(tile, "done") == "done":
                    live.discard(sub)
        step += 1


def _merge_tile(sub, x_ref, mod_ref, o0_ref, o1_ref, o2_ref, l0_ref, l1_ref, l2_ref, fm_ref,
                g1_ref, g2_ref, wg_ref, wap_ref, wfp_ref, wout_ref, wrh_ref, wrl_ref, br_ref,
                expand_ref, ltri_ref, etri_ref, perm1_ref, perm2_ref,
                h_ref, u2_ref, route_ref, routet_ref, tcnt_ref, cnt_ref, cnt_scr):
    ts = MERGE_ROWS
    rs = slice(sub * ts, (sub + 1) * ts)
    x = x_ref[0, rs, :]
    mod = mod_ref[0]
    shift1, scale1, gate1 = mod[0:1], mod[1:2], mod[2:3]
    shift2, scale2 = mod[3:4], mod[4:5]
    u = _rms_mod(x, g1_ref[...], scale1, shift1).astype(BF16)
    yield
    gates = _dot(u, wg_ref[...])
    yield

    os_ = [o0_ref[0, 0, rs, :].astype(F32)]
    lses = [l0_ref[0, 0, rs, :]]
    for d, o_ref, l_ref, perm_ref in ((DILATIONS[1], o1_ref, l1_ref, perm1_ref),
                                      (DILATIONS[2], o2_ref, l2_ref, perm2_ref)):
        perm = perm_ref[...]
        ms = slice(sub * ts // d, (sub + 1) * ts // d)
        os_.append(_dot(perm, o_ref[0, :, ms, :].reshape(ts, ATTN_WIDTH)))
        l_hi, l_lo = _split_bf16(l_ref[0, :, ms, :].reshape(ts, LANES))
        lses.append(_dot(perm, l_hi) + _dot(perm, l_lo))
    yield

    mx = jnp.maximum(jnp.maximum(lses[0], lses[1]), lses[2])
    es = [jnp.exp(l - mx) for l in lses]
    den = es[0] + es[1] + es[2]
    attn = None
    for e, o in zip(es, os_):
        w_hi, w_lo = _split_bf16(e / den)
        wf = _dot(w_hi, expand_ref[...]) + _dot(w_lo, expand_ref[...])
        term = wf * o
        attn = term if attn is None else attn + term
    yield

    a = _dot(attn.astype(BF16), wap_ref[...])
    f = _dot(fm_ref[0, rs, :], wfp_ref[...])
    yield
    merged = jax.nn.sigmoid(gates[:, :D_MODEL]) * a + jax.nn.sigmoid(gates[:, D_MODEL:]) * f
    yield
    h = x + gate1 * _dot(merged.astype(BF16), wout_ref[...])
    h_ref[0, rs, :] = h
    yield

    u2 = _rms_mod(h, g2_ref[...], scale2, shift2)
    u_hi, u_lo = _split_bf16(u2)
    u2_ref[0, rs, :] = u_hi
    yield
    nt = (((1,), (1,)), ((), ()))
    logits = (lax.dot_general(wrh_ref[...], u_hi, nt, preferred_element_type=F32)
              + lax.dot_general(wrh_ref[...], u_lo, nt, preferred_element_type=F32)
              + lax.dot_general(wrl_ref[...], u_hi, nt, preferred_element_type=F32) + br_ref[...])
    erow = lax.broadcasted_iota(jnp.int32, (LANES, ts), 0)
    lg = jnp.where(erow < N_EXPERTS, logits, -jnp.inf)
    vals, ids = [], []
    onehot = jnp.zeros((LANES, ts), F32)
    for _ in range(TOP_K):
        m = jnp.max(lg, axis=0, keepdims=True)
        idx = jnp.min(jnp.where(lg == m, erow, LANES), axis=0, keepdims=True)
        hit = erow == idx
        vals.append(m)
        ids.append(idx)
        onehot = jnp.where(hit, 1.0, onehot)
        lg = jnp.where(hit, -jnp.inf, lg)
    exps = [jnp.exp(v - vals[0]) for v in vals]
    esum = exps[0] + exps[1] + exps[2] + exps[3]
    yield

    n_tile = jnp.sum(onehot, axis=1, keepdims=True)
    n_tile = n_tile + (n_tile - 2.0 * jnp.floor(0.5 * n_tile))
    n_wide = jnp.broadcast_to(n_tile, (LANES, LANES))
    seg_start = _dot(etri_ref[...], n_wide.astype(BF16))[:, 0:1]
    pos_all = lax.dot_general(onehot.astype(BF16), ltri_ref[...], nt, preferred_element_type=F32) + seg_start
    route = jnp.zeros((LANES, ts), F32)
    for k in range(TOP_K):
        pos = jnp.sum(jnp.where(erow == ids[k], pos_all, 0.0), axis=0, keepdims=True)
        route = jnp.where(erow == k, ids[k].astype(F32), route)
        route = jnp.where(erow == TOP_K + k, pos, route)
        route = jnp.where(erow == 2 * TOP_K + k, exps[k] / esum, route)
    route_ref[0, rs, :] = route.T
    routet_ref[sub] = route[0:8, :]
    lane = lax.broadcasted_iota(jnp.int32, (LANES, LANES), 1)
    tcnt = jnp.where(lane == 0, cnt_scr[...], jnp.where(lane == 1, n_wide, 0.0)).T
    tcnt_ref[sub] = tcnt[0:8, :].astype(jnp.int32)
    cnt_scr[...] = cnt_scr[...] + n_tile
    cnt_ref[...] = cnt_scr[...].T[0:8, :]


def _merge(x, mod, os_, lses, fm, g1, g2, wg, wap, wfp, wout, wrh, wrl, br, expand, ltri, etri, perms):
    b, s, _ = x.shape
    ts = MERGE_ROWS * MERGE_TILES_PER_STEP
    tile = lambda w: pl.BlockSpec((1, ts, w), lambda i, j: (i, j, 0))
    strided = lambda w: [pl.BlockSpec((1, d, ts // d, w), lambda i, j: (i, 0, j, 0)) for d in DILATIONS]
    consts = (g1, g2, wg, wap, wfp, wout, wrh, wrl, br, expand, ltri, etri) + tuple(perms)
    per_tile = lambda rows, w: pl.BlockSpec((MERGE_TILES_PER_STEP, rows, w), lambda i, j: (i * per_seq + j, 0, 0))
    n_tiles = b * s // MERGE_ROWS
    in_specs = ([tile(D_MODEL), pl.BlockSpec((1, 6, D_MODEL), lambda i, j: (i, 0, 0))]
                + strided(ATTN_WIDTH) + strided(LANES) + [tile(ATTN_WIDTH)]
                + [_const_spec(a.shape) for a in consts])
    per_seq = s // ts
    return pl.pallas_call(
        _merge_kernel,
        grid=(b, per_seq),
        in_specs=in_specs,
        out_specs=[tile(D_MODEL), tile(D_MODEL), tile(LANES), per_tile(8, MERGE_ROWS), per_tile(8, LANES),
                   pl.BlockSpec((8, LANES), lambda i, j: (0, 0))],
        out_shape=[jax.ShapeDtypeStruct((b, s, D_MODEL), F32),
                   jax.ShapeDtypeStruct((b, s, D_MODEL), BF16),
                   jax.ShapeDtypeStruct((b, s, LANES), F32),
                   jax.ShapeDtypeStruct((n_tiles, 8, MERGE_ROWS), F32),
                   jax.ShapeDtypeStruct((n_tiles, 8, LANES), jnp.int32),
                   jax.ShapeDtypeStruct((8, LANES), F32)],
        scratch_shapes=[pltpu.VMEM((LANES, LANES), F32)],
        compiler_params=pltpu.CompilerParams(
            dimension_semantics=("arbitrary", "arbitrary"), vmem_limit_bytes=VMEM_LIMIT),
        name="merge",
    )(x, mod, *os_, *lses, fm, *consts)


def _to_packed_rows(ref, mat):
    bits = lax.bitcast_convert_type(mat, jnp.uint32)
    for c in range(ROW_WORDS):
        lo = bits[:, c * LANES:(c + 1) * LANES] >> 16
        hi = bits[:, (c + ROW_WORDS) * LANES:(c + ROW_WORDS + 1) * LANES] & jnp.uint32(0xFFFF0000)
        ref[pl.ds(c, mat.shape[0], stride=ROW_WORDS), :] = lo | hi


def _from_packed_rows(ref):
    n = ref.shape[0] // ROW_WORDS
    words = [ref[pl.ds(c, n, stride=ROW_WORDS), :] for c in range(ROW_WORDS)]
    lo = [lax.bitcast_convert_type(w << 16, F32) for w in words]
    hi = [lax.bitcast_convert_type(w & jnp.uint32(0xFFFF0000), F32) for w in words]
    return jnp.concatenate(lo + hi, axis=1)


def _segment_copies(seg_n_ref, seg_row_ref, tile, sorted_ref, hbm_ref, sem, *, to_hbm, start, consecutive=True):
    base = tile * N_EXPERTS

    def copy(src, dst, off, size):
        vm_row = (src + off) if consecutive else 0
        vm = sorted_ref.at[pl.ds(pl.multiple_of(vm_row * ROW_WORDS, 2 * ROW_WORDS), size * ROW_WORDS)]
        hb = hbm_ref.at[pl.ds(pl.multiple_of((dst + off) * ROW_WORDS, 2 * ROW_WORDS), size * ROW_WORDS)]
        cp = pltpu.make_async_copy(vm, hb, sem) if to_hbm else pltpu.make_async_copy(hb, vm, sem)
        if start:
            cp.start()
        else:
            cp.wait()

    def segment(e, src):
        n = seg_n_ref[base + e]
        dst = seg_row_ref[base + e]
        n_long = n >> (LONG_PIECE.bit_length() - 1)

        def long_piece(i, c):
            copy(src, dst, i * LONG_PIECE, LONG_PIECE)
            return c

        lax.fori_loop(0, n_long, long_piece, 0)
        for bit in SHORT_PIECES:
            @pl.when((n & bit) != 0)
            def _():
                copy(src, dst, n & (-2 * bit), bit)
        return src + n

    lax.fori_loop(0, N_EXPERTS, segment, 0)


PIECE_SIZES = (LONG_PIECE,) + SHORT_PIECES
PIECE_SLOTS = 32
PIECE_TABLE = 512


def _piece_tables(n, src0, dst0):
    slots = jnp.arange(PIECE_SLOTS, dtype=jnp.int32)

    def compact(mask, src, dst):
        rank = jnp.cumsum(mask.astype(jnp.int32), axis=1) - 1
        sel = mask[:, :, None] & (rank[:, :, None] == slots)
        pick = lambda v: jnp.sum(jnp.where(sel, v[:, :, None], 0), axis=1)
        return jnp.concatenate([pick(src), pick(dst)], axis=1), jnp.sum(mask.astype(jnp.int32), axis=1)

    n_long = n >> (LONG_PIECE.bit_length() - 1)
    j = jnp.arange(MERGE_ROWS // LONG_PIECE, dtype=jnp.int32)
    offs = (j * LONG_PIECE)[None, :, None]
    flat = lambda v: v.reshape(v.shape[0], -1)
    lists, counts = [], []
    lst, cnt = compact(flat(n_long[:, None, :] > j[None, :, None]), flat(src0[:, None, :] + offs),
                       flat(dst0[:, None, :] + offs))
    lists.append(lst)
    counts.append(cnt)
    for bit in SHORT_PIECES:
        off = n & (-2 * bit)
        lst, cnt = compact((n & bit) != 0, src0 + off, dst0 + off)
        lists.append(lst)
        counts.append(cnt)
    head = jnp.stack(counts, axis=1)
    head = jnp.pad(head, ((0, 0), (0, 2 * PIECE_SLOTS - len(PIECE_SIZES))))
    table = jnp.concatenate([head] + lists, axis=1)
    table = jnp.pad(table, ((0, 0), (0, PIECE_TABLE - table.shape[1])))
    return table.reshape(-1, 1, PIECE_TABLE)


def _start_listed_copies(table_ref, sorted_ref, hbm_ref, sem, *, to_hbm):
    for c, size in enumerate(PIECE_SIZES):
        base = 2 * PIECE_SLOTS * (c + 1)

        def piece(i, carry):
            src = table_ref[0, 0, base + i]
            dst = table_ref[0, 0, base + PIECE_SLOTS + i]
            vm = sorted_ref.at[pl.ds(pl.multiple_of(src * ROW_WORDS, 2 * ROW_WORDS), size * ROW_WORDS)]
            hb = hbm_ref.at[pl.ds(pl.multiple_of(dst * ROW_WORDS, 2 * ROW_WORDS), size * ROW_WORDS)]
            (pltpu.make_async_copy(vm, hb, sem) if to_hbm else pltpu.make_async_copy(hb, vm, sem)).start()
            return carry

        lax.fori_loop(0, table_ref[0, 0, c], piece, 0)


def _wait_tile_copies(seg_n_ref, tile, sorted_ref, hbm_ref, sem, *, to_hbm):
    base = tile * N_EXPERTS
    total = lax.fori_loop(0, N_EXPERTS, lambda e, acc: acc + seg_n_ref[base + e], 0)

    def wait(size):
        vm = sorted_ref.at[pl.ds(0, size * ROW_WORDS)]
        hb = hbm_ref.at[pl.ds(0, size * ROW_WORDS)]
        (pltpu.make_async_copy(vm, hb, sem) if to_hbm else pltpu.make_async_copy(hb, vm, sem)).wait()

    def long_wait(i, c):
        wait(LONG_PIECE)
        return c

    lax.fori_loop(0, total >> (LONG_PIECE.bit_length() - 1), long_wait, 0)
    for bit in SHORT_PIECES:
        @pl.when((total & bit) != 0)
        def _():
            wait(bit)


def _dispatch_kernel(seg_n_ref, seg_row_ref, table_ref, routet_ref, u2_ref, xs_ref, sorted_scr, sems):
    tile = pl.program_id(0)
    n_tiles = pl.num_programs(0)
    slot = tile % 2
    tt = u2_ref.shape[0]
    rows = SORTED_ROWS

    def wait_copies(tile_, slot_):
        _wait_tile_copies(seg_n_ref, tile_, sorted_scr.at[slot_], xs_ref, sems.at[slot_], to_hbm=True)

    @pl.when(tile == 0)
    def _():
        sorted_scr[1, 0:LONG_PIECE * ROW_WORDS, :] = jnp.zeros((LONG_PIECE * ROW_WORDS, LANES), jnp.uint32)
        for start in (True, False):
            _segment_copies(seg_n_ref, seg_row_ref, n_tiles, sorted_scr.at[1], xs_ref, sems.at[1],
                            to_hbm=True, start=start, consecutive=False)

    pos_t = routet_ref[0]
    j = lax.broadcasted_iota(jnp.int32, (rows, tt), 0).astype(F32)
    onehot = jnp.zeros((rows, tt), F32)
    for k in range(TOP_K):
        onehot = jnp.where(j == pos_t[TOP_K + k:TOP_K + k + 1, :], 1.0, onehot)
    _to_packed_rows(sorted_scr.at[slot], _dot(onehot.astype(BF16), u2_ref[...]))
    _start_listed_copies(table_ref, sorted_scr.at[slot], xs_ref, sems.at[slot], to_hbm=True)

    @pl.when(tile > 0)
    def _():
        wait_copies(tile - 1, 1 - slot)

    @pl.when(tile == n_tiles - 1)
    def _():
        wait_copies(tile, slot)


def _dispatch(seg_n, seg_row, table, route_t, u2, n_rows, tt=MERGE_ROWS):
    t = u2.shape[0]
    grid_spec = pltpu.PrefetchScalarGridSpec(
        num_scalar_prefetch=2,
        grid=(t // tt,),
        in_specs=[pl.BlockSpec((1, 1, PIECE_TABLE), lambda i, *_: (i, 0, 0), memory_space=pltpu.SMEM),
                  pl.BlockSpec((1, 8, tt), lambda i, *_: (i, 0, 0)),
                  pl.BlockSpec((tt, D_MODEL), lambda i, *_: (i, 0))],
        out_specs=pl.BlockSpec(memory_space=pl.ANY),
        scratch_shapes=[pltpu.VMEM((2, SORTED_ROWS * ROW_WORDS, LANES), jnp.uint32),
                        pltpu.SemaphoreType.DMA((2,))],
    )
    return pl.pallas_call(
        _dispatch_kernel,
        grid_spec=grid_spec,
        out_shape=jax.ShapeDtypeStruct((n_rows * ROW_WORDS, LANES), jnp.uint32),
        compiler_params=pltpu.CompilerParams(
            dimension_semantics=("arbitrary",), has_side_effects=True, vmem_limit_bytes=VMEM_LIMIT),
        name="dispatch",
    )(seg_n, seg_row, table, route_t, u2)


def _expert_kernel(be_ref, nb_ref, xs_ref, w1_ref, b1_ref, w2_ref, b2_ref, ys_ref):
    del be_ref
    i = pl.program_id(0)

    @pl.when(i < nb_ref[0])
    def _():
        hdn = _dot(_from_packed_rows(xs_ref).astype(BF16), w1_ref[0]) + b1_ref[0]
        glu = jnp.minimum(hdn[:, :D_FF], SWIGLU_LIMIT)
        lin = jnp.clip(hdn[:, D_FF:], -SWIGLU_LIMIT, SWIGLU_LIMIT)
        act = (lin + 1.0) * (glu * jax.nn.sigmoid(SWIGLU_ALPHA * glu))
        y = _dot(act.astype(BF16), w2_ref[0]) + b2_ref[0]
        _to_packed_rows(ys_ref, y.astype(BF16).astype(F32))

    @pl.when(i >= nb_ref[0])
    def _():
        ys_ref[...] = jnp.zeros_like(ys_ref)


def _expert(block_e, nb_used, xs, w1, b1, w2, b2):
    block = (EXPERT_ROWS * ROW_WORDS, LANES)
    n_blocks = xs.shape[0] // block[0]
    grid_spec = pltpu.PrefetchScalarGridSpec(
        num_scalar_prefetch=2,
        grid=(n_blocks,),
        in_specs=[pl.BlockSpec(block, lambda i, be, nb: (jnp.minimum(i, nb[0] - 1), 0)),
                  pl.BlockSpec((1, D_MODEL, 2 * D_FF), lambda i, be, nb: (be[i], 0, 0)),
                  pl.BlockSpec((1, 1, 2 * D_FF), lambda i, be, nb: (be[i], 0, 0)),
                  pl.BlockSpec((1, D_FF, D_MODEL), lambda i, be, nb: (be[i], 0, 0)),
                  pl.BlockSpec((1, 1, D_MODEL), lambda i, be, nb: (be[i], 0, 0))],
        out_specs=pl.BlockSpec(block, lambda i, be, nb: (i, 0)),
    )
    return pl.pallas_call(
        _expert_kernel,
        grid_spec=grid_spec,
        out_shape=jax.ShapeDtypeStruct(xs.shape, xs.dtype),
        compiler_params=pltpu.CompilerParams(
            dimension_semantics=("arbitrary",), vmem_limit_bytes=VMEM_LIMIT),
        name="expert",
    )(block_e, nb_used, xs, w1, b1, w2, b2)


def _combine_kernel(seg_n_ref, table_ref, next_table_ref, route_ref, h_ref, mod_ref, ys_ref, o_ref,
                    sorted_scr, sems):
    tile = pl.program_id(0)
    n_tiles = pl.num_programs(0)
    slot = tile % 2
    tt = h_ref.shape[0]
    rows = SORTED_ROWS

    @pl.when(tile == 0)
    def _():
        sorted_scr[...] = jnp.zeros_like(sorted_scr)
        _start_listed_copies(table_ref, sorted_scr.at[slot], ys_ref, sems.at[slot], to_hbm=False)

    @pl.when(tile + 1 < n_tiles)
    def _():
        _start_listed_copies(next_table_ref, sorted_scr.at[1 - slot], ys_ref, sems.at[1 - slot], to_hbm=False)

    route = route_ref[...]
    j = lax.broadcasted_iota(jnp.int32, (tt, rows), 1).astype(F32)
    wmat = jnp.zeros((tt, rows), F32)
    for k in range(TOP_K):
        wmat = jnp.where(j == route[:, TOP_K + k:TOP_K + k + 1], route[:, 2 * TOP_K + k:2 * TOP_K + k + 1], wmat)
    _wait_tile_copies(seg_n_ref, tile, sorted_scr.at[slot], ys_ref, sems.at[slot], to_hbm=False)
    y = _from_packed_rows(sorted_scr.at[slot]).astype(BF16)
    o_ref[...] = h_ref[...] + mod_ref[0][5:6] * _dot(wmat.astype(BF16), y)


def _combine(seg_n, table, route, h, mod, ys, seq, tt=MERGE_ROWS):
    t = h.shape[0]
    n_tiles = t // tt
    per_seq = seq // tt
    table_spec = lambda ahead: pl.BlockSpec(
        (1, 1, PIECE_TABLE), lambda i, *_: (jnp.minimum(i + ahead, n_tiles - 1), 0, 0), memory_space=pltpu.SMEM)
    grid_spec = pltpu.PrefetchScalarGridSpec(
        num_scalar_prefetch=1,
        grid=(n_tiles,),
        in_specs=[table_spec(0), table_spec(1),
                  pl.BlockSpec((tt, LANES), lambda i, *_: (i, 0)),
                  pl.BlockSpec((tt, D_MODEL), lambda i, *_: (i, 0)),
                  pl.BlockSpec((1, 6, D_MODEL), lambda i, *_: (i // per_seq, 0, 0)),
                  pl.BlockSpec(memory_space=pl.ANY)],
        out_specs=pl.BlockSpec((tt, D_MODEL), lambda i, *_: (i, 0)),
        scratch_shapes=[pltpu.VMEM((2, SORTED_ROWS * ROW_WORDS, LANES), jnp.uint32),
                        pltpu.SemaphoreType.DMA((2,))],
    )
    return pl.pallas_call(
        _combine_kernel,
        grid_spec=grid_spec,
        out_shape=jax.ShapeDtypeStruct((t, D_MODEL), F32),
        compiler_params=pltpu.CompilerParams(
            dimension_semantics=("arbitrary",), vmem_limit_bytes=VMEM_LIMIT),
        name="combine",
    )(seg_n, table, table, route, h, mod, ys)


def _rope_tables(seq, ts):
    half = ROPE_DIM // 2
    inv_freq = jnp.power(jnp.float32(ROPE_THETA), -jnp.arange(half, dtype=F32) * 2.0 / ROPE_DIM)
    ang = jnp.arange(seq, dtype=F32)[:, None] * inv_freq[None, :]
    cos, sin = jnp.cos(ang), jnp.sin(ang)
    ones = jnp.ones((seq, HEAD_DIM - ROPE_DIM), F32)
    zeros = jnp.zeros_like(ones)
    zh = jnp.zeros((seq, half), F32)
    cos_t = jnp.concatenate([cos, cos, ones], axis=1)
    sin_prev = jnp.concatenate([zh, sin, zeros], axis=1)
    sin_next = jnp.concatenate([-sin, zh, zeros], axis=1)
    tables = jnp.stack([jnp.tile(t, (1, LANES // HEAD_DIM)) for t in (cos_t, sin_prev, sin_next)])
    per_group = []
    for d in DILATIONS:
        pos = jnp.arange(seq).reshape(seq // ts, ts // d, d).transpose(0, 2, 1).reshape(seq)
        per_group.append(tables[:, pos, :])
    return jnp.stack(per_group)


def _dft_tables(seq):
    k = jnp.arange(seq, dtype=jnp.int32)
    ang = ((k[:, None] * k[None, :]) % seq).astype(F32) * (2.0 * jnp.pi / seq)
    scale = seq ** -0.5
    cs = jnp.cos(ang) * scale
    ns = -jnp.sin(ang) * scale
    c = jnp.arange(FOURIER_GROUP_DIM, dtype=jnp.int32)
    angc = ((c[:, None] * c[None, :]) % FOURIER_GROUP_DIM).astype(F32) * (2.0 * jnp.pi / FOURIER_GROUP_DIM)
    eye = jnp.eye(ATTN_WIDTH // FOURIER_GROUP_DIM, dtype=F32)
    cscale = FOURIER_GROUP_DIM ** -0.5
    bd = jnp.concatenate([jnp.kron(eye, jnp.cos(angc) * cscale),
                          jnp.kron(eye, jnp.sin(angc) * cscale)], axis=1)
    return bd.astype(BF16), cs.astype(BF16), ns.astype(BF16)


def _trunk(x, c, p):
    b, s, _ = x.shape
    t = b * s
    mod = _ada(c, p["w_ada"], p["b_ada"]).reshape(b, 6, D_MODEL)
    *qkvs, fz = _inproj(x, mod, p["g1"], p["w_qkvf"], p["gains"], p["rope"], p["gmat"])
    os_, lses = zip(*[_attn(qkv, g) for g, qkv in enumerate(qkvs)])
    fm = _fourier(fz, p["bd"], p["cs"], p["ns"])
    h, u2, route, route_t, tcnt, cnt = _merge(
        x, mod, os_, lses, fm, p["g1"], p["g2"], p["wg"], p["wap"], p["wfp"], p["wout"], p["wrh"],
        p["wrl"], p["br"], p["expand"], p["ltri"], p["etri"], p["perms"])

    rows = EXPERT_ROWS
    counts = cnt[0, :N_EXPERTS].astype(jnp.int32)
    padded = (counts + rows - 1) // rows * rows
    pend = jnp.cumsum(padded)
    pstart = pend - padded
    seg_row = jnp.concatenate([pstart[None, :] + tcnt[:, 0, :N_EXPERTS], (pstart + counts)[None, :]]).reshape(-1)
    seg_n = jnp.concatenate([tcnt[:, 1, :N_EXPERTS], (padded - counts)[None, :]]).reshape(-1)
    n_blocks = -(-(t * TOP_K + N_EXPERTS * (t // MERGE_ROWS + rows - 1)) // rows)
    block_start = jnp.arange(n_blocks, dtype=jnp.int32) * rows
    block_e = jnp.minimum(jnp.sum((pend[None, :] <= block_start[:, None]).astype(jnp.int32), axis=-1),
                          N_EXPERTS - 1)
    nb_used = (pend[-1:] // rows).astype(jnp.int32)

    n_tile = tcnt[:, 1, :N_EXPERTS]
    table = _piece_tables(n_tile, jnp.cumsum(n_tile, axis=1) - n_tile, pstart[None, :] + tcnt[:, 0, :N_EXPERTS])
    xs = _dispatch(seg_n, seg_row, table, route_t, u2.reshape(t, D_MODEL), n_blocks * rows)
    ys = _expert(block_e, nb_used, xs, p["w1"], p["b1"], p["w2"], p["b2"])
    out = _combine(seg_n, table, route.reshape(t, LANES), h.reshape(t, D_MODEL), mod, ys, s)
    return out.reshape(b, s, D_MODEL)


def kernel(x_prompt, x_sample, c_prompt, c_sample, w_ada, b_ada, norm1_g, norm2_g, w_in, q_gain, k_gain,
           w_attn_proj, w_fourier_proj, w_out, w_router, b_router, w1, b1, w2, b2):
    seq = x_prompt.shape[1]
    assert w_ada.shape[0] == 1, "single-layer trunk"
    w_in0 = w_in[0]
    gains = jnp.zeros((8, ATTN_WIDTH), F32)
    for g in range(N_GROUPS):
        gains = gains.at[2 * g].set(jnp.tile(q_gain[0, g], HEADS))
        gains = gains.at[2 * g + 1].set(jnp.tile(k_gain[0, g], HEADS))
    heads = jnp.arange(ATTN_WIDTH) // HEAD_DIM
    gmat = ((heads[:, None] == heads[None, :]).astype(F32) / HEAD_DIM).astype(BF16)
    expand = (jnp.arange(LANES)[:, None] == heads[None, :]).astype(BF16)
    ltri = (jnp.arange(MERGE_ROWS)[:, None] > jnp.arange(MERGE_ROWS)[None, :]).astype(BF16)
    etri = (jnp.arange(LANES)[:, None] > jnp.arange(LANES)[None, :]).astype(BF16)
    perms = []
    for d in DILATIONS[1:]:
        n = MERGE_ROWS // d
        src = (jnp.arange(MERGE_ROWS) % d) * n + jnp.arange(MERGE_ROWS) // d
        perms.append((src[:, None] == jnp.arange(MERGE_ROWS)[None, :]).astype(BF16))
    wr = jnp.zeros((LANES, D_MODEL), F32).at[:N_EXPERTS].set(w_router[0].T)
    wrh = wr.astype(BF16)
    wrl = (wr - wrh.astype(F32)).astype(BF16)
    br = jnp.zeros((LANES, MERGE_ROWS), F32).at[:N_EXPERTS].set(b_router[0][:, None])
    bd, cs, ns = _dft_tables(seq)
    p = dict(
        w_ada=w_ada[0], b_ada=b_ada[0], g1=norm1_g, g2=norm2_g,
        w_qkvf=w_in0[:, :QKVF_WIDTH].astype(BF16), wg=w_in0[:, QKVF_WIDTH:].astype(BF16),
        gains=gains, rope=_rope_tables(seq, INPROJ_ROWS), gmat=gmat, bd=bd, cs=cs, ns=ns,
        wap=w_attn_proj[0].astype(BF16), wfp=w_fourier_proj[0].astype(BF16), wout=w_out[0].astype(BF16),
        wrh=wrh, wrl=wrl, br=br, expand=expand, ltri=ltri, etri=etri, perms=perms,
        w1=w1[0].astype(BF16), b1=b1[0].reshape(N_EXPERTS, 1, 2 * D_FF),
        w2=w2[0].astype(BF16), b2=b2[0].reshape(N_EXPERTS, 1, D_MODEL),
    )
    return _trunk(x_prompt, c_prompt, p), _trunk(x_sample, c_sample, p)
```

```python
import functools

import jax
import jax.numpy as jnp
from jax import lax
from jax.experimental import pallas as pl
from jax.experimental.pallas import tpu as pltpu

F32 = jnp.float32
BF16 = jnp.bfloat16

D_MODEL = 1024
HEAD_DIM = 64
HEADS = 8
ATTN_WIDTH = HEADS * HEAD_DIM
DILATIONS = (1, 4, 16)
BAND = 64
N_GROUPS = 3
QKVF_WIDTH = 5120
FOURIER_GROUP_DIM = 64
ROPE_DIM = 16
ROPE_THETA = 500000.0
N_EXPERTS = 32
TOP_K = 4
D_FF = 1024
SWIGLU_ALPHA = 1.702
SWIGLU_LIMIT = 7.0
NORM_EPS = 1e-6
NEG_INF = -1e30
LN2 = 0.6931471805599453
LOG2E = 1.4426950408889634

LANES = 128
ROW_WORDS = D_MODEL // LANES // 2
SORTED_ROWS = 1088
EXPERT_ROWS = 512
LONG_PIECE = 64
SHORT_PIECES = (32, 16, 8, 4, 2)
INPROJ_ROWS = 512
MERGE_ROWS = 256
MERGE_TILES_PER_STEP = 4
ATTN_BLOCKS_PER_STEP = 8
ATTN_STAGE_SKEW = 0
VMEM_LIMIT = 56 * 1024 * 1024


def _dot(a, b):
    return jnp.dot(a, b, preferred_element_type=F32)


def _split_bf16(v):
    hi = v.astype(BF16)
    lo = (v - hi.astype(F32)).astype(BF16)
    return hi, lo


def _rms_mod(x, gain, scale, shift):
    ms = jnp.mean(x * x, axis=-1, keepdims=True)
    return x * lax.rsqrt(ms + NORM_EPS) * gain * (1.0 + scale) + shift


def _run_in_lockstep(stage_generators, skew=0):
    live = dict(enumerate(stage_generators))
    step = 0
    while live:
        for j in [j for j in live if step >= j * skew]:
            if next(live[j], "done") == "done":
                del live[j]
        step += 1


def _const_spec(shape):
    return pl.BlockSpec(shape, lambda *_: (0,) * len(shape), pipeline_mode=pl.Buffered(1))


def _ada_kernel(c_ref, w_ref, b_ref, o_ref):
    c = c_ref[...]
    a = c * jax.nn.sigmoid(c)
    a_hi, a_lo = _split_bf16(a)
    w_hi, w_lo = _split_bf16(w_ref[...])
    o_ref[...] = _dot(a_hi, w_hi) + _dot(a_lo, w_hi) + _dot(a_hi, w_lo) + b_ref[...]


def _ada(c, w_ada, b_ada):
    b = c.shape[0]
    n = w_ada.shape[1]
    tn = D_MODEL
    return pl.pallas_call(
        _ada_kernel,
        grid=(n // tn,),
        in_specs=[pl.BlockSpec((b, D_MODEL), lambda j: (0, 0)),
                  pl.BlockSpec((D_MODEL, tn), lambda j: (0, j)),
                  pl.BlockSpec((1, tn), lambda j: (0, j))],
        out_specs=pl.BlockSpec((b, tn), lambda j: (0, j)),
        out_shape=jax.ShapeDtypeStruct((b, n), F32),
        compiler_params=pltpu.CompilerParams(vmem_limit_bytes=VMEM_LIMIT),
        name="ada",
    )(c, w_ada, b_ada.reshape(1, n))


def _inproj_kernel(x_ref, mod_ref, g1_ref, w_ref, gain_ref, rope_ref, gmat_ref,
                   o0_ref, o1_ref, o2_ref, of_ref, u_scr):
    x = x_ref[0]
    ts = x.shape[0]
    mod = mod_ref[0]
    u = _rms_mod(x, g1_ref[...], mod[1:2], mod[0:1])
    ub = u.astype(BF16)
    for c in range(D_MODEL // LANES):
        u_scr[c] = u[:, c * LANES:(c + 1) * LANES]
    for g, o_ref in enumerate((o0_ref, o1_ref, o2_ref)):
        d = DILATIONS[g]
        n = ts // d
        if d == 1:
            ug = ub
        else:
            ug = jnp.concatenate(
                [jnp.concatenate([u_scr[c, pl.ds(r, n, stride=d), :] for r in range(d)], axis=0)
                 for c in range(D_MODEL // LANES)], axis=1).astype(BF16)
        cos_t, sin_prev, sin_next = rope_ref[g, 0], rope_ref[g, 1], rope_ref[g, 2]
        for j in range(3):
            c0 = (3 * g + j) * ATTN_WIDTH
            z = _dot(ug, w_ref[:, c0:c0 + ATTN_WIDTH])
            if j < 2:
                ms = _dot((z * z).astype(BF16), gmat_ref[...])
                gain = gain_ref[2 * g + j:2 * g + j + 1, :]
                if j == 0:
                    gain = gain * (HEAD_DIM ** -0.5 * LOG2E)
                z = z * lax.rsqrt(ms + NORM_EPS) * gain
                blocks = []
                for blk in range(ATTN_WIDTH // LANES):
                    zb = z[:, blk * LANES:(blk + 1) * LANES]
                    blocks.append(zb * cos_t
                                  + pltpu.roll(zb, ROPE_DIM // 2, 1) * sin_prev
                                  + pltpu.roll(zb, LANES - ROPE_DIM // 2, 1) * sin_next)
                z = jnp.concatenate(blocks, axis=1)
            zb16 = z.astype(BF16)
            for r in range(d):
                o_ref[0, r, :, j * ATTN_WIDTH:(j + 1) * ATTN_WIDTH] = zb16[r * n:(r + 1) * n, :]
    of_ref[0] = _dot(ub, w_ref[:, 9 * ATTN_WIDTH:]).astype(BF16)


def _inproj(x, mod, g1, w_qkvf, gains, rope, gmat, ts=INPROJ_ROWS):
    b, s, _ = x.shape
    qkv_w = 3 * ATTN_WIDTH
    out_specs = [pl.BlockSpec((1, d, ts // d, qkv_w), lambda i, j: (i, 0, j, 0)) for d in DILATIONS]
    out_shape = [jax.ShapeDtypeStruct((b, d, s // d, qkv_w), BF16) for d in DILATIONS]
    return pl.pallas_call(
        _inproj_kernel,
        grid=(b, s // ts),
        in_specs=[pl.BlockSpec((1, ts, D_MODEL), lambda i, j: (i, j, 0)),
                  pl.BlockSpec((1, 6, D_MODEL), lambda i, j: (i, 0, 0)),
                  _const_spec((1, D_MODEL)),
                  _const_spec((D_MODEL, QKVF_WIDTH)),
                  _const_spec((8, ATTN_WIDTH)),
                  pl.BlockSpec((N_GROUPS, 3, ts, LANES), lambda i, j: (0, 0, j, 0)),
                  _const_spec((ATTN_WIDTH, ATTN_WIDTH))],
        out_specs=out_specs + [pl.BlockSpec((1, ts, ATTN_WIDTH), lambda i, j: (i, j, 0))],
        out_shape=out_shape + [jax.ShapeDtypeStruct((b, s, ATTN_WIDTH), BF16)],
        scratch_shapes=[pltpu.VMEM((D_MODEL // LANES, ts, LANES), F32)],
        compiler_params=pltpu.CompilerParams(
            dimension_semantics=("arbitrary", "arbitrary"), vmem_limit_bytes=VMEM_LIMIT),
        name="inproj",
    )(x, mod, g1, w_qkvf, gains, rope, gmat)


def _attn_kernel(qkv_ref, o_ref, lse_ref, s_scr, p_scr, *, length):
    tq = LANES
    tk = min(2 * LANES, length)
    nq = length // tq
    n_pairs = ATTN_WIDTH // LANES
    lane = lax.broadcasted_iota(jnp.int32, (tq, LANES), 1)
    first_head = lane < HEAD_DIM
    first_head_k = lax.broadcasted_iota(jnp.int32, (tk, LANES), 1) < HEAD_DIM
    rel = (lax.broadcasted_iota(jnp.int32, (tq, tk), 1)
           - lax.broadcasted_iota(jnp.int32, (tq, tk), 0))

    def head_pair(blk, it, p, tiles):
        r = it // nq
        i = it % nq
        q0 = pl.multiple_of(i * tq, tq)
        ws = pl.multiple_of(jnp.clip(i * tq - BAND, 0, length - tk), BAND)
        valid = jnp.abs(rel + (ws - i * tq)) <= BAND
        q2 = qkv_ref[0, r, pl.ds(q0, tq), p * LANES:(p + 1) * LANES]
        k2 = qkv_ref[0, r, pl.ds(ws, tk), ATTN_WIDTH + p * LANES:ATTN_WIDTH + (p + 1) * LANES]
        for hh in range(2):
            sel = first_head if hh == 0 else jnp.logical_not(first_head)
            qa = jnp.where(sel, q2, jnp.zeros_like(q2))
            s = lax.dot_general(qa, k2, (((1,), (1,)), ((), ())), preferred_element_type=F32)
            s_scr[blk, 2 * p + hh] = jnp.where(valid, s, NEG_INF)
        yield

        for hh in range(2):
            s = s_scr[blk, 2 * p + hh]
            m = jnp.max(s, axis=-1, keepdims=True)
            p_scr[blk, 2 * p + hh] = jnp.exp2(s - m).astype(BF16)
            tiles["m"] = jnp.where(lane == 2 * p + hh, m, tiles["m"])
        yield

        v2 = qkv_ref[0, r, pl.ds(ws, tk), 2 * ATTN_WIDTH + p * LANES:2 * ATTN_WIDTH + (p + 1) * LANES]
        one = jnp.ones_like(v2)
        o_a = _dot(p_scr[blk, 2 * p], jnp.where(first_head_k, v2, one))
        o_b = _dot(p_scr[blk, 2 * p + 1], jnp.where(first_head_k, one, v2))
        den_ba = jnp.where(first_head, o_b, o_a)
        den_ab = pltpu.roll(den_ba, HEAD_DIM, 1)
        o_ref[0, r, pl.ds(q0, tq), p * LANES:(p + 1) * LANES] = (
            jnp.where(first_head, o_a, o_b) / den_ab).astype(BF16)
        tiles["den"] = jnp.where(lane == 2 * p, den_ab, tiles["den"])
        tiles["den"] = jnp.where(lane == 2 * p + 1, den_ba, tiles["den"])

    def q_blocks(step, carry):
        its = [step * ATTN_BLOCKS_PER_STEP + blk for blk in range(ATTN_BLOCKS_PER_STEP)]
        tiles = [{"m": jnp.zeros((tq, LANES), F32), "den": jnp.ones((tq, LANES), F32)} for _ in its]
        _run_in_lockstep([head_pair(blk, it, p, tiles[blk]) for p in range(n_pairs)
                          for blk, it in enumerate(its)], skew=ATTN_STAGE_SKEW)
        for it, t in zip(its, tiles):
            rows = pl.ds(pl.multiple_of((it % nq) * tq, tq), tq)
            lse_ref[0, it // nq, rows, :] = (t["m"] + jnp.log2(t["den"])) * LN2
        return carry

    lax.fori_loop(0, qkv_ref.shape[1] * nq // ATTN_BLOCKS_PER_STEP, q_blocks, 0)


def _attn(qkv, g):
    b, d, length, width = qkv.shape
    tk = min(2 * LANES, length)
    return pl.pallas_call(
        functools.partial(_attn_kernel, length=length),
        grid=(b,),
        in_specs=[pl.BlockSpec((1, d, length, width), lambda i: (i, 0, 0, 0))],
        out_specs=[pl.BlockSpec((1, d, length, ATTN_WIDTH), lambda i: (i, 0, 0, 0)),
                   pl.BlockSpec((1, d, length, LANES), lambda i: (i, 0, 0, 0))],
        out_shape=[jax.ShapeDtypeStruct((b, d, length, ATTN_WIDTH), BF16),
                   jax.ShapeDtypeStruct((b, d, length, LANES), F32)],
        scratch_shapes=[pltpu.VMEM((ATTN_BLOCKS_PER_STEP, HEADS, LANES, tk), F32),
                        pltpu.VMEM((ATTN_BLOCKS_PER_STEP, HEADS, LANES, tk), BF16)],
        compiler_params=pltpu.CompilerParams(
            dimension_semantics=("arbitrary",), vmem_limit_bytes=VMEM_LIMIT),
        name=f"attn{g}",
    )(qkv)


def _fourier_kernel(x_ref, bd_ref, cs_ref, ns_ref, o_ref, y_ref, *, seq, rows):
    for r in range(seq // rows):
        rs = slice(r * rows, (r + 1) * rows)
        y_ref[rs, :] = _dot(x_ref[0, rs, :], bd_ref[...]).astype(BF16)
    for r in range(seq // rows):
        rs = slice(r * rows, (r + 1) * rows)
        o = _dot(cs_ref[rs, :], y_ref[:, :ATTN_WIDTH]) + _dot(ns_ref[rs, :], y_ref[:, ATTN_WIDTH:])
        o_ref[0, rs, :] = o.astype(BF16)


def _fourier(qkvf, bd, cs, ns):
    b, s, _ = qkvf.shape
    return pl.pallas_call(
        functools.partial(_fourier_kernel, seq=s, rows=256),
        grid=(b,),
        in_specs=[pl.BlockSpec((1, s, ATTN_WIDTH), lambda i: (i, 0, 0)),
                  _const_spec((ATTN_WIDTH, 2 * ATTN_WIDTH)),
                  _const_spec((s, s)),
                  _const_spec((s, s))],
        out_specs=pl.BlockSpec((1, s, ATTN_WIDTH), lambda i: (i, 0, 0)),
        out_shape=jax.ShapeDtypeStruct((b, s, ATTN_WIDTH), BF16),
        scratch_shapes=[pltpu.VMEM((s, 2 * ATTN_WIDTH), BF16)],
        compiler_params=pltpu.CompilerParams(
            dimension_semantics=("arbitrary",), vmem_limit_bytes=VMEM_LIMIT),
        name="fourier",
    )(qkvf, bd, cs, ns)


def _merge_kernel(x_ref, mod_ref, o0_ref, o1_ref, o2_ref, l0_ref, l1_ref, l2_ref, fm_ref,
                  g1_ref, g2_ref, wg_ref, wap_ref, wfp_ref, wout_ref, wrh_ref, wrl_ref, br_ref,
                  expand_ref, ltri_ref, etri_ref, perm1_ref, perm2_ref,
                  h_ref, u2_ref, route_ref, routet_ref, tcnt_ref, cnt_ref, cnt_scr):
    @pl.when((pl.program_id(0) == 0) & (pl.program_id(1) == 0))
    def _():
        cnt_scr[...] = jnp.zeros_like(cnt_scr)

    refs = (x_ref, mod_ref, o0_ref, o1_ref, o2_ref, l0_ref, l1_ref, l2_ref, fm_ref,
            g1_ref, g2_ref, wg_ref, wap_ref, wfp_ref, wout_ref, wrh_ref, wrl_ref, br_ref,
            expand_ref, ltri_ref, etri_ref, perm1_ref, perm2_ref,
            h_ref, u2_ref, route_ref, routet_ref, tcnt_ref, cnt_ref, cnt_scr)
    _run_in_lockstep([_merge_tile(sub, *refs) for sub in range(x_ref.shape[1] // MERGE_ROWS)])


def _merge_tile(sub, x_ref, mod_ref, o0_ref, o1_ref, o2_ref, l0_ref, l1_ref, l2_ref, fm_ref,
                g1_ref, g2_ref, wg_ref, wap_ref, wfp_ref, wout_ref, wrh_ref, wrl_ref, br_ref,
                expand_ref, ltri_ref, etri_ref, perm1_ref, perm2_ref,
                h_ref, u2_ref, route_ref, routet_ref, tcnt_ref, cnt_ref, cnt_scr):
    ts = MERGE_ROWS
    rs = slice(sub * ts, (sub + 1) * ts)
    x = x_ref[0, rs, :]
    mod = mod_ref[0]
    shift1, scale1, gate1 = mod[0:1], mod[1:2], mod[2:3]
    shift2, scale2 = mod[3:4], mod[4:5]
    u = _rms_mod(x, g1_ref[...], scale1, shift1).astype(BF16)
    yield
    gates = _dot(u, wg_ref[...])
    yield

    os_ = [o0_ref[0, 0, rs, :].astype(F32)]
    lses = [l0_ref[0, 0, rs, :]]
    for d, o_ref, l_ref, perm_ref in ((DILATIONS[1], o1_ref, l1_ref, perm1_ref),
                                      (DILATIONS[2], o2_ref, l2_ref, perm2_ref)):
        perm = perm_ref[...]
        ms = slice(sub * ts // d, (sub + 1) * ts // d)
        os_.append(_dot(perm, o_ref[0, :, ms, :].reshape(ts, ATTN_WIDTH)))
        l_hi, l_lo = _split_bf16(l_ref[0, :, ms, :].reshape(ts, LANES))
        lses.append(_dot(perm, l_hi) + _dot(perm, l_lo))
    yield

    mx = jnp.maximum(jnp.maximum(lses[0], lses[1]), lses[2])
    es = [jnp.exp(l - mx) for l in lses]
    den = es[0] + es[1] + es[2]
    attn = None
    for e, o in zip(es, os_):
        w_hi, w_lo = _split_bf16(e / den)
        wf = _dot(w_hi, expand_ref[...]) + _dot(w_lo, expand_ref[...])
        term = wf * o
        attn = term if attn is None else attn + term
    yield

    a = _dot(attn.astype(BF16), wap_ref[...])
    f = _dot(fm_ref[0, rs, :], wfp_ref[...])
    yield
    merged = jax.nn.sigmoid(gates[:, :D_MODEL]) * a + jax.nn.sigmoid(gates[:, D_MODEL:]) * f
    yield
    h = x + gate1 * _dot(merged.astype(BF16), wout_ref[...])
    h_ref[0, rs, :] = h
    yield

    u2 = _rms_mod(h, g2_ref[...], scale2, shift2)
    u_hi, u_lo = _split_bf16(u2)
    u2_ref[0, rs, :] = u_hi
    yield
    nt = (((1,), (1,)), ((), ()))
    logits = (lax.dot_general(wrh_ref[...], u_hi, nt, preferred_element_type=F32)
              + lax.dot_general(wrh_ref[...], u_lo, nt, preferred_element_type=F32)
              + lax.dot_general(wrl_ref[...], u_hi, nt, preferred_element_type=F32) + br_ref[...])
    erow = lax.broadcasted_iota(jnp.int32, (LANES, ts), 0)
    lg = jnp.where(erow < N_EXPERTS, logits, -jnp.inf)
    vals, ids = [], []
    onehot = jnp.zeros((LANES, ts), F32)
    for _ in range(TOP_K):
        m = jnp.max(lg, axis=0, keepdims=True)
        idx = jnp.min(jnp.where(lg == m, erow, LANES), axis=0, keepdims=True)
        hit = erow == idx
        vals.append(m)
        ids.append(idx)
        onehot = jnp.where(hit, 1.0, onehot)
        lg = jnp.where(hit, -jnp.inf, lg)
    exps = [jnp.exp(v - vals[0]) for v in vals]
    esum = exps[0] + exps[1] + exps[2] + exps[3]
    yield

    n_tile = jnp.sum(onehot, axis=1, keepdims=True)
    n_tile = n_tile + (n_tile - 2.0 * jnp.floor(0.5 * n_tile))
    n_wide = jnp.broadcast_to(n_tile, (LANES, LANES))
    seg_start = _dot(etri_ref[...], n_wide.astype(BF16))[:, 0:1]
    pos_all = lax.dot_general(onehot.astype(BF16), ltri_ref[...], nt, preferred_element_type=F32) + seg_start
    route = jnp.zeros((LANES, ts), F32)
    for k in range(TOP_K):
        pos = jnp.sum(jnp.where(erow == ids[k], pos_all, 0.0), axis=0, keepdims=True)
        route = jnp.where(erow == k, ids[k].astype(F32), route)
        route = jnp.where(erow == TOP_K + k, pos, route)
        route = jnp.where(erow == 2 * TOP_K + k, exps[k] / esum, route)
    route_ref[0, rs, :] = route.T
    routet_ref[sub] = route[0:8, :]
    lane = lax.broadcasted_iota(jnp.int32, (LANES, LANES), 1)
    tcnt = jnp.where(lane == 0, cnt_scr[...], jnp.where(lane == 1, n_wide, 0.0)).T
    tcnt_ref[sub] = tcnt[0:8, :].astype(jnp.int32)
    cnt_scr[...] = cnt_scr[...] + n_tile
    cnt_ref[...] = cnt_scr[...].T[0:8, :]


def _merge(x, mod, os_, lses, fm, g1, g2, wg, wap, wfp, wout, wrh, wrl, br, expand, ltri, etri, perms):
    b, s, _ = x.shape
    ts = MERGE_ROWS * MERGE_TILES_PER_STEP
    tile = lambda w: pl.BlockSpec((1, ts, w), lambda i, j: (i, j, 0))
    strided = lambda w: [pl.BlockSpec((1, d, ts // d, w), lambda i, j: (i, 0, j, 0)) for d in DILATIONS]
    consts = (g1, g2, wg, wap, wfp, wout, wrh, wrl, br, expand, ltri, etri) + tuple(perms)
    per_tile = lambda rows, w: pl.BlockSpec((MERGE_TILES_PER_STEP, rows, w), lambda i, j: (i * per_seq + j, 0, 0))
    n_tiles = b * s // MERGE_ROWS
    in_specs = ([tile(D_MODEL), pl.BlockSpec((1, 6, D_MODEL), lambda i, j: (i, 0, 0))]
                + strided(ATTN_WIDTH) + strided(LANES) + [tile(ATTN_WIDTH)]
                + [_const_spec(a.shape) for a in consts])
    per_seq = s // ts
    return pl.pallas_call(
        _merge_kernel,
        grid=(b, per_seq),
        in_specs=in_specs,
        out_specs=[tile(D_MODEL), tile(D_MODEL), tile(LANES), per_tile(8, MERGE_ROWS), per_tile(8, LANES),
                   pl.BlockSpec((8, LANES), lambda i, j: (0, 0))],
        out_shape=[jax.ShapeDtypeStruct((b, s, D_MODEL), F32),
                   jax.ShapeDtypeStruct((b, s, D_MODEL), BF16),
                   jax.ShapeDtypeStruct((b, s, LANES), F32),
                   jax.ShapeDtypeStruct((n_tiles, 8, MERGE_ROWS), F32),
                   jax.ShapeDtypeStruct((n_tiles, 8, LANES), jnp.int32),
                   jax.ShapeDtypeStruct((8, LANES), F32)],
        scratch_shapes=[pltpu.VMEM((LANES, LANES), F32)],
        compiler_params=pltpu.CompilerParams(
            dimension_semantics=("arbitrary", "arbitrary"), vmem_limit_bytes=VMEM_LIMIT),
        name="merge",
    )(x, mod, *os_, *lses, fm, *consts)


def _to_packed_rows(ref, mat):
    bits = lax.bitcast_convert_type(mat, jnp.uint32)
    for c in range(ROW_WORDS):
        lo = bits[:, c * LANES:(c + 1) * LANES] >> 16
        hi = bits[:, (c + ROW_WORDS) * LANES:(c + ROW_WORDS + 1) * LANES] & jnp.uint32(0xFFFF0000)
        ref[pl.ds(c, mat.shape[0], stride=ROW_WORDS), :] = lo | hi


def _from_packed_rows(ref):
    n = ref.shape[0] // ROW_WORDS
    words = [ref[pl.ds(c, n, stride=ROW_WORDS), :] for c in range(ROW_WORDS)]
    lo = [lax.bitcast_convert_type(w << 16, F32) for w in words]
    hi = [lax.bitcast_convert_type(w & jnp.uint32(0xFFFF0000), F32) for w in words]
    return jnp.concatenate(lo + hi, axis=1)


def _segment_copies(seg_n_ref, seg_row_ref, tile, sorted_ref, hbm_ref, sem, *, to_hbm, start, consecutive=True):
    base = tile * N_EXPERTS

    def copy(src, dst, off, size):
        vm_row = (src + off) if consecutive else 0
        vm = sorted_ref.at[pl.ds(pl.multiple_of(vm_row * ROW_WORDS, 2 * ROW_WORDS), size * ROW_WORDS)]
        hb = hbm_ref.at[pl.ds(pl.multiple_of((dst + off) * ROW_WORDS, 2 * ROW_WORDS), size * ROW_WORDS)]
        cp = pltpu.make_async_copy(vm, hb, sem) if to_hbm else pltpu.make_async_copy(hb, vm, sem)
        if start:
            cp.start()
        else:
            cp.wait()

    def segment(e, src):
        n = seg_n_ref[base + e]
        dst = seg_row_ref[base + e]
        n_long = n >> (LONG_PIECE.bit_length() - 1)

        def long_piece(i, c):
            copy(src, dst, i * LONG_PIECE, LONG_PIECE)
            return c

        lax.fori_loop(0, n_long, long_piece, 0)
        for bit in SHORT_PIECES:
            @pl.when((n & bit) != 0)
            def _():
                copy(src, dst, n & (-2 * bit), bit)
        return src + n

    lax.fori_loop(0, N_EXPERTS, segment, 0)


PIECE_SIZES = (LONG_PIECE,) + SHORT_PIECES
PIECE_SLOTS = 32
PIECE_TABLE = 512


def _piece_tables(n, src0, dst0):
    slots = jnp.arange(PIECE_SLOTS, dtype=jnp.int32)

    def compact(mask, src, dst):
        rank = jnp.cumsum(mask.astype(jnp.int32), axis=1) - 1
        sel = mask[:, :, None] & (rank[:, :, None] == slots)
        pick = lambda v: jnp.sum(jnp.where(sel, v[:, :, None], 0), axis=1)
        return jnp.concatenate([pick(src), pick(dst)], axis=1), jnp.sum(mask.astype(jnp.int32), axis=1)

    n_long = n >> (LONG_PIECE.bit_length() - 1)
    j = jnp.arange(MERGE_ROWS // LONG_PIECE, dtype=jnp.int32)
    offs = (j * LONG_PIECE)[None, :, None]
    flat = lambda v: v.reshape(v.shape[0], -1)
    lists, counts = [], []
    lst, cnt = compact(flat(n_long[:, None, :] > j[None, :, None]), flat(src0[:, None, :] + offs),
                       flat(dst0[:, None, :] + offs))
    lists.append(lst)
    counts.append(cnt)
    for bit in SHORT_PIECES:
        off = n & (-2 * bit)
        lst, cnt = compact((n & bit) != 0, src0 + off, dst0 + off)
        lists.append(lst)
        counts.append(cnt)
    head = jnp.stack(counts, axis=1)
    head = jnp.pad(head, ((0, 0), (0, 2 * PIECE_SLOTS - len(PIECE_SIZES))))
    table = jnp.concatenate([head] + lists, axis=1)
    table = jnp.pad(table, ((0, 0), (0, PIECE_TABLE - table.shape[1])))
    return table.reshape(-1, 1, PIECE_TABLE)


def _start_listed_copies(table_ref, sorted_ref, hbm_ref, sem, *, to_hbm):
    for c, size in enumerate(PIECE_SIZES):
        base = 2 * PIECE_SLOTS * (c + 1)

        def piece(i, carry):
            src = table_ref[0, 0, base + i]
            dst = table_ref[0, 0, base + PIECE_SLOTS + i]
            vm = sorted_ref.at[pl.ds(pl.multiple_of(src * ROW_WORDS, 2 * ROW_WORDS), size * ROW_WORDS)]
            hb = hbm_ref.at[pl.ds(pl.multiple_of(dst * ROW_WORDS, 2 * ROW_WORDS), size * ROW_WORDS)]
            (pltpu.make_async_copy(vm, hb, sem) if to_hbm else pltpu.make_async_copy(hb, vm, sem)).start()
            return carry

        lax.fori_loop(0, table_ref[0, 0, c], piece, 0)


def _wait_tile_copies(seg_n_ref, tile, sorted_ref, hbm_ref, sem, *, to_hbm):
    base = tile * N_EXPERTS
    total = lax.fori_loop(0, N_EXPERTS, lambda e, acc: acc + seg_n_ref[base + e], 0)

    def wait(size):
        vm = sorted_ref.at[pl.ds(0, size * ROW_WORDS)]
        hb = hbm_ref.at[pl.ds(0, size * ROW_WORDS)]
        (pltpu.make_async_copy(vm, hb, sem) if to_hbm else pltpu.make_async_copy(hb, vm, sem)).wait()

    def long_wait(i, c):
        wait(LONG_PIECE)
        return c

    lax.fori_loop(0, total >> (LONG_PIECE.bit_length() - 1), long_wait, 0)
    for bit in SHORT_PIECES:
        @pl.when((total & bit) != 0)
        def _():
            wait(bit)


def _dispatch_kernel(seg_n_ref, seg_row_ref, table_ref, routet_ref, u2_ref, xs_ref, sorted_scr, sems):
    tile = pl.program_id(0)
    n_tiles = pl.num_programs(0)
    slot = tile % 2
    tt = u2_ref.shape[0]
    rows = SORTED_ROWS

    def wait_copies(tile_, slot_):
        _wait_tile_copies(seg_n_ref, tile_, sorted_scr.at[slot_], xs_ref, sems.at[slot_], to_hbm=True)

    @pl.when(tile == 0)
    def _():
        sorted_scr[1, 0:LONG_PIECE * ROW_WORDS, :] = jnp.zeros((LONG_PIECE * ROW_WORDS, LANES), jnp.uint32)
        for start in (True, False):
            _segment_copies(seg_n_ref, seg_row_ref, n_tiles, sorted_scr.at[1], xs_ref, sems.at[1],
                            to_hbm=True, start=start, consecutive=False)

    pos_t = routet_ref[0]
    j = lax.broadcasted_iota(jnp.int32, (rows, tt), 0).astype(F32)
    onehot = jnp.zeros((rows, tt), F32)
    for k in range(TOP_K):
        onehot = jnp.where(j == pos_t[TOP_K + k:TOP_K + k + 1, :], 1.0, onehot)
    _to_packed_rows(sorted_scr.at[slot], _dot(onehot.astype(BF16), u2_ref[...]))
    _start_listed_copies(table_ref, sorted_scr.at[slot], xs_ref, sems.at[slot], to_hbm=True)

    @pl.when(tile > 0)
    def _():
        wait_copies(tile - 1, 1 - slot)

    @pl.when(tile == n_tiles - 1)
    def _():
        wait_copies(tile, slot)


def _dispatch(seg_n, seg_row, table, route_t, u2, n_rows, tt=MERGE_ROWS):
    t = u2.shape[0]
    grid_spec = pltpu.PrefetchScalarGridSpec(
        num_scalar_prefetch=2,
        grid=(t // tt,),
        in_specs=[pl.BlockSpec((1, 1, PIECE_TABLE), lambda i, *_: (i, 0, 0), memory_space=pltpu.SMEM),
                  pl.BlockSpec((1, 8, tt), lambda i, *_: (i, 0, 0)),
                  pl.BlockSpec((tt, D_MODEL), lambda i, *_: (i, 0))],
        out_specs=pl.BlockSpec(memory_space=pl.ANY),
        scratch_shapes=[pltpu.VMEM((2, SORTED_ROWS * ROW_WORDS, LANES), jnp.uint32),
                        pltpu.SemaphoreType.DMA((2,))],
    )
    return pl.pallas_call(
        _dispatch_kernel,
        grid_spec=grid_spec,
        out_shape=jax.ShapeDtypeStruct((n_rows * ROW_WORDS, LANES), jnp.uint32),
        compiler_params=pltpu.CompilerParams(
            dimension_semantics=("arbitrary",), has_side_effects=True, vmem_limit_bytes=VMEM_LIMIT),
        name="dispatch",
    )(seg_n, seg_row, table, route_t, u2)


def _expert_kernel(be_ref, nb_ref, xs_ref, w1_ref, b1_ref, w2_ref, b2_ref, ys_ref):
    del be_ref
    i = pl.program_id(0)

    @pl.when(i < nb_ref[0])
    def _():
        hdn = _dot(_from_packed_rows(xs_ref).astype(BF16), w1_ref[0]) + b1_ref[0]
        glu = jnp.minimum(hdn[:, :D_FF], SWIGLU_LIMIT)
        lin = jnp.clip(hdn[:, D_FF:], -SWIGLU_LIMIT, SWIGLU_LIMIT)
        act = (lin + 1.0) * (glu * jax.nn.sigmoid(SWIGLU_ALPHA * glu))
        y = _dot(act.astype(BF16), w2_ref[0]) + b2_ref[0]
        _to_packed_rows(ys_ref, y.astype(BF16).astype(F32))

    @pl.when(i >= nb_ref[0])
    def _():
        ys_ref[...] = jnp.zeros_like(ys_ref)


def _expert(block_e, nb_used, xs, w1, b1, w2, b2):
    block = (EXPERT_ROWS * ROW_WORDS, LANES)
    n_blocks = xs.shape[0] // block[0]
    grid_spec = pltpu.PrefetchScalarGridSpec(
        num_scalar_prefetch=2,
        grid=(n_blocks,),
        in_specs=[pl.BlockSpec(block, lambda i, be, nb: (jnp.minimum(i, nb[0] - 1), 0)),
                  pl.BlockSpec((1, D_MODEL, 2 * D_FF), lambda i, be, nb: (be[i], 0, 0)),
                  pl.BlockSpec((1, 1, 2 * D_FF), lambda i, be, nb: (be[i], 0, 0)),
                  pl.BlockSpec((1, D_FF, D_MODEL), lambda i, be, nb: (be[i], 0, 0)),
                  pl.BlockSpec((1, 1, D_MODEL), lambda i, be, nb: (be[i], 0, 0))],
        out_specs=pl.BlockSpec(block, lambda i, be, nb: (i, 0)),
    )
    return pl.pallas_call(
        _expert_kernel,
        grid_spec=grid_spec,
        out_shape=jax.ShapeDtypeStruct(xs.shape, xs.dtype),
        compiler_params=pltpu.CompilerParams(
            dimension_semantics=("arbitrary",), vmem_limit_bytes=VMEM_LIMIT),
        name="expert",
    )(block_e, nb_used, xs, w1, b1, w2, b2)


def _combine_kernel(seg_n_ref, table_ref, next_table_ref, route_ref, h_ref, mod_ref, ys_ref, o_ref,
                    sorted_scr, sems):
    tile = pl.program_id(0)
    n_tiles = pl.num_programs(0)
    slot = tile % 2
    tt = h_ref.shape[0]
    rows = SORTED_ROWS

    @pl.when(tile == 0)
    def _():
        sorted_scr[...] = jnp.zeros_like(sorted_scr)
        _start_listed_copies(table_ref, sorted_scr.at[slot], ys_ref, sems.at[slot], to_hbm=False)

    @pl.when(tile + 1 < n_tiles)
    def _():
        _start_listed_copies(next_table_ref, sorted_scr.at[1 - slot], ys_ref, sems.at[1 - slot], to_hbm=False)

    route = route_ref[...]
    j = lax.broadcasted_iota(jnp.int32, (tt, rows), 1).astype(F32)
    wmat = jnp.zeros((tt, rows), F32)
    for k in range(TOP_K):
        wmat = jnp.where(j == route[:, TOP_K + k:TOP_K + k + 1], route[:, 2 * TOP_K + k:2 * TOP_K + k + 1], wmat)
    _wait_tile_copies(seg_n_ref, tile, sorted_scr.at[slot], ys_ref, sems.at[slot], to_hbm=False)
    y = _from_packed_rows(sorted_scr.at[slot]).astype(BF16)
    o_ref[...] = h_ref[...] + mod_ref[0][5:6] * _dot(wmat.astype(BF16), y)


def _combine(seg_n, table, route, h, mod, ys, seq, tt=MERGE_ROWS):
    t = h.shape[0]
    n_tiles = t // tt
    per_seq = seq // tt
    table_spec = lambda ahead: pl.BlockSpec(
        (1, 1, PIECE_TABLE), lambda i, *_: (jnp.minimum(i + ahead, n_tiles - 1), 0, 0), memory_space=pltpu.SMEM)
    grid_spec = pltpu.PrefetchScalarGridSpec(
        num_scalar_prefetch=1,
        grid=(n_tiles,),
        in_specs=[table_spec(0), table_spec(1),
                  pl.BlockSpec((tt, LANES), lambda i, *_: (i, 0)),
                  pl.BlockSpec((tt, D_MODEL), lambda i, *_: (i, 0)),
                  pl.BlockSpec((1, 6, D_MODEL), lambda i, *_: (i // per_seq, 0, 0)),
                  pl.BlockSpec(memory_space=pl.ANY)],
        out_specs=pl.BlockSpec((tt, D_MODEL), lambda i, *_: (i, 0)),
        scratch_shapes=[pltpu.VMEM((2, SORTED_ROWS * ROW_WORDS, LANES), jnp.uint32),
                        pltpu.SemaphoreType.DMA((2,))],
    )
    return pl.pallas_call(
        _combine_kernel,
        grid_spec=grid_spec,
        out_shape=jax.ShapeDtypeStruct((t, D_MODEL), F32),
        compiler_params=pltpu.CompilerParams(
            dimension_semantics=("arbitrary",), vmem_limit_bytes=VMEM_LIMIT),
        name="combine",
    )(seg_n, table, table, route, h, mod, ys)


def _rope_tables(seq, ts):
    half = ROPE_DIM // 2
    inv_freq = jnp.power(jnp.float32(ROPE_THETA), -jnp.arange(half, dtype=F32) * 2.0 / ROPE_DIM)
    ang = jnp.arange(seq, dtype=F32)[:, None] * inv_freq[None, :]
    cos, sin = jnp.cos(ang), jnp.sin(ang)
    ones = jnp.ones((seq, HEAD_DIM - ROPE_DIM), F32)
    zeros = jnp.zeros_like(ones)
    zh = jnp.zeros((seq, half), F32)
    cos_t = jnp.concatenate([cos, cos, ones], axis=1)
    sin_prev = jnp.concatenate([zh, sin, zeros], axis=1)
    sin_next = jnp.concatenate([-sin, zh, zeros], axis=1)
    tables = jnp.stack([jnp.tile(t, (1, LANES // HEAD_DIM)) for t in (cos_t, sin_prev, sin_next)])
    per_group = []
    for d in DILATIONS:
        pos = jnp.arange(seq).reshape(seq // ts, ts // d, d).transpose(0, 2, 1).reshape(seq)
        per_group.append(tables[:, pos, :])
    return jnp.stack(per_group)


def _dft_tables(seq):
    k = jnp.arange(seq, dtype=jnp.int32)
    ang = ((k[:, None] * k[None, :]) % seq).astype(F32) * (2.0 * jnp.pi / seq)
    scale = seq ** -0.5
    cs = jnp.cos(ang) * scale
    ns = -jnp.sin(ang) * scale
    c = jnp.arange(FOURIER_GROUP_DIM, dtype=jnp.int32)
    angc = ((c[:, None] * c[None, :]) % FOURIER_GROUP_DIM).astype(F32) * (2.0 * jnp.pi / FOURIER_GROUP_DIM)
    eye = jnp.eye(ATTN_WIDTH // FOURIER_GROUP_DIM, dtype=F32)
    cscale = FOURIER_GROUP_DIM ** -0.5
    bd = jnp.concatenate([jnp.kron(eye, jnp.cos(angc) * cscale),
                          jnp.kron(eye, jnp.sin(angc) * cscale)], axis=1)
    return bd.astype(BF16), cs.astype(BF16), ns.astype(BF16)


def _trunk(x, c, p):
    b, s, _ = x.shape
    t = b * s
    mod = _ada(c, p["w_ada"], p["b_ada"]).reshape(b, 6, D_MODEL)
    *qkvs, fz = _inproj(x, mod, p["g1"], p["w_qkvf"], p["gains"], p["rope"], p["gmat"])
    os_, lses = zip(*[_attn(qkv, g) for g, qkv in enumerate(qkvs)])
    fm = _fourier(fz, p["bd"], p["cs"], p["ns"])
    h, u2, route, route_t, tcnt, cnt = _merge(
        x, mod, os_, lses, fm, p["g1"], p["g2"], p["wg"], p["wap"], p["wfp"], p["wout"], p["wrh"],
        p["wrl"], p["br"], p["expand"], p["ltri"], p["etri"], p["perms"])

    rows = EXPERT_ROWS
    counts = cnt[0, :N_EXPERTS].astype(jnp.int32)
    padded = (counts + rows - 1) // rows * rows
    pend = jnp.cumsum(padded)
    pstart = pend - padded
    seg_row = jnp.concatenate([pstart[None, :] + tcnt[:, 0, :N_EXPERTS], (pstart + counts)[None, :]]).reshape(-1)
    seg_n = jnp.concatenate([tcnt[:, 1, :N_EXPERTS], (padded - counts)[None, :]]).reshape(-1)
    n_blocks = -(-(t * TOP_K + N_EXPERTS * (t // MERGE_ROWS + rows - 1)) // rows)
    block_start = jnp.arange(n_blocks, dtype=jnp.int32) * rows
    block_e = jnp.minimum(jnp.sum((pend[None, :] <= block_start[:, None]).astype(jnp.int32), axis=-1),
                          N_EXPERTS - 1)
    nb_used = (pend[-1:] // rows).astype(jnp.int32)

    n_tile = tcnt[:, 1, :N_EXPERTS]
    table = _piece_tables(n_tile, jnp.cumsum(n_tile, axis=1) - n_tile, pstart[None, :] + tcnt[:, 0, :N_EXPERTS])
    xs = _dispatch(seg_n, seg_row, table, route_t, u2.reshape(t, D_MODEL), n_blocks * rows)
    ys = _expert(block_e, nb_used, xs, p["w1"], p["b1"], p["w2"], p["b2"])
    out = _combine(seg_n, table, route.reshape(t, LANES), h.reshape(t, D_MODEL), mod, ys, s)
    return out.reshape(b, s, D_MODEL)


def kernel(x_prompt, x_sample, c_prompt, c_sample, w_ada, b_ada, norm1_g, norm2_g, w_in, q_gain, k_gain,
           w_attn_proj, w_fourier_proj, w_out, w_router, b_router, w1, b1, w2, b2):
    seq = x_prompt.shape[1]
    assert w_ada.shape[0] == 1, "single-layer trunk"
    w_in0 = w_in[0]
    gains = jnp.zeros((8, ATTN_WIDTH), F32)
    for g in range(N_GROUPS):
        gains = gains.at[2 * g].set(jnp.tile(q_gain[0, g], HEADS))
        gains = gains.at[2 * g + 1].set(jnp.tile(k_gain[0, g], HEADS))
    heads = jnp.arange(ATTN_WIDTH) // HEAD_DIM
    gmat = ((heads[:, None] == heads[None, :]).astype(F32) / HEAD_DIM).astype(BF16)
    expand = (jnp.arange(LANES)[:, None] == heads[None, :]).astype(BF16)
    ltri = (jnp.arange(MERGE_ROWS)[:, None] > jnp.arange(MERGE_ROWS)[None, :]).astype(BF16)
    etri = (jnp.arange(LANES)[:, None] > jnp.arange(LANES)[None, :]).astype(BF16)
    perms = []
    for d in DILATIONS[1:]:
        n = MERGE_ROWS // d
        src = (jnp.arange(MERGE_ROWS) % d) * n + jnp.arange(MERGE_ROWS) // d
        perms.append((src[:, None] == jnp.arange(MERGE_ROWS)[None, :]).astype(BF16))
    wr = jnp.zeros((LANES, D_MODEL), F32).at[:N_EXPERTS].set(w_router[0].T)
    wrh = wr.astype(BF16)
    wrl = (wr - wrh.astype(F32)).astype(BF16)
    br = jnp.zeros((LANES, MERGE_ROWS), F32).at[:N_EXPERTS].set(b_router[0][:, None])
    bd, cs, ns = _dft_tables(seq)
    p = dict(
        w_ada=w_ada[0], b_ada=b_ada[0], g1=norm1_g, g2=norm2_g,
        w_qkvf=w_in0[:, :QKVF_WIDTH].astype(BF16), wg=w_in0[:, QKVF_WIDTH:].astype(BF16),
        gains=gains, rope=_rope_tables(seq, INPROJ_ROWS), gmat=gmat, bd=bd, cs=cs, ns=ns,
        wap=w_attn_proj[0].astype(BF16), wfp=w_fourier_proj[0].astype(BF16), wout=w_out[0].astype(BF16),
        wrh=wrh, wrl=wrl, br=br, expand=expand, ltri=ltri, etri=etri, perms=perms,
        w1=w1[0].astype(BF16), b1=b1[0].reshape(N_EXPERTS, 1, 2 * D_FF),
        w2=w2[0].astype(BF16), b2=b2[0].reshape(N_EXPERTS, 1, D_MODEL),
    )
    return _trunk(x_prompt, c_prompt, p), _trunk(x_sample, c_sample, p)
```

```python
import functools

import jax
import jax.numpy as jnp
from jax import lax
from jax.experimental import pallas as pl
from jax.experimental.pallas import tpu as pltpu

F32 = jnp.float32
BF16 = jnp.bfloat16

D_MODEL = 1024
HEAD_DIM = 64
HEADS = 8
ATTN_WIDTH = HEADS * HEAD_DIM
DILATIONS = (1, 4, 16)
BAND = 64
N_GROUPS = 3
QKVF_WIDTH = 5120
FOURIER_GROUP_DIM = 64
ROPE_DIM = 16
ROPE_THETA = 500000.0
N_EXPERTS = 32
TOP_K = 4
D_FF = 1024
SWIGLU_ALPHA = 1.702
SWIGLU_LIMIT = 7.0
NORM_EPS = 1e-6
NEG_INF = -1e30
LN2 = 0.6931471805599453
LOG2E = 1.4426950408889634

LANES = 128
ROW_WORDS = D_MODEL // LANES // 2
SORTED_ROWS = 1088
EXPERT_ROWS = 512
LONG_PIECE = 64
SHORT_PIECES = (32, 16, 8, 4, 2)
INPROJ_ROWS = 512
MERGE_ROWS = 256
MERGE_TILES_PER_STEP = 4
ATTN_BLOCKS_PER_STEP = 8
ATTN_STAGE_SKEW = 0
VMEM_LIMIT = 56 * 1024 * 1024


def _dot(a, b):
    return jnp.dot(a, b, preferred_element_type=F32)


def _split_bf16(v):
    hi = v.astype(BF16)
    lo = (v - hi.astype(F32)).astype(BF16)
    return hi, lo


def _rms_mod(x, gain, scale, shift):
    ms = jnp.mean(x * x, axis=-1, keepdims=True)
    return x * lax.rsqrt(ms + NORM_EPS) * gain * (1.0 + scale) + shift


def _run_in_lockstep(stage_generators, skew=0):
    live = dict(enumerate(stage_generators))
    step = 0
    while live:
        for j in [j for j in live if step >= j * skew]:
            if next(live[j], "done") == "done":
                del live[j]
        step += 1


def _const_spec(shape):
    return pl.BlockSpec(shape, lambda *_: (0,) * len(shape), pipeline_mode=pl.Buffered(1))


def _ada_kernel(c_ref, w_ref, b_ref, o_ref):
    c = c_ref[...]
    a = c * jax.nn.sigmoid(c)
    a_hi, a_lo = _split_bf16(a)
    w_hi, w_lo = _split_bf16(w_ref[...])
    o_ref[...] = _dot(a_hi, w_hi) + _dot(a_lo, w_hi) + _dot(a_hi, w_lo) + b_ref[...]


def _ada(c, w_ada, b_ada):
    b = c.shape[0]
    n = w_ada.shape[1]
    tn = D_MODEL
    return pl.pallas_call(
        _ada_kernel,
        grid=(n // tn,),
        in_specs=[pl.BlockSpec((b, D_MODEL), lambda j: (0, 0)),
                  pl.BlockSpec((D_MODEL, tn), lambda j: (0, j)),
                  pl.BlockSpec((1, tn), lambda j: (0, j))],
        out_specs=pl.BlockSpec((b, tn), lambda j: (0, j)),
        out_shape=jax.ShapeDtypeStruct((b, n), F32),
        compiler_params=pltpu.CompilerParams(vmem_limit_bytes=VMEM_LIMIT),
        name="ada",
    )(c, w_ada, b_ada.reshape(1, n))


def _inproj_kernel(x_ref, mod_ref, g1_ref, w_ref, gain_ref, rope_ref, gmat_ref,
                   o0_ref, o1_ref, o2_ref, of_ref, u_scr):
    x = x_ref[0]
    ts = x.shape[0]
    mod = mod_ref[0]
    u = _rms_mod(x, g1_ref[...], mod[1:2], mod[0:1])
    ub = u.astype(BF16)
    for c in range(D_MODEL // LANES):
        u_scr[c] = u[:, c * LANES:(c + 1) * LANES]
    for g, o_ref in enumerate((o0_ref, o1_ref, o2_ref)):
        d = DILATIONS[g]
        n = ts // d
        if d == 1:
            ug = ub
        else:
            ug = jnp.concatenate(
                [jnp.concatenate([u_scr[c, pl.ds(r, n, stride=d), :] for r in range(d)], axis=0)
                 for c in range(D_MODEL // LANES)], axis=1).astype(BF16)
        cos_t, sin_prev, sin_next = rope_ref[g, 0], rope_ref[g, 1], rope_ref[g, 2]
        for j in range(3):
            c0 = (3 * g + j) * ATTN_WIDTH
            z = _dot(ug, w_ref[:, c0:c0 + ATTN_WIDTH])
            if j < 2:
                ms = _dot((z * z).astype(BF16), gmat_ref[...])
                gain = gain_ref[2 * g + j:2 * g + j + 1, :]
                if j == 0:
                    gain = gain * (HEAD_DIM ** -0.5 * LOG2E)
                z = z * lax.rsqrt(ms + NORM_EPS) * gain
                blocks = []
                for blk in range(ATTN_WIDTH // LANES):
                    zb = z[:, blk * LANES:(blk + 1) * LANES]
                    blocks.append(zb * cos_t
                                  + pltpu.roll(zb, ROPE_DIM // 2, 1) * sin_prev
                                  + pltpu.roll(zb, LANES - ROPE_DIM // 2, 1) * sin_next)
                z = jnp.concatenate(blocks, axis=1)
            zb16 = z.astype(BF16)
            for r in range(d):
                o_ref[0, r, :, j * ATTN_WIDTH:(j + 1) * ATTN_WIDTH] = zb16[r * n:(r + 1) * n, :]
    of_ref[0] = _dot(ub, w_ref[:, 9 * ATTN_WIDTH:]).astype(BF16)


def _inproj(x, mod, g1, w_qkvf, gains, rope, gmat, ts=INPROJ_ROWS):
    b, s, _ = x.shape
    qkv_w = 3 * ATTN_WIDTH
    out_specs = [pl.BlockSpec((1, d, ts // d, qkv_w), lambda i, j: (i, 0, j, 0)) for d in DILATIONS]
    out_shape = [jax.ShapeDtypeStruct((b, d, s // d, qkv_w), BF16) for d in DILATIONS]
    return pl.pallas_call(
        _inproj_kernel,
        grid=(b, s // ts),
        in_specs=[pl.BlockSpec((1, ts, D_MODEL), lambda i, j: (i, j, 0)),
                  pl.BlockSpec((1, 6, D_MODEL), lambda i, j: (i, 0, 0)),
                  _const_spec((1, D_MODEL)),
                  _const_spec((D_MODEL, QKVF_WIDTH)),
                  _const_spec((8, ATTN_WIDTH)),
                  pl.BlockSpec((N_GROUPS, 3, ts, LANES), lambda i, j: (0, 0, j, 0)),
                  _const_spec((ATTN_WIDTH, ATTN_WIDTH))],
        out_specs=out_specs + [pl.BlockSpec((1, ts, ATTN_WIDTH), lambda i, j: (i, j, 0))],
        out_shape=out_shape + [jax.ShapeDtypeStruct((b, s, ATTN_WIDTH), BF16)],
        scratch_shapes=[pltpu.VMEM((D_MODEL // LANES, ts, LANES), F32)],
        compiler_params=pltpu.CompilerParams(
            dimension_semantics=("arbitrary", "arbitrary"), vmem_limit_bytes=VMEM_LIMIT),
        name="inproj",
    )(x, mod, g1, w_qkvf, gains, rope, gmat)


def _attn_kernel(qkv_ref, o_ref, lse_ref, s_scr, p_scr, *, length):
    tq = LANES
    tk = min(2 * LANES, length)
    nq = length // tq
    n_pairs = ATTN_WIDTH // LANES
    lane = lax.broadcasted_iota(jnp.int32, (tq, LANES), 1)
    first_head = lane < HEAD_DIM
    first_head_k = lax.broadcasted_iota(jnp.int32, (tk, LANES), 1) < HEAD_DIM
    rel = (lax.broadcasted_iota(jnp.int32, (tq, tk), 1)
           - lax.broadcasted_iota(jnp.int32, (tq, tk), 0))

    def head_pair(blk, it, p, tiles):
        r = it // nq
        i = it % nq
        q0 = pl.multiple_of(i * tq, tq)
        ws = pl.multiple_of(jnp.clip(i * tq - BAND, 0, length - tk), BAND)
        valid = jnp.abs(rel + (ws - i * tq)) <= BAND
        q2 = qkv_ref[0, r, pl.ds(q0, tq), p * LANES:(p + 1) * LANES]
        k2 = qkv_ref[0, r, pl.ds(ws, tk), ATTN_WIDTH + p * LANES:ATTN_WIDTH + (p + 1) * LANES]
        for hh in range(2):
            sel = first_head if hh == 0 else jnp.logical_not(first_head)
            qa = jnp.where(sel, q2, jnp.zeros_like(q2))
            s = lax.dot_general(qa, k2, (((1,), (1,)), ((), ())), preferred_element_type=F32)
            s_scr[blk, 2 * p + hh] = jnp.where(valid, s, NEG_INF)
        yield

        for hh in range(2):
            s = s_scr[blk, 2 * p + hh]
            m = jnp.max(s, axis=-1, keepdims=True)
            p_scr[blk, 2 * p + hh] = jnp.exp2(s - m).astype(BF16)
            tiles["m"] = jnp.where(lane == 2 * p + hh, m, tiles["m"])
        yield

        v2 = qkv_ref[0, r, pl.ds(ws, tk), 2 * ATTN_WIDTH + p * LANES:2 * ATTN_WIDTH + (p + 1) * LANES]
        one = jnp.ones_like(v2)
        o_a = _dot(p_scr[blk, 2 * p], jnp.where(first_head_k, v2, one))
        o_b = _dot(p_scr[blk, 2 * p + 1], jnp.where(first_head_k, one, v2))
        den_ba = jnp.where(first_head, o_b, o_a)
        den_ab = pltpu.roll(den_ba, HEAD_DIM, 1)
        o_ref[0, r, pl.ds(q0, tq), p * LANES:(p + 1) * LANES] = (
            jnp.where(first_head, o_a, o_b) / den_ab).astype(BF16)
        tiles["den"] = jnp.where(lane == 2 * p, den_ab, tiles["den"])
        tiles["den"] = jnp.where(lane == 2 * p + 1, den_ba, tiles["den"])

    def q_blocks(step, carry):
        its = [step * ATTN_BLOCKS_PER_STEP + blk for blk in range(ATTN_BLOCKS_PER_STEP)]
        tiles = [{"m": jnp.zeros((tq, LANES), F32), "den": jnp.ones((tq, LANES), F32)} for _ in its]
        _run_in_lockstep([head_pair(blk, it, p, tiles[blk]) for p in range(n_pairs)
                          for blk, it in enumerate(its)], skew=ATTN_STAGE_SKEW)
        for it, t in zip(its, tiles):
            rows = pl.ds(pl.multiple_of((it % nq) * tq, tq), tq)
            lse_ref[0, it // nq, rows, :] = (t["m"] + jnp.log2(t["den"])) * LN2
        return carry

    lax.fori_loop(0, qkv_ref.shape[1] * nq // ATTN_BLOCKS_PER_STEP, q_blocks, 0)


def _attn(qkv, g):
    b, d, length, width = qkv.shape
    tk = min(2 * LANES, length)
    return pl.pallas_call(
        functools.partial(_attn_kernel, length=length),
        grid=(b,),
        in_specs=[pl.BlockSpec((1, d, length, width), lambda i: (i, 0, 0, 0))],
        out_specs=[pl.BlockSpec((1, d, length, ATTN_WIDTH), lambda i: (i, 0, 0, 0)),
                   pl.BlockSpec((1, d, length, LANES), lambda i: (i, 0, 0, 0))],
        out_shape=[jax.ShapeDtypeStruct((b, d, length, ATTN_WIDTH), BF16),
                   jax.ShapeDtypeStruct((b, d, length, LANES), F32)],
        scratch_shapes=[pltpu.VMEM((ATTN_BLOCKS_PER_STEP, HEADS, LANES, tk), F32),
                        pltpu.VMEM((ATTN_BLOCKS_PER_STEP, HEADS, LANES, tk), BF16)],
        compiler_params=pltpu.CompilerParams(
            dimension_semantics=("arbitrary",), vmem_limit_bytes=VMEM_LIMIT),
        name=f"attn{g}",
    )(qkv)


def _fourier_kernel(x_ref, bd_ref, cs_ref, ns_ref, o_ref, y_ref, *, seq, rows):
    for r in range(seq // rows):
        rs = slice(r * rows, (r + 1) * rows)
        y_ref[rs, :] = _dot(x_ref[0, rs, :], bd_ref[...]).astype(BF16)
    for r in range(seq // rows):
        rs = slice(r * rows, (r + 1) * rows)
        o = _dot(cs_ref[rs, :], y_ref[:, :ATTN_WIDTH]) + _dot(ns_ref[rs, :], y_ref[:, ATTN_WIDTH:])
        o_ref[0, rs, :] = o.astype(BF16)


def _fourier(qkvf, bd, cs, ns):
    b, s, _ = qkvf.shape
    return pl.pallas_call(
        functools.partial(_fourier_kernel, seq=s, rows=256),
        grid=(b,),
        in_specs=[pl.BlockSpec((1, s, ATTN_WIDTH), lambda i: (i, 0, 0)),
                  _const_spec((ATTN_WIDTH, 2 * ATTN_WIDTH)),
                  _const_spec((s, s)),
                  _const_spec((s, s))],
        out_specs=pl.BlockSpec((1, s, ATTN_WIDTH), lambda i: (i, 0, 0)),
        out_shape=jax.ShapeDtypeStruct((b, s, ATTN_WIDTH), BF16),
        scratch_shapes=[pltpu.VMEM((s, 2 * ATTN_WIDTH), BF16)],
        compiler_params=pltpu.CompilerParams(
            dimension_semantics=("arbitrary",), vmem_limit_bytes=VMEM_LIMIT),
        name="fourier",
    )(qkvf, bd, cs, ns)


def _merge_kernel(x_ref, mod_ref, o0_ref, o1_ref, o2_ref, l0_ref, l1_ref, l2_ref, fm_ref,
                  g1_ref, g2_ref, wg_ref, wap_ref, wfp_ref, wout_ref, wrh_ref, wrl_ref, br_ref,
                  expand_ref, ltri_ref, etri_ref, perm1_ref, perm2_ref,
                  h_ref, u2_ref, route_ref, routet_ref, tcnt_ref, cnt_ref, cnt_scr):
    @pl.when((pl.program_id(0) == 0) & (pl.program_id(1) == 0))
    def _():
        cnt_scr[...] = jnp.zeros_like(cnt_scr)

    refs = (x_ref, mod_ref, o0_ref, o1_ref, o2_ref, l0_ref, l1_ref, l2_ref, fm_ref,
            g1_ref, g2_ref, wg_ref, wap_ref, wfp_ref, wout_ref, wrh_ref, wrl_ref, br_ref,
            expand_ref, ltri_ref, etri_ref, perm1_ref, perm2_ref,
            h_ref, u2_ref, route_ref, routet_ref, tcnt_ref, cnt_ref, cnt_scr)
    _run_in_lockstep([_merge_tile(sub, *refs) for sub in range(x_ref.shape[1] // MERGE_ROWS)])


def _merge_tile(sub, x_ref, mod_ref, o0_ref, o1_ref, o2_ref, l0_ref, l1_ref, l2_ref, fm_ref,
                g1_ref, g2_ref, wg_ref, wap_ref, wfp_ref, wout_ref, wrh_ref, wrl_ref, br_ref,
                expand_ref, ltri_ref, etri_ref, perm1_ref, perm2_ref,
                h_ref, u2_ref, route_ref, routet_ref, tcnt_ref, cnt_ref, cnt_scr):
    ts = MERGE_ROWS
    rs = slice(sub * ts, (sub + 1) * ts)
    x = x_ref[0, rs, :]
    mod = mod_ref[0]
    shift1, scale1, gate1 = mod[0:1], mod[1:2], mod[2:3]
    shift2, scale2 = mod[3:4], mod[4:5]
    u = _rms_mod(x, g1_ref[...], scale1, shift1).astype(BF16)
    yield
    gates = _dot(u, wg_ref[...])
    yield

    os_ = [o0_ref[0, 0, rs, :].astype(F32)]
    lses = [l0_ref[0, 0, rs, :]]
    for d, o_ref, l_ref, perm_ref in ((DILATIONS[1], o1_ref, l1_ref, perm1_ref),
                                      (DILATIONS[2], o2_ref, l2_ref, perm2_ref)):
        perm = perm_ref[...]
        ms = slice(sub * ts // d, (sub + 1) * ts // d)
        os_.append(_dot(perm, o_ref[0, :, ms, :].reshape(ts, ATTN_WIDTH)))
        l_hi, l_lo = _split_bf16(l_ref[0, :, ms, :].reshape(ts, LANES))
        lses.append(_dot(perm, l_hi) + _dot(perm, l_lo))
    yield

    mx = jnp.maximum(jnp.maximum(lses[0], lses[1]), lses[2])
    es = [jnp.exp(l - mx) for l in lses]
    den = es[0] + es[1] + es[2]
    attn = None
    for e, o in zip(es, os_):
        w_hi, w_lo = _split_bf16(e / den)
        wf = _dot(w_hi, expand_ref[...]) + _dot(w_lo, expand_ref[...])
        term = wf * o
        attn = term if attn is None else attn + term
    yield

    a = _dot(attn.astype(BF16), wap_ref[...])
    f = _dot(fm_ref[0, rs, :], wfp_ref[...])
    yield
    merged = jax.nn.sigmoid(gates[:, :D_MODEL]) * a + jax.nn.sigmoid(gates[:, D_MODEL:]) * f
    yield
    h = x + gate1 * _dot(merged.astype(BF16), wout_ref[...])
    h_ref[0, rs, :] = h
    yield

    u2 = _rms_mod(h, g2_ref[...], scale2, shift2)
    u_hi, u_lo = _split_bf16(u2)
    u2_ref[0, rs, :] = u_hi
    yield
    nt = (((1,), (1,)), ((), ()))
    logits = (lax.dot_general(wrh_ref[...], u_hi, nt, preferred_element_type=F32)
              + lax.dot_general(wrh_ref[...], u_lo, nt, preferred_element_type=F32)
              + lax.dot_general(wrl_ref[...], u_hi, nt, preferred_element_type=F32) + br_ref[...])
    erow = lax.broadcasted_iota(jnp.int32, (LANES, ts), 0)
    lg = jnp.where(erow < N_EXPERTS, logits, -jnp.inf)
    vals, ids = [], []
    onehot = jnp.zeros((LANES, ts), F32)
    for _ in range(TOP_K):
        m = jnp.max(lg, axis=0, keepdims=True)
        idx = jnp.min(jnp.where(lg == m, erow, LANES), axis=0, keepdims=True)
        hit = erow == idx
        vals.append(m)
        ids.append(idx)
        onehot = jnp.where(hit, 1.0, onehot)
        lg = jnp.where(hit, -jnp.inf, lg)
    exps = [jnp.exp(v - vals[0]) for v in vals]
    esum = exps[0] + exps[1] + exps[2] + exps[3]
    yield

    n_tile = jnp.sum(onehot, axis=1, keepdims=True)
    n_tile = n_tile + (n_tile - 2.0 * jnp.floor(0.5 * n_tile))
    n_wide = jnp.broadcast_to(n_tile, (LANES, LANES))
    seg_start = _dot(etri_ref[...], n_wide.astype(BF16))[:, 0:1]
    pos_all = lax.dot_general(onehot.astype(BF16), ltri_ref[...], nt, preferred_element_type=F32) + seg_start
    route = jnp.zeros((LANES, ts), F32)
    for k in range(TOP_K):
        pos = jnp.sum(jnp.where(erow == ids[k], pos_all, 0.0), axis=0, keepdims=True)
        route = jnp.where(erow == k, ids[k].astype(F32), route)
        route = jnp.where(erow == TOP_K + k, pos, route)
        route = jnp.where(erow == 2 * TOP_K + k, exps[k] / esum, route)
    route_ref[0, rs, :] = route.T
    routet_ref[sub] = route[0:8, :]
    lane = lax.broadcasted_iota(jnp.int32, (LANES, LANES), 1)
    tcnt = jnp.where(lane == 0, cnt_scr[...], jnp.where(lane == 1, n_wide, 0.0)).T
    tcnt_ref[sub] = tcnt[0:8, :].astype(jnp.int32)
    cnt_scr[...] = cnt_scr[...] + n_tile
    cnt_ref[...] = cnt_scr[...].T[0:8, :]


def _merge(x, mod, os_, lses, fm, g1, g2, wg, wap, wfp, wout, wrh, wrl, br, expand, ltri, etri, perms):
    b, s, _ = x.shape
    ts = MERGE_ROWS * MERGE_TILES_PER_STEP
    tile = lambda w: pl.BlockSpec((1, ts, w), lambda i, j: (i, j, 0))
    strided = lambda w: [pl.BlockSpec((1, d, ts // d, w), lambda i, j: (i, 0, j, 0)) for d in DILATIONS]
    consts = (g1, g2, wg, wap, wfp, wout, wrh, wrl, br, expand, ltri, etri) + tuple(perms)
    per_tile = lambda rows, w: pl.BlockSpec((MERGE_TILES_PER_STEP, rows, w), lambda i, j: (i * per_seq + j, 0, 0))
    n_tiles = b * s // MERGE_ROWS
    in_specs = ([tile(D_MODEL), pl.BlockSpec((1, 6, D_MODEL), lambda i, j: (i, 0, 0))]
                + strided(ATTN_WIDTH) + strided(LANES) + [tile(ATTN_WIDTH)]
                + [_const_spec(a.shape) for a in consts])
    per_seq = s // ts
    return pl.pallas_call(
        _merge_kernel,
        grid=(b, per_seq),
        in_specs=in_specs,
        out_specs=[tile(D_MODEL), tile(D_MODEL), tile(LANES), per_tile(8, MERGE_ROWS), per_tile(8, LANES),
                   pl.BlockSpec((8, LANES), lambda i, j: (0, 0))],
        out_shape=[jax.ShapeDtypeStruct((b, s, D_MODEL), F32),
                   jax.ShapeDtypeStruct((b, s, D_MODEL), BF16),
                   jax.ShapeDtypeStruct((b, s, LANES), F32),
                   jax.ShapeDtypeStruct((n_tiles, 8, MERGE_ROWS), F32),
                   jax.ShapeDtypeStruct((n_tiles, 8, LANES), jnp.int32),
                   jax.ShapeDtypeStruct((8, LANES), F32)],
        scratch_shapes=[pltpu.VMEM((LANES, LANES), F32)],
        compiler_params=pltpu.CompilerParams(
            dimension_semantics=("arbitrary", "arbitrary"), vmem_limit_bytes=VMEM_LIMIT),
        name="merge",
    )(x, mod, *os_, *lses, fm, *consts)


def _to_packed_rows(ref, mat):
    bits = lax.bitcast_convert_type(mat, jnp.uint32)
    for c in range(ROW_WORDS):
        lo = bits[:, c * LANES:(c + 1) * LANES] >> 16
        hi = bits[:, (c + ROW_WORDS) * LANES:(c + ROW_WORDS + 1) * LANES] & jnp.uint32(0xFFFF0000)
        ref[pl.ds(c, mat.shape[0], stride=ROW_WORDS), :] = lo | hi


def _from_packed_rows(ref):
    n = ref.shape[0] // ROW_WORDS
    words = [ref[pl.ds(c, n, stride=ROW_WORDS), :] for c in range(ROW_WORDS)]
    lo = [lax.bitcast_convert_type(w << 16, F32) for w in words]
    hi = [lax.bitcast_convert_type(w & jnp.uint32(0xFFFF0000), F32) for w in words]
    return jnp.concatenate(lo + hi, axis=1)


def _segment_copies(seg_n_ref, seg_row_ref, tile, sorted_ref, hbm_ref, sem, *, to_hbm, start, consecutive=True):
    base = tile * N_EXPERTS

    def copy(src, dst, off, size):
        vm_row = (src + off) if consecutive else 0
        vm = sorted_ref.at[pl.ds(pl.multiple_of(vm_row * ROW_WORDS, 2 * ROW_WORDS), size * ROW_WORDS)]
        hb = hbm_ref.at[pl.ds(pl.multiple_of((dst + off) * ROW_WORDS, 2 * ROW_WORDS), size * ROW_WORDS)]
        cp = pltpu.make_async_copy(vm, hb, sem) if to_hbm else pltpu.make_async_copy(hb, vm, sem)
        if start:
            cp.start()
        else:
            cp.wait()

    def segment(e, src):
        n = seg_n_ref[base + e]
        dst = seg_row_ref[base + e]
        n_long = n >> (LONG_PIECE.bit_length() - 1)

        def long_piece(i, c):
            copy(src, dst, i * LONG_PIECE, LONG_PIECE)
            return c

        lax.fori_loop(0, n_long, long_piece, 0)
        for bit in SHORT_PIECES:
            @pl.when((n & bit) != 0)
            def _():
                copy(src, dst, n & (-2 * bit), bit)
        return src + n

    lax.fori_loop(0, N_EXPERTS, segment, 0)


PIECE_SIZES = (LONG_PIECE,) + SHORT_PIECES
PIECE_SLOTS = 32
PIECE_TABLE = 512


def _piece_tables(n, src0, dst0):
    slots = jnp.arange(PIECE_SLOTS, dtype=jnp.int32)

    def compact(mask, src, dst):
        rank = jnp.cumsum(mask.astype(jnp.int32), axis=1) - 1
        sel = mask[:, :, None] & (rank[:, :, None] == slots)
        pick = lambda v: jnp.sum(jnp.where(sel, v[:, :, None], 0), axis=1)
        return jnp.concatenate([pick(src), pick(dst)], axis=1), jnp.sum(mask.astype(jnp.int32), axis=1)

    n_long = n >> (LONG_PIECE.bit_length() - 1)
    j = jnp.arange(MERGE_ROWS // LONG_PIECE, dtype=jnp.int32)
    offs = (j * LONG_PIECE)[None, :, None]
    flat = lambda v: v.reshape(v.shape[0], -1)
    lists, counts = [], []
    lst, cnt = compact(flat(n_long[:, None, :] > j[None, :, None]), flat(src0[:, None, :] + offs),
                       flat(dst0[:, None, :] + offs))
    lists.append(lst)
    counts.append(cnt)
    for bit in SHORT_PIECES:
        off = n & (-2 * bit)
        lst, cnt = compact((n & bit) != 0, src0 + off, dst0 + off)
        lists.append(lst)
        counts.append(cnt)
    head = jnp.stack(counts, axis=1)
    head = jnp.pad(head, ((0, 0), (0, 2 * PIECE_SLOTS - len(PIECE_SIZES))))
    table = jnp.concatenate([head] + lists, axis=1)
    table = jnp.pad(table, ((0, 0), (0, PIECE_TABLE - table.shape[1])))
    return table.reshape(-1, 1, PIECE_TABLE)


def _start_listed_copies(table_ref, sorted_ref, hbm_ref, sem, *, to_hbm):
    for c, size in enumerate(PIECE_SIZES):
        base = 2 * PIECE_SLOTS * (c + 1)

        def piece(i, carry):
            src = table_ref[0, 0, base + i]
            dst = table_ref[0, 0, base + PIECE_SLOTS + i]
            vm = sorted_ref.at[pl.ds(pl.multiple_of(src * ROW_WORDS, 2 * ROW_WORDS), size * ROW_WORDS)]
            hb = hbm_ref.at[pl.ds(pl.multiple_of(dst * ROW_WORDS, 2 * ROW_WORDS), size * ROW_WORDS)]
            (pltpu.make_async_copy(vm, hb, sem) if to_hbm else pltpu.make_async_copy(hb, vm, sem)).start()
            return carry

        lax.fori_loop(0, table_ref[0, 0, c], piece, 0)


def _wait_tile_copies(seg_n_ref, tile, sorted_ref, hbm_ref, sem, *, to_hbm):
    base = tile * N_EXPERTS
    total = lax.fori_loop(0, N_EXPERTS, lambda e, acc: acc + seg_n_ref[base + e], 0)

    def wait(size):
        vm = sorted_ref.at[pl.ds(0, size * ROW_WORDS)]
        hb = hbm_ref.at[pl.ds(0, size * ROW_WORDS)]
        (pltpu.make_async_copy(vm, hb, sem) if to_hbm else pltpu.make_async_copy(hb, vm, sem)).wait()

    def long_wait(i, c):
        wait(LONG_PIECE)
        return c

    lax.fori_loop(0, total >> (LONG_PIECE.bit_length() - 1), long_wait, 0)
    for bit in SHORT_PIECES:
        @pl.when((total & bit) != 0)
        def _():
            wait(bit)


def _dispatch_kernel(seg_n_ref, seg_row_ref, table_ref, routet_ref, u2_ref, xs_ref, sorted_scr, sems):
    tile = pl.program_id(0)
    n_tiles = pl.num_programs(0)
    slot = tile % 2
    tt = u2_ref.shape[0]
    rows = SORTED_ROWS

    def wait_copies(tile_, slot_):
        _wait_tile_copies(seg_n_ref, tile_, sorted_scr.at[slot_], xs_ref, sems.at[slot_], to_hbm=True)

    @pl.when(tile == 0)
    def _():
        sorted_scr[1, 0:LONG_PIECE * ROW_WORDS, :] = jnp.zeros((LONG_PIECE * ROW_WORDS, LANES), jnp.uint32)
        for start in (True, False):
            _segment_copies(seg_n_ref, seg_row_ref, n_tiles, sorted_scr.at[1], xs_ref, sems.at[1],
                            to_hbm=True, start=start, consecutive=False)

    pos_t = routet_ref[0].astype(jnp.int32).astype(jnp.int16)
    j = lax.broadcasted_iota(jnp.int32, (rows, tt), 0).astype(jnp.int16)
    onehot = jnp.zeros((rows, tt), BF16)
    for k in range(TOP_K):
        onehot = jnp.where(j == pos_t[TOP_K + k:TOP_K + k + 1, :], jnp.ones_like(onehot), onehot)
    _to_packed_rows(sorted_scr.at[slot], _dot(onehot, u2_ref[...]))
    _start_listed_copies(table_ref, sorted_scr.at[slot], xs_ref, sems.at[slot], to_hbm=True)

    @pl.when(tile > 0)
    def _():
        wait_copies(tile - 1, 1 - slot)

    @pl.when(tile == n_tiles - 1)
    def _():
        wait_copies(tile, slot)


def _dispatch(seg_n, seg_row, table, route_t, u2, n_rows, tt=MERGE_ROWS):
    t = u2.shape[0]
    grid_spec = pltpu.PrefetchScalarGridSpec(
        num_scalar_prefetch=2,
        grid=(t // tt,),
        in_specs=[pl.BlockSpec((1, 1, PIECE_TABLE), lambda i, *_: (i, 0, 0), memory_space=pltpu.SMEM),
                  pl.BlockSpec((1, 8, tt), lambda i, *_: (i, 0, 0)),
                  pl.BlockSpec((tt, D_MODEL), lambda i, *_: (i, 0))],
        out_specs=pl.BlockSpec(memory_space=pl.ANY),
        scratch_shapes=[pltpu.VMEM((2, SORTED_ROWS * ROW_WORDS, LANES), jnp.uint32),
                        pltpu.SemaphoreType.DMA((2,))],
    )
    return pl.pallas_call(
        _dispatch_kernel,
        grid_spec=grid_spec,
        out_shape=jax.ShapeDtypeStruct((n_rows * ROW_WORDS, LANES), jnp.uint32),
        compiler_params=pltpu.CompilerParams(
            dimension_semantics=("arbitrary",), has_side_effects=True, vmem_limit_bytes=VMEM_LIMIT),
        name="dispatch",
    )(seg_n, seg_row, table, route_t, u2)


def _expert_kernel(be_ref, nb_ref, xs_ref, w1_ref, b1_ref, w2_ref, b2_ref, ys_ref):
    del be_ref
    i = pl.program_id(0)

    @pl.when(i < nb_ref[0])
    def _():
        hdn = _dot(_from_packed_rows(xs_ref).astype(BF16), w1_ref[0]) + b1_ref[0]
        glu = jnp.minimum(hdn[:, :D_FF], SWIGLU_LIMIT)
        lin = jnp.clip(hdn[:, D_FF:], -SWIGLU_LIMIT, SWIGLU_LIMIT)
        act = (lin + 1.0) * (glu * jax.nn.sigmoid(SWIGLU_ALPHA * glu))
        y = _dot(act.astype(BF16), w2_ref[0]) + b2_ref[0]
        _to_packed_rows(ys_ref, y.astype(BF16).astype(F32))

    @pl.when(i >= nb_ref[0])
    def _():
        ys_ref[...] = jnp.zeros_like(ys_ref)


def _expert(block_e, nb_used, xs, w1, b1, w2, b2):
    block = (EXPERT_ROWS * ROW_WORDS, LANES)
    n_blocks = xs.shape[0] // block[0]
    grid_spec = pltpu.PrefetchScalarGridSpec(
        num_scalar_prefetch=2,
        grid=(n_blocks,),
        in_specs=[pl.BlockSpec(block, lambda i, be, nb: (jnp.minimum(i, nb[0] - 1), 0)),
                  pl.BlockSpec((1, D_MODEL, 2 * D_FF), lambda i, be, nb: (be[i], 0, 0)),
                  pl.BlockSpec((1, 1, 2 * D_FF), lambda i, be, nb: (be[i], 0, 0)),
                  pl.BlockSpec((1, D_FF, D_MODEL), lambda i, be, nb: (be[i], 0, 0)),
                  pl.BlockSpec((1, 1, D_MODEL), lambda i, be, nb: (be[i], 0, 0))],
        out_specs=pl.BlockSpec(block, lambda i, be, nb: (i, 0)),
    )
    return pl.pallas_call(
        _expert_kernel,
        grid_spec=grid_spec,
        out_shape=jax.ShapeDtypeStruct(xs.shape, xs.dtype),
        compiler_params=pltpu.CompilerParams(
            dimension_semantics=("arbitrary",), vmem_limit_bytes=VMEM_LIMIT),
        name="expert",
    )(block_e, nb_used, xs, w1, b1, w2, b2)


def _combine_kernel(seg_n_ref, table_ref, next_table_ref, route_ref, h_ref, mod_ref, ys_ref, o_ref,
                    sorted_scr, sems):
    tile = pl.program_id(0)
    n_tiles = pl.num_programs(0)
    slot = tile % 2
    tt = h_ref.shape[0]
    rows = SORTED_ROWS

    @pl.when(tile == 0)
    def _():
        sorted_scr[...] = jnp.zeros_like(sorted_scr)
        _start_listed_copies(table_ref, sorted_scr.at[slot], ys_ref, sems.at[slot], to_hbm=False)

    @pl.when(tile + 1 < n_tiles)
    def _():
        _start_listed_copies(next_table_ref, sorted_scr.at[1 - slot], ys_ref, sems.at[1 - slot], to_hbm=False)

    route = route_ref[...]
    pos = route.astype(jnp.int32).astype(jnp.int16)
    weight = route.astype(BF16)
    j = lax.broadcasted_iota(jnp.int32, (tt, rows), 1).astype(jnp.int16)
    wmat = jnp.zeros((tt, rows), BF16)
    for k in range(TOP_K):
        wmat = jnp.where(j == pos[:, TOP_K + k:TOP_K + k + 1],
                         jnp.broadcast_to(weight[:, 2 * TOP_K + k:2 * TOP_K + k + 1], wmat.shape), wmat)
    _wait_tile_copies(seg_n_ref, tile, sorted_scr.at[slot], ys_ref, sems.at[slot], to_hbm=False)
    y = _from_packed_rows(sorted_scr.at[slot]).astype(BF16)
    o_ref[...] = h_ref[...] + mod_ref[0][5:6] * _dot(wmat.astype(BF16), y)


def _combine(seg_n, table, route, h, mod, ys, seq, tt=MERGE_ROWS):
    t = h.shape[0]
    n_tiles = t // tt
    per_seq = seq // tt
    table_spec = lambda ahead: pl.BlockSpec(
        (1, 1, PIECE_TABLE), lambda i, *_: (jnp.minimum(i + ahead, n_tiles - 1), 0, 0), memory_space=pltpu.SMEM)
    grid_spec = pltpu.PrefetchScalarGridSpec(
        num_scalar_prefetch=1,
        grid=(n_tiles,),
        in_specs=[table_spec(0), table_spec(1),
                  pl.BlockSpec((tt, LANES), lambda i, *_: (i, 0)),
                  pl.BlockSpec((tt, D_MODEL), lambda i, *_: (i, 0)),
                  pl.BlockSpec((1, 6, D_MODEL), lambda i, *_: (i // per_seq, 0, 0)),
                  pl.BlockSpec(memory_space=pl.ANY)],
        out_specs=pl.BlockSpec((tt, D_MODEL), lambda i, *_: (i, 0)),
        scratch_shapes=[pltpu.VMEM((2, SORTED_ROWS * ROW_WORDS, LANES), jnp.uint32),
                        pltpu.SemaphoreType.DMA((2,))],
    )
    return pl.pallas_call(
        _combine_kernel,
        grid_spec=grid_spec,
        out_shape=jax.ShapeDtypeStruct((t, D_MODEL), F32),
        compiler_params=pltpu.CompilerParams(
            dimension_semantics=("arbitrary",), vmem_limit_bytes=VMEM_LIMIT),
        name="combine",
    )(seg_n, table, table, route, h, mod, ys)


def _rope_tables(seq, ts):
    half = ROPE_DIM // 2
    inv_freq = jnp.power(jnp.float32(ROPE_THETA), -jnp.arange(half, dtype=F32) * 2.0 / ROPE_DIM)
    ang = jnp.arange(seq, dtype=F32)[:, None] * inv_freq[None, :]
    cos, sin = jnp.cos(ang), jnp.sin(ang)
    ones = jnp.ones((seq, HEAD_DIM - ROPE_DIM), F32)
    zeros = jnp.zeros_like(ones)
    zh = jnp.zeros((seq, half), F32)
    cos_t = jnp.concatenate([cos, cos, ones], axis=1)
    sin_prev = jnp.concatenate([zh, sin, zeros], axis=1)
    sin_next = jnp.concatenate([-sin, zh, zeros], axis=1)
    tables = jnp.stack([jnp.tile(t, (1, LANES // HEAD_DIM)) for t in (cos_t, sin_prev, sin_next)])
    per_group = []
    for d in DILATIONS:
        pos = jnp.arange(seq).reshape(seq // ts, ts // d, d).transpose(0, 2, 1).reshape(seq)
        per_group.append(tables[:, pos, :])
    return jnp.stack(per_group)


def _dft_tables(seq):
    k = jnp.arange(seq, dtype=jnp.int32)
    ang = ((k[:, None] * k[None, :]) % seq).astype(F32) * (2.0 * jnp.pi / seq)
    scale = seq ** -0.5
    cs = jnp.cos(ang) * scale
    ns = -jnp.sin(ang) * scale
    c = jnp.arange(FOURIER_GROUP_DIM, dtype=jnp.int32)
    angc = ((c[:, None] * c[None, :]) % FOURIER_GROUP_DIM).astype(F32) * (2.0 * jnp.pi / FOURIER_GROUP_DIM)
    eye = jnp.eye(ATTN_WIDTH // FOURIER_GROUP_DIM, dtype=F32)
    cscale = FOURIER_GROUP_DIM ** -0.5
    bd = jnp.concatenate([jnp.kron(eye, jnp.cos(angc) * cscale),
                          jnp.kron(eye, jnp.sin(angc) * cscale)], axis=1)
    return bd.astype(BF16), cs.astype(BF16), ns.astype(BF16)


def _trunk(x, c, p):
    b, s, _ = x.shape
    t = b * s
    mod = _ada(c, p["w_ada"], p["b_ada"]).reshape(b, 6, D_MODEL)
    *qkvs, fz = _inproj(x, mod, p["g1"], p["w_qkvf"], p["gains"], p["rope"], p["gmat"])
    os_, lses = zip(*[_attn(qkv, g) for g, qkv in enumerate(qkvs)])
    fm = _fourier(fz, p["bd"], p["cs"], p["ns"])
    h, u2, route, route_t, tcnt, cnt = _merge(
        x, mod, os_, lses, fm, p["g1"], p["g2"], p["wg"], p["wap"], p["wfp"], p["wout"], p["wrh"],
        p["wrl"], p["br"], p["expand"], p["ltri"], p["etri"], p["perms"])

    rows = EXPERT_ROWS
    counts = cnt[0, :N_EXPERTS].astype(jnp.int32)
    padded = (counts + rows - 1) // rows * rows
    pend = jnp.cumsum(padded)
    pstart = pend - padded
    seg_row = jnp.concatenate([pstart[None, :] + tcnt[:, 0, :N_EXPERTS], (pstart + counts)[None, :]]).reshape(-1)
    seg_n = jnp.concatenate([tcnt[:, 1, :N_EXPERTS], (padded - counts)[None, :]]).reshape(-1)
    n_blocks = -(-(t * TOP_K + N_EXPERTS * (t // MERGE_ROWS + rows - 1)) // rows)
    block_start = jnp.arange(n_blocks, dtype=jnp.int32) * rows
    block_e = jnp.minimum(jnp.sum((pend[None, :] <= block_start[:, None]).astype(jnp.int32), axis=-1),
                          N_EXPERTS - 1)
    nb_used = (pend[-1:] // rows).astype(jnp.int32)

    n_tile = tcnt[:, 1, :N_EXPERTS]
    table = _piece_tables(n_tile, jnp.cumsum(n_tile, axis=1) - n_tile, pstart[None, :] + tcnt[:, 0, :N_EXPERTS])
    xs = _dispatch(seg_n, seg_row, table, route_t, u2.reshape(t, D_MODEL), n_blocks * rows)
    ys = _expert(block_e, nb_used, xs, p["w1"], p["b1"], p["w2"], p["b2"])
    out = _combine(seg_n, table, route.reshape(t, LANES), h.reshape(t, D_MODEL), mod, ys, s)
    return out.reshape(b, s, D_MODEL)


def kernel(x_prompt, x_sample, c_prompt, c_sample, w_ada, b_ada, norm1_g, norm2_g, w_in, q_gain, k_gain,
           w_attn_proj, w_fourier_proj, w_out, w_router, b_router, w1, b1, w2, b2):
    seq = x_prompt.shape[1]
    assert w_ada.shape[0] == 1, "single-layer trunk"
    w_in0 = w_in[0]
    gains = jnp.zeros((8, ATTN_WIDTH), F32)
    for g in range(N_GROUPS):
        gains = gains.at[2 * g].set(jnp.tile(q_gain[0, g], HEADS))
        gains = gains.at[2 * g + 1].set(jnp.tile(k_gain[0, g], HEADS))
    heads = jnp.arange(ATTN_WIDTH) // HEAD_DIM
    gmat = ((heads[:, None] == heads[None, :]).astype(F32) / HEAD_DIM).astype(BF16)
    expand = (jnp.arange(LANES)[:, None] == heads[None, :]).astype(BF16)
    ltri = (jnp.arange(MERGE_ROWS)[:, None] > jnp.arange(MERGE_ROWS)[None, :]).astype(BF16)
    etri = (jnp.arange(LANES)[:, None] > jnp.arange(LANES)[None, :]).astype(BF16)
    perms = []
    for d in DILATIONS[1:]:
        n = MERGE_ROWS // d
        src = (jnp.arange(MERGE_ROWS) % d) * n + jnp.arange(MERGE_ROWS) // d
        perms.append((src[:, None] == jnp.arange(MERGE_ROWS)[None, :]).astype(BF16))
    wr = jnp.zeros((LANES, D_MODEL), F32).at[:N_EXPERTS].set(w_router[0].T)
    wrh = wr.astype(BF16)
    wrl = (wr - wrh.astype(F32)).astype(BF16)
    br = jnp.zeros((LANES, MERGE_ROWS), F32).at[:N_EXPERTS].set(b_router[0][:, None])
    bd, cs, ns = _dft_tables(seq)
    p = dict(
        w_ada=w_ada[0], b_ada=b_ada[0], g1=norm1_g, g2=norm2_g,
        w_qkvf=w_in0[:, :QKVF_WIDTH].astype(BF16), wg=w_in0[:, QKVF_WIDTH:].astype(BF16),
        gains=gains, rope=_rope_tables(seq, INPROJ_ROWS), gmat=gmat, bd=bd, cs=cs, ns=ns,
        wap=w_attn_proj[0].astype(BF16), wfp=w_fourier_proj[0].astype(BF16), wout=w_out[0].astype(BF16),
        wrh=wrh, wrl=wrl, br=br, expand=expand, ltri=ltri, etri=etri, perms=perms,
        w1=w1[0].astype(BF16), b1=b1[0].reshape(N_EXPERTS, 1, 2 * D_FF),
        w2=w2[0].astype(BF16), b2=b2[0].reshape(N_EXPERTS, 1, D_MODEL),
    )
    return _trunk(x_prompt, c_prompt, p), _trunk(x_sample, c_sample, p)
```

```python
import functools

import jax
import jax.numpy as jnp
from jax import lax
from jax.experimental import pallas as pl
from jax.experimental.pallas import tpu as pltpu

F32 = jnp.float32
BF16 = jnp.bfloat16

D_MODEL = 1024
HEAD_DIM = 64
HEADS = 8
ATTN_WIDTH = HEADS * HEAD_DIM
DILATIONS = (1, 4, 16)
BAND = 64
N_GROUPS = 3
QKVF_WIDTH = 5120
FOURIER_GROUP_DIM = 64
ROPE_DIM = 16
ROPE_THETA = 500000.0
N_EXPERTS = 32
TOP_K = 4
D_FF = 1024
SWIGLU_ALPHA = 1.702
SWIGLU_LIMIT = 7.0
NORM_EPS = 1e-6
NEG_INF = -1e30
LN2 = 0.6931471805599453
LOG2E = 1.4426950408889634

LANES = 128
ROW_WORDS = D_MODEL // LANES // 2
SORTED_ROWS = 1088
EXPERT_ROWS = 512
LONG_PIECE = 64
SHORT_PIECES = (32, 16, 8, 4, 2)
INPROJ_ROWS = 512
MERGE_ROWS = 256
MERGE_TILES_PER_STEP = 4
ATTN_BLOCKS_PER_STEP = 8
ATTN_STAGE_SKEW = 0
VMEM_LIMIT = 56 * 1024 * 1024


def _dot(a, b):
    return jnp.dot(a, b, preferred_element_type=F32)


def _split_bf16(v):
    hi = v.astype(BF16)
    lo = (v - hi.astype(F32)).astype(BF16)
    return hi, lo


def _rms_mod(x, gain, scale, shift):
    ms = jnp.mean(x * x, axis=-1, keepdims=True)
    return x * lax.rsqrt(ms + NORM_EPS) * gain * (1.0 + scale) + shift


def _run_in_lockstep(stage_generators, skew=0):
    live = dict(enumerate(stage_generators))
    step = 0
    while live:
        for j in [j for j in live if step >= j * skew]:
            if next(live[j], "done") == "done":
                del live[j]
        step += 1


def _const_spec(shape):
    return pl.BlockSpec(shape, lambda *_: (0,) * len(shape), pipeline_mode=pl.Buffered(1))


def _ada_kernel(c_ref, w_ref, b_ref, o_ref):
    c = c_ref[...]
    a = c * jax.nn.sigmoid(c)
    a_hi, a_lo = _split_bf16(a)
    w_hi, w_lo = _split_bf16(w_ref[...])
    o_ref[...] = _dot(a_hi, w_hi) + _dot(a_lo, w_hi) + _dot(a_hi, w_lo) + b_ref[...]


def _ada(c, w_ada, b_ada):
    b = c.shape[0]
    n = w_ada.shape[1]
    tn = D_MODEL
    return pl.pallas_call(
        _ada_kernel,
        grid=(n // tn,),
        in_specs=[pl.BlockSpec((b, D_MODEL), lambda j: (0, 0)),
                  pl.BlockSpec((D_MODEL, tn), lambda j: (0, j)),
                  pl.BlockSpec((1, tn), lambda j: (0, j))],
        out_specs=pl.BlockSpec((b, tn), lambda j: (0, j)),
        out_shape=jax.ShapeDtypeStruct((b, n), F32),
        compiler_params=pltpu.CompilerParams(vmem_limit_bytes=VMEM_LIMIT),
        name="ada",
    )(c, w_ada, b_ada.reshape(1, n))


def _inproj_kernel(x_ref, mod_ref, g1_ref, w_ref, gain_ref, rope_ref, gmat_ref,
                   o0_ref, o1_ref, o2_ref, of_ref, u_scr):
    x = x_ref[0]
    ts = x.shape[0]
    mod = mod_ref[0]
    u = _rms_mod(x, g1_ref[...], mod[1:2], mod[0:1])
    ub = u.astype(BF16)
    for c in range(D_MODEL // LANES):
        u_scr[c] = u[:, c * LANES:(c + 1) * LANES]
    for g, o_ref in enumerate((o0_ref, o1_ref, o2_ref)):
        d = DILATIONS[g]
        n = ts // d
        if d == 1:
            ug = ub
        else:
            ug = jnp.concatenate(
                [jnp.concatenate([u_scr[c, pl.ds(r, n, stride=d), :] for r in range(d)], axis=0)
                 for c in range(D_MODEL // LANES)], axis=1).astype(BF16)
        cos_t, sin_prev, sin_next = rope_ref[g, 0], rope_ref[g, 1], rope_ref[g, 2]
        for j in range(3):
            c0 = (3 * g + j) * ATTN_WIDTH
            z = _dot(ug, w_ref[:, c0:c0 + ATTN_WIDTH])
            if j < 2:
                ms = _dot((z * z).astype(BF16), gmat_ref[...])
                gain = gain_ref[2 * g + j:2 * g + j + 1, :]
                if j == 0:
                    gain = gain * (HEAD_DIM ** -0.5 * LOG2E)
                z = z * lax.rsqrt(ms + NORM_EPS) * gain
                blocks = []
                for blk in range(ATTN_WIDTH // LANES):
                    zb = z[:, blk * LANES:(blk + 1) * LANES]
                    blocks.append(zb * cos_t
                                  + pltpu.roll(zb, ROPE_DIM // 2, 1) * sin_prev
                                  + pltpu.roll(zb, LANES - ROPE_DIM // 2, 1) * sin_next)
                z = jnp.concatenate(blocks, axis=1)
            zb16 = z.astype(BF16)
            for r in range(d):
                o_ref[0, r, :, j * ATTN_WIDTH:(j + 1) * ATTN_WIDTH] = zb16[r * n:(r + 1) * n, :]
    of_ref[0] = _dot(ub, w_ref[:, 9 * ATTN_WIDTH:]).astype(BF16)


def _inproj(x, mod, g1, w_qkvf, gains, rope, gmat, ts=INPROJ_ROWS):
    b, s, _ = x.shape
    qkv_w = 3 * ATTN_WIDTH
    out_specs = [pl.BlockSpec((1, d, ts // d, qkv_w), lambda i, j: (i, 0, j, 0)) for d in DILATIONS]
    out_shape = [jax.ShapeDtypeStruct((b, d, s // d, qkv_w), BF16) for d in DILATIONS]
    return pl.pallas_call(
        _inproj_kernel,
        grid=(b, s // ts),
        in_specs=[pl.BlockSpec((1, ts, D_MODEL), lambda i, j: (i, j, 0)),
                  pl.BlockSpec((1, 6, D_MODEL), lambda i, j: (i, 0, 0)),
                  _const_spec((1, D_MODEL)),
                  _const_spec((D_MODEL, QKVF_WIDTH)),
                  _const_spec((8, ATTN_WIDTH)),
                  pl.BlockSpec((N_GROUPS, 3, ts, LANES), lambda i, j: (0, 0, j, 0)),
                  _const_spec((ATTN_WIDTH, ATTN_WIDTH))],
        out_specs=out_specs + [pl.BlockSpec((1, ts, ATTN_WIDTH), lambda i, j: (i, j, 0))],
        out_shape=out_shape + [jax.ShapeDtypeStruct((b, s, ATTN_WIDTH), BF16)],
        scratch_shapes=[pltpu.VMEM((D_MODEL // LANES, ts, LANES), F32)],
        compiler_params=pltpu.CompilerParams(
            dimension_semantics=("arbitrary", "arbitrary"), vmem_limit_bytes=VMEM_LIMIT),
        name="inproj",
    )(x, mod, g1, w_qkvf, gains, rope, gmat)


def _attn_kernel(qkv_ref, o_ref, lse_ref, s_scr, p_scr, *, length):
    tq = LANES
    tk = min(2 * LANES, length)
    nq = length // tq
    n_pairs = ATTN_WIDTH // LANES
    lane = lax.broadcasted_iota(jnp.int32, (tq, LANES), 1)
    first_head = lane < HEAD_DIM
    first_head_k = lax.broadcasted_iota(jnp.int32, (tk, LANES), 1) < HEAD_DIM
    rel = (lax.broadcasted_iota(jnp.int32, (tq, tk), 1)
           - lax.broadcasted_iota(jnp.int32, (tq, tk), 0))

    def head_pair(blk, it, p, tiles):
        r = it // nq
        i = it % nq
        q0 = pl.multiple_of(i * tq, tq)
        ws = pl.multiple_of(jnp.clip(i * tq - BAND, 0, length - tk), BAND)
        valid = jnp.abs(rel + (ws - i * tq)) <= BAND
        q2 = qkv_ref[0, r, pl.ds(q0, tq), p * LANES:(p + 1) * LANES]
        k2 = qkv_ref[0, r, pl.ds(ws, tk), ATTN_WIDTH + p * LANES:ATTN_WIDTH + (p + 1) * LANES]
        for hh in range(2):
            sel = first_head if hh == 0 else jnp.logical_not(first_head)
            qa = jnp.where(sel, q2, jnp.zeros_like(q2))
            s = lax.dot_general(qa, k2, (((1,), (1,)), ((), ())), preferred_element_type=F32)
            s_scr[blk, 2 * p + hh] = jnp.where(valid, s, NEG_INF)
        yield

        for hh in range(2):
            s = s_scr[blk, 2 * p + hh]
            m = jnp.max(s, axis=-1, keepdims=True)
            p_scr[blk, 2 * p + hh] = jnp.exp2(s - m).astype(BF16)
            tiles["m"] = jnp.where(lane == 2 * p + hh, m, tiles["m"])
        yield

        v2 = qkv_ref[0, r, pl.ds(ws, tk), 2 * ATTN_WIDTH + p * LANES:2 * ATTN_WIDTH + (p + 1) * LANES]
        one = jnp.ones_like(v2)
        o_a = _dot(p_scr[blk, 2 * p], jnp.where(first_head_k, v2, one))
        o_b = _dot(p_scr[blk, 2 * p + 1], jnp.where(first_head_k, one, v2))
        den_ba = jnp.where(first_head, o_b, o_a)
        den_ab = pltpu.roll(den_ba, HEAD_DIM, 1)
        o_ref[0, r, pl.ds(q0, tq), p * LANES:(p + 1) * LANES] = (
            jnp.where(first_head, o_a, o_b) / den_ab).astype(BF16)
        tiles["den"] = jnp.where(lane == 2 * p, den_ab, tiles["den"])
        tiles["den"] = jnp.where(lane == 2 * p + 1, den_ba, tiles["den"])

    def q_blocks(step, carry):
        its = [step * ATTN_BLOCKS_PER_STEP + blk for blk in range(ATTN_BLOCKS_PER_STEP)]
        tiles = [{"m": jnp.zeros((tq, LANES), F32), "den": jnp.ones((tq, LANES), F32)} for _ in its]
        _run_in_lockstep([head_pair(blk, it, p, tiles[blk]) for p in range(n_pairs)
                          for blk, it in enumerate(its)], skew=ATTN_STAGE_SKEW)
        for it, t in zip(its, tiles):
            rows = pl.ds(pl.multiple_of((it % nq) * tq, tq), tq)
            lse_ref[0, it // nq, rows, :] = (t["m"] + jnp.log2(t["den"])) * LN2
        return carry

    lax.fori_loop(0, qkv_ref.shape[1] * nq // ATTN_BLOCKS_PER_STEP, q_blocks, 0)


def _attn(qkv, g):
    b, d, length, width = qkv.shape
    tk = min(2 * LANES, length)
    return pl.pallas_call(
        functools.partial(_attn_kernel, length=length),
        grid=(b,),
        in_specs=[pl.BlockSpec((1, d, length, width), lambda i: (i, 0, 0, 0))],
        out_specs=[pl.BlockSpec((1, d, length, ATTN_WIDTH), lambda i: (i, 0, 0, 0)),
                   pl.BlockSpec((1, d, length, LANES), lambda i: (i, 0, 0, 0))],
        out_shape=[jax.ShapeDtypeStruct((b, d, length, ATTN_WIDTH), BF16),
                   jax.ShapeDtypeStruct((b, d, length, LANES), F32)],
        scratch_shapes=[pltpu.VMEM((ATTN_BLOCKS_PER_STEP, HEADS, LANES, tk), F32),
                        pltpu.VMEM((ATTN_BLOCKS_PER_STEP, HEADS, LANES, tk), BF16)],
        compiler_params=pltpu.CompilerParams(
            dimension_semantics=("arbitrary",), vmem_limit_bytes=VMEM_LIMIT),
        name=f"attn{g}",
    )(qkv)


def _fourier_kernel(x_ref, bd_ref, cs_ref, ns_ref, o_ref, y_ref, *, seq, rows):
    for r in range(seq // rows):
        rs = slice(r * rows, (r + 1) * rows)
        y_ref[rs, :] = _dot(x_ref[0, rs, :], bd_ref[...]).astype(BF16)
    for r in range(seq // rows):
        rs = slice(r * rows, (r + 1) * rows)
        o = _dot(cs_ref[rs, :], y_ref[:, :ATTN_WIDTH]) + _dot(ns_ref[rs, :], y_ref[:, ATTN_WIDTH:])
        o_ref[0, rs, :] = o.astype(BF16)


def _fourier(qkvf, bd, cs, ns):
    b, s, _ = qkvf.shape
    return pl.pallas_call(
        functools.partial(_fourier_kernel, seq=s, rows=256),
        grid=(b,),
        in_specs=[pl.BlockSpec((1, s, ATTN_WIDTH), lambda i: (i, 0, 0)),
                  _const_spec((ATTN_WIDTH, 2 * ATTN_WIDTH)),
                  _const_spec((s, s)),
                  _const_spec((s, s))],
        out_specs=pl.BlockSpec((1, s, ATTN_WIDTH), lambda i: (i, 0, 0)),
        out_shape=jax.ShapeDtypeStruct((b, s, ATTN_WIDTH), BF16),
        scratch_shapes=[pltpu.VMEM((s, 2 * ATTN_WIDTH), BF16)],
        compiler_params=pltpu.CompilerParams(
            dimension_semantics=("arbitrary",), vmem_limit_bytes=VMEM_LIMIT),
        name="fourier",
    )(qkvf, bd, cs, ns)


def _merge_kernel(x_ref, mod_ref, o0_ref, o1_ref, o2_ref, l0_ref, l1_ref, l2_ref, fm_ref,
                  g1_ref, g2_ref, wg_ref, wap_ref, wfp_ref, wout_ref, wrh_ref, wrl_ref, br_ref,
                  expand_ref, ltri_ref, etri_ref, perm1_ref, perm2_ref,
                  h_ref, u2_ref, route_ref, routet_ref, tcnt_ref, cnt_ref, cnt_scr):
    @pl.when((pl.program_id(0) == 0) & (pl.program_id(1) == 0))
    def _():
        cnt_scr[...] = jnp.zeros_like(cnt_scr)

    refs = (x_ref, mod_ref, o0_ref, o1_ref, o2_ref, l0_ref, l1_ref, l2_ref, fm_ref,
            g1_ref, g2_ref, wg_ref, wap_ref, wfp_ref, wout_ref, wrh_ref, wrl_ref, br_ref,
            expand_ref, ltri_ref, etri_ref, perm1_ref, perm2_ref,
            h_ref, u2_ref, route_ref, routet_ref, tcnt_ref, cnt_ref, cnt_scr)
    _run_in_lockstep([_merge_tile(sub, *refs) for sub in range(x_ref.shape[1] // MERGE_ROWS)])


def _merge_tile(sub, x_ref, mod_ref, o0_ref, o1_ref, o2_ref, l0_ref, l1_ref, l2_ref, fm_ref,
                g1_ref, g2_ref, wg_ref, wap_ref, wfp_ref, wout_ref, wrh_ref, wrl_ref, br_ref,
                expand_ref, ltri_ref, etri_ref, perm1_ref, perm2_ref,
                h_ref, u2_ref, route_ref, routet_ref, tcnt_ref, cnt_ref, cnt_scr):
    ts = MERGE_ROWS
    rs = slice(sub * ts, (sub + 1) * ts)
    x = x_ref[0, rs, :]
    mod = mod_ref[0]
    shift1, scale1, gate1 = mod[0:1], mod[1:2], mod[2:3]
    shift2, scale2 = mod[3:4], mod[4:5]
    u = _rms_mod(x, g1_ref[...], scale1, shift1).astype(BF16)
    yield
    gates = _dot(u, wg_ref[...])
    yield

    os_ = [o0_ref[0, 0, rs, :].astype(F32)]
    lses = [l0_ref[0, 0, rs, :]]
    for d, o_ref, l_ref, perm_ref in ((DILATIONS[1], o1_ref, l1_ref, perm1_ref),
                                      (DILATIONS[2], o2_ref, l2_ref, perm2_ref)):
        perm = perm_ref[...]
        ms = slice(sub * ts // d, (sub + 1) * ts // d)
        os_.append(_dot(perm, o_ref[0, :, ms, :].reshape(ts, ATTN_WIDTH)))
        l_hi, l_lo = _split_bf16(l_ref[0, :, ms, :].reshape(ts, LANES))
        lses.append(_dot(perm, l_hi) + _dot(perm, l_lo))
    yield

    mx = jnp.maximum(jnp.maximum(lses[0], lses[1]), lses[2])
    es = [jnp.exp(l - mx) for l in lses]
    den = es[0] + es[1] + es[2]
    attn = None
    for e, o in zip(es, os_):
        w_hi, w_lo = _split_bf16(e / den)
        wf = _dot(w_hi, expand_ref[...]) + _dot(w_lo, expand_ref[...])
        term = wf * o
        attn = term if attn is None else attn + term
    yield

    a = _dot(attn.astype(BF16), wap_ref[...])
    f = _dot(fm_ref[0, rs, :], wfp_ref[...])
    yield
    merged = jax.nn.sigmoid(gates[:, :D_MODEL]) * a + jax.nn.sigmoid(gates[:, D_MODEL:]) * f
    yield
    h = x + gate1 * _dot(merged.astype(BF16), wout_ref[...])
    h_ref[0, rs, :] = h
    yield

    u2 = _rms_mod(h, g2_ref[...], scale2, shift2)
    u_hi, u_lo = _split_bf16(u2)
    u2_ref[0, rs, :] = u_hi
    yield
    nt = (((1,), (1,)), ((), ()))
    logits = (lax.dot_general(wrh_ref[...], u_hi, nt, preferred_element_type=F32)
              + lax.dot_general(wrh_ref[...], u_lo, nt, preferred_element_type=F32)
              + lax.dot_general(wrl_ref[...], u_hi, nt, preferred_element_type=F32) + br_ref[...])
    erow = lax.broadcasted_iota(jnp.int32, (LANES, ts), 0)
    lg = jnp.where(erow < N_EXPERTS, logits, -jnp.inf)
    vals, ids = [], []
    onehot = jnp.zeros((LANES, ts), F32)
    for _ in range(TOP_K):
        m = jnp.max(lg, axis=0, keepdims=True)
        idx = jnp.min(jnp.where(lg == m, erow, LANES), axis=0, keepdims=True)
        hit = erow == idx
        vals.append(m)
        ids.append(idx)
        onehot = jnp.where(hit, 1.0, onehot)
        lg = jnp.where(hit, -jnp.inf, lg)
    exps = [jnp.exp(v - vals[0]) for v in vals]
    esum = exps[0] + exps[1] + exps[2] + exps[3]
    yield

    n_tile = jnp.sum(onehot, axis=1, keepdims=True)
    n_tile = n_tile + (n_tile - 2.0 * jnp.floor(0.5 * n_tile))
    n_wide = jnp.broadcast_to(n_tile, (LANES, LANES))
    seg_start = _dot(etri_ref[...], n_wide.astype(BF16))[:, 0:1]
    pos_all = lax.dot_general(onehot.astype(BF16), ltri_ref[...], nt, preferred_element_type=F32) + seg_start
    route = jnp.zeros((LANES, ts), F32)
    for k in range(TOP_K):
        pos = jnp.sum(jnp.where(erow == ids[k], pos_all, 0.0), axis=0, keepdims=True)
        route = jnp.where(erow == k, ids[k].astype(F32), route)
        route = jnp.where(erow == TOP_K + k, pos, route)
        route = jnp.where(erow == 2 * TOP_K + k, exps[k] / esum, route)
    route_ref[0, rs, :] = route.T
    routet_ref[sub] = route[0:8, :]
    lane = lax.broadcasted_iota(jnp.int32, (LANES, LANES), 1)
    tcnt = jnp.where(lane == 0, cnt_scr[...], jnp.where(lane == 1, n_wide, 0.0)).T
    tcnt_ref[sub] = tcnt[0:8, :].astype(jnp.int32)
    cnt_scr[...] = cnt_scr[...] + n_tile
    cnt_ref[...] = cnt_scr[...].T[0:8, :]


def _merge(x, mod, os_, lses, fm, g1, g2, wg, wap, wfp, wout, wrh, wrl, br, expand, ltri, etri, perms):
    b, s, _ = x.shape
    ts = MERGE_ROWS * MERGE_TILES_PER_STEP
    tile = lambda w: pl.BlockSpec((1, ts, w), lambda i, j: (i, j, 0))
    strided = lambda w: [pl.BlockSpec((1, d, ts // d, w), lambda i, j: (i, 0, j, 0)) for d in DILATIONS]
    consts = (g1, g2, wg, wap, wfp, wout, wrh, wrl, br, expand, ltri, etri) + tuple(perms)
    per_tile = lambda rows, w: pl.BlockSpec((MERGE_TILES_PER_STEP, rows, w), lambda i, j: (i * per_seq + j, 0, 0))
    n_tiles = b * s // MERGE_ROWS
    in_specs = ([tile(D_MODEL), pl.BlockSpec((1, 6, D_MODEL), lambda i, j: (i, 0, 0))]
                + strided(ATTN_WIDTH) + strided(LANES) + [tile(ATTN_WIDTH)]
                + [_const_spec(a.shape) for a in consts])
    per_seq = s // ts
    return pl.pallas_call(
        _merge_kernel,
        grid=(b, per_seq),
        in_specs=in_specs,
        out_specs=[tile(D_MODEL), tile(D_MODEL), tile(LANES), per_tile(8, MERGE_ROWS), per_tile(8, LANES),
                   pl.BlockSpec((8, LANES), lambda i, j: (0, 0))],
        out_shape=[jax.ShapeDtypeStruct((b, s, D_MODEL), F32),
                   jax.ShapeDtypeStruct((b, s, D_MODEL), BF16),
                   jax.ShapeDtypeStruct((b, s, LANES), F32),
                   jax.ShapeDtypeStruct((n_tiles, 8, MERGE_ROWS), F32),
                   jax.ShapeDtypeStruct((n_tiles, 8, LANES), jnp.int32),
                   jax.ShapeDtypeStruct((8, LANES), F32)],
        scratch_shapes=[pltpu.VMEM((LANES, LANES), F32)],
        compiler_params=pltpu.CompilerParams(
            dimension_semantics=("arbitrary", "arbitrary"), vmem_limit_bytes=VMEM_LIMIT),
        name="merge",
    )(x, mod, *os_, *lses, fm, *consts)


def _to_packed_rows(ref, mat):
    bits = lax.bitcast_convert_type(mat, jnp.uint32)
    for c in range(ROW_WORDS):
        lo = bits[:, c * LANES:(c + 1) * LANES] >> 16
        hi = bits[:, (c + ROW_WORDS) * LANES:(c + ROW_WORDS + 1) * LANES] & jnp.uint32(0xFFFF0000)
        ref[pl.ds(c, mat.shape[0], stride=ROW_WORDS), :] = lo | hi


def _from_packed_rows(ref):
    n = ref.shape[0] // ROW_WORDS
    words = [ref[pl.ds(c, n, stride=ROW_WORDS), :] for c in range(ROW_WORDS)]
    lo = [lax.bitcast_convert_type(w << 16, F32) for w in words]
    hi = [lax.bitcast_convert_type(w & jnp.uint32(0xFFFF0000), F32) for w in words]
    return jnp.concatenate(lo + hi, axis=1)


def _segment_copies(seg_n_ref, seg_row_ref, tile, sorted_ref, hbm_ref, sem, *, to_hbm, start, consecutive=True):
    base = tile * N_EXPERTS

    def copy(src, dst, off, size):
        vm_row = (src + off) if consecutive else 0
        vm = sorted_ref.at[pl.ds(pl.multiple_of(vm_row * ROW_WORDS, 2 * ROW_WORDS), size * ROW_WORDS)]
        hb = hbm_ref.at[pl.ds(pl.multiple_of((dst + off) * ROW_WORDS, 2 * ROW_WORDS), size * ROW_WORDS)]
        cp = pltpu.make_async_copy(vm, hb, sem) if to_hbm else pltpu.make_async_copy(hb, vm, sem)
        if start:
            cp.start()
        else:
            cp.wait()

    def segment(e, src):
        n = seg_n_ref[base + e]
        dst = seg_row_ref[base + e]
        n_long = n >> (LONG_PIECE.bit_length() - 1)

        def long_piece(i, c):
            copy(src, dst, i * LONG_PIECE, LONG_PIECE)
            return c

        lax.fori_loop(0, n_long, long_piece, 0)
        for bit in SHORT_PIECES:
            @pl.when((n & bit) != 0)
            def _():
                copy(src, dst, n & (-2 * bit), bit)
        return src + n

    lax.fori_loop(0, N_EXPERTS, segment, 0)


PIECE_SIZES = (LONG_PIECE,) + SHORT_PIECES
PIECE_SLOTS = 32
PIECE_TABLE = 512


def _piece_tables(n, src0, dst0):
    slots = jnp.arange(PIECE_SLOTS, dtype=jnp.int32)

    def compact(mask, src, dst):
        rank = jnp.cumsum(mask.astype(jnp.int32), axis=1) - 1
        sel = mask[:, :, None] & (rank[:, :, None] == slots)
        pick = lambda v: jnp.sum(jnp.where(sel, v[:, :, None], 0), axis=1)
        return jnp.concatenate([pick(src), pick(dst)], axis=1), jnp.sum(mask.astype(jnp.int32), axis=1)

    n_long = n >> (LONG_PIECE.bit_length() - 1)
    j = jnp.arange(MERGE_ROWS // LONG_PIECE, dtype=jnp.int32)
    offs = (j * LONG_PIECE)[None, :, None]
    flat = lambda v: v.reshape(v.shape[0], -1)
    lists, counts = [], []
    lst, cnt = compact(flat(n_long[:, None, :] > j[None, :, None]), flat(src0[:, None, :] + offs),
                       flat(dst0[:, None, :] + offs))
    lists.append(lst)
    counts.append(cnt)
    for bit in SHORT_PIECES:
        off = n & (-2 * bit)
        lst, cnt = compact((n & bit) != 0, src0 + off, dst0 + off)
        lists.append(lst)
        counts.append(cnt)
    head = jnp.stack(counts, axis=1)
    head = jnp.pad(head, ((0, 0), (0, 2 * PIECE_SLOTS - len(PIECE_SIZES))))
    table = jnp.concatenate([head] + lists, axis=1)
    table = jnp.pad(table, ((0, 0), (0, PIECE_TABLE - table.shape[1])))
    return table.reshape(-1, 1, PIECE_TABLE)


def _start_listed_copies(table_ref, sorted_ref, hbm_ref, sem, *, to_hbm):
    for c, size in enumerate(PIECE_SIZES):
        base = 2 * PIECE_SLOTS * (c + 1)

        def piece(i, carry):
            src = table_ref[0, 0, base + i]
            dst = table_ref[0, 0, base + PIECE_SLOTS + i]
            vm = sorted_ref.at[pl.ds(pl.multiple_of(src * ROW_WORDS, 2 * ROW_WORDS), size * ROW_WORDS)]
            hb = hbm_ref.at[pl.ds(pl.multiple_of(dst * ROW_WORDS, 2 * ROW_WORDS), size * ROW_WORDS)]
            (pltpu.make_async_copy(vm, hb, sem) if to_hbm else pltpu.make_async_copy(hb, vm, sem)).start()
            return carry

        lax.fori_loop(0, table_ref[0, 0, c], piece, 0)


def _wait_tile_copies(seg_n_ref, tile, sorted_ref, hbm_ref, sem, *, to_hbm):
    base = tile * N_EXPERTS
    total = lax.fori_loop(0, N_EXPERTS, lambda e, acc: acc + seg_n_ref[base + e], 0)

    def wait(size):
        vm = sorted_ref.at[pl.ds(0, size * ROW_WORDS)]
        hb = hbm_ref.at[pl.ds(0, size * ROW_WORDS)]
        (pltpu.make_async_copy(vm, hb, sem) if to_hbm else pltpu.make_async_copy(hb, vm, sem)).wait()

    def long_wait(i, c):
        wait(LONG_PIECE)
        return c

    lax.fori_loop(0, total >> (LONG_PIECE.bit_length() - 1), long_wait, 0)
    for bit in SHORT_PIECES:
        @pl.when((total & bit) != 0)
        def _():
            wait(bit)


def _dispatch_kernel(seg_n_ref, seg_row_ref, table_ref, routet_ref, u2_ref, xs_ref, sorted_scr, sems):
    tile = pl.program_id(0)
    n_tiles = pl.num_programs(0)
    slot = tile % 2
    tt = u2_ref.shape[0]
    rows = SORTED_ROWS

    def wait_copies(tile_, slot_):
        _wait_tile_copies(seg_n_ref, tile_, sorted_scr.at[slot_], xs_ref, sems.at[slot_], to_hbm=True)

    @pl.when(tile == 0)
    def _():
        sorted_scr[1, 0:LONG_PIECE * ROW_WORDS, :] = jnp.zeros((LONG_PIECE * ROW_WORDS, LANES), jnp.uint32)
        for zero_row in (n_tiles, n_tiles + 1):
            for start in (True, False):
                _segment_copies(seg_n_ref, seg_row_ref, zero_row, sorted_scr.at[1], xs_ref, sems.at[1],
                                to_hbm=True, start=start, consecutive=False)

    pos_t = routet_ref[0].astype(jnp.int32).astype(jnp.int16)
    j = lax.broadcasted_iota(jnp.int32, (rows, tt), 0).astype(jnp.int16)
    onehot = jnp.zeros((rows, tt), BF16)
    for k in range(TOP_K):
        onehot = jnp.where(j == pos_t[TOP_K + k:TOP_K + k + 1, :], jnp.ones_like(onehot), onehot)
    _to_packed_rows(sorted_scr.at[slot], _dot(onehot, u2_ref[...]))
    _start_listed_copies(table_ref, sorted_scr.at[slot], xs_ref, sems.at[slot], to_hbm=True)

    @pl.when(tile > 0)
    def _():
        wait_copies(tile - 1, 1 - slot)

    @pl.when(tile == n_tiles - 1)
    def _():
        wait_copies(tile, slot)


def _dispatch(seg_n, seg_row, table, route_t, u2, n_rows, tt=MERGE_ROWS):
    t = u2.shape[0]
    grid_spec = pltpu.PrefetchScalarGridSpec(
        num_scalar_prefetch=2,
        grid=(t // tt,),
        in_specs=[pl.BlockSpec((1, 1, PIECE_TABLE), lambda i, *_: (i, 0, 0), memory_space=pltpu.SMEM),
                  pl.BlockSpec((1, 8, tt), lambda i, *_: (i, 0, 0)),
                  pl.BlockSpec((tt, D_MODEL), lambda i, *_: (i, 0))],
        out_specs=pl.BlockSpec(memory_space=pl.ANY),
        scratch_shapes=[pltpu.VMEM((2, SORTED_ROWS * ROW_WORDS, LANES), jnp.uint32),
                        pltpu.SemaphoreType.DMA((2,))],
    )
    return pl.pallas_call(
        _dispatch_kernel,
        grid_spec=grid_spec,
        out_shape=jax.ShapeDtypeStruct((n_rows * ROW_WORDS, LANES), jnp.uint32),
        compiler_params=pltpu.CompilerParams(
            dimension_semantics=("arbitrary",), has_side_effects=True, vmem_limit_bytes=VMEM_LIMIT),
        name="dispatch",
    )(seg_n, seg_row, table, route_t, u2)


def _expert_kernel(be_ref, nb_ref, xs_ref, w1_ref, b1_ref, w2_ref, b2_ref, ys_ref):
    del be_ref
    i = pl.program_id(0)

    @pl.when(i < nb_ref[0])
    def _():
        hdn = _dot(_from_packed_rows(xs_ref).astype(BF16), w1_ref[0]) + b1_ref[0]
        glu = jnp.minimum(hdn[:, :D_FF], SWIGLU_LIMIT)
        lin = jnp.clip(hdn[:, D_FF:], -SWIGLU_LIMIT, SWIGLU_LIMIT)
        act = (lin + 1.0) * (glu * jax.nn.sigmoid(SWIGLU_ALPHA * glu))
        y = _dot(act.astype(BF16), w2_ref[0]) + b2_ref[0]
        _to_packed_rows(ys_ref, y.astype(BF16).astype(F32))

    @pl.when(i >= nb_ref[0])
    def _():
        ys_ref[...] = jnp.zeros_like(ys_ref)


def _expert(block_e, nb_used, xs, w1, b1, w2, b2):
    block = (EXPERT_ROWS * ROW_WORDS, LANES)
    n_blocks = xs.shape[0] // block[0]
    grid_spec = pltpu.PrefetchScalarGridSpec(
        num_scalar_prefetch=2,
        grid=(n_blocks,),
        in_specs=[pl.BlockSpec(block, lambda i, be, nb: (jnp.minimum(i, nb[0] - 1), 0)),
                  pl.BlockSpec((1, D_MODEL, 2 * D_FF), lambda i, be, nb: (be[i], 0, 0)),
                  pl.BlockSpec((1, 1, 2 * D_FF), lambda i, be, nb: (be[i], 0, 0)),
                  pl.BlockSpec((1, D_FF, D_MODEL), lambda i, be, nb: (be[i], 0, 0)),
                  pl.BlockSpec((1, 1, D_MODEL), lambda i, be, nb: (be[i], 0, 0))],
        out_specs=pl.BlockSpec(block, lambda i, be, nb: (i, 0)),
    )
    return pl.pallas_call(
        _expert_kernel,
        grid_spec=grid_spec,
        out_shape=jax.ShapeDtypeStruct(xs.shape, xs.dtype),
        compiler_params=pltpu.CompilerParams(
            dimension_semantics=("arbitrary",), vmem_limit_bytes=VMEM_LIMIT),
        name="expert",
    )(block_e, nb_used, xs, w1, b1, w2, b2)


def _combine_kernel(seg_n_ref, table_ref, next_table_ref, route_ref, h_ref, mod_ref, ys_ref, o_ref,
                    sorted_scr, sems):
    tile = pl.program_id(0)
    n_tiles = pl.num_programs(0)
    slot = tile % 2
    tt = h_ref.shape[0]
    rows = SORTED_ROWS

    @pl.when(tile == 0)
    def _():
        sorted_scr[...] = jnp.zeros_like(sorted_scr)
        _start_listed_copies(table_ref, sorted_scr.at[slot], ys_ref, sems.at[slot], to_hbm=False)

    @pl.when(tile + 1 < n_tiles)
    def _():
        _start_listed_copies(next_table_ref, sorted_scr.at[1 - slot], ys_ref, sems.at[1 - slot], to_hbm=False)

    route = route_ref[...]
    pos = route.astype(jnp.int32).astype(jnp.int16)
    weight = route.astype(BF16)
    j = lax.broadcasted_iota(jnp.int32, (tt, rows), 1).astype(jnp.int16)
    wmat = jnp.zeros((tt, rows), BF16)
    for k in range(TOP_K):
        wmat = jnp.where(j == pos[:, TOP_K + k:TOP_K + k + 1],
                         jnp.broadcast_to(weight[:, 2 * TOP_K + k:2 * TOP_K + k + 1], wmat.shape), wmat)
    _wait_tile_copies(seg_n_ref, tile, sorted_scr.at[slot], ys_ref, sems.at[slot], to_hbm=False)
    y = _from_packed_rows(sorted_scr.at[slot]).astype(BF16)
    o_ref[...] = h_ref[...] + mod_ref[0][5:6] * _dot(wmat.astype(BF16), y)


def _combine(seg_n, table, route, h, mod, ys, seq, tt=MERGE_ROWS):
    t = h.shape[0]
    n_tiles = t // tt
    per_seq = seq // tt
    table_spec = lambda ahead: pl.BlockSpec(
        (1, 1, PIECE_TABLE), lambda i, *_: (jnp.minimum(i + ahead, n_tiles - 1), 0, 0), memory_space=pltpu.SMEM)
    grid_spec = pltpu.PrefetchScalarGridSpec(
        num_scalar_prefetch=1,
        grid=(n_tiles,),
        in_specs=[table_spec(0), table_spec(1),
                  pl.BlockSpec((tt, LANES), lambda i, *_: (i, 0)),
                  pl.BlockSpec((tt, D_MODEL), lambda i, *_: (i, 0)),
                  pl.BlockSpec((1, 6, D_MODEL), lambda i, *_: (i // per_seq, 0, 0)),
                  pl.BlockSpec(memory_space=pl.ANY)],
        out_specs=pl.BlockSpec((tt, D_MODEL), lambda i, *_: (i, 0)),
        scratch_shapes=[pltpu.VMEM((2, SORTED_ROWS * ROW_WORDS, LANES), jnp.uint32),
                        pltpu.SemaphoreType.DMA((2,))],
    )
    return pl.pallas_call(
        _combine_kernel,
        grid_spec=grid_spec,
        out_shape=jax.ShapeDtypeStruct((t, D_MODEL), F32),
        compiler_params=pltpu.CompilerParams(
            dimension_semantics=("arbitrary",), vmem_limit_bytes=VMEM_LIMIT),
        name="combine",
    )(seg_n, table, table, route, h, mod, ys)


def _rope_tables(seq, ts):
    half = ROPE_DIM // 2
    inv_freq = jnp.power(jnp.float32(ROPE_THETA), -jnp.arange(half, dtype=F32) * 2.0 / ROPE_DIM)
    ang = jnp.arange(seq, dtype=F32)[:, None] * inv_freq[None, :]
    cos, sin = jnp.cos(ang), jnp.sin(ang)
    ones = jnp.ones((seq, HEAD_DIM - ROPE_DIM), F32)
    zeros = jnp.zeros_like(ones)
    zh = jnp.zeros((seq, half), F32)
    cos_t = jnp.concatenate([cos, cos, ones], axis=1)
    sin_prev = jnp.concatenate([zh, sin, zeros], axis=1)
    sin_next = jnp.concatenate([-sin, zh, zeros], axis=1)
    tables = jnp.stack([jnp.tile(t, (1, LANES // HEAD_DIM)) for t in (cos_t, sin_prev, sin_next)])
    per_group = []
    for d in DILATIONS:
        pos = jnp.arange(seq).reshape(seq // ts, ts // d, d).transpose(0, 2, 1).reshape(seq)
        per_group.append(tables[:, pos, :])
    return jnp.stack(per_group)


def _dft_tables(seq):
    k = jnp.arange(seq, dtype=jnp.int32)
    ang = ((k[:, None] * k[None, :]) % seq).astype(F32) * (2.0 * jnp.pi / seq)
    scale = seq ** -0.5
    cs = jnp.cos(ang) * scale
    ns = -jnp.sin(ang) * scale
    c = jnp.arange(FOURIER_GROUP_DIM, dtype=jnp.int32)
    angc = ((c[:, None] * c[None, :]) % FOURIER_GROUP_DIM).astype(F32) * (2.0 * jnp.pi / FOURIER_GROUP_DIM)
    eye = jnp.eye(ATTN_WIDTH // FOURIER_GROUP_DIM, dtype=F32)
    cscale = FOURIER_GROUP_DIM ** -0.5
    bd = jnp.concatenate([jnp.kron(eye, jnp.cos(angc) * cscale),
                          jnp.kron(eye, jnp.sin(angc) * cscale)], axis=1)
    return bd.astype(BF16), cs.astype(BF16), ns.astype(BF16)


def _trunk(x, c, p):
    b, s, _ = x.shape
    t = b * s
    mod = _ada(c, p["w_ada"], p["b_ada"]).reshape(b, 6, D_MODEL)
    *qkvs, fz = _inproj(x, mod, p["g1"], p["w_qkvf"], p["gains"], p["rope"], p["gmat"])
    os_, lses = zip(*[_attn(qkv, g) for g, qkv in enumerate(qkvs)])
    fm = _fourier(fz, p["bd"], p["cs"], p["ns"])
    h, u2, route, route_t, tcnt, cnt = _merge(
        x, mod, os_, lses, fm, p["g1"], p["g2"], p["wg"], p["wap"], p["wfp"], p["wout"], p["wrh"],
        p["wrl"], p["br"], p["expand"], p["ltri"], p["etri"], p["perms"])

    rows = EXPERT_ROWS
    counts = cnt[0, :N_EXPERTS].astype(jnp.int32)
    padded = (counts + rows - 1) // rows * rows
    pend = jnp.cumsum(padded)
    pstart = pend - padded
    n_blocks = -(-(t * TOP_K + N_EXPERTS * (t // MERGE_ROWS + rows - 1)) // rows)
    first = jnp.arange(N_EXPERTS) == 0
    seg_row = jnp.concatenate([pstart[None, :] + tcnt[:, 0, :N_EXPERTS], (pstart + counts)[None, :],
                               jnp.where(first, pend[-1], 0)[None, :]]).reshape(-1)
    seg_n = jnp.concatenate([tcnt[:, 1, :N_EXPERTS], (padded - counts)[None, :],
                             jnp.where(first, n_blocks * rows - pend[-1], 0)[None, :]]).reshape(-1)
    block_start = jnp.arange(n_blocks, dtype=jnp.int32) * rows
    block_e = jnp.minimum(jnp.sum((pend[None, :] <= block_start[:, None]).astype(jnp.int32), axis=-1),
                          N_EXPERTS - 1)
    nb_used = (pend[-1:] // rows).astype(jnp.int32)

    n_tile = tcnt[:, 1, :N_EXPERTS]
    table = _piece_tables(n_tile, jnp.cumsum(n_tile, axis=1) - n_tile, pstart[None, :] + tcnt[:, 0, :N_EXPERTS])
    xs = _dispatch(seg_n, seg_row, table, route_t, u2.reshape(t, D_MODEL), n_blocks * rows)
    ys = _expert(block_e, nb_used, xs, p["w1"], p["b1"], p["w2"], p["b2"])
    out = _combine(seg_n, table, route.reshape(t, LANES), h.reshape(t, D_MODEL), mod, ys, s)
    return out.reshape(b, s, D_MODEL)


def kernel(x_prompt, x_sample, c_prompt, c_sample, w_ada, b_ada, norm1_g, norm2_g, w_in, q_gain, k_gain,
           w_attn_proj, w_fourier_proj, w_out, w_router, b_router, w1, b1, w2, b2):
    seq = x_prompt.shape[1]
    assert w_ada.shape[0] == 1, "single-layer trunk"
    w_in0 = w_in[0]
    gains = jnp.zeros((8, ATTN_WIDTH), F32)
    for g in range(N_GROUPS):
        gains = gains.at[2 * g].set(jnp.tile(q_gain[0, g], HEADS))
        gains = gains.at[2 * g + 1].set(jnp.tile(k_gain[0, g], HEADS))
    heads = jnp.arange(ATTN_WIDTH) // HEAD_DIM
    gmat = ((heads[:, None] == heads[None, :]).astype(F32) / HEAD_DIM).astype(BF16)
    expand = (jnp.arange(LANES)[:, None] == heads[None, :]).astype(BF16)
    ltri = (jnp.arange(MERGE_ROWS)[:, None] > jnp.arange(MERGE_ROWS)[None, :]).astype(BF16)
    etri = (jnp.arange(LANES)[:, None] > jnp.arange(LANES)[None, :]).astype(BF16)
    perms = []
    for d in DILATIONS[1:]:
        n = MERGE_ROWS // d
        src = (jnp.arange(MERGE_ROWS) % d) * n + jnp.arange(MERGE_ROWS) // d
        perms.append((src[:, None] == jnp.arange(MERGE_ROWS)[None, :]).astype(BF16))
    wr = jnp.zeros((LANES, D_MODEL), F32).at[:N_EXPERTS].set(w_router[0].T)
    wrh = wr.astype(BF16)
    wrl = (wr - wrh.astype(F32)).astype(BF16)
    br = jnp.zeros((LANES, MERGE_ROWS), F32).at[:N_EXPERTS].set(b_router[0][:, None])
    bd, cs, ns = _dft_tables(seq)
    p = dict(
        w_ada=w_ada[0], b_ada=b_ada[0], g1=norm1_g, g2=norm2_g,
        w_qkvf=w_in0[:, :QKVF_WIDTH].astype(BF16), wg=w_in0[:, QKVF_WIDTH:].astype(BF16),
        gains=gains, rope=_rope_tables(seq, INPROJ_ROWS), gmat=gmat, bd=bd, cs=cs, ns=ns,
        wap=w_attn_proj[0].astype(BF16), wfp=w_fourier_proj[0].astype(BF16), wout=w_out[0].astype(BF16),
        wrh=wrh, wrl=wrl, br=br, expand=expand, ltri=ltri, etri=etri, perms=perms,
        w1=w1[0].astype(BF16), b1=b1[0].reshape(N_EXPERTS, 1, 2 * D_FF),
        w2=w2[0].astype(BF16), b2=b2[0].reshape(N_EXPERTS, 1, D_MODEL),
    )
    return _trunk(x_prompt, c_prompt, p), _trunk(x_sample, c_sample, p)
```

```python
import functools

import jax
import jax.numpy as jnp
from jax import lax
from jax.experimental import pallas as pl
from jax.experimental.pallas import tpu as pltpu

F32 = jnp.float32
BF16 = jnp.bfloat16

D_MODEL = 1024
HEAD_DIM = 64
HEADS = 8
ATTN_WIDTH = HEADS * HEAD_DIM
DILATIONS = (1, 4, 16)
BAND = 64
N_GROUPS = 3
QKVF_WIDTH = 5120
FOURIER_GROUP_DIM = 64
ROPE_DIM = 16
ROPE_THETA = 500000.0
N_EXPERTS = 32
TOP_K = 4
D_FF = 1024
SWIGLU_ALPHA = 1.702
SWIGLU_LIMIT = 7.0
NORM_EPS = 1e-6
NEG_INF = -1e30
LN2 = 0.6931471805599453
LOG2E = 1.4426950408889634

LANES = 128
ROW_WORDS = D_MODEL // LANES // 2
SORTED_ROWS = 1088
EXPERT_ROWS = 512
LONG_PIECE = 64
SHORT_PIECES = (32, 16, 8, 4, 2)
INPROJ_ROWS = 512
MERGE_ROWS = 256
MERGE_TILES_PER_STEP = 4
ATTN_BLOCKS_PER_STEP = 8
ATTN_STAGE_SKEW = 0
VMEM_LIMIT = 56 * 1024 * 1024


def _dot(a, b):
    return jnp.dot(a, b, preferred_element_type=F32)


def _split_bf16(v):
    hi = v.astype(BF16)
    lo = (v - hi.astype(F32)).astype(BF16)
    return hi, lo


def _rms_mod(x, gain, scale, shift):
    ms = jnp.mean(x * x, axis=-1, keepdims=True)
    return x * lax.rsqrt(ms + NORM_EPS) * gain * (1.0 + scale) + shift


def _run_in_lockstep(stage_generators, skew=0):
    live = dict(enumerate(stage_generators))
    step = 0
    while live:
        for j in [j for j in live if step >= j * skew]:
            if next(live[j], "done") == "done":
                del live[j]
        step += 1


def _const_spec(shape):
    return pl.BlockSpec(shape, lambda *_: (0,) * len(shape), pipeline_mode=pl.Buffered(1))


def _ada_kernel(c_ref, w_ref, b_ref, o_ref):
    c = c_ref[...]
    a = c * jax.nn.sigmoid(c)
    a_hi, a_lo = _split_bf16(a)
    w_hi, w_lo = _split_bf16(w_ref[...])
    o_ref[...] = _dot(a_hi, w_hi) + _dot(a_lo, w_hi) + _dot(a_hi, w_lo) + b_ref[...]


def _ada(c, w_ada, b_ada):
    b = c.shape[0]
    n = w_ada.shape[1]
    tn = D_MODEL
    return pl.pallas_call(
        _ada_kernel,
        grid=(n // tn,),
        in_specs=[pl.BlockSpec((b, D_MODEL), lambda j: (0, 0)),
                  pl.BlockSpec((D_MODEL, tn), lambda j: (0, j)),
                  pl.BlockSpec((1, tn), lambda j: (0, j))],
        out_specs=pl.BlockSpec((b, tn), lambda j: (0, j)),
        out_shape=jax.ShapeDtypeStruct((b, n), F32),
        compiler_params=pltpu.CompilerParams(vmem_limit_bytes=VMEM_LIMIT),
        name="ada",
    )(c, w_ada, b_ada.reshape(1, n))


def _inproj_kernel(x_ref, mod_ref, g1_ref, w_ref, gain_ref, rope_ref, gmat_ref,
                   o0_ref, o1_ref, o2_ref, of_ref, u_scr):
    x = x_ref[0]
    ts = x.shape[0]
    mod = mod_ref[0]
    u = _rms_mod(x, g1_ref[...], mod[1:2], mod[0:1])
    ub = u.astype(BF16)
    for c in range(D_MODEL // LANES):
        u_scr[c] = u[:, c * LANES:(c + 1) * LANES]
    for g, o_ref in enumerate((o0_ref, o1_ref, o2_ref)):
        d = DILATIONS[g]
        n = ts // d
        if d == 1:
            ug = ub
        else:
            ug = jnp.concatenate(
                [jnp.concatenate([u_scr[c, pl.ds(r, n, stride=d), :] for r in range(d)], axis=0)
                 for c in range(D_MODEL // LANES)], axis=1).astype(BF16)
        cos_t, sin_prev, sin_next = rope_ref[g, 0], rope_ref[g, 1], rope_ref[g, 2]
        for j in range(3):
            c0 = (3 * g + j) * ATTN_WIDTH
            z = _dot(ug, w_ref[:, c0:c0 + ATTN_WIDTH])
            if j < 2:
                ms = _dot((z * z).astype(BF16), gmat_ref[...])
                gain = gain_ref[2 * g + j:2 * g + j + 1, :]
                if j == 0:
                    gain = gain * (HEAD_DIM ** -0.5 * LOG2E)
                z = z * lax.rsqrt(ms + NORM_EPS) * gain
                blocks = []
                for blk in range(ATTN_WIDTH // LANES):
                    zb = z[:, blk * LANES:(blk + 1) * LANES]
                    blocks.append(zb * cos_t
                                  + pltpu.roll(zb, ROPE_DIM // 2, 1) * sin_prev
                                  + pltpu.roll(zb, LANES - ROPE_DIM // 2, 1) * sin_next)
                z = jnp.concatenate(blocks, axis=1)
            zb16 = z.astype(BF16)
            for r in range(d):
                o_ref[0, r, :, j * ATTN_WIDTH:(j + 1) * ATTN_WIDTH] = zb16[r * n:(r + 1) * n, :]
    of_ref[0] = _dot(ub, w_ref[:, 9 * ATTN_WIDTH:]).astype(BF16)


def _inproj(x, mod, g1, w_qkvf, gains, rope, gmat, ts=INPROJ_ROWS):
    b, s, _ = x.shape
    qkv_w = 3 * ATTN_WIDTH
    out_specs = [pl.BlockSpec((1, d, ts // d, qkv_w), lambda i, j: (i, 0, j, 0)) for d in DILATIONS]
    out_shape = [jax.ShapeDtypeStruct((b, d, s // d, qkv_w), BF16) for d in DILATIONS]
    return pl.pallas_call(
        _inproj_kernel,
        grid=(b, s // ts),
        in_specs=[pl.BlockSpec((1, ts, D_MODEL), lambda i, j: (i, j, 0)),
                  pl.BlockSpec((1, 6, D_MODEL), lambda i, j: (i, 0, 0)),
                  _const_spec((1, D_MODEL)),
                  _const_spec((D_MODEL, QKVF_WIDTH)),
                  _const_spec((8, ATTN_WIDTH)),
                  pl.BlockSpec((N_GROUPS, 3, ts, LANES), lambda i, j: (0, 0, j, 0)),
                  _const_spec((ATTN_WIDTH, ATTN_WIDTH))],
        out_specs=out_specs + [pl.BlockSpec((1, ts, ATTN_WIDTH), lambda i, j: (i, j, 0))],
        out_shape=out_shape + [jax.ShapeDtypeStruct((b, s, ATTN_WIDTH), BF16)],
        scratch_shapes=[pltpu.VMEM((D_MODEL // LANES, ts, LANES), F32)],
        compiler_params=pltpu.CompilerParams(
            dimension_semantics=("arbitrary", "arbitrary"), vmem_limit_bytes=VMEM_LIMIT),
        name="inproj",
    )(x, mod, g1, w_qkvf, gains, rope, gmat)


def _attn_kernel(qkv_ref, o_ref, lse_ref, s_scr, p_scr, *, length):
    tq = LANES
    tk = min(2 * LANES, length)
    nq = length // tq
    n_pairs = ATTN_WIDTH // LANES
    lane = lax.broadcasted_iota(jnp.int32, (tq, LANES), 1)
    first_head = lane < HEAD_DIM
    first_head_k = lax.broadcasted_iota(jnp.int32, (tk, LANES), 1) < HEAD_DIM
    rel = (lax.broadcasted_iota(jnp.int32, (tq, tk), 1)
           - lax.broadcasted_iota(jnp.int32, (tq, tk), 0))

    def head_pair(blk, it, p, tiles):
        r = it // nq
        i = it % nq
        q0 = pl.multiple_of(i * tq, tq)
        ws = pl.multiple_of(jnp.clip(i * tq - BAND, 0, length - tk), BAND)
        valid = jnp.abs(rel + (ws - i * tq)) <= BAND
        q2 = qkv_ref[0, r, pl.ds(q0, tq), p * LANES:(p + 1) * LANES]
        k2 = qkv_ref[0, r, pl.ds(ws, tk), ATTN_WIDTH + p * LANES:ATTN_WIDTH + (p + 1) * LANES]
        for hh in range(2):
            sel = first_head if hh == 0 else jnp.logical_not(first_head)
            qa = jnp.where(sel, q2, jnp.zeros_like(q2))
            s = lax.dot_general(qa, k2, (((1,), (1,)), ((), ())), preferred_element_type=F32)
            s_scr[blk, 2 * p + hh] = jnp.where(valid, s, NEG_INF)
        yield

        for hh in range(2):
            s = s_scr[blk, 2 * p + hh]
            m = jnp.max(s, axis=-1, keepdims=True)
            p_scr[blk, 2 * p + hh] = jnp.exp2(s - m).astype(BF16)
            tiles["m"] = jnp.where(lane == 2 * p + hh, m, tiles["m"])
        yield

        v2 = qkv_ref[0, r, pl.ds(ws, tk), 2 * ATTN_WIDTH + p * LANES:2 * ATTN_WIDTH + (p + 1) * LANES]
        one = jnp.ones_like(v2)
        o_a = _dot(p_scr[blk, 2 * p], jnp.where(first_head_k, v2, one))
        o_b = _dot(p_scr[blk, 2 * p + 1], jnp.where(first_head_k, one, v2))
        den_ba = jnp.where(first_head, o_b, o_a)
        den_ab = pltpu.roll(den_ba, HEAD_DIM, 1)
        o_ref[0, r, pl.ds(q0, tq), p * LANES:(p + 1) * LANES] = (
            jnp.where(first_head, o_a, o_b) / den_ab).astype(BF16)
        tiles["den"] = jnp.where(lane == 2 * p, den_ab, tiles["den"])
        tiles["den"] = jnp.where(lane == 2 * p + 1, den_ba, tiles["den"])

    def q_blocks(step, carry):
        its = [step * ATTN_BLOCKS_PER_STEP + blk for blk in range(ATTN_BLOCKS_PER_STEP)]
        tiles = [{"m": jnp.zeros((tq, LANES), F32), "den": jnp.ones((tq, LANES), F32)} for _ in its]
        _run_in_lockstep([head_pair(blk, it, p, tiles[blk]) for p in range(n_pairs)
                          for blk, it in enumerate(its)], skew=ATTN_STAGE_SKEW)
        for it, t in zip(its, tiles):
            rows = pl.ds(pl.multiple_of((it % nq) * tq, tq), tq)
            lse_ref[0, it // nq, rows, :] = (t["m"] + jnp.log2(t["den"])) * LN2
        return carry

    lax.fori_loop(0, qkv_ref.shape[1] * nq // ATTN_BLOCKS_PER_STEP, q_blocks, 0)


def _attn(qkv, g):
    b, d, length, width = qkv.shape
    tk = min(2 * LANES, length)
    return pl.pallas_call(
        functools.partial(_attn_kernel, length=length),
        grid=(b,),
        in_specs=[pl.BlockSpec((1, d, length, width), lambda i: (i, 0, 0, 0))],
        out_specs=[pl.BlockSpec((1, d, length, ATTN_WIDTH), lambda i: (i, 0, 0, 0)),
                   pl.BlockSpec((1, d, length, LANES), lambda i: (i, 0, 0, 0))],
        out_shape=[jax.ShapeDtypeStruct((b, d, length, ATTN_WIDTH), BF16),
                   jax.ShapeDtypeStruct((b, d, length, LANES), F32)],
        scratch_shapes=[pltpu.VMEM((ATTN_BLOCKS_PER_STEP, HEADS, LANES, tk), F32),
                        pltpu.VMEM((ATTN_BLOCKS_PER_STEP, HEADS, LANES, tk), BF16)],
        compiler_params=pltpu.CompilerParams(
            dimension_semantics=("arbitrary",), vmem_limit_bytes=VMEM_LIMIT),
        name=f"attn{g}",
    )(qkv)


def _fourier_kernel(x_ref, bd_ref, cs_ref, ns_ref, o_ref, y_ref, *, seq, rows):
    for r in range(seq // rows):
        rs = slice(r * rows, (r + 1) * rows)
        y_ref[rs, :] = _dot(x_ref[0, rs, :], bd_ref[...]).astype(BF16)
    for r in range(seq // rows):
        rs = slice(r * rows, (r + 1) * rows)
        o = _dot(cs_ref[rs, :], y_ref[:, :ATTN_WIDTH]) + _dot(ns_ref[rs, :], y_ref[:, ATTN_WIDTH:])
        o_ref[0, rs, :] = o.astype(BF16)


def _fourier(qkvf, bd, cs, ns):
    b, s, _ = qkvf.shape
    return pl.pallas_call(
        functools.partial(_fourier_kernel, seq=s, rows=256),
        grid=(b,),
        in_specs=[pl.BlockSpec((1, s, ATTN_WIDTH), lambda i: (i, 0, 0)),
                  _const_spec((ATTN_WIDTH, 2 * ATTN_WIDTH)),
                  _const_spec((s, s)),
                  _const_spec((s, s))],
        out_specs=pl.BlockSpec((1, s, ATTN_WIDTH), lambda i: (i, 0, 0)),
        out_shape=jax.ShapeDtypeStruct((b, s, ATTN_WIDTH), BF16),
        scratch_shapes=[pltpu.VMEM((s, 2 * ATTN_WIDTH), BF16)],
        compiler_params=pltpu.CompilerParams(
            dimension_semantics=("arbitrary",), vmem_limit_bytes=VMEM_LIMIT),
        name="fourier",
    )(qkvf, bd, cs, ns)


def _merge_kernel(x_ref, mod_ref, o0_ref, o1_ref, o2_ref, l0_ref, l1_ref, l2_ref, fm_ref,
                  g1_ref, g2_ref, wg_ref, wap_ref, wfp_ref, wout_ref, wrh_ref, wrl_ref, br_ref,
                  expand_ref, ltri_ref, etri_ref, perm1_ref, perm2_ref,
                  h_ref, u2_ref, route_ref, routet_ref, tcnt_ref, cnt_ref, cnt_scr):
    @pl.when((pl.program_id(0) == 0) & (pl.program_id(1) == 0))
    def _():
        cnt_scr[...] = jnp.zeros_like(cnt_scr)

    refs = (x_ref, mod_ref, o0_ref, o1_ref, o2_ref, l0_ref, l1_ref, l2_ref, fm_ref,
            g1_ref, g2_ref, wg_ref, wap_ref, wfp_ref, wout_ref, wrh_ref, wrl_ref, br_ref,
            expand_ref, ltri_ref, etri_ref, perm1_ref, perm2_ref,
            h_ref, u2_ref, route_ref, routet_ref, tcnt_ref, cnt_ref, cnt_scr)
    _run_in_lockstep([_merge_tile(sub, *refs) for sub in range(x_ref.shape[1] // MERGE_ROWS)])


def _merge_tile(sub, x_ref, mod_ref, o0_ref, o1_ref, o2_ref, l0_ref, l1_ref, l2_ref, fm_ref,
                g1_ref, g2_ref, wg_ref, wap_ref, wfp_ref, wout_ref, wrh_ref, wrl_ref, br_ref,
                expand_ref, ltri_ref, etri_ref, perm1_ref, perm2_ref,
                h_ref, u2_ref, route_ref, routet_ref, tcnt_ref, cnt_ref, cnt_scr):
    ts = MERGE_ROWS
    rs = slice(sub * ts, (sub + 1) * ts)
    x = x_ref[0, rs, :]
    mod = mod_ref[0]
    shift1, scale1, gate1 = mod[0:1], mod[1:2], mod[2:3]
    shift2, scale2 = mod[3:4], mod[4:5]
    u = _rms_mod(x, g1_ref[...], scale1, shift1).astype(BF16)
    yield
    gates = _dot(u, wg_ref[...])
    yield

    os_ = [o0_ref[0, 0, rs, :].astype(F32)]
    lses = [l0_ref[0, 0, rs, :]]
    for d, o_ref, l_ref, perm_ref in ((DILATIONS[1], o1_ref, l1_ref, perm1_ref),
                                      (DILATIONS[2], o2_ref, l2_ref, perm2_ref)):
        perm = perm_ref[...]
        ms = slice(sub * ts // d, (sub + 1) * ts // d)
        os_.append(_dot(perm, o_ref[0, :, ms, :].reshape(ts, ATTN_WIDTH)))
        l_hi, l_lo = _split_bf16(l_ref[0, :, ms, :].reshape(ts, LANES))
        lses.append(_dot(perm, l_hi) + _dot(perm, l_lo))
    yield

    mx = jnp.maximum(jnp.maximum(lses[0], lses[1]), lses[2])
    es = [jnp.exp(l - mx) for l in lses]
    den = es[0] + es[1] + es[2]
    attn = None
    for e, o in zip(es, os_):
        w_hi, w_lo = _split_bf16(e / den)
        wf = _dot(w_hi, expand_ref[...])
        term = wf * o
        attn = term if attn is None else attn + term
    yield

    a = _dot(attn.astype(BF16), wap_ref[...])
    f = _dot(fm_ref[0, rs, :], wfp_ref[...])
    yield
    merged = jax.nn.sigmoid(gates[:, :D_MODEL]) * a + jax.nn.sigmoid(gates[:, D_MODEL:]) * f
    yield
    h = x + gate1 * _dot(merged.astype(BF16), wout_ref[...])
    h_ref[0, rs, :] = h
    yield

    u2 = _rms_mod(h, g2_ref[...], scale2, shift2)
    u_hi, u_lo = _split_bf16(u2)
    u2_ref[0, rs, :] = u_hi
    yield
    nt = (((1,), (1,)), ((), ()))
    logits = (lax.dot_general(wrh_ref[...], u_hi, nt, preferred_element_type=F32)
              + lax.dot_general(wrh_ref[...], u_lo, nt, preferred_element_type=F32)
              + lax.dot_general(wrl_ref[...], u_hi, nt, preferred_element_type=F32) + br_ref[...])
    erow = lax.broadcasted_iota(jnp.int32, (LANES, ts), 0)
    lg = jnp.where(erow < N_EXPERTS, logits, -jnp.inf)
    vals, ids = [], []
    onehot = jnp.zeros((LANES, ts), F32)
    for _ in range(TOP_K):
        m = jnp.max(lg, axis=0, keepdims=True)
        idx = jnp.min(jnp.where(lg == m, erow, LANES), axis=0, keepdims=True)
        hit = erow == idx
        vals.append(m)
        ids.append(idx)
        onehot = jnp.where(hit, 1.0, onehot)
        lg = jnp.where(hit, -jnp.inf, lg)
    exps = [jnp.exp(v - vals[0]) for v in vals]
    esum = exps[0] + exps[1] + exps[2] + exps[3]
    yield

    n_tile = jnp.sum(onehot, axis=1, keepdims=True)
    n_tile = n_tile + (n_tile - 2.0 * jnp.floor(0.5 * n_tile))
    n_wide = jnp.broadcast_to(n_tile, (LANES, LANES))
    seg_start = _dot(etri_ref[...], n_wide.astype(BF16))[:, 0:1]
    pos_all = lax.dot_general(onehot.astype(BF16), ltri_ref[...], nt, preferred_element_type=F32) + seg_start
    route = jnp.zeros((LANES, ts), F32)
    for k in range(TOP_K):
        pos = jnp.sum(jnp.where(erow == ids[k], pos_all, 0.0), axis=0, keepdims=True)
        route = jnp.where(erow == k, ids[k].astype(F32), route)
        route = jnp.where(erow == TOP_K + k, pos, route)
        route = jnp.where(erow == 2 * TOP_K + k, exps[k] / esum, route)
    route_ref[0, rs, :] = route.T
    routet_ref[sub] = route[0:8, :]
    lane = lax.broadcasted_iota(jnp.int32, (LANES, LANES), 1)
    tcnt = jnp.where(lane == 0, cnt_scr[...], jnp.where(lane == 1, n_wide, 0.0)).T
    tcnt_ref[sub] = tcnt[0:8, :].astype(jnp.int32)
    cnt_scr[...] = cnt_scr[...] + n_tile
    cnt_ref[...] = cnt_scr[...].T[0:8, :]


def _merge(x, mod, os_, lses, fm, g1, g2, wg, wap, wfp, wout, wrh, wrl, br, expand, ltri, etri, perms):
    b, s, _ = x.shape
    ts = MERGE_ROWS * MERGE_TILES_PER_STEP
    tile = lambda w: pl.BlockSpec((1, ts, w), lambda i, j: (i, j, 0))
    strided = lambda w: [pl.BlockSpec((1, d, ts // d, w), lambda i, j: (i, 0, j, 0)) for d in DILATIONS]
    consts = (g1, g2, wg, wap, wfp, wout, wrh, wrl, br, expand, ltri, etri) + tuple(perms)
    per_tile = lambda rows, w: pl.BlockSpec((MERGE_TILES_PER_STEP, rows, w), lambda i, j: (i * per_seq + j, 0, 0))
    n_tiles = b * s // MERGE_ROWS
    in_specs = ([tile(D_MODEL), pl.BlockSpec((1, 6, D_MODEL), lambda i, j: (i, 0, 0))]
                + strided(ATTN_WIDTH) + strided(LANES) + [tile(ATTN_WIDTH)]
                + [_const_spec(a.shape) for a in consts])
    per_seq = s // ts
    return pl.pallas_call(
        _merge_kernel,
        grid=(b, per_seq),
        in_specs=in_specs,
        out_specs=[tile(D_MODEL), tile(D_MODEL), tile(LANES), per_tile(8, MERGE_ROWS), per_tile(8, LANES),
                   pl.BlockSpec((8, LANES), lambda i, j: (0, 0))],
        out_shape=[jax.ShapeDtypeStruct((b, s, D_MODEL), F32),
                   jax.ShapeDtypeStruct((b, s, D_MODEL), BF16),
                   jax.ShapeDtypeStruct((b, s, LANES), F32),
                   jax.ShapeDtypeStruct((n_tiles, 8, MERGE_ROWS), F32),
                   jax.ShapeDtypeStruct((n_tiles, 8, LANES), jnp.int32),
                   jax.ShapeDtypeStruct((8, LANES), F32)],
        scratch_shapes=[pltpu.VMEM((LANES, LANES), F32)],
        compiler_params=pltpu.CompilerParams(
            dimension_semantics=("arbitrary", "arbitrary"), vmem_limit_bytes=VMEM_LIMIT),
        name="merge",
    )(x, mod, *os_, *lses, fm, *consts)


def _to_packed_rows(ref, mat):
    bits = lax.bitcast_convert_type(mat, jnp.uint32)
    for c in range(ROW_WORDS):
        lo = bits[:, c * LANES:(c + 1) * LANES] >> 16
        hi = bits[:, (c + ROW_WORDS) * LANES:(c + ROW_WORDS + 1) * LANES] & jnp.uint32(0xFFFF0000)
        ref[pl.ds(c, mat.shape[0], stride=ROW_WORDS), :] = lo | hi


def _from_packed_rows(ref):
    n = ref.shape[0] // ROW_WORDS
    words = [ref[pl.ds(c, n, stride=ROW_WORDS), :] for c in range(ROW_WORDS)]
    lo = [lax.bitcast_convert_type(w << 16, F32) for w in words]
    hi = [lax.bitcast_convert_type(w & jnp.uint32(0xFFFF0000), F32) for w in words]
    return jnp.concatenate(lo + hi, axis=1)


def _segment_copies(seg_n_ref, seg_row_ref, tile, sorted_ref, hbm_ref, sem, *, to_hbm, start, consecutive=True):
    base = tile * N_EXPERTS

    def copy(src, dst, off, size):
        vm_row = (src + off) if consecutive else 0
        vm = sorted_ref.at[pl.ds(pl.multiple_of(vm_row * ROW_WORDS, 2 * ROW_WORDS), size * ROW_WORDS)]
        hb = hbm_ref.at[pl.ds(pl.multiple_of((dst + off) * ROW_WORDS, 2 * ROW_WORDS), size * ROW_WORDS)]
        cp = pltpu.make_async_copy(vm, hb, sem) if to_hbm else pltpu.make_async_copy(hb, vm, sem)
        if start:
            cp.start()
        else:
            cp.wait()

    def segment(e, src):
        n = seg_n_ref[base + e]
        dst = seg_row_ref[base + e]
        n_long = n >> (LONG_PIECE.bit_length() - 1)

        def long_piece(i, c):
            copy(src, dst, i * LONG_PIECE, LONG_PIECE)
            return c

        lax.fori_loop(0, n_long, long_piece, 0)
        for bit in SHORT_PIECES:
            @pl.when((n & bit) != 0)
            def _():
                copy(src, dst, n & (-2 * bit), bit)
        return src + n

    lax.fori_loop(0, N_EXPERTS, segment, 0)


PIECE_SIZES = (LONG_PIECE,) + SHORT_PIECES
PIECE_SLOTS = 32
PIECE_TABLE = 512


def _piece_tables(n, src0, dst0):
    slots = jnp.arange(PIECE_SLOTS, dtype=jnp.int32)

    def compact(mask, src, dst):
        rank = jnp.cumsum(mask.astype(jnp.int32), axis=1) - 1
        sel = mask[:, :, None] & (rank[:, :, None] == slots)
        pick = lambda v: jnp.sum(jnp.where(sel, v[:, :, None], 0), axis=1)
        return jnp.concatenate([pick(src), pick(dst)], axis=1), jnp.sum(mask.astype(jnp.int32), axis=1)

    n_long = n >> (LONG_PIECE.bit_length() - 1)
    j = jnp.arange(MERGE_ROWS // LONG_PIECE, dtype=jnp.int32)
    offs = (j * LONG_PIECE)[None, :, None]
    flat = lambda v: v.reshape(v.shape[0], -1)
    lists, counts = [], []
    lst, cnt = compact(flat(n_long[:, None, :] > j[None, :, None]), flat(src0[:, None, :] + offs),
                       flat(dst0[:, None, :] + offs))
    lists.append(lst)
    counts.append(cnt)
    for bit in SHORT_PIECES:
        off = n & (-2 * bit)
        lst, cnt = compact((n & bit) != 0, src0 + off, dst0 + off)
        lists.append(lst)
        counts.append(cnt)
    head = jnp.stack(counts, axis=1)
    head = jnp.pad(head, ((0, 0), (0, 2 * PIECE_SLOTS - len(PIECE_SIZES))))
    table = jnp.concatenate([head] + lists, axis=1)
    table = jnp.pad(table, ((0, 0), (0, PIECE_TABLE - table.shape[1])))
    return table.reshape(-1, 1, PIECE_TABLE)


def _start_listed_copies(table_ref, sorted_ref, hbm_ref, sem, *, to_hbm):
    for c, size in enumerate(PIECE_SIZES):
        base = 2 * PIECE_SLOTS * (c + 1)

        def piece(i, carry):
            src = table_ref[0, 0, base + i]
            dst = table_ref[0, 0, base + PIECE_SLOTS + i]
            vm = sorted_ref.at[pl.ds(pl.multiple_of(src * ROW_WORDS, 2 * ROW_WORDS), size * ROW_WORDS)]
            hb = hbm_ref.at[pl.ds(pl.multiple_of(dst * ROW_WORDS, 2 * ROW_WORDS), size * ROW_WORDS)]
            (pltpu.make_async_copy(vm, hb, sem) if to_hbm else pltpu.make_async_copy(hb, vm, sem)).start()
            return carry

        lax.fori_loop(0, table_ref[0, 0, c], piece, 0)


def _wait_tile_copies(seg_n_ref, tile, sorted_ref, hbm_ref, sem, *, to_hbm):
    base = tile * N_EXPERTS
    total = lax.fori_loop(0, N_EXPERTS, lambda e, acc: acc + seg_n_ref[base + e], 0)

    def wait(size):
        vm = sorted_ref.at[pl.ds(0, size * ROW_WORDS)]
        hb = hbm_ref.at[pl.ds(0, size * ROW_WORDS)]
        (pltpu.make_async_copy(vm, hb, sem) if to_hbm else pltpu.make_async_copy(hb, vm, sem)).wait()

    def long_wait(i, c):
        wait(LONG_PIECE)
        return c

    lax.fori_loop(0, total >> (LONG_PIECE.bit_length() - 1), long_wait, 0)
    for bit in SHORT_PIECES:
        @pl.when((total & bit) != 0)
        def _():
            wait(bit)


def _dispatch_kernel(seg_n_ref, seg_row_ref, table_ref, routet_ref, u2_ref, xs_ref, sorted_scr, sems):
    tile = pl.program_id(0)
    n_tiles = pl.num_programs(0)
    slot = tile % 2
    tt = u2_ref.shape[0]
    rows = SORTED_ROWS

    def wait_copies(tile_, slot_):
        _wait_tile_copies(seg_n_ref, tile_, sorted_scr.at[slot_], xs_ref, sems.at[slot_], to_hbm=True)

    @pl.when(tile == 0)
    def _():
        sorted_scr[1, 0:LONG_PIECE * ROW_WORDS, :] = jnp.zeros((LONG_PIECE * ROW_WORDS, LANES), jnp.uint32)
        for zero_row in (n_tiles, n_tiles + 1):
            for start in (True, False):
                _segment_copies(seg_n_ref, seg_row_ref, zero_row, sorted_scr.at[1], xs_ref, sems.at[1],
                                to_hbm=True, start=start, consecutive=False)

    pos_t = routet_ref[0].astype(jnp.int32).astype(jnp.int16)
    j = lax.broadcasted_iota(jnp.int32, (rows, tt), 0).astype(jnp.int16)
    onehot = jnp.zeros((rows, tt), BF16)
    for k in range(TOP_K):
        onehot = jnp.where(j == pos_t[TOP_K + k:TOP_K + k + 1, :], jnp.ones_like(onehot), onehot)
    _to_packed_rows(sorted_scr.at[slot], _dot(onehot, u2_ref[...]))
    _start_listed_copies(table_ref, sorted_scr.at[slot], xs_ref, sems.at[slot], to_hbm=True)

    @pl.when(tile > 0)
    def _():
        wait_copies(tile - 1, 1 - slot)

    @pl.when(tile == n_tiles - 1)
    def _():
        wait_copies(tile, slot)


def _dispatch(seg_n, seg_row, table, route_t, u2, n_rows, tt=MERGE_ROWS):
    t = u2.shape[0]
    grid_spec = pltpu.PrefetchScalarGridSpec(
        num_scalar_prefetch=2,
        grid=(t // tt,),
        in_specs=[pl.BlockSpec((1, 1, PIECE_TABLE), lambda i, *_: (i, 0, 0), memory_space=pltpu.SMEM),
                  pl.BlockSpec((1, 8, tt), lambda i, *_: (i, 0, 0)),
                  pl.BlockSpec((tt, D_MODEL), lambda i, *_: (i, 0))],
        out_specs=pl.BlockSpec(memory_space=pl.ANY),
        scratch_shapes=[pltpu.VMEM((2, SORTED_ROWS * ROW_WORDS, LANES), jnp.uint32),
                        pltpu.SemaphoreType.DMA((2,))],
    )
    return pl.pallas_call(
        _dispatch_kernel,
        grid_spec=grid_spec,
        out_shape=jax.ShapeDtypeStruct((n_rows * ROW_WORDS, LANES), jnp.uint32),
        compiler_params=pltpu.CompilerParams(
            dimension_semantics=("arbitrary",), has_side_effects=True, vmem_limit_bytes=VMEM_LIMIT),
        name="dispatch",
    )(seg_n, seg_row, table, route_t, u2)


def _expert_kernel(be_ref, nb_ref, xs_ref, w1_ref, b1_ref, w2_ref, b2_ref, ys_ref):
    del be_ref
    i = pl.program_id(0)

    @pl.when(i < nb_ref[0])
    def _():
        hdn = _dot(_from_packed_rows(xs_ref).astype(BF16), w1_ref[0]) + b1_ref[0]
        glu = jnp.minimum(hdn[:, :D_FF], SWIGLU_LIMIT)
        lin = jnp.clip(hdn[:, D_FF:], -SWIGLU_LIMIT, SWIGLU_LIMIT)
        act = (lin + 1.0) * (glu * jax.nn.sigmoid(SWIGLU_ALPHA * glu))
        y = _dot(act.astype(BF16), w2_ref[0]) + b2_ref[0]
        _to_packed_rows(ys_ref, y.astype(BF16).astype(F32))

    @pl.when(i >= nb_ref[0])
    def _():
        ys_ref[...] = jnp.zeros_like(ys_ref)


def _expert(block_e, nb_used, xs, w1, b1, w2, b2):
    block = (EXPERT_ROWS * ROW_WORDS, LANES)
    n_blocks = xs.shape[0] // block[0]
    grid_spec = pltpu.PrefetchScalarGridSpec(
        num_scalar_prefetch=2,
        grid=(n_blocks,),
        in_specs=[pl.BlockSpec(block, lambda i, be, nb: (jnp.minimum(i, nb[0] - 1), 0)),
                  pl.BlockSpec((1, D_MODEL, 2 * D_FF), lambda i, be, nb: (be[i], 0, 0)),
                  pl.BlockSpec((1, 1, 2 * D_FF), lambda i, be, nb: (be[i], 0, 0)),
                  pl.BlockSpec((1, D_FF, D_MODEL), lambda i, be, nb: (be[i], 0, 0)),
                  pl.BlockSpec((1, 1, D_MODEL), lambda i, be, nb: (be[i], 0, 0))],
        out_specs=pl.BlockSpec(block, lambda i, be, nb: (i, 0)),
    )
    return pl.pallas_call(
        _expert_kernel,
        grid_spec=grid_spec,
        out_shape=jax.ShapeDtypeStruct(xs.shape, xs.dtype),
        compiler_params=pltpu.CompilerParams(
            dimension_semantics=("arbitrary",), vmem_limit_bytes=VMEM_LIMIT),
        name="expert",
    )(block_e, nb_used, xs, w1, b1, w2, b2)


def _combine_kernel(seg_n_ref, table_ref, next_table_ref, route_ref, h_ref, mod_ref, ys_ref, o_ref,
                    sorted_scr, sems):
    tile = pl.program_id(0)
    n_tiles = pl.num_programs(0)
    slot = tile % 2
    tt = h_ref.shape[0]
    rows = SORTED_ROWS

    @pl.when(tile == 0)
    def _():
        sorted_scr[...] = jnp.zeros_like(sorted_scr)
        _start_listed_copies(table_ref, sorted_scr.at[slot], ys_ref, sems.at[slot], to_hbm=False)

    @pl.when(tile + 1 < n_tiles)
    def _():
        _start_listed_copies(next_table_ref, sorted_scr.at[1 - slot], ys_ref, sems.at[1 - slot], to_hbm=False)

    route = route_ref[...]
    pos = route.astype(jnp.int32).astype(jnp.int16)
    weight = route.astype(BF16)
    j = lax.broadcasted_iota(jnp.int32, (tt, rows), 1).astype(jnp.int16)
    wmat = jnp.zeros((tt, rows), BF16)
    for k in range(TOP_K):
        wmat = jnp.where(j == pos[:, TOP_K + k:TOP_K + k + 1],
                         jnp.broadcast_to(weight[:, 2 * TOP_K + k:2 * TOP_K + k + 1], wmat.shape), wmat)
    _wait_tile_copies(seg_n_ref, tile, sorted_scr.at[slot], ys_ref, sems.at[slot], to_hbm=False)
    y = _from_packed_rows(sorted_scr.at[slot]).astype(BF16)
    o_ref[...] = h_ref[...] + mod_ref[0][5:6] * _dot(wmat.astype(BF16), y)


def _combine(seg_n, table, route, h, mod, ys, seq, tt=MERGE_ROWS):
    t = h.shape[0]
    n_tiles = t // tt
    per_seq = seq // tt
    table_spec = lambda ahead: pl.BlockSpec(
        (1, 1, PIECE_TABLE), lambda i, *_: (jnp.minimum(i + ahead, n_tiles - 1), 0, 0), memory_space=pltpu.SMEM)
    grid_spec = pltpu.PrefetchScalarGridSpec(
        num_scalar_prefetch=1,
        grid=(n_tiles,),
        in_specs=[table_spec(0), table_spec(1),
                  pl.BlockSpec((tt, LANES), lambda i, *_: (i, 0)),
                  pl.BlockSpec((tt, D_MODEL), lambda i, *_: (i, 0)),
                  pl.BlockSpec((1, 6, D_MODEL), lambda i, *_: (i // per_seq, 0, 0)),
                  pl.BlockSpec(memory_space=pl.ANY)],
        out_specs=pl.BlockSpec((tt, D_MODEL), lambda i, *_: (i, 0)),
        scratch_shapes=[pltpu.VMEM((2, SORTED_ROWS * ROW_WORDS, LANES), jnp.uint32),
                        pltpu.SemaphoreType.DMA((2,))],
    )
    return pl.pallas_call(
        _combine_kernel,
        grid_spec=grid_spec,
        out_shape=jax.ShapeDtypeStruct((t, D_MODEL), F32),
        compiler_params=pltpu.CompilerParams(
            dimension_semantics=("arbitrary",), vmem_limit_bytes=VMEM_LIMIT),
        name="combine",
    )(seg_n, table, table, route, h, mod, ys)


def _rope_tables(seq, ts):
    half = ROPE_DIM // 2
    inv_freq = jnp.power(jnp.float32(ROPE_THETA), -jnp.arange(half, dtype=F32) * 2.0 / ROPE_DIM)
    ang = jnp.arange(seq, dtype=F32)[:, None] * inv_freq[None, :]
    cos, sin = jnp.cos(ang), jnp.sin(ang)
    ones = jnp.ones((seq, HEAD_DIM - ROPE_DIM), F32)
    zeros = jnp.zeros_like(ones)
    zh = jnp.zeros((seq, half), F32)
    cos_t = jnp.concatenate([cos, cos, ones], axis=1)
    sin_prev = jnp.concatenate([zh, sin, zeros], axis=1)
    sin_next = jnp.concatenate([-sin, zh, zeros], axis=1)
    tables = jnp.stack([jnp.tile(t, (1, LANES // HEAD_DIM)) for t in (cos_t, sin_prev, sin_next)])
    per_group = []
    for d in DILATIONS:
        pos = jnp.arange(seq).reshape(seq // ts, ts // d, d).transpose(0, 2, 1).reshape(seq)
        per_group.append(tables[:, pos, :])
    return jnp.stack(per_group)


def _dft_tables(seq):
    k = jnp.arange(seq, dtype=jnp.int32)
    ang = ((k[:, None] * k[None, :]) % seq).astype(F32) * (2.0 * jnp.pi / seq)
    scale = seq ** -0.5
    cs = jnp.cos(ang) * scale
    ns = -jnp.sin(ang) * scale
    c = jnp.arange(FOURIER_GROUP_DIM, dtype=jnp.int32)
    angc = ((c[:, None] * c[None, :]) % FOURIER_GROUP_DIM).astype(F32) * (2.0 * jnp.pi / FOURIER_GROUP_DIM)
    eye = jnp.eye(ATTN_WIDTH // FOURIER_GROUP_DIM, dtype=F32)
    cscale = FOURIER_GROUP_DIM ** -0.5
    bd = jnp.concatenate([jnp.kron(eye, jnp.cos(angc) * cscale),
                          jnp.kron(eye, jnp.sin(angc) * cscale)], axis=1)
    return bd.astype(BF16), cs.astype(BF16), ns.astype(BF16)


def _trunk(x, c, p):
    b, s, _ = x.shape
    t = b * s
    mod = _ada(c, p["w_ada"], p["b_ada"]).reshape(b, 6, D_MODEL)
    *qkvs, fz = _inproj(x, mod, p["g1"], p["w_qkvf"], p["gains"], p["rope"], p["gmat"])
    os_, lses = zip(*[_attn(qkv, g) for g, qkv in enumerate(qkvs)])
    fm = _fourier(fz, p["bd"], p["cs"], p["ns"])
    h, u2, route, route_t, tcnt, cnt = _merge(
        x, mod, os_, lses, fm, p["g1"], p["g2"], p["wg"], p["wap"], p["wfp"], p["wout"], p["wrh"],
        p["wrl"], p["br"], p["expand"], p["ltri"], p["etri"], p["perms"])

    rows = EXPERT_ROWS
    counts = cnt[0, :N_EXPERTS].astype(jnp.int32)
    padded = (counts + rows - 1) // rows * rows
    pend = jnp.cumsum(padded)
    pstart = pend - padded
    n_blocks = -(-(t * TOP_K + N_EXPERTS * (t // MERGE_ROWS + rows - 1)) // rows)
    first = jnp.arange(N_EXPERTS) == 0
    seg_row = jnp.concatenate([pstart[None, :] + tcnt[:, 0, :N_EXPERTS], (pstart + counts)[None, :],
                               jnp.where(first, pend[-1], 0)[None, :]]).reshape(-1)
    seg_n = jnp.concatenate([tcnt[:, 1, :N_EXPERTS], (padded - counts)[None, :],
                             jnp.where(first, n_blocks * rows - pend[-1], 0)[None, :]]).reshape(-1)
    block_start = jnp.arange(n_blocks, dtype=jnp.int32) * rows
    block_e = jnp.minimum(jnp.sum((pend[None, :] <= block_start[:, None]).astype(jnp.int32), axis=-1),
                          N_EXPERTS - 1)
    nb_used = (pend[-1:] // rows).astype(jnp.int32)

    n_tile = tcnt[:, 1, :N_EXPERTS]
    table = _piece_tables(n_tile, jnp.cumsum(n_tile, axis=1) - n_tile, pstart[None, :] + tcnt[:, 0, :N_EXPERTS])
    xs = _dispatch(seg_n, seg_row, table, route_t, u2.reshape(t, D_MODEL), n_blocks * rows)
    ys = _expert(block_e, nb_used, xs, p["w1"], p["b1"], p["w2"], p["b2"])
    out = _combine(seg_n, table, route.reshape(t, LANES), h.reshape(t, D_MODEL), mod, ys, s)
    return out.reshape(b, s, D_MODEL)


def kernel(x_prompt, x_sample, c_prompt, c_sample, w_ada, b_ada, norm1_g, norm2_g, w_in, q_gain, k_gain,
           w_attn_proj, w_fourier_proj, w_out, w_router, b_router, w1, b1, w2, b2):
    seq = x_prompt.shape[1]
    assert w_ada.shape[0] == 1, "single-layer trunk"
    w_in0 = w_in[0]
    gains = jnp.zeros((8, ATTN_WIDTH), F32)
    for g in range(N_GROUPS):
        gains = gains.at[2 * g].set(jnp.tile(q_gain[0, g], HEADS))
        gains = gains.at[2 * g + 1].set(jnp.tile(k_gain[0, g], HEADS))
    heads = jnp.arange(ATTN_WIDTH) // HEAD_DIM
    gmat = ((heads[:, None] == heads[None, :]).astype(F32) / HEAD_DIM).astype(BF16)
    expand = (jnp.arange(LANES)[:, None] == heads[None, :]).astype(BF16)
    ltri = (jnp.arange(MERGE_ROWS)[:, None] > jnp.arange(MERGE_ROWS)[None, :]).astype(BF16)
    etri = (jnp.arange(LANES)[:, None] > jnp.arange(LANES)[None, :]).astype(BF16)
    perms = []
    for d in DILATIONS[1:]:
        n = MERGE_ROWS // d
        src = (jnp.arange(MERGE_ROWS) % d) * n + jnp.arange(MERGE_ROWS) // d
        perms.append((src[:, None] == jnp.arange(MERGE_ROWS)[None, :]).astype(BF16))
    wr = jnp.zeros((LANES, D_MODEL), F32).at[:N_EXPERTS].set(w_router[0].T)
    wrh = wr.astype(BF16)
    wrl = (wr - wrh.astype(F32)).astype(BF16)
    br = jnp.zeros((LANES, MERGE_ROWS), F32).at[:N_EXPERTS].set(b_router[0][:, None])
    bd, cs, ns = _dft_tables(seq)
    p = dict(
        w_ada=w_ada[0], b_ada=b_ada[0], g1=norm1_g, g2=norm2_g,
        w_qkvf=w_in0[:, :QKVF_WIDTH].astype(BF16), wg=w_in0[:, QKVF_WIDTH:].astype(BF16),
        gains=gains, rope=_rope_tables(seq, INPROJ_ROWS), gmat=gmat, bd=bd, cs=cs, ns=ns,
        wap=w_attn_proj[0].astype(BF16), wfp=w_fourier_proj[0].astype(BF16), wout=w_out[0].astype(BF16),
        wrh=wrh, wrl=wrl, br=br, expand=expand, ltri=ltri, etri=etri, perms=perms,
        w1=w1[0].astype(BF16), b1=b1[0].reshape(N_EXPERTS, 1, 2 * D_FF),
        w2=w2[0].astype(BF16), b2=b2[0].reshape(N_EXPERTS, 1, D_MODEL),
    )
    return _trunk(x_prompt, c_prompt, p), _trunk(x_sample, c_sample, p)
```
